```python
import math
import jax, jax.numpy as jnp
from jax import lax
import numpy as np

D_MODEL = 1024
BATCH = 16
SEQ = 256
DEPTH = 4
DEC_BATCH = 8
DEC_SEQ = 1024
PAST_LEN = 512

GRID_W = 64
N_BRANCH = 4
BRANCH_W = D_MODEL // 4
POOL_GROUPS = 4
POOL_GC = BRANCH_W // POOL_GROUPS
POOL_WINDOWS = (2, 4, 8, 16)
DA_HEADS = 4
DA_HD = BRANCH_W // DA_HEADS
DA_D = DA_HD // 2
FT_GROUPS = 4
FT_GC = BRANCH_W // FT_GROUPS
MLA_HEADS = 4
MLA_NOPE = 64
MLA_ROPE = 32
MLA_V = 64
MLA_QK = MLA_NOPE + MLA_ROPE
Q_LORA = 192
KV_LORA = 128
FFN_HIDDEN = 2816
CONV_W = 3
ROPE_BASE = 10000.0
EPS = 1e-6
Q_BLOCK = 128
GATE_COLS = N_BRANCH * D_MODEL
POOL_COLS = BRANCH_W
DA_COLS = 3 * DA_HEADS * DA_HD
FT_COLS = BRANCH_W
MLA_COLS = Q_LORA + KV_LORA + MLA_ROPE
IN_COLS = GATE_COLS + POOL_COLS + DA_COLS + FT_COLS + MLA_COLS
IN_SPLITS = [GATE_COLS, GATE_COLS + POOL_COLS, GATE_COLS + POOL_COLS + DA_COLS,
             GATE_COLS + POOL_COLS + DA_COLS + FT_COLS]

kernel_name = 'hybrid_diffusion_prefix_trunk_step'


def _rms(x, g):
    xf = x.astype(jnp.float32)
    y = xf * lax.rsqrt(jnp.mean(xf * xf, axis=-1, keepdims=True) + EPS)
    return (y * g.astype(jnp.float32)).astype(x.dtype)


def _rope_half(x, pos):
    half = x.shape[-1] // 2
    inv = ROPE_BASE ** (-jnp.arange(half, dtype=jnp.float32) / half)
    ang = pos.astype(jnp.float32)[:, None] * inv[None, :]
    cos = jnp.cos(ang)[:, None, :].astype(x.dtype)
    sin = jnp.sin(ang)[:, None, :].astype(x.dtype)
    x1, x2 = x[..., :half], x[..., half:]
    return jnp.concatenate([x1 * cos - x2 * sin, x1 * sin + x2 * cos], axis=-1)


def _rope_2d(x, pos):
    rows, cols = pos
    half = x.shape[-1] // 2
    return jnp.concatenate([_rope_half(x[..., :half], rows), _rope_half(x[..., half:], cols)], axis=-1)


def _map_query_blocks(fn, *qs):
    B, Q = qs[0].shape[:2]
    nb = Q // Q_BLOCK
    blocks = tuple(jnp.swapaxes(q.reshape((B, nb, Q_BLOCK) + q.shape[2:]), 0, 1) for q in qs)
    out = lax.map(lambda args: fn(*args), blocks)
    return jnp.swapaxes(out, 0, 1).reshape((B, Q) + out.shape[3:])


def _pool_mixer(xa, pool_w, pool_scale):
    B, T, _ = xa.shape
    xf = xa.reshape(B, T, POOL_GROUPS, POOL_GC).astype(jnp.float32)
    csum = jnp.concatenate([jnp.zeros_like(xf[:, :1]), jnp.cumsum(xf, axis=1)], axis=1)
    t = jnp.arange(T)[:, None]
    half = jnp.array(POOL_WINDOWS, dtype=jnp.int32)[None, :] // 2
    lo = jnp.clip(t - half, 0, T)
    hi = jnp.clip(t + half, 0, T)
    g = jnp.arange(POOL_GROUPS)[None, :]
    win = csum[:, hi, g, :] - csum[:, lo, g, :]
    mean = win / (hi - lo).astype(jnp.float32)[None, :, :, None]
    pooled = (mean - xf).astype(xa.dtype)
    y = jnp.einsum('btgc,gce->btge', pooled, pool_w)
    return y.reshape(B, T, BRANCH_W) * pool_scale


def _fourier(xc):
    B, T, _ = xc.shape
    xf = xc.reshape(B, T, FT_GROUPS, FT_GC).astype(jnp.float32)
    y = jnp.fft.fft2(xf, axes=(1, 3), norm='ortho').real
    return y.reshape(B, T, BRANCH_W).astype(xc.dtype)


def _diff_attention(q1, q2, k, v, lam):
    k1, k2 = k[..., :DA_D], k[..., DA_D:]
    scale = DA_D ** -0.5

    def blk(q1b, q2b):
        a1 = jax.nn.softmax((jnp.einsum('bqhd,bkhd->bhqk', q1b, k1) * scale).astype(jnp.float32), axis=-1)
        a2 = jax.nn.softmax((jnp.einsum('bqhd,bkhd->bhqk', q2b, k2) * scale).astype(jnp.float32), axis=-1)
        a = (a1 - lam * a2).astype(v.dtype)
        return jnp.einsum('bhqk,bkhd->bqhd', a, v)

    return _map_query_blocks(blk, q1, q2)


def _softmax_attention(q, k, v):
    scale = q.shape[-1] ** -0.5

    def blk(qb):
        s = jnp.einsum('bqhd,bkhd->bhqk', qb, k) * scale
        a = jax.nn.softmax(s.astype(jnp.float32), axis=-1).astype(v.dtype)
        return jnp.einsum('bhqk,bkhd->bqhd', a, v)

    return _map_query_blocks(blk, q)


def _mla_keys(ckv_n, krope, w_ukv, knorm, pos):
    B, T, _ = ckv_n.shape
    kv = (ckv_n @ w_ukv).reshape(B, T, MLA_HEADS, MLA_NOPE + MLA_V)
    k_nope, v = kv[..., :MLA_NOPE], kv[..., MLA_NOPE:]
    k_r = jnp.broadcast_to(krope[:, :, None, :], (B, T, MLA_HEADS, MLA_ROPE))
    k = _rms(jnp.concatenate([k_nope, k_r], axis=-1), knorm)
    if pos is not None:
        k = jnp.concatenate([k[..., :MLA_NOPE], _rope_2d(k[..., MLA_NOPE:], pos)], axis=-1)
    return k, v


def _conv_ffn(h, w_up, conv_w, conv_b, w_down):
    u = h @ w_up
    up = jnp.pad(u, ((0, 0), (1, 1), (0, 0)))
    u = up[:, :-2] * conv_w[0] + up[:, 1:-1] * conv_w[1] + up[:, 2:] * conv_w[2] + conv_b
    gate, val = jnp.split(u, 2, axis=-1)
    return (jax.nn.silu(gate) * val) @ w_down


def _layer(x, cond, p, lam_init, pos, ctx):
    B, T, _ = x.shape
    mod = jnp.einsum('bd,de->be', jax.nn.silu(cond), p['w_ada']) + p['b_ada']
    sh1, sc1, g1, sh2, sc2, g2 = jnp.split(mod[:, None, :], 6, axis=-1)
    h = _rms(x, p['g_norm1']) * (1 + sc1) + sh1
    u = h @ p['w_in']
    gates, xa, qkv_da, xc, mla_in = jnp.split(u, IN_SPLITS, axis=-1)

    out_a = _pool_mixer(xa, p['pool_w'], p['pool_scale'])

    qkv = qkv_da.reshape(B, T, 3, DA_HEADS, DA_HD)
    q_da, k_da, v_da = qkv[:, :, 0], qkv[:, :, 1], qkv[:, :, 2]
    q1 = _rms(q_da[..., :DA_D], p['diff_qnorm'])
    q2 = _rms(q_da[..., DA_D:], p['diff_qnorm'])
    k1 = _rms(k_da[..., :DA_D], p['diff_knorm'])
    k2 = _rms(k_da[..., DA_D:], p['diff_knorm'])
    k_cache_form = jnp.concatenate([k1, k2], axis=-1)
    if pos is not None:
        q1, q2 = _rope_2d(q1, pos), _rope_2d(q2, pos)
        k_self = jnp.concatenate([_rope_2d(k1, pos), _rope_2d(k2, pos)], axis=-1)
    else:
        k_self = k_cache_form
    if ctx is None:
        k_all, v_all = k_self, v_da
    else:
        k_all = jnp.concatenate([ctx['diff_k'], k_self], axis=1)
        v_all = jnp.concatenate([ctx['diff_v'], v_da], axis=1)
    lq = p['diff_lambda'].astype(jnp.float32)
    lam = jnp.exp(jnp.sum(lq[0] * lq[1])) - jnp.exp(jnp.sum(lq[2] * lq[3])) + lam_init
    out_b = _diff_attention(q1, q2, k_all, v_all, lam)
    out_b = (_rms(out_b, p['diff_subln']) * (1.0 - lam_init)).reshape(B, T, BRANCH_W)

    out_c = _fourier(xc)

    c_q = mla_in[..., :Q_LORA]
    c_kv = mla_in[..., Q_LORA:Q_LORA + KV_LORA]
    k_rope = mla_in[..., Q_LORA + KV_LORA:]
    q = (_rms(c_q, p['mla_qa_norm']) @ p['w_uq']).reshape(B, T, MLA_HEADS, MLA_QK)
    q = _rms(q, p['mla_qnorm'])
    if pos is not None:
        q = jnp.concatenate([q[..., :MLA_NOPE], _rope_2d(q[..., MLA_NOPE:], pos)], axis=-1)
    ckv_n = _rms(c_kv, p['mla_kv_norm'])
    k_m, v_m = _mla_keys(ckv_n, k_rope, p['w_ukv'], p['mla_knorm'], pos)
    if ctx is not None:
        k_c, v_c = _mla_keys(ctx['ckv'], ctx['krope'], p['w_ukv'], p['mla_knorm'], None)
        k_m = jnp.concatenate([k_c, k_m], axis=1)
        v_m = jnp.concatenate([v_c, v_m], axis=1)
    out_d = _softmax_attention(q, k_m, v_m).reshape(B, T, MLA_HEADS * MLA_V)

    branches = jnp.stack([out_a, out_b, out_c, out_d], axis=2)
    proj = jnp.einsum('btnw,nwd->btnd', branches, p['w_branch'])
    gsig = jax.nn.sigmoid(gates.reshape(B, T, N_BRANCH, D_MODEL).astype(jnp.float32)).astype(x.dtype)
    merged = jnp.sum(gsig * proj, axis=2)
    x = x + g1 * (merged @ p['w_out'])

    h2 = _rms(x, p['g_norm2']) * (1 + sc2) + sh2
    x = x + g2 * _conv_ffn(h2, p['w_up'], p['conv_w'], p['conv_b'], p['w_down'])
    return x, (k_cache_form, v_da, ckv_n, k_rope)


def _lambda_init(l):
    return 0.8 - 0.6 * math.exp(-0.3 * l)


def setup_inputs(seed: int = 0) -> dict:
    key = jax.random.key(seed)
    ks = jax.random.split(key, 32)
    f32 = jnp.float32

    def nrm(k, shape, scale=1.0):
        return jax.random.normal(k, shape, f32) * scale

    def gain(k, shape):
        return 1.0 + 0.05 * jax.random.normal(k, shape, f32)

    F2 = 2 * FFN_HIDDEN
    return {
        'x_prompt': nrm(ks[0], (BATCH, SEQ, D_MODEL)),
        'x_sample': nrm(ks[1], (DEC_BATCH, DEC_SEQ, D_MODEL)),
        'c': nrm(ks[2], (DEC_BATCH, D_MODEL)),
        'c_ctx': nrm(ks[3], (D_MODEL,)),
        'cache_diff_k': nrm(ks[4], (DEC_BATCH, DEPTH, PAST_LEN, DA_HEADS, DA_HD)),
        'cache_diff_v': nrm(ks[5], (DEC_BATCH, DEPTH, PAST_LEN, DA_HEADS, DA_HD)),
        'cache_mla_ckv': nrm(ks[6], (DEC_BATCH, DEPTH, PAST_LEN, KV_LORA)),
        'cache_mla_krope': nrm(ks[7], (DEC_BATCH, DEPTH, PAST_LEN, MLA_ROPE)),
        'w_ada': nrm(ks[8], (DEPTH, D_MODEL, 6 * D_MODEL), 0.5 * D_MODEL ** -0.5),
        'b_ada': nrm(ks[9], (DEPTH, 6 * D_MODEL), 0.01),
        'g_norm1': gain(ks[10], (DEPTH, D_MODEL)),
        'g_norm2': gain(ks[11], (DEPTH, D_MODEL)),
        'w_in': nrm(ks[12], (DEPTH, D_MODEL, IN_COLS), D_MODEL ** -0.5),
        'pool_w': nrm(ks[13], (DEPTH, POOL_GROUPS, POOL_GC, POOL_GC), POOL_GC ** -0.5),
        'pool_scale': gain(ks[14], (DEPTH, BRANCH_W)),
        'diff_qnorm': gain(ks[15], (DEPTH, DA_D)),
        'diff_knorm': gain(ks[16], (DEPTH, DA_D)),
        'diff_lambda': nrm(ks[17], (DEPTH, 4, DA_D), 0.1),
        'diff_subln': gain(ks[18], (DEPTH, DA_HD)),
        'mla_qa_norm': gain(ks[19], (DEPTH, Q_LORA)),
        'w_uq': nrm(ks[20], (DEPTH, Q_LORA, MLA_HEADS * MLA_QK), Q_LORA ** -0.5),
        'mla_kv_norm': gain(ks[21], (DEPTH, KV_LORA)),
        'w_ukv': nrm(ks[22], (DEPTH, KV_LORA, MLA_HEADS * (MLA_NOPE + MLA_V)), KV_LORA ** -0.5),
        'mla_qnorm': gain(ks[23], (DEPTH, MLA_QK)),
        'mla_knorm': gain(ks[24], (DEPTH, MLA_QK)),
        'w_branch': nrm(ks[25], (DEPTH, N_BRANCH, BRANCH_W, D_MODEL), BRANCH_W ** -0.5),
        'w_out': nrm(ks[26], (DEPTH, D_MODEL, D_MODEL), D_MODEL ** -0.5),
        'w_up': nrm(ks[27], (DEPTH, D_MODEL, F2), D_MODEL ** -0.5),
        'conv_w': nrm(ks[28], (DEPTH, CONV_W, F2), CONV_W ** -0.5),
        'conv_b': nrm(ks[29], (DEPTH, F2), 0.01),
        'w_down': nrm(ks[30], (DEPTH, FFN_HIDDEN, D_MODEL), FFN_HIDDEN ** -0.5),
    }


def reference(x_prompt, x_sample, c, c_ctx, cache_diff_k, cache_diff_v, cache_mla_ckv, cache_mla_krope,
              w_ada, b_ada, g_norm1, g_norm2, w_in, pool_w, pool_scale, diff_qnorm, diff_knorm,
              diff_lambda, diff_subln, mla_qa_norm, w_uq, mla_kv_norm, w_ukv, mla_qnorm, mla_knorm,
              w_branch, w_out, w_up, conv_w, conv_b, w_down):
    def params(l):
        return {
            'w_ada': w_ada[l], 'b_ada': b_ada[l], 'g_norm1': g_norm1[l], 'g_norm2': g_norm2[l],
            'w_in': w_in[l], 'pool_w': pool_w[l], 'pool_scale': pool_scale[l],
            'diff_qnorm': diff_qnorm[l], 'diff_knorm': diff_knorm[l], 'diff_lambda': diff_lambda[l],
            'diff_subln': diff_subln[l], 'mla_qa_norm': mla_qa_norm[l], 'w_uq': w_uq[l],
            'mla_kv_norm': mla_kv_norm[l], 'w_ukv': w_ukv[l], 'mla_qnorm': mla_qnorm[l],
            'mla_knorm': mla_knorm[l], 'w_branch': w_branch[l], 'w_out': w_out[l],
            'w_up': w_up[l], 'conv_w': conv_w[l], 'conv_b': conv_b[l], 'w_down': w_down[l],
        }

    y = x_prompt
    ks_, vs_, cs_, rs_ = [], [], [], []
    for l in range(DEPTH):
        y, (kc, vc, ckv, kr) = _layer(y, c_ctx[None, :], params(l), _lambda_init(l), None, None)
        ks_.append(kc)
        vs_.append(vc)
        cs_.append(ckv)
        rs_.append(kr)
    y_prompt = y
    new_diff_k = jnp.stack(ks_, axis=1)
    new_diff_v = jnp.stack(vs_, axis=1)
    new_mla_ckv = jnp.stack(cs_, axis=1)
    new_mla_krope = jnp.stack(rs_, axis=1)

    n_tok = x_sample.shape[1]
    n_rows = n_tok // GRID_W
    rows = jnp.repeat(jnp.arange(n_rows), GRID_W)
    cols = jnp.tile(jnp.arange(GRID_W), n_rows)
    pos = (rows, cols)
    z = x_sample
    for l in range(DEPTH):
        ctx = {'diff_k': cache_diff_k[:, l], 'diff_v': cache_diff_v[:, l],
               'ckv': cache_mla_ckv[:, l], 'krope': cache_mla_krope[:, l]}
        z, _ = _layer(z, c, params(l), _lambda_init(l), pos, ctx)
    y_sample = z

    return (y_prompt, y_sample, new_diff_k, new_diff_v, new_mla_ckv, new_mla_krope)
```

```python
import functools
import math

import jax
import jax.numpy as jnp
import numpy as np
from jax import lax
from jax.experimental import pallas as pl
from jax.experimental.pallas import tpu as pltpu

F32 = jnp.float32
BF16 = jnp.bfloat16

D = 1024
DEPTH = 4
CTX_B, CTX_T = 16, 256
LAT_B, LAT_T = 8, 1024
PAST = 512
TILE = 1024
CTX_TILES = CTX_B * CTX_T // TILE
LAT_TILES = LAT_B * LAT_T // TILE
TILES = CTX_TILES + LAT_TILES
ROWS = TILES * TILE
CTX_ROWS = CTX_TILES * TILE
LAT_ROWS = LAT_TILES * TILE
GRID_W = 64
BW = 256
DA_D = 32
DA_HD = 64
HEADS = 4
MLA_NOPE, MLA_ROPE, MLA_V, MLA_QK = 64, 32, 64, 96
Q_LORA, KV_LORA = 192, 128
FFN = 2816
FFN_CHUNK = 256
FFN_CHUNKS = FFN // FFN_CHUNK
POOL_HALF = (1, 2, 4, 8)
POOL_PAD = 16
ROPE_BASE = 10000.0
EPS = 1e-6
LOG2E = 1.4426950408889634
TQ = 256
LANES = 128
VMEM_LIMIT = 56 * 1024 * 1024

C_XA, C_Q, C_K, C_V, C_XC, C_CQ, C_CKV, C_KR, C_END = 0, 256, 512, 768, 1024, 1280, 1536, 1664, 1792


def _lambda_init(l):
    return 0.8 - 0.6 * math.exp(-0.3 * l)


def _sigmoid(x):
    return 0.5 * jnp.tanh(0.5 * x) + 0.5


def _rms(x, n):
    ms = jnp.sum(x * x, axis=-1, keepdims=True) * (1.0 / n)
    return x * lax.rsqrt(ms + EPS)


def _dot(a, b):
    return jnp.dot(a, b, preferred_element_type=F32)


def _dot_nt(a, b):
    return lax.dot_general(a, b, (((1,), (1,)), ((), ())), preferred_element_type=F32)


def _const(shape):
    n = len(shape)
    return pl.BlockSpec(shape, lambda *_: (0,) * n, pipeline_mode=pl.Buffered(1))


def _layer(shape, l):
    n = len(shape)
    return pl.BlockSpec((None,) + shape, lambda *_: (l,) + (0,) * n, pipeline_mode=pl.Buffered(1))


def _params(n_grid):
    return pltpu.CompilerParams(dimension_semantics=("arbitrary",) * n_grid, vmem_limit_bytes=VMEM_LIMIT)


def _row_spec(width):
    return pl.BlockSpec((TILE, width), lambda i: (i, 0))


def _ctx_row_spec(width):
    return pl.BlockSpec((TILE, width), lambda i: (jnp.minimum(i, CTX_TILES - 1), 0))


def _lat_row_spec(width):
    return pl.BlockSpec((TILE, width), lambda i: (jnp.maximum(i - CTX_TILES, 0), 0))


def _x_specs(split):
    return [_ctx_row_spec(D), _lat_row_spec(D)] if split else [_row_spec(D)]


def _read_x(x_refs, is_lat):
    if len(x_refs) == 2:
        return jnp.where(is_lat, x_refs[1][...], x_refs[0][...])
    return x_refs[0][...]


def _mod_spec(l):
    return pl.BlockSpec((None, 1, 1, 6 * D), lambda i: (l, jnp.maximum(i - (CTX_TILES - 1), 0), 0, 0))


def _group_spec(shape):
    n = len(shape)
    return pl.BlockSpec((None,) + shape, lambda i: (jnp.minimum(i // CTX_TILES, 1),) + (0,) * n,
                        pipeline_mode=pl.Buffered(1))


def _mod_kernel(c_ref, w_ref, b_ref, o_ref):
    c = c_ref[...]
    s = (c * _sigmoid(c)).astype(BF16)
    o_ref[0] = _dot(s, w_ref[0].astype(BF16)) + b_ref[0]


def _mod_call(cond, w_ada, b_ada):
    nc = 4
    cw = 6 * D // nc
    return pl.pallas_call(
        _mod_kernel,
        grid=(DEPTH, nc),
        in_specs=[
            pl.BlockSpec((16, D), lambda l, j: (0, 0)),
            pl.BlockSpec((1, D, cw), lambda l, j: (l, 0, j)),
            pl.BlockSpec((1, 1, cw), lambda l, j: (l, 0, j)),
        ],
        out_specs=pl.BlockSpec((1, 16, cw), lambda l, j: (l, 0, j)),
        out_shape=jax.ShapeDtypeStruct((DEPTH, 16, 6 * D), F32),
        compiler_params=_params(2),
        name="adaln_mod",
    )(cond, w_ada, b_ada.reshape(DEPTH, 1, 6 * D))


def _mla_key_heads(knope, kr_pad, gkn):
    out = []
    for h in range(HEADS):
        kh = knope[:, h * LANES:(h + 1) * LANES] + kr_pad
        out.append(_rms(kh, MLA_QK) * gkn)
    return out


def _ctxkv_kernel(ckv_ref, kr_ref, wuk_ref, wuv_ref, gkn_ref, kc_ref, vc_ref):
    cb = ckv_ref[...].astype(BF16)
    knope = _dot(cb, wuk_ref[...])
    heads = _mla_key_heads(knope, kr_ref[...], gkn_ref[...])
    for h in range(HEADS):
        kc_ref[:, h * LANES:(h + 1) * LANES] = heads[h].astype(BF16)
    vc_ref[...] = _dot(cb, wuv_ref[...]).astype(BF16)


def _ctxkv_call(ckv, kr_pad, w):
    cache = lambda width: pl.BlockSpec((None, None, PAST, width), lambda b, l: (b, l, 0, 0))
    wspec = lambda r, c: pl.BlockSpec((None, r, c), lambda b, l: (l, 0, 0))
    return pl.pallas_call(
        _ctxkv_kernel,
        grid=(LAT_B, DEPTH),
        in_specs=[cache(KV_LORA), cache(LANES), wspec(KV_LORA, 512), wspec(KV_LORA, BW), wspec(1, LANES)],
        out_specs=[cache(512), cache(BW)],
        out_shape=[
            jax.ShapeDtypeStruct((LAT_B, DEPTH, PAST, 512), BF16),
            jax.ShapeDtypeStruct((LAT_B, DEPTH, PAST, BW), BF16),
        ],
        compiler_params=_params(2),
        name="mla_ctx_keys",
    )(ckv, kr_pad, w["wuk"], w["wuv"], w["gkn"])


def _norm_mod(x, g, sc, sh):
    return (_rms(x, D) * g) * (1.0 + sc) + sh


def _rope(x, tab_ref):
    return (x * tab_ref[0] + pltpu.roll(x, LANES - 8, 1) * tab_ref[1]
            + pltpu.roll(x, 8, 1) * tab_ref[2])


def _seg_rms(x, seg, gain):
    sq = x * x
    hi = sq.astype(BF16)
    lo = (sq - hi.astype(F32)).astype(BF16)
    ms = (_dot(hi, seg) + _dot(lo, seg)) * (1.0 / DA_D)
    return x * lax.rsqrt(ms + EPS) * gain


def _inproj_kernel(n_x, *refs):
    x_refs = refs[:n_x]
    (mod_ref, g1_ref, win_ref, poolw_ref, pools_ref, seg_ref, dgq_ref, dgk_ref,
     ropd_ref, ropm_ref, cc_ref, ss_ref, ft_ref, gqa_ref, wuq_ref, gqn_ref, gkv_ref,
     wuk_ref, wuv_ref, gkn_ref,
     pa_ref, qd_ref, kd_ref, vd_ref, oc_ref, qm_ref, km_ref, vm_ref,
     kcf_ref, vf_ref, ckv_ref, kr_ref, pad_ref) = refs[n_x:]
    is_lat = pl.program_id(0) >= CTX_TILES
    mod = mod_ref[0]
    hb = _norm_mod(_read_x(x_refs, is_lat), g1_ref[...], mod[:, D:2 * D], mod[:, 0:D]).astype(BF16)

    xa = _dot(hb, win_ref[:, C_XA:C_XA + BW])
    pad_ref[0:POOL_PAD, :] = jnp.zeros((POOL_PAD, BW), F32)
    pad_ref[POOL_PAD + TILE:, :] = jnp.zeros((POOL_PAD, BW), F32)
    pad_ref[POOL_PAD:POOL_PAD + TILE, :] = xa
    tm1 = jnp.where(is_lat, LAT_T - 1, CTX_T - 1)
    pos = lax.broadcasted_iota(jnp.int32, (TILE, BW), 0) & tm1
    lane = lax.broadcasted_iota(jnp.int32, (TILE, BW), 1)

    def term(s):
        v = pad_ref[POOL_PAD + s:POOL_PAD + s + TILE, :]
        if s < 0:
            return jnp.where(pos >= -s, v, 0.0)
        return jnp.where(pos + s <= tm1, v, 0.0)

    acc = xa
    wins, cnts = [], []
    lo_s, hi_s = 0, 0
    for half in POOL_HALF:
        for s in list(range(-half, lo_s)) + list(range(hi_s + 1, half)):
            acc = acc + term(s)
        lo_s, hi_s = -half, half - 1
        wins.append(acc)
        cnts.append((jnp.minimum(pos + half, tm1 + 1) - jnp.maximum(pos - half, 0)).astype(F32))
    win = jnp.where(lane < 64, wins[0], jnp.where(lane < 128, wins[1], jnp.where(lane < 192, wins[2], wins[3])))
    cnt = jnp.where(lane < 64, cnts[0], jnp.where(lane < 128, cnts[1], jnp.where(lane < 192, cnts[2], cnts[3])))
    pooled = (win / cnt - xa).astype(BF16)
    pa_ref[...] = (_dot(pooled, poolw_ref[...]) * pools_ref[...]).astype(BF16)

    seg = seg_ref[...]
    qscale = DA_D ** -0.5 * LOG2E
    q = _dot(hb, win_ref[:, C_Q:C_Q + BW])
    k = _dot(hb, win_ref[:, C_K:C_K + BW])
    v = _dot(hb, win_ref[:, C_V:C_V + BW])
    kns = []
    for s in range(BW // LANES):
        sl = slice(s * LANES, (s + 1) * LANES)
        qn = _seg_rms(q[:, sl], seg, dgq_ref[...])
        kn = _seg_rms(k[:, sl], seg, dgk_ref[...])
        kns.append(kn)
        qd_ref[:, sl] = (_rope(qn, ropd_ref) * qscale).astype(BF16)
        kd_ref[:, sl] = _rope(kn, ropd_ref).astype(BF16)
    vd_ref[...] = v.astype(BF16)

    xcb = _dot(hb, win_ref[:, C_XC:C_XC + BW]).astype(BF16)
    xcc = _dot(xcb, cc_ref[...]).astype(BF16)
    xcs = _dot(xcb, ss_ref[...]).astype(BF16)
    y = _dot(ft_ref[0], xcc) + _dot(ft_ref[1], xcs)
    fnorm = jnp.where(is_lat, (LAT_T * 64.0) ** -0.5, (CTX_T * 64.0) ** -0.5)
    oc_ref[...] = (y * fnorm).astype(BF16)

    cq = _dot(hb, win_ref[:, C_CQ:C_CQ + BW])
    qa = (_rms(cq, Q_LORA) * gqa_ref[...]).astype(BF16)
    qf = _dot(qa, wuq_ref[...])
    mscale = MLA_QK ** -0.5 * LOG2E
    for h in range(HEADS):
        sl = slice(h * LANES, (h + 1) * LANES)
        qh = _rms(qf[:, sl], MLA_QK) * gqn_ref[...]
        qm_ref[:, sl] = (_rope(qh, ropm_ref) * mscale).astype(BF16)
    ckv = _dot(hb, win_ref[:, C_CKV:C_CKV + KV_LORA])
    ckv_n = _rms(ckv, KV_LORA) * gkv_ref[...]
    cb = ckv_n.astype(BF16)
    kr_pad = _dot(hb, win_ref[:, C_KR:C_KR + LANES])
    heads = _mla_key_heads(_dot(cb, wuk_ref[...]), kr_pad, gkn_ref[...])
    for h in range(HEADS):
        km_ref[:, h * LANES:(h + 1) * LANES] = _rope(heads[h], ropm_ref).astype(BF16)
    vm_ref[...] = _dot(cb, wuv_ref[...]).astype(BF16)

    @pl.when(jnp.logical_not(is_lat))
    def _():
        for s in range(BW // LANES):
            kcf_ref[:, s * LANES:(s + 1) * LANES] = kns[s]
        vf_ref[...] = v
        ckv_ref[...] = ckv_n
        kr_ref[...] = kr_pad[:, MLA_NOPE:MLA_NOPE + MLA_ROPE]


def _inproj_call(l, xs, mods, w, consts):
    row_outs = [(BW, BF16), (BW, BF16), (BW, BF16), (BW, BF16), (BW, BF16), (512, BF16), (512, BF16), (BW, BF16)]
    ctx_outs = [(BW, F32), (BW, F32), (KV_LORA, F32), (MLA_ROPE, F32)]
    return pl.pallas_call(
        functools.partial(_inproj_kernel, len(xs)),
        grid=(TILES,),
        in_specs=_x_specs(len(xs) == 2) + [
            _mod_spec(l), _layer((1, D), l), _layer((D, C_END), l), _layer((BW, BW), l), _layer((1, BW), l),
            _const((LANES, LANES)), _layer((1, LANES), l), _layer((1, LANES), l),
            _group_spec((3, TILE, LANES)), _group_spec((3, TILE, LANES)),
            _const((BW, BW)), _const((BW, BW)), _group_spec((2, TILE, TILE)),
            _layer((1, BW), l), _layer((BW, 512), l), _layer((1, LANES), l), _layer((1, KV_LORA), l),
            _layer((KV_LORA, 512), l), _layer((KV_LORA, BW), l), _layer((1, LANES), l),
        ],
        out_specs=[_row_spec(wd) for wd, _ in row_outs] + [_ctx_row_spec(wd) for wd, _ in ctx_outs],
        out_shape=([jax.ShapeDtypeStruct((ROWS, wd), dt) for wd, dt in row_outs]
                   + [jax.ShapeDtypeStruct((CTX_ROWS, wd), dt) for wd, dt in ctx_outs]),
        scratch_shapes=[pltpu.VMEM((TILE + 2 * POOL_PAD, BW), F32)],
        compiler_params=_params(1),
        name="inproj",
    )(*xs, mods, w["g1"], w["win_r"], w["poolw"], w["pools"], consts["seg"], w["dgq"], w["dgk"],
      consts["ropd"], consts["ropm"], consts["cc"], consts["ss"], consts["ft"],
      w["gqa"], w["wuq"], w["gqn"], w["gkv"], w["wuk"], w["wuv"], w["gkn"])


def _softmax_parts(q, ks):
    ss = [_dot_nt(q, k) for k in ks]
    m = ss[0].max(axis=-1, keepdims=True)
    for s in ss[1:]:
        m = jnp.maximum(m, s.max(axis=-1, keepdims=True))
    ps = [jnp.exp2(s - m) for s in ss]
    l = ps[0].sum(axis=-1, keepdims=True)
    for p in ps[1:]:
        l = l + p.sum(axis=-1, keepdims=True)
    return [p.astype(BF16) for p in ps], 1.0 / l


def _attend(qd, kds, vds, qm, kms, vms, lam, gsub, lam_init, ob_ref, od_ref):
    n = qd.shape[0]
    lane = lax.broadcasted_iota(jnp.int32, (n, BW), 1)
    zero = jnp.zeros((), BF16)
    out_b = jnp.zeros((n, BW), F32)
    out_d = jnp.zeros((n, BW), F32)
    for h in range(HEADS):
        head = (lane >= h * DA_HD) & (lane < (h + 1) * DA_HD)
        o = None
        for half in range(2):
            lo = h * DA_HD + half * DA_D
            qc = jnp.where((lane >= lo) & (lane < lo + DA_D), qd, zero)
            ps, rl = _softmax_parts(qc, kds)
            pv = sum(_dot(p, v) for p, v in zip(ps, vds)) * rl
            o = pv if half == 0 else o - lam * pv
        o = jnp.where(head, o, 0.0)
        ms = jnp.sum(o * o, axis=-1, keepdims=True) * (1.0 / DA_HD)
        out_b = out_b + o * lax.rsqrt(ms + EPS)
        sl = slice(h * LANES, (h + 1) * LANES)
        ps, rl = _softmax_parts(qm[:, sl], [k[:, sl] for k in kms])
        pv = sum(_dot(p, v) for p, v in zip(ps, vms)) * rl
        out_d = out_d + jnp.where(head, pv, 0.0)
    ob_ref[...] = (out_b * gsub * (1.0 - lam_init)).astype(BF16)
    od_ref[...] = out_d.astype(BF16)


def _attn_kernel(lam_init, qd_ref, kd_ref, vd_ref, cdk_ref, cdv_ref, qm_ref, km_ref, vm_ref, kc_ref, vc_ref,
                 dl_ref, gsub_ref, ob_ref, od_ref):
    i = pl.program_id(0)
    j = pl.program_id(1)
    dl = dl_ref[...]
    lam = (jnp.exp(jnp.sum(dl[0:1] * dl[1:2], axis=-1, keepdims=True))
           - jnp.exp(jnp.sum(dl[2:3] * dl[3:4], axis=-1, keepdims=True)) + lam_init)
    rows = pl.ds(pl.multiple_of(j * TQ, TQ), TQ)
    qd = qd_ref[rows, :]
    qm = qm_ref[rows, :]

    @pl.when(i < CTX_TILES)
    def _():
        _attend(qd, [kd_ref[rows, :]], [vd_ref[rows, :]], qm, [km_ref[rows, :]], [vm_ref[rows, :]],
                lam, gsub_ref[...], lam_init, ob_ref, od_ref)

    @pl.when(i >= CTX_TILES)
    def _():
        _attend(qd, [cdk_ref[...].astype(BF16), kd_ref[...]], [cdv_ref[...].astype(BF16), vd_ref[...]],
                qm, [kc_ref[...], km_ref[...]], [vc_ref[...], vm_ref[...]],
                lam, gsub_ref[...], lam_init, ob_ref, od_ref)


def _attn_call(l, qd, kd, vd, cdk, cdv, qm, km, vm, kc, vc, w):
    tile = lambda width: pl.BlockSpec((TILE, width), lambda i, j: (i, 0))
    cache = lambda width: pl.BlockSpec((None, None, PAST, width),
                                       lambda i, j: (jnp.maximum(i - CTX_TILES, 0), l, 0, 0))
    blk = pl.BlockSpec((TQ, BW), lambda i, j: (i * (TILE // TQ) + j, 0))
    return pl.pallas_call(
        functools.partial(_attn_kernel, _lambda_init(l)),
        grid=(TILES, TILE // TQ),
        in_specs=[tile(BW), tile(BW), tile(BW), cache(BW), cache(BW), tile(512), tile(512), tile(BW),
                  cache(512), cache(BW), _layer((4, DA_D), l), _layer((1, BW), l)],
        out_specs=[blk, blk],
        out_shape=[jax.ShapeDtypeStruct((ROWS, BW), BF16)] * 2,
        compiler_params=_params(2),
        name="attention",
    )(qd, kd, vd, cdk, cdv, qm, km, vm, kc, vc, w["dl"], w["gsub"])


def _merge_kernel(n_x, *refs):
    x_refs = refs[:n_x]
    mod_ref, g1_ref, wg_ref, wb_ref, wo_ref, pa_ref, pb_ref, pc_ref, pd_ref, o_ref = refs[n_x:]
    mod = mod_ref[0]
    x = _read_x(x_refs, pl.program_id(0) >= CTX_TILES)
    hb = _norm_mod(x, g1_ref[...], mod[:, D:2 * D], mod[:, 0:D]).astype(BF16)
    merged = None
    for n, br_ref in enumerate((pa_ref, pb_ref, pc_ref, pd_ref)):
        gate = _sigmoid(_dot(hb, wg_ref[:, n * D:(n + 1) * D]))
        t = gate * _dot(br_ref[...], wb_ref[n])
        merged = t if merged is None else merged + t
    o_ref[...] = x + mod[:, 2 * D:3 * D] * _dot(merged.astype(BF16), wo_ref[...])


def _merge_call(l, xs, mods, w, pa, pb, pc, pd):
    return pl.pallas_call(
        functools.partial(_merge_kernel, len(xs)),
        grid=(TILES,),
        in_specs=_x_specs(len(xs) == 2) + [
            _mod_spec(l), _layer((1, D), l), _layer((D, 4 * D), l), _layer((4, BW, D), l), _layer((D, D), l),
            _row_spec(BW), _row_spec(BW), _row_spec(BW), _row_spec(BW)],
        out_specs=_row_spec(D),
        out_shape=jax.ShapeDtypeStruct((ROWS, D), F32),
        compiler_params=_params(1),
        name="merge",
    )(*xs, mods, w["g1"], w["win_g"], w["wbr"], w["wout"], pa, pb, pc, pd)


def _ffn_kernel(n_o, x_ref, mod_ref, g2_ref, wug_ref, wuv_ref, cvg_ref, cvv_ref, wd_ref, *refs):
    o_refs, acc_ref = refs[:n_o], refs[n_o]
    is_lat = pl.program_id(0) >= CTX_TILES
    mod = mod_ref[0]
    x = x_ref[...]
    hb = _norm_mod(x, g2_ref[...], mod[:, 4 * D:5 * D], mod[:, 3 * D:4 * D]).astype(BF16)
    tm1 = jnp.where(is_lat, LAT_T - 1, CTX_T - 1)
    pos = lax.broadcasted_iota(jnp.int32, (TILE, FFN_CHUNK), 0) & tm1
    first = pos == 0
    last = pos == tm1

    def conv(u, cv):
        prev = jnp.where(first, 0.0, pltpu.roll(u, 1, 0))
        nxt = jnp.where(last, 0.0, pltpu.roll(u, TILE - 1, 0))
        return prev * cv[0:1] + u * cv[1:2] + nxt * cv[2:3] + cv[3:4]

    acc_ref[...] = jnp.zeros((TILE, D), F32)

    def body(c, carry):
        g = conv(_dot(hb, wug_ref[c]), cvg_ref[c])
        v = conv(_dot(hb, wuv_ref[c]), cvv_ref[c])
        act = (g * _sigmoid(g) * v).astype(BF16)
        acc_ref[...] += _dot(act, wd_ref[c])
        return carry

    lax.fori_loop(0, FFN_CHUNKS, body, 0)
    y = x + mod[:, 5 * D:6 * D] * acc_ref[...]
    if n_o == 1:
        o_refs[0][...] = y
    else:
        @pl.when(jnp.logical_not(is_lat))
        def _():
            o_refs[0][...] = y

        @pl.when(is_lat)
        def _():
            o_refs[1][...] = y


def _ffn_call(l, x, mods, w, split_out):
    if split_out:
        out_specs = _x_specs(True)
        out_shape = [jax.ShapeDtypeStruct((CTX_ROWS, D), F32), jax.ShapeDtypeStruct((LAT_ROWS, D), F32)]
    else:
        out_specs = _x_specs(False)
        out_shape = [jax.ShapeDtypeStruct((ROWS, D), F32)]
    return pl.pallas_call(
        functools.partial(_ffn_kernel, len(out_shape)),
        grid=(TILES,),
        in_specs=[_row_spec(D), _mod_spec(l), _layer((1, D), l),
                  _layer((FFN_CHUNKS, D, FFN_CHUNK), l), _layer((FFN_CHUNKS, D, FFN_CHUNK), l),
                  _layer((FFN_CHUNKS, 8, FFN_CHUNK), l), _layer((FFN_CHUNKS, 8, FFN_CHUNK), l),
                  _layer((FFN_CHUNKS, FFN_CHUNK, D), l)],
        out_specs=out_specs,
        out_shape=out_shape,
        scratch_shapes=[pltpu.VMEM((TILE, D), F32)],
        compiler_params=_params(1),
        name="ffn",
    )(x, mods, w["g2"], w["wup_g"], w["wup_v"], w["cv_g"], w["cv_v"], w["wdown"])


def _rope_tables():
    half = 8
    inv = ROPE_BASE ** (-np.arange(half, dtype=np.float64) / half)
    t = np.arange(LAT_T)
    rows, cols = (t // GRID_W).astype(np.float64), (t % GRID_W).astype(np.float64)
    ang = np.concatenate([rows[:, None] * inv[None, :]] * 2 + [cols[:, None] * inv[None, :]] * 2, axis=1)
    cos, sin = np.cos(ang), np.sin(ang)
    first = (np.arange(32) % 16) < 8
    sa = np.where(first[None, :], -sin, 0.0)
    sb = np.where(first[None, :], 0.0, sin)
    one, zero = np.ones_like(cos), np.zeros_like(cos)

    def diff_layout(a):
        return np.tile(a, (1, LANES // 32))

    def mla_layout(a, fill):
        out = np.full((LAT_T, LANES), fill)
        out[:, MLA_NOPE:MLA_QK] = a
        return out

    ropd = np.stack([np.stack([diff_layout(one), diff_layout(zero), diff_layout(zero)]),
                     np.stack([diff_layout(cos), diff_layout(sa), diff_layout(sb)])])
    ropm = np.stack([np.stack([mla_layout(one, 1.0), mla_layout(zero, 0.0), mla_layout(zero, 0.0)]),
                     np.stack([mla_layout(cos, 1.0), mla_layout(sa, 0.0), mla_layout(sb, 0.0)])])
    return jnp.asarray(ropd, F32), jnp.asarray(ropm, F32)


def _dft_tables():
    def cs(n):
        k = np.arange(n)
        ang = 2.0 * np.pi * ((k[:, None] * k[None, :]) % n) / n
        return np.cos(ang), np.sin(ang)

    c64, s64 = cs(64)
    eye4 = np.eye(4)
    cl, sl = cs(LAT_T)
    cc_, sc_ = cs(CTX_T)
    ft = np.stack([np.stack([np.kron(eye4, cc_), -np.kron(eye4, sc_)]), np.stack([cl, -sl])])
    seg = np.kron(np.eye(LANES // DA_D), np.ones((DA_D, DA_D)))
    as_bf16 = lambda a: jnp.asarray(a, F32).astype(BF16)
    return as_bf16(np.kron(eye4, c64)), as_bf16(np.kron(eye4, s64)), as_bf16(ft), as_bf16(seg)


def _pad_last(a, lo, hi):
    return jnp.pad(a, ((0, 0),) * (a.ndim - 1) + ((lo, hi),))


def _prep_weights(p):
    w_in = p["w_in"]
    rest = w_in[:, :, 4 * D:]
    xa, qkv, xc, mla = rest[..., 0:256], rest[..., 256:1024], rest[..., 1024:1280], rest[..., 1280:]
    cq, ckv, kr = mla[..., :Q_LORA], mla[..., Q_LORA:Q_LORA + KV_LORA], mla[..., Q_LORA + KV_LORA:]
    win_r = jnp.concatenate([xa, qkv, xc, _pad_last(cq, 0, BW - Q_LORA), ckv,
                             _pad_last(kr, MLA_NOPE, LANES - MLA_QK)], axis=-1).astype(BF16)
    poolw = jnp.einsum("lgce,gh->lgche", p["pool_w"], jnp.eye(4, dtype=F32)).reshape(DEPTH, BW, BW).astype(BF16)
    wuq = p["w_uq"].reshape(DEPTH, Q_LORA, HEADS, MLA_QK)
    wuq = jnp.pad(wuq, ((0, 0), (0, BW - Q_LORA), (0, 0), (0, LANES - MLA_QK))).reshape(DEPTH, BW, HEADS * LANES)
    wukv = p["w_ukv"].reshape(DEPTH, KV_LORA, HEADS, MLA_NOPE + MLA_V)
    wuk = _pad_last(wukv[..., :MLA_NOPE], 0, LANES - MLA_NOPE).reshape(DEPTH, KV_LORA, HEADS * LANES)
    wuv = wukv[..., MLA_NOPE:].reshape(DEPTH, KV_LORA, HEADS * MLA_V)
    w_up = p["w_up"]
    cv = jnp.concatenate([p["conv_w"], p["conv_b"][:, None, :], jnp.zeros((DEPTH, 4, 2 * FFN), F32)], axis=1)

    def chunks(a):
        return a.reshape(DEPTH, a.shape[1], FFN_CHUNKS, FFN_CHUNK).transpose(0, 2, 1, 3)

    row = lambda a, width: _pad_last(a, 0, width - a.shape[-1])[:, None, :]
    tiled = lambda a, reps: jnp.tile(a, (1, reps))[:, None, :]
    return {
        "g1": row(p["g_norm1"], D), "g2": row(p["g_norm2"], D),
        "win_g": w_in[:, :, :4 * D].astype(BF16), "win_r": win_r,
        "poolw": poolw, "pools": row(p["pool_scale"], BW),
        "dgq": tiled(p["diff_qnorm"], LANES // DA_D), "dgk": tiled(p["diff_knorm"], LANES // DA_D),
        "dl": p["diff_lambda"], "gsub": tiled(p["diff_subln"], HEADS),
        "gqa": row(p["mla_qa_norm"], BW), "wuq": wuq.astype(BF16), "gqn": row(p["mla_qnorm"], LANES),
        "gkv": row(p["mla_kv_norm"], KV_LORA), "wuk": wuk.astype(BF16), "wuv": wuv.astype(BF16),
        "gkn": row(p["mla_knorm"], LANES),
        "wbr": p["w_branch"].astype(BF16), "wout": p["w_out"].astype(BF16),
        "wup_g": chunks(w_up[..., :FFN]).astype(BF16), "wup_v": chunks(w_up[..., FFN:]).astype(BF16),
        "cv_g": chunks(cv[..., :FFN]), "cv_v": chunks(cv[..., FFN:]),
        "wdown": p["w_down"].reshape(DEPTH, FFN_CHUNKS, FFN_CHUNK, D).astype(BF16),
    }


def kernel(x_prompt, x_sample, c, c_ctx, cache_diff_k, cache_diff_v, cache_mla_ckv, cache_mla_krope, w_ada, b_ada, g_norm1, g_norm2, w_in, pool_w, pool_scale, diff_qnorm, diff_knorm, diff_lambda, diff_subln, mla_qa_norm, w_uq, mla_kv_norm, w_ukv, mla_qnorm, mla_knorm, w_branch, w_out, w_up, conv_w, conv_b, w_down):
    w = _prep_weights(dict(
        w_in=w_in, pool_w=pool_w, pool_scale=pool_scale, diff_qnorm=diff_qnorm, diff_knorm=diff_knorm,
        diff_lambda=diff_lambda, diff_subln=diff_subln, mla_qa_norm=mla_qa_norm, w_uq=w_uq,
        mla_kv_norm=mla_kv_norm, w_ukv=w_ukv, mla_qnorm=mla_qnorm, mla_knorm=mla_knorm, w_branch=w_branch,
        w_out=w_out, w_up=w_up, conv_w=conv_w, conv_b=conv_b, w_down=w_down, g_norm1=g_norm1,
        g_norm2=g_norm2))
    ropd, ropm = _rope_tables()
    cc, ss, ft, seg = _dft_tables()
    consts = dict(ropd=ropd, ropm=ropm, cc=cc, ss=ss, ft=ft, seg=seg)

    cond = jnp.concatenate([c_ctx[None, :], c, jnp.zeros((16 - 1 - LAT_B, D), F32)], axis=0)
    mods = _mod_call(cond, w_ada, b_ada).reshape(DEPTH, 16, 1, 6 * D)

    kc, vc = _ctxkv_call(cache_mla_ckv, _pad_last(cache_mla_krope, MLA_NOPE, LANES - MLA_QK), w)
    cdk = cache_diff_k.reshape(LAT_B, DEPTH, PAST, BW)
    cdv = cache_diff_v.reshape(LAT_B, DEPTH, PAST, BW)

    xs = [x_prompt.reshape(CTX_ROWS, D), x_sample.reshape(LAT_ROWS, D)]
    new = [[], [], [], []]
    for l in range(DEPTH):
        pa, qd, kd, vd, oc, qm, km, vm, *ctx_outs = _inproj_call(l, xs, mods, w, consts)
        ob, od = _attn_call(l, qd, kd, vd, cdk, cdv, qm, km, vm, kc, vc, w)
        x = _merge_call(l, xs, mods, w, pa, ob, oc, od)
        xs = _ffn_call(l, x, mods, w, split_out=(l == DEPTH - 1))
        for acc, a in zip(new, ctx_outs):
            acc.append(a)

    stack = lambda arrs, shape: jnp.stack(arrs, axis=1).reshape((CTX_B, DEPTH, CTX_T) + shape)
    return (xs[0].reshape(CTX_B, CTX_T, D), xs[1].reshape(LAT_B, LAT_T, D),
            stack([a.reshape(CTX_B, CTX_T, BW) for a in new[0]], (HEADS, DA_HD)),
            stack([a.reshape(CTX_B, CTX_T, BW) for a in new[1]], (HEADS, DA_HD)),
            stack([a.reshape(CTX_B, CTX_T, KV_LORA) for a in new[2]], (KV_LORA,)),
            stack([a.reshape(CTX_B, CTX_T, MLA_ROPE) for a in new[3]], (MLA_ROPE,)))
```

```python
import functools
import math

import jax
import jax.numpy as jnp
import numpy as np
from jax import lax
from jax.experimental import pallas as pl
from jax.experimental.pallas import tpu as pltpu

F32 = jnp.float32
BF16 = jnp.bfloat16

D = 1024
DEPTH = 4
CTX_B, CTX_T = 16, 256
LAT_B, LAT_T = 8, 1024
PAST = 512
TILE = 1024
CTX_TILES = CTX_B * CTX_T // TILE
LAT_TILES = LAT_B * LAT_T // TILE
TILES = CTX_TILES + LAT_TILES
ROWS = TILES * TILE
CTX_ROWS = CTX_TILES * TILE
LAT_ROWS = LAT_TILES * TILE
GRID_W = 64
BW = 256
DA_D = 32
DA_HD = 64
HEADS = 4
MLA_NOPE, MLA_ROPE, MLA_V, MLA_QK = 64, 32, 64, 96
Q_LORA, KV_LORA = 192, 128
FFN = 2816
FFN_CHUNK = 256
FFN_CHUNKS = FFN // FFN_CHUNK
POOL_HALF = (1, 2, 4, 8)
POOL_PAD = 16
ROPE_BASE = 10000.0
EPS = 1e-6
LOG2E = 1.4426950408889634
TQ = 256
LANES = 128
VMEM_LIMIT = 56 * 1024 * 1024

C_XA, C_Q, C_K, C_V, C_XC, C_CQ, C_CKV, C_KR, C_END = 0, 256, 512, 768, 1024, 1280, 1536, 1664, 1792


def _lambda_init(l):
    return 0.8 - 0.6 * math.exp(-0.3 * l)


def _sigmoid(x):
    return 0.5 * jnp.tanh(0.5 * x) + 0.5


def _rms(x, n):
    ms = jnp.sum(x * x, axis=-1, keepdims=True) * (1.0 / n)
    return x * lax.rsqrt(ms + EPS)


def _dot(a, b):
    return jnp.dot(a, b, preferred_element_type=F32)


def _dot_nt(a, b):
    return lax.dot_general(a, b, (((1,), (1,)), ((), ())), preferred_element_type=F32)


def _const(shape):
    n = len(shape)
    return pl.BlockSpec(shape, lambda *_: (0,) * n, pipeline_mode=pl.Buffered(1))


def _layer(shape, l):
    n = len(shape)
    return pl.BlockSpec((None,) + shape, lambda *_: (l,) + (0,) * n, pipeline_mode=pl.Buffered(1))


def _params(n_grid):
    return pltpu.CompilerParams(dimension_semantics=("arbitrary",) * n_grid, vmem_limit_bytes=VMEM_LIMIT)


def _row_spec(width):
    return pl.BlockSpec((TILE, width), lambda i: (i, 0))


def _ctx_row_spec(width):
    return pl.BlockSpec((TILE, width), lambda i: (jnp.minimum(i, CTX_TILES - 1), 0))


def _lat_row_spec(width):
    return pl.BlockSpec((TILE, width), lambda i: (jnp.maximum(i - CTX_TILES, 0), 0))


def _x_specs(split):
    return [_ctx_row_spec(D), _lat_row_spec(D)] if split else [_row_spec(D)]


def _read_x(x_refs, is_lat):
    if len(x_refs) == 2:
        return jnp.where(is_lat, x_refs[1][...], x_refs[0][...])
    return x_refs[0][...]


def _mod_spec(l):
    return pl.BlockSpec((None, 1, 1, 6 * D), lambda i: (l, jnp.maximum(i - (CTX_TILES - 1), 0), 0, 0))


def _group_spec(shape):
    n = len(shape)
    return pl.BlockSpec((None,) + shape, lambda i: (jnp.minimum(i // CTX_TILES, 1),) + (0,) * n,
                        pipeline_mode=pl.Buffered(1))


def _mod_kernel(c_ref, w_ref, b_ref, o_ref):
    c = c_ref[...]
    s = (c * _sigmoid(c)).astype(BF16)
    o_ref[0] = _dot(s, w_ref[0].astype(BF16)) + b_ref[0]


def _mod_call(cond, w_ada, b_ada):
    nc = 4
    cw = 6 * D // nc
    return pl.pallas_call(
        _mod_kernel,
        grid=(DEPTH, nc),
        in_specs=[
            pl.BlockSpec((16, D), lambda l, j: (0, 0)),
            pl.BlockSpec((1, D, cw), lambda l, j: (l, 0, j)),
            pl.BlockSpec((1, 1, cw), lambda l, j: (l, 0, j)),
        ],
        out_specs=pl.BlockSpec((1, 16, cw), lambda l, j: (l, 0, j)),
        out_shape=jax.ShapeDtypeStruct((DEPTH, 16, 6 * D), F32),
        compiler_params=_params(2),
        name="adaln_mod",
    )(cond, w_ada, b_ada.reshape(DEPTH, 1, 6 * D))


def _mla_key_heads(knope, kr_pad, gkn):
    out = []
    for h in range(HEADS):
        kh = knope[:, h * LANES:(h + 1) * LANES] + kr_pad
        out.append(_rms(kh, MLA_QK) * gkn)
    return out


def _ctxkv_kernel(ckv_ref, kr_ref, wuk_ref, wuv_ref, gkn_ref, kc_ref, vc_ref):
    cb = ckv_ref[...].astype(BF16)
    knope = _dot(cb, wuk_ref[...])
    heads = _mla_key_heads(knope, kr_ref[...], gkn_ref[...])
    for h in range(HEADS):
        kc_ref[:, h * LANES:(h + 1) * LANES] = heads[h].astype(BF16)
    vc_ref[...] = _dot(cb, wuv_ref[...]).astype(BF16)


def _ctxkv_call(ckv, kr_pad, w):
    cache = lambda width: pl.BlockSpec((None, None, PAST, width), lambda b, l: (b, l, 0, 0))
    wspec = lambda r, c: pl.BlockSpec((None, r, c), lambda b, l: (l, 0, 0))
    return pl.pallas_call(
        _ctxkv_kernel,
        grid=(LAT_B, DEPTH),
        in_specs=[cache(KV_LORA), cache(LANES), wspec(KV_LORA, 512), wspec(KV_LORA, BW), wspec(1, LANES)],
        out_specs=[cache(512), cache(BW)],
        out_shape=[
            jax.ShapeDtypeStruct((LAT_B, DEPTH, PAST, 512), BF16),
            jax.ShapeDtypeStruct((LAT_B, DEPTH, PAST, BW), BF16),
        ],
        compiler_params=_params(2),
        name="mla_ctx_keys",
    )(ckv, kr_pad, w["wuk"], w["wuv"], w["gkn"])


def _norm_mod(x, g, sc, sh):
    return (_rms(x, D) * g) * (1.0 + sc) + sh


def _rope(x, tab_ref):
    return (x * tab_ref[0] + pltpu.roll(x, LANES - 8, 1) * tab_ref[1]
            + pltpu.roll(x, 8, 1) * tab_ref[2])


def _seg_rms(x, seg, gain):
    sq = x * x
    hi = sq.astype(BF16)
    lo = (sq - hi.astype(F32)).astype(BF16)
    ms = (_dot(hi, seg) + _dot(lo, seg)) * (1.0 / DA_D)
    return x * lax.rsqrt(ms + EPS) * gain


def _inproj_kernel(n_x, *refs):
    x_refs = refs[:n_x]
    (mod_ref, g1_ref, win_ref, poolw_ref, pools_ref, seg_ref, dgq_ref, dgk_ref,
     ropd_ref, ropm_ref, cc_ref, ss_ref, ft_ref, gqa_ref, wuq_ref, gqn_ref, gkv_ref,
     wuk_ref, wuv_ref, gkn_ref,
     pa_ref, qd_ref, kd_ref, vd_ref, oc_ref, qm_ref, km_ref, vm_ref,
     kcf_ref, vf_ref, ckv_ref, kr_ref, pad_ref) = refs[n_x:]
    is_lat = pl.program_id(0) >= CTX_TILES
    mod = mod_ref[0]
    hb = _norm_mod(_read_x(x_refs, is_lat), g1_ref[...], mod[:, D:2 * D], mod[:, 0:D]).astype(BF16)

    xa = _dot(hb, win_ref[:, C_XA:C_XA + BW])
    pad_ref[0:POOL_PAD, :] = jnp.zeros((POOL_PAD, BW), F32)
    pad_ref[POOL_PAD + TILE:, :] = jnp.zeros((POOL_PAD, BW), F32)
    pad_ref[POOL_PAD:POOL_PAD + TILE, :] = xa
    tm1 = jnp.where(is_lat, LAT_T - 1, CTX_T - 1)
    pos = lax.broadcasted_iota(jnp.int32, (TILE, BW), 0) & tm1
    lane = lax.broadcasted_iota(jnp.int32, (TILE, BW), 1)

    def term(s):
        v = pad_ref[POOL_PAD + s:POOL_PAD + s + TILE, :]
        if s < 0:
            return jnp.where(pos >= -s, v, 0.0)
        return jnp.where(pos + s <= tm1, v, 0.0)

    acc = xa
    wins, cnts = [], []
    lo_s, hi_s = 0, 0
    for half in POOL_HALF:
        for s in list(range(-half, lo_s)) + list(range(hi_s + 1, half)):
            acc = acc + term(s)
        lo_s, hi_s = -half, half - 1
        wins.append(acc)
        cnts.append((jnp.minimum(pos + half, tm1 + 1) - jnp.maximum(pos - half, 0)).astype(F32))
    win = jnp.where(lane < 64, wins[0], jnp.where(lane < 128, wins[1], jnp.where(lane < 192, wins[2], wins[3])))
    cnt = jnp.where(lane < 64, cnts[0], jnp.where(lane < 128, cnts[1], jnp.where(lane < 192, cnts[2], cnts[3])))
    pooled = (win / cnt - xa).astype(BF16)
    pa_ref[...] = (_dot(pooled, poolw_ref[...]) * pools_ref[...]).astype(BF16)

    seg = seg_ref[...]
    qscale = DA_D ** -0.5 * LOG2E
    q = _dot(hb, win_ref[:, C_Q:C_Q + BW])
    k = _dot(hb, win_ref[:, C_K:C_K + BW])
    v = _dot(hb, win_ref[:, C_V:C_V + BW])
    kns = []
    for s in range(BW // LANES):
        sl = slice(s * LANES, (s + 1) * LANES)
        qn = _seg_rms(q[:, sl], seg, dgq_ref[...])
        kn = _seg_rms(k[:, sl], seg, dgk_ref[...])
        kns.append(kn)
        qd_ref[:, sl] = (_rope(qn, ropd_ref) * qscale).astype(BF16)
        kd_ref[:, sl] = _rope(kn, ropd_ref).astype(BF16)
    vd_ref[...] = v.astype(BF16)

    xcb = _dot(hb, win_ref[:, C_XC:C_XC + BW]).astype(BF16)
    xcc = _dot(xcb, cc_ref[...]).astype(BF16)
    xcs = _dot(xcb, ss_ref[...]).astype(BF16)
    y = _dot(ft_ref[0], xcc) + _dot(ft_ref[1], xcs)
    fnorm = jnp.where(is_lat, (LAT_T * 64.0) ** -0.5, (CTX_T * 64.0) ** -0.5)
    oc_ref[...] = (y * fnorm).astype(BF16)

    cq = _dot(hb, win_ref[:, C_CQ:C_CQ + BW])
    qa = (_rms(cq, Q_LORA) * gqa_ref[...]).astype(BF16)
    qf = _dot(qa, wuq_ref[...])
    mscale = MLA_QK ** -0.5 * LOG2E
    for h in range(HEADS):
        sl = slice(h * LANES, (h + 1) * LANES)
        qh = _rms(qf[:, sl], MLA_QK) * gqn_ref[...]
        qm_ref[:, sl] = (_rope(qh, ropm_ref) * mscale).astype(BF16)
    ckv = _dot(hb, win_ref[:, C_CKV:C_CKV + KV_LORA])
    ckv_n = _rms(ckv, KV_LORA) * gkv_ref[...]
    cb = ckv_n.astype(BF16)
    kr_pad = _dot(hb, win_ref[:, C_KR:C_KR + LANES])
    heads = _mla_key_heads(_dot(cb, wuk_ref[...]), kr_pad, gkn_ref[...])
    for h in range(HEADS):
        km_ref[:, h * LANES:(h + 1) * LANES] = _rope(heads[h], ropm_ref).astype(BF16)
    vm_ref[...] = _dot(cb, wuv_ref[...]).astype(BF16)

    @pl.when(jnp.logical_not(is_lat))
    def _():
        for s in range(BW // LANES):
            kcf_ref[:, s * LANES:(s + 1) * LANES] = kns[s]
        vf_ref[...] = v
        ckv_ref[...] = ckv_n
        kr_ref[...] = kr_pad[:, MLA_NOPE:MLA_NOPE + MLA_ROPE]


def _inproj_call(l, xs, mods, w, consts):
    row_outs = [(BW, BF16), (BW, BF16), (BW, BF16), (BW, BF16), (BW, BF16), (512, BF16), (512, BF16), (BW, BF16)]
    ctx_outs = [(BW, F32), (BW, F32), (KV_LORA, F32), (MLA_ROPE, F32)]
    return pl.pallas_call(
        functools.partial(_inproj_kernel, len(xs)),
        grid=(TILES,),
        in_specs=_x_specs(len(xs) == 2) + [
            _mod_spec(l), _layer((1, D), l), _layer((D, C_END), l), _layer((BW, BW), l), _layer((1, BW), l),
            _const((LANES, LANES)), _layer((1, LANES), l), _layer((1, LANES), l),
            _group_spec((3, TILE, LANES)), _group_spec((3, TILE, LANES)),
            _const((BW, BW)), _const((BW, BW)), _group_spec((2, TILE, TILE)),
            _layer((1, BW), l), _layer((BW, 512), l), _layer((1, LANES), l), _layer((1, KV_LORA), l),
            _layer((KV_LORA, 512), l), _layer((KV_LORA, BW), l), _layer((1, LANES), l),
        ],
        out_specs=[_row_spec(wd) for wd, _ in row_outs] + [_ctx_row_spec(wd) for wd, _ in ctx_outs],
        out_shape=([jax.ShapeDtypeStruct((ROWS, wd), dt) for wd, dt in row_outs]
                   + [jax.ShapeDtypeStruct((CTX_ROWS, wd), dt) for wd, dt in ctx_outs]),
        scratch_shapes=[pltpu.VMEM((TILE + 2 * POOL_PAD, BW), F32)],
        compiler_params=_params(1),
        name="inproj",
    )(*xs, mods, w["g1"], w["win_r"], w["poolw"], w["pools"], consts["seg"], w["dgq"], w["dgk"],
      consts["ropd"], consts["ropm"], consts["cc"], consts["ss"], consts["ft"],
      w["gqa"], w["wuq"], w["gqn"], w["gkv"], w["wuk"], w["wuv"], w["gkn"])


def _softmax_parts(q, ks):
    ss = [_dot_nt(q, k) for k in ks]
    m = ss[0].max(axis=-1, keepdims=True)
    for s in ss[1:]:
        m = jnp.maximum(m, s.max(axis=-1, keepdims=True))
    ps = [jnp.exp2(s - m) for s in ss]
    l = ps[0].sum(axis=-1, keepdims=True)
    for p in ps[1:]:
        l = l + p.sum(axis=-1, keepdims=True)
    return [p.astype(BF16) for p in ps], 1.0 / l


def _attend(qd, kds, vds, qm, kms, vms, lam, gsub, lam_init, ob_ref, od_ref):
    n = qd.shape[0]
    lane = lax.broadcasted_iota(jnp.int32, (n, BW), 1)
    zero = jnp.zeros((), BF16)
    out_b = jnp.zeros((n, BW), F32)
    out_d = jnp.zeros((n, BW), F32)
    for h in range(HEADS):
        head = (lane >= h * DA_HD) & (lane < (h + 1) * DA_HD)
        o = None
        for half in range(2):
            lo = h * DA_HD + half * DA_D
            qc = jnp.where((lane >= lo) & (lane < lo + DA_D), qd, zero)
            ps, rl = _softmax_parts(qc, kds)
            pv = sum(_dot(p, v) for p, v in zip(ps, vds)) * rl
            o = pv if half == 0 else o - lam * pv
        o = jnp.where(head, o, 0.0)
        ms = jnp.sum(o * o, axis=-1, keepdims=True) * (1.0 / DA_HD)
        out_b = out_b + o * lax.rsqrt(ms + EPS)
        sl = slice(h * LANES, (h + 1) * LANES)
        ps, rl = _softmax_parts(qm[:, sl], [k[:, sl] for k in kms])
        pv = sum(_dot(p, v) for p, v in zip(ps, vms)) * rl
        out_d = out_d + jnp.where(head, pv, 0.0)
    ob_ref[...] = (out_b * gsub * (1.0 - lam_init)).astype(BF16)
    od_ref[...] = out_d.astype(BF16)


def _attn_kernel(lam_init, qd_ref, kd_ref, vd_ref, cdk_ref, cdv_ref, qm_ref, km_ref, vm_ref, kc_ref, vc_ref,
                 dl_ref, gsub_ref, ob_ref, od_ref):
    i = pl.program_id(0)
    j = pl.program_id(1)
    dl = dl_ref[...]
    lam = (jnp.exp(jnp.sum(dl[0:1] * dl[1:2], axis=-1, keepdims=True))
           - jnp.exp(jnp.sum(dl[2:3] * dl[3:4], axis=-1, keepdims=True)) + lam_init)
    rows = pl.ds(pl.multiple_of(j * TQ, TQ), TQ)
    qd = qd_ref[rows, :]
    qm = qm_ref[rows, :]

    @pl.when(i < CTX_TILES)
    def _():
        _attend(qd, [kd_ref[rows, :]], [vd_ref[rows, :]], qm, [km_ref[rows, :]], [vm_ref[rows, :]],
                lam, gsub_ref[...], lam_init, ob_ref, od_ref)

    @pl.when(i >= CTX_TILES)
    def _():
        _attend(qd, [cdk_ref[...].astype(BF16), kd_ref[...]], [cdv_ref[...].astype(BF16), vd_ref[...]],
                qm, [kc_ref[...], km_ref[...]], [vc_ref[...], vm_ref[...]],
                lam, gsub_ref[...], lam_init, ob_ref, od_ref)


def _attn_call(l, qd, kd, vd, cdk, cdv, qm, km, vm, kc, vc, w):
    tile = lambda width: pl.BlockSpec((TILE, width), lambda i, j: (i, 0))
    cache = lambda width: pl.BlockSpec((None, None, PAST, width),
                                       lambda i, j: (jnp.maximum(i - CTX_TILES, 0), l, 0, 0))
    blk = pl.BlockSpec((TQ, BW), lambda i, j: (i * (TILE // TQ) + j, 0))
    return pl.pallas_call(
        functools.partial(_attn_kernel, _lambda_init(l)),
        grid=(TILES, TILE // TQ),
        in_specs=[tile(BW), tile(BW), tile(BW), cache(BW), cache(BW), tile(512), tile(512), tile(BW),
                  cache(512), cache(BW), _layer((4, DA_D), l), _layer((1, BW), l)],
        out_specs=[blk, blk],
        out_shape=[jax.ShapeDtypeStruct((ROWS, BW), BF16)] * 2,
        compiler_params=_params(2),
        name="attention",
    )(qd, kd, vd, cdk, cdv, qm, km, vm, kc, vc, w["dl"], w["gsub"])


def _merge_kernel(n_x, *refs):
    x_refs = refs[:n_x]
    mod_ref, g1_ref, wg_ref, wb_ref, wo_ref, pa_ref, pb_ref, pc_ref, pd_ref, o_ref = refs[n_x:]
    mod = mod_ref[0]
    x = _read_x(x_refs, pl.program_id(0) >= CTX_TILES)
    hb = _norm_mod(x, g1_ref[...], mod[:, D:2 * D], mod[:, 0:D]).astype(BF16)
    merged = None
    for n, br_ref in enumerate((pa_ref, pb_ref, pc_ref, pd_ref)):
        gate = _sigmoid(_dot(hb, wg_ref[:, n * D:(n + 1) * D]))
        t = gate * _dot(br_ref[...], wb_ref[n])
        merged = t if merged is None else merged + t
    o_ref[...] = x + mod[:, 2 * D:3 * D] * _dot(merged.astype(BF16), wo_ref[...])


def _merge_call(l, xs, mods, w, pa, pb, pc, pd):
    return pl.pallas_call(
        functools.partial(_merge_kernel, len(xs)),
        grid=(TILES,),
        in_specs=_x_specs(len(xs) == 2) + [
            _mod_spec(l), _layer((1, D), l), _layer((D, 4 * D), l), _layer((4, BW, D), l), _layer((D, D), l),
            _row_spec(BW), _row_spec(BW), _row_spec(BW), _row_spec(BW)],
        out_specs=_row_spec(D),
        out_shape=jax.ShapeDtypeStruct((ROWS, D), F32),
        compiler_params=_params(1),
        name="merge",
    )(*xs, mods, w["g1"], w["win_g"], w["wbr"], w["wout"], pa, pb, pc, pd)


def _ffn_kernel(n_o, x_ref, mod_ref, g2_ref, wu_ref, cv_ref, wd_ref, *refs):
    o_refs = refs[:n_o]
    hb_ref, u_ref, act_ref, acc_ref = refs[n_o:]
    is_lat = pl.program_id(0) >= CTX_TILES
    is_ctx = jnp.logical_not(is_lat)
    mod = mod_ref[0]
    hb_ref[...] = _norm_mod(x_ref[...], g2_ref[...], mod[:, 4 * D:5 * D], mod[:, 3 * D:4 * D]).astype(BF16)
    tm1 = jnp.where(is_lat, LAT_T - 1, CTX_T - 1)
    pos = lax.broadcasted_iota(jnp.int32, (TILE, 2 * FFN_CHUNK), 0) & tm1
    seq_start = pos == 0
    seq_end = pos == tm1

    def shifted(u):
        return (jnp.where(seq_start, 0.0, pltpu.roll(u, 1, 0)),
                jnp.where(seq_end, 0.0, pltpu.roll(u, TILE - 1, 0)))

    def up(c, slot):
        u_ref[slot] = _dot(hb_ref[...], wu_ref[c])

    def act(c, slot):
        u = u_ref[slot]
        cv = cv_ref[c]
        prev, nxt = shifted(u)
        y = prev * cv[0:1] + u * cv[1:2] + nxt * cv[2:3] + cv[3:4]
        h, v = y[:, :FFN_CHUNK], y[:, FFN_CHUNK:]
        act_ref[slot] = (h * (1.0 + jnp.tanh(h)) * v).astype(BF16)

    def down(c, slot):
        acc_ref[...] += _dot(act_ref[slot], wd_ref[c])

    def stages(c, slot):
        up(c + 1, 1 - slot)
        act(c, slot)
        down(c - 1, 1 - slot)

    acc_ref[...] = jnp.zeros((TILE, D), F32)

    def body(c, carry):
        up(c, 0)
        act(c, 0)
        down(c, 0)
        return carry

    lax.fori_loop(0, FFN_CHUNKS, body, 0)
    y = x_ref[...] + mod[:, 5 * D:6 * D] * acc_ref[...]
    if n_o == 1:
        o_refs[0][...] = y
    else:
        @pl.when(is_ctx)
        def _():
            o_refs[0][...] = y

        @pl.when(is_lat)
        def _():
            o_refs[1][...] = y


def _ffn_call(l, x, mods, w, split_out):
    if split_out:
        out_specs = _x_specs(True)
        out_shape = [jax.ShapeDtypeStruct((CTX_ROWS, D), F32), jax.ShapeDtypeStruct((LAT_ROWS, D), F32)]
    else:
        out_specs = _x_specs(False)
        out_shape = [jax.ShapeDtypeStruct((ROWS, D), F32)]
    return pl.pallas_call(
        functools.partial(_ffn_kernel, len(out_shape)),
        grid=(TILES,),
        in_specs=[_row_spec(D), _mod_spec(l), _layer((1, D), l),
                  _layer((FFN_CHUNKS, D, 2 * FFN_CHUNK), l), _layer((FFN_CHUNKS, 8, 2 * FFN_CHUNK), l),
                  _layer((FFN_CHUNKS, FFN_CHUNK, D), l)],
        out_specs=out_specs,
        out_shape=out_shape,
        scratch_shapes=[pltpu.VMEM((TILE, D), BF16), pltpu.VMEM((2, TILE, 2 * FFN_CHUNK), F32),
                        pltpu.VMEM((2, TILE, FFN_CHUNK), BF16), pltpu.VMEM((TILE, D), F32)],
        compiler_params=_params(1),
        name="ffn",
    )(x, mods, w["g2"], w["wup"], w["cv"], w["wdown"])


def _rope_tables():
    half = 8
    inv = ROPE_BASE ** (-np.arange(half, dtype=np.float64) / half)
    t = np.arange(LAT_T)
    rows, cols = (t // GRID_W).astype(np.float64), (t % GRID_W).astype(np.float64)
    ang = np.concatenate([rows[:, None] * inv[None, :]] * 2 + [cols[:, None] * inv[None, :]] * 2, axis=1)
    cos, sin = np.cos(ang), np.sin(ang)
    first = (np.arange(32) % 16) < 8
    sa = np.where(first[None, :], -sin, 0.0)
    sb = np.where(first[None, :], 0.0, sin)
    one, zero = np.ones_like(cos), np.zeros_like(cos)

    def diff_layout(a):
        return np.tile(a, (1, LANES // 32))

    def mla_layout(a, fill):
        out = np.full((LAT_T, LANES), fill)
        out[:, MLA_NOPE:MLA_QK] = a
        return out

    ropd = np.stack([np.stack([diff_layout(one), diff_layout(zero), diff_layout(zero)]),
                     np.stack([diff_layout(cos), diff_layout(sa), diff_layout(sb)])])
    ropm = np.stack([np.stack([mla_layout(one, 1.0), mla_layout(zero, 0.0), mla_layout(zero, 0.0)]),
                     np.stack([mla_layout(cos, 1.0), mla_layout(sa, 0.0), mla_layout(sb, 0.0)])])
    return jnp.asarray(ropd, F32), jnp.asarray(ropm, F32)


def _dft_tables():
    def cs(n):
        k = np.arange(n)
        ang = 2.0 * np.pi * ((k[:, None] * k[None, :]) % n) / n
        return np.cos(ang), np.sin(ang)

    c64, s64 = cs(64)
    eye4 = np.eye(4)
    cl, sl = cs(LAT_T)
    cc_, sc_ = cs(CTX_T)
    ft = np.stack([np.stack([np.kron(eye4, cc_), -np.kron(eye4, sc_)]), np.stack([cl, -sl])])
    seg = np.kron(np.eye(LANES // DA_D), np.ones((DA_D, DA_D)))
    as_bf16 = lambda a: jnp.asarray(a, F32).astype(BF16)
    return as_bf16(np.kron(eye4, c64)), as_bf16(np.kron(eye4, s64)), as_bf16(ft), as_bf16(seg)


def _pad_last(a, lo, hi):
    return jnp.pad(a, ((0, 0),) * (a.ndim - 1) + ((lo, hi),))


def _prep_weights(p):
    w_in = p["w_in"]
    rest = w_in[:, :, 4 * D:]
    xa, qkv, xc, mla = rest[..., 0:256], rest[..., 256:1024], rest[..., 1024:1280], rest[..., 1280:]
    cq, ckv, kr = mla[..., :Q_LORA], mla[..., Q_LORA:Q_LORA + KV_LORA], mla[..., Q_LORA + KV_LORA:]
    win_r = jnp.concatenate([xa, qkv, xc, _pad_last(cq, 0, BW - Q_LORA), ckv,
                             _pad_last(kr, MLA_NOPE, LANES - MLA_QK)], axis=-1).astype(BF16)
    poolw = jnp.einsum("lgce,gh->lgche", p["pool_w"], jnp.eye(4, dtype=F32)).reshape(DEPTH, BW, BW).astype(BF16)
    wuq = p["w_uq"].reshape(DEPTH, Q_LORA, HEADS, MLA_QK)
    wuq = jnp.pad(wuq, ((0, 0), (0, BW - Q_LORA), (0, 0), (0, LANES - MLA_QK))).reshape(DEPTH, BW, HEADS * LANES)
    wukv = p["w_ukv"].reshape(DEPTH, KV_LORA, HEADS, MLA_NOPE + MLA_V)
    wuk = _pad_last(wukv[..., :MLA_NOPE], 0, LANES - MLA_NOPE).reshape(DEPTH, KV_LORA, HEADS * LANES)
    wuv = wukv[..., MLA_NOPE:].reshape(DEPTH, KV_LORA, HEADS * MLA_V)
    w_up = p["w_up"]
    cv = jnp.concatenate([p["conv_w"], p["conv_b"][:, None, :], jnp.zeros((DEPTH, 4, 2 * FFN), F32)], axis=1)

    def chunks(a):
        return a.reshape(DEPTH, a.shape[1], FFN_CHUNKS, FFN_CHUNK).transpose(0, 2, 1, 3)

    row = lambda a, width: _pad_last(a, 0, width - a.shape[-1])[:, None, :]
    tiled = lambda a, reps: jnp.tile(a, (1, reps))[:, None, :]
    return {
        "g1": row(p["g_norm1"], D), "g2": row(p["g_norm2"], D),
        "win_g": w_in[:, :, :4 * D].astype(BF16), "win_r": win_r,
        "poolw": poolw, "pools": row(p["pool_scale"], BW),
        "dgq": tiled(p["diff_qnorm"], LANES // DA_D), "dgk": tiled(p["diff_knorm"], LANES // DA_D),
        "dl": p["diff_lambda"], "gsub": tiled(p["diff_subln"], HEADS),
        "gqa": row(p["mla_qa_norm"], BW), "wuq": wuq.astype(BF16), "gqn": row(p["mla_qnorm"], LANES),
        "gkv": row(p["mla_kv_norm"], KV_LORA), "wuk": wuk.astype(BF16), "wuv": wuv.astype(BF16),
        "gkn": row(p["mla_knorm"], LANES),
        "wbr": p["w_branch"].astype(BF16), "wout": p["w_out"].astype(BF16),
        "wup": jnp.concatenate([chunks(w_up[..., :FFN]), chunks(w_up[..., FFN:])], axis=-1).astype(BF16),
        "cv": jnp.concatenate([0.5 * chunks(cv[..., :FFN]), chunks(cv[..., FFN:])], axis=-1),
        "wdown": p["w_down"].reshape(DEPTH, FFN_CHUNKS, FFN_CHUNK, D).astype(BF16),
    }


def kernel(x_prompt, x_sample, c, c_ctx, cache_diff_k, cache_diff_v, cache_mla_ckv, cache_mla_krope, w_ada, b_ada, g_norm1, g_norm2, w_in, pool_w, pool_scale, diff_qnorm, diff_knorm, diff_lambda, diff_subln, mla_qa_norm, w_uq, mla_kv_norm, w_ukv, mla_qnorm, mla_knorm, w_branch, w_out, w_up, conv_w, conv_b, w_down):
    w = _prep_weights(dict(
        w_in=w_in, pool_w=pool_w, pool_scale=pool_scale, diff_qnorm=diff_qnorm, diff_knorm=diff_knorm,
        diff_lambda=diff_lambda, diff_subln=diff_subln, mla_qa_norm=mla_qa_norm, w_uq=w_uq,
        mla_kv_norm=mla_kv_norm, w_ukv=w_ukv, mla_qnorm=mla_qnorm, mla_knorm=mla_knorm, w_branch=w_branch,
        w_out=w_out, w_up=w_up, conv_w=conv_w, conv_b=conv_b, w_down=w_down, g_norm1=g_norm1,
        g_norm2=g_norm2))
    ropd, ropm = _rope_tables()
    cc, ss, ft, seg = _dft_tables()
    consts = dict(ropd=ropd, ropm=ropm, cc=cc, ss=ss, ft=ft, seg=seg)

    cond = jnp.concatenate([c_ctx[None, :], c, jnp.zeros((16 - 1 - LAT_B, D), F32)], axis=0)
    mods = _mod_call(cond, w_ada, b_ada).reshape(DEPTH, 16, 1, 6 * D)

    kc, vc = _ctxkv_call(cache_mla_ckv, _pad_last(cache_mla_krope, MLA_NOPE, LANES - MLA_QK), w)
    cdk = cache_diff_k.reshape(LAT_B, DEPTH, PAST, BW)
    cdv = cache_diff_v.reshape(LAT_B, DEPTH, PAST, BW)

    xs = [x_prompt.reshape(CTX_ROWS, D), x_sample.reshape(LAT_ROWS, D)]
    new = [[], [], [], []]
    for l in range(DEPTH):
        pa, qd, kd, vd, oc, qm, km, vm, *ctx_outs = _inproj_call(l, xs, mods, w, consts)
        ob, od = _attn_call(l, qd, kd, vd, cdk, cdv, qm, km, vm, kc, vc, w)
        x = _merge_call(l, xs, mods, w, pa, ob, oc, od)
        xs = _ffn_call(l, x, mods, w, split_out=False)
        for acc, a in zip(new, ctx_outs):
            acc.append(a)
    xs = [xs[0][:CTX_ROWS], xs[0][CTX_ROWS:]]

    stack = lambda arrs, shape: jnp.stack(arrs, axis=1).reshape((CTX_B, DEPTH, CTX_T) + shape)
    return (xs[0].reshape(CTX_B, CTX_T, D), xs[1].reshape(LAT_B, LAT_T, D),
            stack([a.reshape(CTX_B, CTX_T, BW) for a in new[0]], (HEADS, DA_HD)),
            stack([a.reshape(CTX_B, CTX_T, BW) for a in new[1]], (HEADS, DA_HD)),
            stack([a.reshape(CTX_B, CTX_T, KV_LORA) for a in new[2]], (KV_LORA,)),
            stack([a.reshape(CTX_B, CTX_T, MLA_ROPE) for a in new[3]], (MLA_ROPE,)))
```

```python
import functools
import math

import jax
import jax.numpy as jnp
import numpy as np
from jax import lax
from jax.experimental import pallas as pl
from jax.experimental.pallas import tpu as pltpu

F32 = jnp.float32
BF16 = jnp.bfloat16

D = 1024
DEPTH = 4
CTX_B, CTX_T = 16, 256
LAT_B, LAT_T = 8, 1024
PAST = 512
TILE = 1024
CTX_TILES = CTX_B * CTX_T // TILE
LAT_TILES = LAT_B * LAT_T // TILE
TILES = CTX_TILES + LAT_TILES
ROWS = TILES * TILE
CTX_ROWS = CTX_TILES * TILE
LAT_ROWS = LAT_TILES * TILE
GRID_W = 64
BW = 256
DA_D = 32
DA_HD = 64
HEADS = 4
MLA_NOPE, MLA_ROPE, MLA_V, MLA_QK = 64, 32, 64, 96
Q_LORA, KV_LORA = 192, 128
FFN = 2816
FFN_CHUNK = 256
FFN_CHUNKS = FFN // FFN_CHUNK
POOL_HALF = (1, 2, 4, 8)
POOL_PAD = 16
ROPE_BASE = 10000.0
EPS = 1e-6
LOG2E = 1.4426950408889634
TQ = 256
LANES = 128
VMEM_LIMIT = 56 * 1024 * 1024

W_BLOCKS = 4 * D // BW
W_MLA = 4 * D + 5 * BW
C_CQ, C_CKV, C_KR, C_END = 0, 256, 384, 512


def _lambda_init(l):
    return 0.8 - 0.6 * math.exp(-0.3 * l)


def _sigmoid(x):
    return 0.5 * jnp.tanh(0.5 * x) + 0.5


def _rms(x, n):
    ms = jnp.sum(x * x, axis=-1, keepdims=True) * (1.0 / n)
    return x * lax.rsqrt(ms + EPS)


def _dot(a, b):
    return jnp.dot(a, b, preferred_element_type=F32)


def _dot_nt(a, b):
    return lax.dot_general(a, b, (((1,), (1,)), ((), ())), preferred_element_type=F32)


def _const(shape):
    n = len(shape)
    return pl.BlockSpec(shape, lambda *_: (0,) * n, pipeline_mode=pl.Buffered(1))


def _layer(shape, l):
    n = len(shape)
    return pl.BlockSpec((None,) + shape, lambda *_: (l,) + (0,) * n, pipeline_mode=pl.Buffered(1))


def _params(n_grid):
    return pltpu.CompilerParams(dimension_semantics=("arbitrary",) * n_grid, vmem_limit_bytes=VMEM_LIMIT)


def _row_spec(width):
    return pl.BlockSpec((TILE, width), lambda i: (i, 0))


def _ctx_row_spec(width):
    return pl.BlockSpec((TILE, width), lambda i: (jnp.minimum(i, CTX_TILES - 1), 0))


def _lat_row_spec(width):
    return pl.BlockSpec((TILE, width), lambda i: (jnp.maximum(i - CTX_TILES, 0), 0))


def _x_specs(split):
    return [_ctx_row_spec(D), _lat_row_spec(D)] if split else [_row_spec(D)]


def _read_x(x_refs, is_lat):
    if len(x_refs) == 2:
        return jnp.where(is_lat, x_refs[1][...], x_refs[0][...])
    return x_refs[0][...]


def _in_cols(l, n):
    return pl.BlockSpec((None, D, BW), lambda *_: (l, 0, n), pipeline_mode=pl.Buffered(1))


def _mod_spec(l):
    return pl.BlockSpec((None, 1, 1, 6 * D), lambda i: (l, jnp.maximum(i - (CTX_TILES - 1), 0), 0, 0))


def _group_spec(shape):
    n = len(shape)
    return pl.BlockSpec((None,) + shape, lambda i: (jnp.minimum(i // CTX_TILES, 1),) + (0,) * n,
                        pipeline_mode=pl.Buffered(1))


def _mod_kernel(c_ref, w_ref, b_ref, o_ref):
    c = c_ref[...]
    s = (c * _sigmoid(c)).astype(BF16)
    o_ref[0] = _dot(s, w_ref[0].astype(BF16)) + b_ref[0]


def _mod_call(cond, w_ada, b_ada):
    nc = 4
    cw = 6 * D // nc
    return pl.pallas_call(
        _mod_kernel,
        grid=(DEPTH, nc),
        in_specs=[
            pl.BlockSpec((16, D), lambda l, j: (0, 0)),
            pl.BlockSpec((1, D, cw), lambda l, j: (l, 0, j)),
            pl.BlockSpec((1, 1, cw), lambda l, j: (l, 0, j)),
        ],
        out_specs=pl.BlockSpec((1, 16, cw), lambda l, j: (l, 0, j)),
        out_shape=jax.ShapeDtypeStruct((DEPTH, 16, 6 * D), F32),
        compiler_params=_params(2),
        name="adaln_mod",
    )(cond, w_ada, b_ada.reshape(DEPTH, 1, 6 * D))


def _mla_key_heads(knope, kr_pad, gkn):
    out = []
    for h in range(HEADS):
        kh = knope[:, h * LANES:(h + 1) * LANES] + kr_pad
        out.append(_rms(kh, MLA_QK) * gkn)
    return out


def _ctxkv_kernel(ckv_ref, kr_ref, wuk_ref, wuv_ref, gkn_ref, kc_ref, vc_ref):
    cb = ckv_ref[...].astype(BF16)
    knope = _dot(cb, wuk_ref[...])
    heads = _mla_key_heads(knope, kr_ref[...], gkn_ref[...])
    for h in range(HEADS):
        kc_ref[:, h * LANES:(h + 1) * LANES] = heads[h].astype(BF16)
    vc_ref[...] = _dot(cb, wuv_ref[...]).astype(BF16)


def _ctxkv_call(ckv, kr_pad, w):
    cache = lambda width: pl.BlockSpec((None, None, PAST, width), lambda b, l: (b, l, 0, 0))
    wspec = lambda r, c: pl.BlockSpec((None, r, c), lambda b, l: (l, 0, 0))
    return pl.pallas_call(
        _ctxkv_kernel,
        grid=(LAT_B, DEPTH),
        in_specs=[cache(KV_LORA), cache(LANES), wspec(KV_LORA, 512), wspec(KV_LORA, BW), wspec(1, LANES)],
        out_specs=[cache(512), cache(BW)],
        out_shape=[
            jax.ShapeDtypeStruct((LAT_B, DEPTH, PAST, 512), BF16),
            jax.ShapeDtypeStruct((LAT_B, DEPTH, PAST, BW), BF16),
        ],
        compiler_params=_params(2),
        name="mla_ctx_keys",
    )(ckv, kr_pad, w["wuk"], w["wuv"], w["gkn"])


def _norm_mod(x, g, sc, sh):
    return (_rms(x, D) * g) * (1.0 + sc) + sh


def _rope(x, tab_ref):
    return (x * tab_ref[0] + pltpu.roll(x, LANES - 8, 1) * tab_ref[1]
            + pltpu.roll(x, 8, 1) * tab_ref[2])


def _seg_rms(x, seg, gain):
    sq = x * x
    hi = sq.astype(BF16)
    lo = (sq - hi.astype(F32)).astype(BF16)
    ms = (_dot(hi, seg) + _dot(lo, seg)) * (1.0 / DA_D)
    return x * lax.rsqrt(ms + EPS) * gain


def _inproj_kernel(n_x, *refs):
    x_refs = refs[:n_x]
    (mod_ref, g1_ref, wxa_ref, wq_ref, wk_ref, wv_ref, wxc_ref, wmla_ref,
     poolw_ref, pools_ref, seg_ref, dgq_ref, dgk_ref,
     ropd_ref, ropm_ref, cc_ref, ss_ref, ft_ref, gqa_ref, wuq_ref, gqn_ref, gkv_ref,
     wuk_ref, wuv_ref, gkn_ref,
     pa_ref, qd_ref, kd_ref, vd_ref, oc_ref, qm_ref, km_ref, vm_ref,
     kcf_ref, vf_ref, ckv_ref, kr_ref, pad_ref) = refs[n_x:]
    is_lat = pl.program_id(0) >= CTX_TILES
    mod = mod_ref[0]
    hb = _norm_mod(_read_x(x_refs, is_lat), g1_ref[...], mod[:, D:2 * D], mod[:, 0:D]).astype(BF16)

    xa = _dot(hb, wxa_ref[...])
    pad_ref[0:POOL_PAD, :] = jnp.zeros((POOL_PAD, BW), F32)
    pad_ref[POOL_PAD + TILE:, :] = jnp.zeros((POOL_PAD, BW), F32)
    pad_ref[POOL_PAD:POOL_PAD + TILE, :] = xa
    tm1 = jnp.where(is_lat, LAT_T - 1, CTX_T - 1)
    pos = lax.broadcasted_iota(jnp.int32, (TILE, BW), 0) & tm1
    lane = lax.broadcasted_iota(jnp.int32, (TILE, BW), 1)

    def term(s):
        v = pad_ref[POOL_PAD + s:POOL_PAD + s + TILE, :]
        if s < 0:
            return jnp.where(pos >= -s, v, 0.0)
        return jnp.where(pos + s <= tm1, v, 0.0)

    acc = xa
    wins, cnts = [], []
    lo_s, hi_s = 0, 0
    for half in POOL_HALF:
        for s in list(range(-half, lo_s)) + list(range(hi_s + 1, half)):
            acc = acc + term(s)
        lo_s, hi_s = -half, half - 1
        wins.append(acc)
        cnts.append((jnp.minimum(pos + half, tm1 + 1) - jnp.maximum(pos - half, 0)).astype(F32))
    win = jnp.where(lane < 64, wins[0], jnp.where(lane < 128, wins[1], jnp.where(lane < 192, wins[2], wins[3])))
    cnt = jnp.where(lane < 64, cnts[0], jnp.where(lane < 128, cnts[1], jnp.where(lane < 192, cnts[2], cnts[3])))
    pooled = (win / cnt - xa).astype(BF16)
    pa_ref[...] = (_dot(pooled, poolw_ref[...]) * pools_ref[...]).astype(BF16)

    seg = seg_ref[...]
    qscale = DA_D ** -0.5 * LOG2E
    q = _dot(hb, wq_ref[...])
    k = _dot(hb, wk_ref[...])
    v = _dot(hb, wv_ref[...])
    kns = []
    for s in range(BW // LANES):
        sl = slice(s * LANES, (s + 1) * LANES)
        qn = _seg_rms(q[:, sl], seg, dgq_ref[...])
        kn = _seg_rms(k[:, sl], seg, dgk_ref[...])
        kns.append(kn)
        qd_ref[:, sl] = (_rope(qn, ropd_ref) * qscale).astype(BF16)
        kd_ref[:, sl] = _rope(kn, ropd_ref).astype(BF16)
    vd_ref[...] = v.astype(BF16)

    xcb = _dot(hb, wxc_ref[...]).astype(BF16)
    xcc = _dot(xcb, cc_ref[...]).astype(BF16)
    xcs = _dot(xcb, ss_ref[...]).astype(BF16)
    y = _dot(ft_ref[0], xcc) + _dot(ft_ref[1], xcs)
    fnorm = jnp.where(is_lat, (LAT_T * 64.0) ** -0.5, (CTX_T * 64.0) ** -0.5)
    oc_ref[...] = (y * fnorm).astype(BF16)

    cq = _dot(hb, wmla_ref[:, C_CQ:C_CQ + BW])
    qa = (_rms(cq, Q_LORA) * gqa_ref[...]).astype(BF16)
    qf = _dot(qa, wuq_ref[...])
    mscale = MLA_QK ** -0.5 * LOG2E
    for h in range(HEADS):
        sl = slice(h * LANES, (h + 1) * LANES)
        qh = _rms(qf[:, sl], MLA_QK) * gqn_ref[...]
        qm_ref[:, sl] = (_rope(qh, ropm_ref) * mscale).astype(BF16)
    ckv = _dot(hb, wmla_ref[:, C_CKV:C_CKV + KV_LORA])
    ckv_n = _rms(ckv, KV_LORA) * gkv_ref[...]
    cb = ckv_n.astype(BF16)
    kr_pad = _dot(hb, wmla_ref[:, C_KR:C_KR + LANES])
    heads = _mla_key_heads(_dot(cb, wuk_ref[...]), kr_pad, gkn_ref[...])
    for h in range(HEADS):
        km_ref[:, h * LANES:(h + 1) * LANES] = _rope(heads[h], ropm_ref).astype(BF16)
    vm_ref[...] = _dot(cb, wuv_ref[...]).astype(BF16)

    @pl.when(jnp.logical_not(is_lat))
    def _():
        for s in range(BW // LANES):
            kcf_ref[:, s * LANES:(s + 1) * LANES] = kns[s]
        vf_ref[...] = v
        ckv_ref[...] = ckv_n
        kr_ref[...] = kr_pad[:, MLA_NOPE:MLA_NOPE + MLA_ROPE]


def _inproj_call(l, xs, mods, w, consts):
    row_outs = [(BW, BF16), (BW, BF16), (BW, BF16), (BW, BF16), (BW, BF16), (512, BF16), (512, BF16), (BW, BF16)]
    ctx_outs = [(BW, F32), (BW, F32), (KV_LORA, F32), (MLA_ROPE, F32)]
    return pl.pallas_call(
        functools.partial(_inproj_kernel, len(xs)),
        grid=(TILES,),
        in_specs=_x_specs(len(xs) == 2) + [
            _mod_spec(l), _layer((1, D), l), *[_in_cols(l, W_BLOCKS + n) for n in range(5)],
            _layer((D, C_END), l), _layer((BW, BW), l), _layer((1, BW), l),
            _const((LANES, LANES)), _layer((1, LANES), l), _layer((1, LANES), l),
            _group_spec((3, TILE, LANES)), _group_spec((3, TILE, LANES)),
            _const((BW, BW)), _const((BW, BW)), _group_spec((2, TILE, TILE)),
            _layer((1, BW), l), _layer((BW, 512), l), _layer((1, LANES), l), _layer((1, KV_LORA), l),
            _layer((KV_LORA, 512), l), _layer((KV_LORA, BW), l), _layer((1, LANES), l),
        ],
        out_specs=[_row_spec(wd) for wd, _ in row_outs] + [_ctx_row_spec(wd) for wd, _ in ctx_outs],
        out_shape=([jax.ShapeDtypeStruct((ROWS, wd), dt) for wd, dt in row_outs]
                   + [jax.ShapeDtypeStruct((CTX_ROWS, wd), dt) for wd, dt in ctx_outs]),
        scratch_shapes=[pltpu.VMEM((TILE + 2 * POOL_PAD, BW), F32)],
        compiler_params=_params(1),
        name="inproj",
    )(*xs, mods, w["g1"], *[w["win"]] * 5, w["wmla"], w["poolw"], w["pools"], consts["seg"], w["dgq"], w["dgk"],
      consts["ropd"], consts["ropm"], consts["cc"], consts["ss"], consts["ft"],
      w["gqa"], w["wuq"], w["gqn"], w["gkv"], w["wuk"], w["wuv"], w["gkn"])


def _softmax_parts(q, ks):
    ss = [_dot_nt(q, k) for k in ks]
    m = ss[0].max(axis=-1, keepdims=True)
    for s in ss[1:]:
        m = jnp.maximum(m, s.max(axis=-1, keepdims=True))
    ps = [jnp.exp2(s - m) for s in ss]
    l = ps[0].sum(axis=-1, keepdims=True)
    for p in ps[1:]:
        l = l + p.sum(axis=-1, keepdims=True)
    return [p.astype(BF16) for p in ps], 1.0 / l


def _attend(qd, kds, vds, qm, kms, vms, lam, gsub, lam_init, ob_ref, od_ref):
    n = qd.shape[0]
    lane = lax.broadcasted_iota(jnp.int32, (n, BW), 1)
    zero = jnp.zeros((), BF16)
    out_b = jnp.zeros((n, BW), F32)
    out_d = jnp.zeros((n, BW), F32)
    for h in range(HEADS):
        head = (lane >= h * DA_HD) & (lane < (h + 1) * DA_HD)
        o = None
        for half in range(2):
            lo = h * DA_HD + half * DA_D
            qc = jnp.where((lane >= lo) & (lane < lo + DA_D), qd, zero)
            ps, rl = _softmax_parts(qc, kds)
            pv = sum(_dot(p, v) for p, v in zip(ps, vds)) * rl
            o = pv if half == 0 else o - lam * pv
        o = jnp.where(head, o, 0.0)
        ms = jnp.sum(o * o, axis=-1, keepdims=True) * (1.0 / DA_HD)
        out_b = out_b + o * lax.rsqrt(ms + EPS)
        sl = slice(h * LANES, (h + 1) * LANES)
        ps, rl = _softmax_parts(qm[:, sl], [k[:, sl] for k in kms])
        pv = sum(_dot(p, v) for p, v in zip(ps, vms)) * rl
        out_d = out_d + jnp.where(head, pv, 0.0)
    ob_ref[...] = (out_b * gsub * (1.0 - lam_init)).astype(BF16)
    od_ref[...] = out_d.astype(BF16)


def _attn_kernel(lam_init, qd_ref, kd_ref, vd_ref, cdk_ref, cdv_ref, qm_ref, km_ref, vm_ref, kc_ref, vc_ref,
                 dl_ref, gsub_ref, ob_ref, od_ref):
    i = pl.program_id(0)
    j = pl.program_id(1)
    dl = dl_ref[...]
    lam = (jnp.exp(jnp.sum(dl[0:1] * dl[1:2], axis=-1, keepdims=True))
           - jnp.exp(jnp.sum(dl[2:3] * dl[3:4], axis=-1, keepdims=True)) + lam_init)
    rows = pl.ds(pl.multiple_of(j * TQ, TQ), TQ)
    qd = qd_ref[rows, :]
    qm = qm_ref[rows, :]

    @pl.when(i < CTX_TILES)
    def _():
        _attend(qd, [kd_ref[rows, :]], [vd_ref[rows, :]], qm, [km_ref[rows, :]], [vm_ref[rows, :]],
                lam, gsub_ref[...], lam_init, ob_ref, od_ref)

    @pl.when(i >= CTX_TILES)
    def _():
        _attend(qd, [cdk_ref[...].astype(BF16), kd_ref[...]], [cdv_ref[...].astype(BF16), vd_ref[...]],
                qm, [kc_ref[...], km_ref[...]], [vc_ref[...], vm_ref[...]],
                lam, gsub_ref[...], lam_init, ob_ref, od_ref)


def _attn_call(l, qd, kd, vd, cdk, cdv, qm, km, vm, kc, vc, w):
    tile = lambda width: pl.BlockSpec((TILE, width), lambda i, j: (i, 0))
    cache = lambda width: pl.BlockSpec((None, None, PAST, width),
                                       lambda i, j: (jnp.maximum(i - CTX_TILES, 0), l, 0, 0))
    blk = pl.BlockSpec((TQ, BW), lambda i, j: (i * (TILE // TQ) + j, 0))
    return pl.pallas_call(
        functools.partial(_attn_kernel, _lambda_init(l)),
        grid=(TILES, TILE // TQ),
        in_specs=[tile(BW), tile(BW), tile(BW), cache(BW), cache(BW), tile(512), tile(512), tile(BW),
                  cache(512), cache(BW), _layer((4, DA_D), l), _layer((1, BW), l)],
        out_specs=[blk, blk],
        out_shape=[jax.ShapeDtypeStruct((ROWS, BW), BF16)] * 2,
        compiler_params=_params(2),
        name="attention",
    )(qd, kd, vd, cdk, cdv, qm, km, vm, kc, vc, w["dl"], w["gsub"])


def _merge_kernel(n_x, *refs):
    x_refs = refs[:n_x]
    mod_ref, g1_ref, wg_ref, wb_ref, wo_ref, pa_ref, pb_ref, pc_ref, pd_ref, o_ref = refs[n_x:]
    mod = mod_ref[0]
    x = _read_x(x_refs, pl.program_id(0) >= CTX_TILES)
    hb = _norm_mod(x, g1_ref[...], mod[:, D:2 * D], mod[:, 0:D]).astype(BF16)
    merged = None
    for n, br_ref in enumerate((pa_ref, pb_ref, pc_ref, pd_ref)):
        gate = _sigmoid(_dot(hb, wg_ref[:, n * D:(n + 1) * D]))
        t = gate * _dot(br_ref[...], wb_ref[n])
        merged = t if merged is None else merged + t
    o_ref[...] = x + mod[:, 2 * D:3 * D] * _dot(merged.astype(BF16), wo_ref[...])


def _merge_call(l, xs, mods, w, pa, pb, pc, pd):
    return pl.pallas_call(
        functools.partial(_merge_kernel, len(xs)),
        grid=(TILES,),
        in_specs=_x_specs(len(xs) == 2) + [
            _mod_spec(l), _layer((1, D), l), _layer((D, 4 * D), l), _layer((4, BW, D), l), _layer((D, D), l),
            _row_spec(BW), _row_spec(BW), _row_spec(BW), _row_spec(BW)],
        out_specs=_row_spec(D),
        out_shape=jax.ShapeDtypeStruct((ROWS, D), F32),
        compiler_params=_params(1),
        name="merge",
    )(*xs, mods, w["g1"], w["win"], w["wbr"], w["wout"], pa, pb, pc, pd)


def _ffn_kernel(n_o, x_ref, mod_ref, g2_ref, wu_ref, cv_ref, wd_ref, *refs):
    o_refs = refs[:n_o]
    hb_ref, acc_ref = refs[n_o:]
    is_lat = pl.program_id(0) >= CTX_TILES
    is_ctx = jnp.logical_not(is_lat)
    mod = mod_ref[0]
    hb_ref[...] = _norm_mod(x_ref[...], g2_ref[...], mod[:, 4 * D:5 * D], mod[:, 3 * D:4 * D]).astype(BF16)
    tm1 = jnp.where(is_lat, LAT_T - 1, CTX_T - 1)
    pos = lax.broadcasted_iota(jnp.int32, (TILE, FFN_CHUNK), 0) & tm1
    seq_start = pos == 0
    seq_end = pos == tm1
    acc_ref[...] = jnp.zeros((TILE, D), F32)

    def conv(cols):
        u = _dot(hb_ref[...], wu_ref[:, cols])
        cv = cv_ref[:, cols]
        prev = jnp.where(seq_start, 0.0, pltpu.roll(u, 1, 0))
        nxt = jnp.where(seq_end, 0.0, pltpu.roll(u, TILE - 1, 0))
        return prev * cv[0:1] + u * cv[1:2] + nxt * cv[2:3] + cv[3:4]

    def body(c, carry):
        c0 = pl.multiple_of(c * FFN_CHUNK, FFN_CHUNK)
        h = conv(pl.ds(c0, FFN_CHUNK))
        v = conv(pl.ds(FFN + c0, FFN_CHUNK))
        act = (h * (1.0 + jnp.tanh(h)) * v).astype(BF16)
        acc_ref[...] += _dot(act, wd_ref[pl.ds(c0, FFN_CHUNK), :])
        return carry

    lax.fori_loop(0, FFN_CHUNKS, body, 0)
    y = x_ref[...] + mod[:, 5 * D:6 * D] * acc_ref[...]
    if n_o == 1:
        o_refs[0][...] = y
    else:
        @pl.when(is_ctx)
        def _():
            o_refs[0][...] = y

        @pl.when(is_lat)
        def _():
            o_refs[1][...] = y


def _ffn_call(l, x, mods, w, split_out):
    if split_out:
        out_specs = _x_specs(True)
        out_shape = [jax.ShapeDtypeStruct((CTX_ROWS, D), F32), jax.ShapeDtypeStruct((LAT_ROWS, D), F32)]
    else:
        out_specs = _x_specs(False)
        out_shape = [jax.ShapeDtypeStruct((ROWS, D), F32)]
    return pl.pallas_call(
        functools.partial(_ffn_kernel, len(out_shape)),
        grid=(TILES,),
        in_specs=[_row_spec(D), _mod_spec(l), _layer((1, D), l),
                  _layer((D, 2 * FFN), l), _layer((8, 2 * FFN), l), _layer((FFN, D), l)],
        out_specs=out_specs,
        out_shape=out_shape,
        scratch_shapes=[pltpu.VMEM((TILE, D), BF16), pltpu.VMEM((TILE, D), F32)],
        compiler_params=_params(1),
        name="ffn",
    )(x, mods, w["g2"], w["wup"], w["cv"], w["wdown"])


def _rope_tables():
    half = 8
    inv = ROPE_BASE ** (-np.arange(half, dtype=np.float64) / half)
    t = np.arange(LAT_T)
    rows, cols = (t // GRID_W).astype(np.float64), (t % GRID_W).astype(np.float64)
    ang = np.concatenate([rows[:, None] * inv[None, :]] * 2 + [cols[:, None] * inv[None, :]] * 2, axis=1)
    cos, sin = np.cos(ang), np.sin(ang)
    first = (np.arange(32) % 16) < 8
    sa = np.where(first[None, :], -sin, 0.0)
    sb = np.where(first[None, :], 0.0, sin)
    one, zero = np.ones_like(cos), np.zeros_like(cos)

    def diff_layout(a):
        return np.tile(a, (1, LANES // 32))

    def mla_layout(a, fill):
        out = np.full((LAT_T, LANES), fill)
        out[:, MLA_NOPE:MLA_QK] = a
        return out

    ropd = np.stack([np.stack([diff_layout(one), diff_layout(zero), diff_layout(zero)]),
                     np.stack([diff_layout(cos), diff_layout(sa), diff_layout(sb)])])
    ropm = np.stack([np.stack([mla_layout(one, 1.0), mla_layout(zero, 0.0), mla_layout(zero, 0.0)]),
                     np.stack([mla_layout(cos, 1.0), mla_layout(sa, 0.0), mla_layout(sb, 0.0)])])
    return jnp.asarray(ropd, F32), jnp.asarray(ropm, F32)


def _dft_tables():
    def cs(n):
        k = np.arange(n)
        ang = 2.0 * np.pi * ((k[:, None] * k[None, :]) % n) / n
        return np.cos(ang), np.sin(ang)

    c64, s64 = cs(64)
    eye4 = np.eye(4)
    cl, sl = cs(LAT_T)
    cc_, sc_ = cs(CTX_T)
    ft = np.stack([np.stack([np.kron(eye4, cc_), -np.kron(eye4, sc_)]), np.stack([cl, -sl])])
    seg = np.kron(np.eye(LANES // DA_D), np.ones((DA_D, DA_D)))
    as_bf16 = lambda a: jnp.asarray(a, F32).astype(BF16)
    return as_bf16(np.kron(eye4, c64)), as_bf16(np.kron(eye4, s64)), as_bf16(ft), as_bf16(seg)


def _pad_last(a, lo, hi):
    return jnp.pad(a, ((0, 0),) * (a.ndim - 1) + ((lo, hi),))


def _prep_weights(p):
    w_in = p["w_in"]
    mla = w_in[:, :, W_MLA:]
    cq, ckv, kr = mla[..., :Q_LORA], mla[..., Q_LORA:Q_LORA + KV_LORA], mla[..., Q_LORA + KV_LORA:]
    wmla = jnp.concatenate([_pad_last(cq, 0, BW - Q_LORA), ckv, _pad_last(kr, MLA_NOPE, LANES - MLA_QK)],
                           axis=-1).astype(BF16)
    poolw = jnp.einsum("lgce,gh->lgche", p["pool_w"], jnp.eye(4, dtype=F32)).reshape(DEPTH, BW, BW).astype(BF16)
    wuq = p["w_uq"].reshape(DEPTH, Q_LORA, HEADS, MLA_QK)
    wuq = jnp.pad(wuq, ((0, 0), (0, BW - Q_LORA), (0, 0), (0, LANES - MLA_QK))).reshape(DEPTH, BW, HEADS * LANES)
    wukv = p["w_ukv"].reshape(DEPTH, KV_LORA, HEADS, MLA_NOPE + MLA_V)
    wuk = _pad_last(wukv[..., :MLA_NOPE], 0, LANES - MLA_NOPE).reshape(DEPTH, KV_LORA, HEADS * LANES)
    wuv = wukv[..., MLA_NOPE:].reshape(DEPTH, KV_LORA, HEADS * MLA_V)
    cv = jnp.concatenate([p["conv_w"], p["conv_b"][:, None, :], jnp.zeros((DEPTH, 4, 2 * FFN), F32)], axis=1)
    cv = cv * jnp.where(jnp.arange(2 * FFN) < FFN, 0.5, 1.0)

    row = lambda a, width: _pad_last(a, 0, width - a.shape[-1])[:, None, :]
    tiled = lambda a, reps: jnp.tile(a, (1, reps))[:, None, :]
    return {
        "g1": row(p["g_norm1"], D), "g2": row(p["g_norm2"], D),
        "win": w_in.astype(BF16), "wmla": wmla,
        "poolw": poolw, "pools": row(p["pool_scale"], BW),
        "dgq": tiled(p["diff_qnorm"], LANES // DA_D), "dgk": tiled(p["diff_knorm"], LANES // DA_D),
        "dl": p["diff_lambda"], "gsub": tiled(p["diff_subln"], HEADS),
        "gqa": row(p["mla_qa_norm"], BW), "wuq": wuq.astype(BF16), "gqn": row(p["mla_qnorm"], LANES),
        "gkv": row(p["mla_kv_norm"], KV_LORA), "wuk": wuk.astype(BF16), "wuv": wuv.astype(BF16),
        "gkn": row(p["mla_knorm"], LANES),
        "wbr": p["w_branch"].astype(BF16), "wout": p["w_out"].astype(BF16),
        "wup": p["w_up"].astype(BF16), "cv": cv, "wdown": p["w_down"].astype(BF16),
    }


def kernel(x_prompt, x_sample, c, c_ctx, cache_diff_k, cache_diff_v, cache_mla_ckv, cache_mla_krope, w_ada, b_ada, g_norm1, g_norm2, w_in, pool_w, pool_scale, diff_qnorm, diff_knorm, diff_lambda, diff_subln, mla_qa_norm, w_uq, mla_kv_norm, w_ukv, mla_qnorm, mla_knorm, w_branch, w_out, w_up, conv_w, conv_b, w_down):
    w = _prep_weights(dict(
        w_in=w_in, pool_w=pool_w, pool_scale=pool_scale, diff_qnorm=diff_qnorm, diff_knorm=diff_knorm,
        diff_lambda=diff_lambda, diff_subln=diff_subln, mla_qa_norm=mla_qa_norm, w_uq=w_uq,
        mla_kv_norm=mla_kv_norm, w_ukv=w_ukv, mla_qnorm=mla_qnorm, mla_knorm=mla_knorm, w_branch=w_branch,
        w_out=w_out, w_up=w_up, conv_w=conv_w, conv_b=conv_b, w_down=w_down, g_norm1=g_norm1,
        g_norm2=g_norm2))
    ropd, ropm = _rope_tables()
    cc, ss, ft, seg = _dft_tables()
    consts = dict(ropd=ropd, ropm=ropm, cc=cc, ss=ss, ft=ft, seg=seg)

    cond = jnp.concatenate([c_ctx[None, :], c, jnp.zeros((16 - 1 - LAT_B, D), F32)], axis=0)
    mods = _mod_call(cond, w_ada, b_ada).reshape(DEPTH, 16, 1, 6 * D)

    kc, vc = _ctxkv_call(cache_mla_ckv, _pad_last(cache_mla_krope, MLA_NOPE, LANES - MLA_QK), w)
    cdk = cache_diff_k.reshape(LAT_B, DEPTH, PAST, BW)
    cdv = cache_diff_v.reshape(LAT_B, DEPTH, PAST, BW)

    xs = [x_prompt.reshape(CTX_ROWS, D), x_sample.reshape(LAT_ROWS, D)]
    new = [[], [], [], []]
    for l in range(DEPTH):
        pa, qd, kd, vd, oc, qm, km, vm, *ctx_outs = _inproj_call(l, xs, mods, w, consts)
        ob, od = _attn_call(l, qd, kd, vd, cdk, cdv, qm, km, vm, kc, vc, w)
        x = _merge_call(l, xs, mods, w, pa, ob, oc, od)
        xs = _ffn_call(l, x, mods, w, split_out=(l == DEPTH - 1))
        for acc, a in zip(new, ctx_outs):
            acc.append(a)

    stack = lambda arrs, shape: jnp.stack(arrs, axis=1).reshape((CTX_B, DEPTH, CTX_T) + shape)
    return (xs[0].reshape(CTX_B, CTX_T, D), xs[1].reshape(LAT_B, LAT_T, D),
            stack([a.reshape(CTX_B, CTX_T, BW) for a in new[0]], (HEADS, DA_HD)),
            stack([a.reshape(CTX_B, CTX_T, BW) for a in new[1]], (HEADS, DA_HD)),
            stack([a.reshape(CTX_B, CTX_T, KV_LORA) for a in new[2]], (KV_LORA,)),
            stack([a.reshape(CTX_B, CTX_T, MLA_ROPE) for a in new[3]], (MLA_ROPE,)))
```

```python
import functools
import math

import jax
import jax.numpy as jnp
import numpy as np
from jax import lax
from jax.experimental import pallas as pl
from jax.experimental.pallas import tpu as pltpu

F32 = jnp.float32
BF16 = jnp.bfloat16

D = 1024
DEPTH = 4
CTX_B, CTX_T = 16, 256
LAT_B, LAT_T = 8, 1024
PAST = 512
TILE = 1024
CTX_TILES = CTX_B * CTX_T // TILE
LAT_TILES = LAT_B * LAT_T // TILE
TILES = CTX_TILES + LAT_TILES
ROWS = TILES * TILE
CTX_ROWS = CTX_TILES * TILE
LAT_ROWS = LAT_TILES * TILE
GRID_W = 64
BW = 256
DA_D = 32
DA_HD = 64
HEADS = 4
MLA_NOPE, MLA_ROPE, MLA_V, MLA_QK = 64, 32, 64, 96
Q_LORA, KV_LORA = 192, 128
FFN = 2816
FFN_CHUNK = 256
FFN_CHUNKS = FFN // FFN_CHUNK
POOL_HALF = (1, 2, 4, 8)
POOL_PAD = 16
ROPE_BASE = 10000.0
EPS = 1e-6
LOG2E = 1.4426950408889634
TQ = 256
LANES = 128
VMEM_LIMIT = 56 * 1024 * 1024

W_BLOCKS = 4 * D // BW
W_MLA = 4 * D + 5 * BW
C_CQ, C_CKV, C_KR, C_END = 0, 256, 384, 512


def _lambda_init(l):
    return 0.8 - 0.6 * math.exp(-0.3 * l)


def _sigmoid(x):
    return 0.5 * jnp.tanh(0.5 * x) + 0.5


def _rms(x, n):
    ms = jnp.sum(x * x, axis=-1, keepdims=True) * (1.0 / n)
    return x * lax.rsqrt(ms + EPS)


def _dot(a, b):
    return jnp.dot(a, b, preferred_element_type=F32)


def _dot_nt(a, b):
    return lax.dot_general(a, b, (((1,), (1,)), ((), ())), preferred_element_type=F32)


def _const(shape):
    n = len(shape)
    return pl.BlockSpec(shape, lambda *_: (0,) * n, pipeline_mode=pl.Buffered(1))


def _layer(shape, l):
    n = len(shape)
    return pl.BlockSpec((None,) + shape, lambda *_: (l,) + (0,) * n, pipeline_mode=pl.Buffered(1))


def _params(n_grid):
    return pltpu.CompilerParams(dimension_semantics=("arbitrary",) * n_grid, vmem_limit_bytes=VMEM_LIMIT)


def _row_spec(width):
    return pl.BlockSpec((TILE, width), lambda i: (i, 0))


def _ctx_row_spec(width):
    return pl.BlockSpec((TILE, width), lambda i: (jnp.minimum(i, CTX_TILES - 1), 0))


def _lat_row_spec(width):
    return pl.BlockSpec((TILE, width), lambda i: (jnp.maximum(i - CTX_TILES, 0), 0))


def _x_specs(split):
    return [_ctx_row_spec(D), _lat_row_spec(D)] if split else [_row_spec(D)]


def _read_x(x_refs, is_lat):
    if len(x_refs) == 2:
        return jnp.where(is_lat, x_refs[1][...], x_refs[0][...])
    return x_refs[0][...]


def _in_cols(l, n):
    return pl.BlockSpec((None, D, BW), lambda *_: (l, 0, n), pipeline_mode=pl.Buffered(1))


def _mod_spec(l):
    return pl.BlockSpec((None, 1, 1, 6 * D), lambda i: (l, jnp.maximum(i - (CTX_TILES - 1), 0), 0, 0))


def _group_spec(shape):
    n = len(shape)
    return pl.BlockSpec((None,) + shape, lambda i: (jnp.minimum(i // CTX_TILES, 1),) + (0,) * n,
                        pipeline_mode=pl.Buffered(1))


def _mod_kernel(c_ref, w_ref, b_ref, o_ref):
    c = c_ref[...]
    s = (c * _sigmoid(c)).astype(BF16)
    o_ref[0] = _dot(s, w_ref[0].astype(BF16)) + b_ref[0]


def _mod_call(cond, w_ada, b_ada):
    nc = 4
    cw = 6 * D // nc
    return pl.pallas_call(
        _mod_kernel,
        grid=(DEPTH, nc),
        in_specs=[
            pl.BlockSpec((16, D), lambda l, j: (0, 0)),
            pl.BlockSpec((1, D, cw), lambda l, j: (l, 0, j)),
            pl.BlockSpec((1, 1, cw), lambda l, j: (l, 0, j)),
        ],
        out_specs=pl.BlockSpec((1, 16, cw), lambda l, j: (l, 0, j)),
        out_shape=jax.ShapeDtypeStruct((DEPTH, 16, 6 * D), F32),
        compiler_params=_params(2),
        name="adaln_mod",
    )(cond, w_ada, b_ada.reshape(DEPTH, 1, 6 * D))


def _mla_key_heads(knope, kr_pad, gkn):
    out = []
    for h in range(HEADS):
        kh = knope[:, h * LANES:(h + 1) * LANES] + kr_pad
        out.append(_rms(kh, MLA_QK) * gkn)
    return out


def _ctxkv_kernel(ckv_ref, kr_ref, wuk_ref, wuv_ref, gkn_ref, kc_ref, vc_ref):
    cb = ckv_ref[...].astype(BF16)
    knope = _dot(cb, wuk_ref[...])
    heads = _mla_key_heads(knope, kr_ref[...], gkn_ref[...])
    for h in range(HEADS):
        kc_ref[:, h * LANES:(h + 1) * LANES] = heads[h].astype(BF16)
    vc_ref[...] = _dot(cb, wuv_ref[...]).astype(BF16)


def _ctxkv_call(ckv, kr_pad, w):
    cache = lambda width: pl.BlockSpec((None, None, PAST, width), lambda b, l: (b, l, 0, 0))
    wspec = lambda r, c: pl.BlockSpec((None, r, c), lambda b, l: (l, 0, 0))
    return pl.pallas_call(
        _ctxkv_kernel,
        grid=(LAT_B, DEPTH),
        in_specs=[cache(KV_LORA), cache(LANES), wspec(KV_LORA, 512), wspec(KV_LORA, BW), wspec(1, LANES)],
        out_specs=[cache(512), cache(BW)],
        out_shape=[
            jax.ShapeDtypeStruct((LAT_B, DEPTH, PAST, 512), BF16),
            jax.ShapeDtypeStruct((LAT_B, DEPTH, PAST, BW), BF16),
        ],
        compiler_params=_params(2),
        name="mla_ctx_keys",
    )(ckv, kr_pad, w["wuk"], w["wuv"], w["gkn"])


def _norm_mod(x, g, sc, sh):
    return (_rms(x, D) * g) * (1.0 + sc) + sh


def _rope(x, tab_ref, rot):
    hi = x.astype(BF16)
    pieces = jnp.concatenate([hi, (x - hi.astype(F32)).astype(BF16)], axis=1)
    return x * tab_ref[0] + _dot(pieces, rot) * tab_ref[1]


def _group_inv_rms(x, ones, n):
    sq = (x * x).astype(BF16)
    ms = jnp.concatenate([_dot(sq[:, c:c + BW], ones) for c in range(0, x.shape[1], BW)], axis=1)
    return lax.rsqrt(ms * (1.0 / n) + EPS)


def _inproj_kernel(n_x, *refs):
    x_refs = refs[:n_x]
    (mod_ref, g1_ref, wxa_ref, wq_ref, wk_ref, wv_ref, wxc_ref, wmla_ref,
     poolw_ref, pools_ref, seg_ref, ones_ref, rotd_ref, rotm_ref, dgq_ref, dgk_ref,
     ropd_ref, ropm_ref, cc_ref, ss_ref, ft_ref, gqa_ref, wuq_ref, gqn_ref, gkv_ref,
     wuk_ref, wuv_ref, gkn_ref,
     pa_ref, qd_ref, kd_ref, vd_ref, oc_ref, qm_ref, km_ref, vm_ref,
     kcf_ref, vf_ref, ckv_ref, kr_ref, pad_ref) = refs[n_x:]
    is_lat = pl.program_id(0) >= CTX_TILES
    mod = mod_ref[0]
    hb = _norm_mod(_read_x(x_refs, is_lat), g1_ref[...], mod[:, D:2 * D], mod[:, 0:D]).astype(BF16)

    xa = _dot(hb, wxa_ref[...])
    pad_ref[0:POOL_PAD, :] = jnp.zeros((POOL_PAD, BW), F32)
    pad_ref[POOL_PAD + TILE:, :] = jnp.zeros((POOL_PAD, BW), F32)
    pad_ref[POOL_PAD:POOL_PAD + TILE, :] = xa
    tm1 = jnp.where(is_lat, LAT_T - 1, CTX_T - 1)
    pos = lax.broadcasted_iota(jnp.int32, (TILE, BW), 0) & tm1
    lane = lax.broadcasted_iota(jnp.int32, (TILE, BW), 1)

    def term(s):
        v = pad_ref[POOL_PAD + s:POOL_PAD + s + TILE, :]
        if s < 0:
            return jnp.where(pos >= -s, v, 0.0)
        return jnp.where(pos + s <= tm1, v, 0.0)

    acc = xa
    wins, cnts = [], []
    lo_s, hi_s = 0, 0
    for half in POOL_HALF:
        for s in list(range(-half, lo_s)) + list(range(hi_s + 1, half)):
            acc = acc + term(s)
        lo_s, hi_s = -half, half - 1
        wins.append(acc)
        cnts.append((jnp.minimum(pos + half, tm1 + 1) - jnp.maximum(pos - half, 0)).astype(F32))
    win = jnp.where(lane < 64, wins[0], jnp.where(lane < 128, wins[1], jnp.where(lane < 192, wins[2], wins[3])))
    cnt = jnp.where(lane < 64, cnts[0], jnp.where(lane < 128, cnts[1], jnp.where(lane < 192, cnts[2], cnts[3])))
    pooled = (win / cnt - xa).astype(BF16)
    pa_ref[...] = (_dot(pooled, poolw_ref[...]) * pools_ref[...]).astype(BF16)

    qscale = DA_D ** -0.5 * LOG2E
    q = _dot(hb, wq_ref[...])
    k = _dot(hb, wk_ref[...])
    v = _dot(hb, wv_ref[...])
    rq = _group_inv_rms(q, seg_ref[...], DA_D) * qscale
    rk = _group_inv_rms(k, seg_ref[...], DA_D)
    kns = []
    for s in range(BW // LANES):
        sl = slice(s * LANES, (s + 1) * LANES)
        kg = k[:, sl] * dgk_ref[...]
        kns.append(kg * rk[:, sl])
        qd_ref[:, sl] = (_rope(q[:, sl] * dgq_ref[...], ropd_ref, rotd_ref[...]) * rq[:, sl]).astype(BF16)
        kd_ref[:, sl] = (_rope(kg, ropd_ref, rotd_ref[...]) * rk[:, sl]).astype(BF16)
    vd_ref[...] = v.astype(BF16)

    xcb = _dot(hb, wxc_ref[...]).astype(BF16)
    xcc = _dot(xcb, cc_ref[...]).astype(BF16)
    xcs = _dot(xcb, ss_ref[...]).astype(BF16)
    y = _dot(ft_ref[0], xcc) + _dot(ft_ref[1], xcs)
    fnorm = jnp.where(is_lat, (LAT_T * 64.0) ** -0.5, (CTX_T * 64.0) ** -0.5)
    oc_ref[...] = (y * fnorm).astype(BF16)

    cq = _dot(hb, wmla_ref[:, C_CQ:C_CQ + BW])
    qa = (_rms(cq, Q_LORA) * gqa_ref[...]).astype(BF16)
    qf = _dot(qa, wuq_ref[...])
    mscale = MLA_QK ** -0.5 * LOG2E
    rqm = _group_inv_rms(qf, ones_ref[...], MLA_QK) * mscale
    for h in range(HEADS):
        sl = slice(h * LANES, (h + 1) * LANES)
        qm_ref[:, sl] = (_rope(qf[:, sl] * gqn_ref[...], ropm_ref, rotm_ref[...]) * rqm[:, sl]).astype(BF16)
    ckv = _dot(hb, wmla_ref[:, C_CKV:C_CKV + KV_LORA])
    ckv_n = _rms(ckv, KV_LORA) * gkv_ref[...]
    cb = ckv_n.astype(BF16)
    kr_pad = _dot(hb, wmla_ref[:, C_KR:C_KR + LANES])
    knope = _dot(cb, wuk_ref[...])
    kr_rot = _rope(kr_pad * gkn_ref[...], ropm_ref, rotm_ref[...])
    kr_ssq = _dot((kr_pad * kr_pad).astype(BF16), ones_ref[0:LANES, 0:LANES])
    kn_ssq = jnp.concatenate([_dot((knope[:, c:c + BW] * knope[:, c:c + BW]).astype(BF16), ones_ref[...])
                              for c in range(0, HEADS * LANES, BW)], axis=1)
    for h in range(HEADS):
        sl = slice(h * LANES, (h + 1) * LANES)
        rkm = lax.rsqrt((kn_ssq[:, sl] + kr_ssq) * (1.0 / MLA_QK) + EPS)
        km_ref[:, sl] = ((knope[:, sl] * gkn_ref[...] + kr_rot) * rkm).astype(BF16)
    vm_ref[...] = _dot(cb, wuv_ref[...]).astype(BF16)

    @pl.when(jnp.logical_not(is_lat))
    def _():
        for s, kn in enumerate(kns):
            kcf_ref[:, s * LANES:(s + 1) * LANES] = kn
        vf_ref[...] = v
        ckv_ref[...] = ckv_n
        kr_ref[...] = kr_pad[:, MLA_NOPE:MLA_NOPE + MLA_ROPE]


def _inproj_call(l, xs, mods, w, consts):
    row_outs = [(BW, BF16), (BW, BF16), (BW, BF16), (BW, BF16), (BW, BF16), (512, BF16), (512, BF16), (BW, BF16)]
    ctx_outs = [(BW, F32), (BW, F32), (KV_LORA, F32), (MLA_ROPE, F32)]
    return pl.pallas_call(
        functools.partial(_inproj_kernel, len(xs)),
        grid=(TILES,),
        in_specs=_x_specs(len(xs) == 2) + [
            _mod_spec(l), _layer((1, D), l), *[_in_cols(l, W_BLOCKS + n) for n in range(5)],
            _layer((D, C_END), l), _layer((BW, BW), l), _layer((1, BW), l),
            _const((BW, BW)), _const((BW, BW)), _const((BW, LANES)), _const((BW, LANES)),
            _layer((1, LANES), l), _layer((1, LANES), l),
            _group_spec((2, TILE, LANES)), _group_spec((2, TILE, LANES)),
            _const((BW, BW)), _const((BW, BW)), _group_spec((2, TILE, TILE)),
            _layer((1, BW), l), _layer((BW, 512), l), _layer((1, LANES), l), _layer((1, KV_LORA), l),
            _layer((KV_LORA, 512), l), _layer((KV_LORA, BW), l), _layer((1, LANES), l),
        ],
        out_specs=[_row_spec(wd) for wd, _ in row_outs] + [_ctx_row_spec(wd) for wd, _ in ctx_outs],
        out_shape=([jax.ShapeDtypeStruct((ROWS, wd), dt) for wd, dt in row_outs]
                   + [jax.ShapeDtypeStruct((CTX_ROWS, wd), dt) for wd, dt in ctx_outs]),
        scratch_shapes=[pltpu.VMEM((TILE + 2 * POOL_PAD, BW), F32)],
        compiler_params=_params(1),
        name="inproj",
    )(*xs, mods, w["g1"], *[w["win"]] * 5, w["wmla"], w["poolw"], w["pools"], consts["seg"],
      consts["ones"], consts["rotd"], consts["rotm"], w["dgq"], w["dgk"],
      consts["ropd"], consts["ropm"], consts["cc"], consts["ss"], consts["ft"],
      w["gqa"], w["wuq"], w["gqn"], w["gkv"], w["wuk"], w["wuv"], w["gkn"])


def _softmax_parts(q, ks):
    ss = [_dot_nt(q, k) for k in ks]
    m = ss[0].max(axis=-1, keepdims=True)
    for s in ss[1:]:
        m = jnp.maximum(m, s.max(axis=-1, keepdims=True))
    ps = [jnp.exp2(s - m) for s in ss]
    l = ps[0].sum(axis=-1, keepdims=True)
    for p in ps[1:]:
        l = l + p.sum(axis=-1, keepdims=True)
    return [p.astype(BF16) for p in ps], 1.0 / l


def _attend(qd, kds, vds, qm, kms, vms, lam, gsub, lam_init, ob_ref, od_ref):
    n = qd.shape[0]
    lane = lax.broadcasted_iota(jnp.int32, (n, BW), 1)
    zero = jnp.zeros((), BF16)
    out_b = jnp.zeros((n, BW), F32)
    out_d = jnp.zeros((n, BW), F32)
    for h in range(HEADS):
        head = (lane >= h * DA_HD) & (lane < (h + 1) * DA_HD)
        o = None
        for half in range(2):
            lo = h * DA_HD + half * DA_D
            qc = jnp.where((lane >= lo) & (lane < lo + DA_D), qd, zero)
            ps, rl = _softmax_parts(qc, kds)
            pv = sum(_dot(p, v) for p, v in zip(ps, vds)) * rl
            o = pv if half == 0 else o - lam * pv
        o = jnp.where(head, o, 0.0)
        ms = jnp.sum(o * o, axis=-1, keepdims=True) * (1.0 / DA_HD)
        out_b = out_b + o * lax.rsqrt(ms + EPS)
        sl = slice(h * LANES, (h + 1) * LANES)
        ps, rl = _softmax_parts(qm[:, sl], [k[:, sl] for k in kms])
        pv = sum(_dot(p, v) for p, v in zip(ps, vms)) * rl
        out_d = out_d + jnp.where(head, pv, 0.0)
    ob_ref[...] = (out_b * gsub * (1.0 - lam_init)).astype(BF16)
    od_ref[...] = out_d.astype(BF16)


def _attn_kernel(lam_init, qd_ref, kd_ref, vd_ref, cdk_ref, cdv_ref, qm_ref, km_ref, vm_ref, kc_ref, vc_ref,
                 dl_ref, gsub_ref, ob_ref, od_ref):
    i = pl.program_id(0)
    j = pl.program_id(1)
    dl = dl_ref[...]
    lam = (jnp.exp(jnp.sum(dl[0:1] * dl[1:2], axis=-1, keepdims=True))
           - jnp.exp(jnp.sum(dl[2:3] * dl[3:4], axis=-1, keepdims=True)) + lam_init)
    rows = pl.ds(pl.multiple_of(j * TQ, TQ), TQ)
    qd = qd_ref[rows, :]
    qm = qm_ref[rows, :]

    @pl.when(i < CTX_TILES)
    def _():
        _attend(qd, [kd_ref[rows, :]], [vd_ref[rows, :]], qm, [km_ref[rows, :]], [vm_ref[rows, :]],
                lam, gsub_ref[...], lam_init, ob_ref, od_ref)

    @pl.when(i >= CTX_TILES)
    def _():
        _attend(qd, [cdk_ref[...].astype(BF16), kd_ref[...]], [cdv_ref[...].astype(BF16), vd_ref[...]],
                qm, [kc_ref[...], km_ref[...]], [vc_ref[...], vm_ref[...]],
                lam, gsub_ref[...], lam_init, ob_ref, od_ref)


def _attn_call(l, qd, kd, vd, cdk, cdv, qm, km, vm, kc, vc, w):
    tile = lambda width: pl.BlockSpec((TILE, width), lambda i, j: (i, 0))
    cache = lambda width: pl.BlockSpec((None, None, PAST, width),
                                       lambda i, j: (jnp.maximum(i - CTX_TILES, 0), l, 0, 0))
    blk = pl.BlockSpec((TQ, BW), lambda i, j: (i * (TILE // TQ) + j, 0))
    return pl.pallas_call(
        functools.partial(_attn_kernel, _lambda_init(l)),
        grid=(TILES, TILE // TQ),
        in_specs=[tile(BW), tile(BW), tile(BW), cache(BW), cache(BW), tile(512), tile(512), tile(BW),
                  cache(512), cache(BW), _layer((4, DA_D), l), _layer((1, BW), l)],
        out_specs=[blk, blk],
        out_shape=[jax.ShapeDtypeStruct((ROWS, BW), BF16)] * 2,
        compiler_params=_params(2),
        name="attention",
    )(qd, kd, vd, cdk, cdv, qm, km, vm, kc, vc, w["dl"], w["gsub"])


def _merge_kernel(n_x, *refs):
    x_refs = refs[:n_x]
    mod_ref, g1_ref, wg_ref, wb_ref, wo_ref, pa_ref, pb_ref, pc_ref, pd_ref, o_ref = refs[n_x:]
    mod = mod_ref[0]
    x = _read_x(x_refs, pl.program_id(0) >= CTX_TILES)
    hb = _norm_mod(x, g1_ref[...], mod[:, D:2 * D], mod[:, 0:D]).astype(BF16)
    merged = None
    for n, br_ref in enumerate((pa_ref, pb_ref, pc_ref, pd_ref)):
        gate = _sigmoid(_dot(hb, wg_ref[:, n * D:(n + 1) * D]))
        t = gate * _dot(br_ref[...], wb_ref[n])
        merged = t if merged is None else merged + t
    o_ref[...] = x + mod[:, 2 * D:3 * D] * _dot(merged.astype(BF16), wo_ref[...])


def _merge_call(l, xs, mods, w, pa, pb, pc, pd):
    return pl.pallas_call(
        functools.partial(_merge_kernel, len(xs)),
        grid=(TILES,),
        in_specs=_x_specs(len(xs) == 2) + [
            _mod_spec(l), _layer((1, D), l), _layer((D, 4 * D), l), _layer((4, BW, D), l), _layer((D, D), l),
            _row_spec(BW), _row_spec(BW), _row_spec(BW), _row_spec(BW)],
        out_specs=_row_spec(D),
        out_shape=jax.ShapeDtypeStruct((ROWS, D), F32),
        compiler_params=_params(1),
        name="merge",
    )(*xs, mods, w["g1"], w["win"], w["wbr"], w["wout"], pa, pb, pc, pd)


def _ffn_kernel(n_o, x_ref, mod_ref, g2_ref, wu_ref, cv_ref, wd_ref, *refs):
    o_refs = refs[:n_o]
    hb_ref, acc_ref = refs[n_o:]
    is_lat = pl.program_id(0) >= CTX_TILES
    is_ctx = jnp.logical_not(is_lat)
    mod = mod_ref[0]
    hb_ref[...] = _norm_mod(x_ref[...], g2_ref[...], mod[:, 4 * D:5 * D], mod[:, 3 * D:4 * D]).astype(BF16)
    tm1 = jnp.where(is_lat, LAT_T - 1, CTX_T - 1)
    pos = lax.broadcasted_iota(jnp.int32, (TILE, FFN_CHUNK), 0) & tm1
    seq_start = pos == 0
    seq_end = pos == tm1
    acc_ref[...] = jnp.zeros((TILE, D), F32)

    def conv(cols):
        u = _dot(hb_ref[...], wu_ref[:, cols])
        cv = cv_ref[:, cols]
        prev = jnp.where(seq_start, 0.0, pltpu.roll(u, 1, 0))
        nxt = jnp.where(seq_end, 0.0, pltpu.roll(u, TILE - 1, 0))
        return prev * cv[0:1] + u * cv[1:2] + nxt * cv[2:3] + cv[3:4]

    def body(c, carry):
        c0 = pl.multiple_of(c * FFN_CHUNK, FFN_CHUNK)
        h = conv(pl.ds(c0, FFN_CHUNK))
        v = conv(pl.ds(FFN + c0, FFN_CHUNK))
        act = (h * (1.0 + jnp.tanh(h)) * v).astype(BF16)
        acc_ref[...] += _dot(act, wd_ref[pl.ds(c0, FFN_CHUNK), :])
        return carry

    lax.fori_loop(0, FFN_CHUNKS, body, 0)
    y = x_ref[...] + mod[:, 5 * D:6 * D] * acc_ref[...]
    if n_o == 1:
        o_refs[0][...] = y
    else:
        @pl.when(is_ctx)
        def _():
            o_refs[0][...] = y

        @pl.when(is_lat)
        def _():
            o_refs[1][...] = y


def _ffn_call(l, x, mods, w, split_out):
    if split_out:
        out_specs = _x_specs(True)
        out_shape = [jax.ShapeDtypeStruct((CTX_ROWS, D), F32), jax.ShapeDtypeStruct((LAT_ROWS, D), F32)]
    else:
        out_specs = _x_specs(False)
        out_shape = [jax.ShapeDtypeStruct((ROWS, D), F32)]
    return pl.pallas_call(
        functools.partial(_ffn_kernel, len(out_shape)),
        grid=(TILES,),
        in_specs=[_row_spec(D), _mod_spec(l), _layer((1, D), l),
                  _layer((D, 2 * FFN), l), _layer((8, 2 * FFN), l), _layer((FFN, D), l)],
        out_specs=out_specs,
        out_shape=out_shape,
        scratch_shapes=[pltpu.VMEM((TILE, D), BF16), pltpu.VMEM((TILE, D), F32)],
        compiler_params=_params(1),
        name="ffn",
    )(x, mods, w["g2"], w["wup"], w["cv"], w["wdown"])


def _rope_tables():
    half = 8
    inv = ROPE_BASE ** (-np.arange(half, dtype=np.float64) / half)
    t = np.arange(LAT_T)
    rows, cols = (t // GRID_W).astype(np.float64), (t % GRID_W).astype(np.float64)
    ang = np.concatenate([rows[:, None] * inv[None, :]] * 2 + [cols[:, None] * inv[None, :]] * 2, axis=1)
    cos, sin = np.cos(ang), np.sin(ang)

    def diff_layout(a):
        return np.tile(a, (1, LANES // 32))

    def mla_layout(a, fill):
        out = np.full((LAT_T, LANES), fill)
        out[:, MLA_NOPE:MLA_QK] = a
        return out

    ident = lambda layout: np.stack([layout(np.ones_like(cos)), layout(np.zeros_like(cos))])
    ropd = np.stack([ident(diff_layout), np.stack([diff_layout(cos), diff_layout(sin)])])
    ropm = np.stack([np.stack([mla_layout(np.ones_like(cos), 1.0), mla_layout(np.zeros_like(cos), 0.0)]),
                     np.stack([mla_layout(cos, 1.0), mla_layout(sin, 0.0)])])
    return jnp.asarray(ropd, F32), jnp.asarray(ropm, F32)


def _rotate_half_matrix(lanes):
    r = np.zeros((LANES, LANES))
    for j in lanes:
        if j % 16 < 8:
            r[j + 8, j] = -1.0
        else:
            r[j - 8, j] = 1.0
    return np.concatenate([r, r], axis=0)


def _matrix_tables():
    def cs(n):
        k = np.arange(n)
        ang = 2.0 * np.pi * ((k[:, None] * k[None, :]) % n) / n
        return np.cos(ang), np.sin(ang)

    c64, s64 = cs(64)
    eye4 = np.eye(4)
    cl, sl = cs(LAT_T)
    cc_, sc_ = cs(CTX_T)
    ft = np.stack([np.stack([np.kron(eye4, cc_), -np.kron(eye4, sc_)]), np.stack([cl, -sl])])
    as_bf16 = lambda a: jnp.asarray(a, F32).astype(BF16)
    return dict(
        cc=as_bf16(np.kron(eye4, c64)), ss=as_bf16(np.kron(eye4, s64)), ft=as_bf16(ft),
        seg=as_bf16(np.kron(np.eye(BW // DA_D), np.ones((DA_D, DA_D)))),
        ones=as_bf16(np.kron(np.eye(BW // LANES), np.ones((LANES, LANES)))),
        rotd=as_bf16(_rotate_half_matrix(range(LANES))),
        rotm=as_bf16(_rotate_half_matrix(range(MLA_NOPE, MLA_QK))))


def _pad_last(a, lo, hi):
    return jnp.pad(a, ((0, 0),) * (a.ndim - 1) + ((lo, hi),))


def _prep_weights(p):
    w_in = p["w_in"]
    mla = w_in[:, :, W_MLA:]
    cq, ckv, kr = mla[..., :Q_LORA], mla[..., Q_LORA:Q_LORA + KV_LORA], mla[..., Q_LORA + KV_LORA:]
    wmla = jnp.concatenate([_pad_last(cq, 0, BW - Q_LORA), ckv, _pad_last(kr, MLA_NOPE, LANES - MLA_QK)],
                           axis=-1).astype(BF16)
    poolw = jnp.einsum("lgce,gh->lgche", p["pool_w"], jnp.eye(4, dtype=F32)).reshape(DEPTH, BW, BW).astype(BF16)
    wuq = p["w_uq"].reshape(DEPTH, Q_LORA, HEADS, MLA_QK)
    wuq = jnp.pad(wuq, ((0, 0), (0, BW - Q_LORA), (0, 0), (0, LANES - MLA_QK))).reshape(DEPTH, BW, HEADS * LANES)
    wukv = p["w_ukv"].reshape(DEPTH, KV_LORA, HEADS, MLA_NOPE + MLA_V)
    wuk = _pad_last(wukv[..., :MLA_NOPE], 0, LANES - MLA_NOPE).reshape(DEPTH, KV_LORA, HEADS * LANES)
    wuv = wukv[..., MLA_NOPE:].reshape(DEPTH, KV_LORA, HEADS * MLA_V)
    cv = jnp.concatenate([p["conv_w"], p["conv_b"][:, None, :], jnp.zeros((DEPTH, 4, 2 * FFN), F32)], axis=1)
    cv = cv * jnp.where(jnp.arange(2 * FFN) < FFN, 0.5, 1.0)

    row = lambda a, width: _pad_last(a, 0, width - a.shape[-1])[:, None, :]
    tiled = lambda a, reps: jnp.tile(a, (1, reps))[:, None, :]
    return {
        "g1": row(p["g_norm1"], D), "g2": row(p["g_norm2"], D),
        "win": w_in.astype(BF16), "wmla": wmla,
        "poolw": poolw, "pools": row(p["pool_scale"], BW),
        "dgq": tiled(p["diff_qnorm"], LANES // DA_D), "dgk": tiled(p["diff_knorm"], LANES // DA_D),
        "dl": p["diff_lambda"], "gsub": tiled(p["diff_subln"], HEADS),
        "gqa": row(p["mla_qa_norm"], BW), "wuq": wuq.astype(BF16), "gqn": row(p["mla_qnorm"], LANES),
        "gkv": row(p["mla_kv_norm"], KV_LORA), "wuk": wuk.astype(BF16), "wuv": wuv.astype(BF16),
        "gkn": row(p["mla_knorm"], LANES),
        "wbr": p["w_branch"].astype(BF16), "wout": p["w_out"].astype(BF16),
        "wup": p["w_up"].astype(BF16), "cv": cv, "wdown": p["w_down"].astype(BF16),
    }


def kernel(x_prompt, x_sample, c, c_ctx, cache_diff_k, cache_diff_v, cache_mla_ckv, cache_mla_krope, w_ada, b_ada, g_norm1, g_norm2, w_in, pool_w, pool_scale, diff_qnorm, diff_knorm, diff_lambda, diff_subln, mla_qa_norm, w_uq, mla_kv_norm, w_ukv, mla_qnorm, mla_knorm, w_branch, w_out, w_up, conv_w, conv_b, w_down):
    w = _prep_weights(dict(
        w_in=w_in, pool_w=pool_w, pool_scale=pool_scale, diff_qnorm=diff_qnorm, diff_knorm=diff_knorm,
        diff_lambda=diff_lambda, diff_subln=diff_subln, mla_qa_norm=mla_qa_norm, w_uq=w_uq,
        mla_kv_norm=mla_kv_norm, w_ukv=w_ukv, mla_qnorm=mla_qnorm, mla_knorm=mla_knorm, w_branch=w_branch,
        w_out=w_out, w_up=w_up, conv_w=conv_w, conv_b=conv_b, w_down=w_down, g_norm1=g_norm1,
        g_norm2=g_norm2))
    ropd, ropm = _rope_tables()
    consts = dict(_matrix_tables(), ropd=ropd, ropm=ropm)

    cond = jnp.concatenate([c_ctx[None, :], c, jnp.zeros((16 - 1 - LAT_B, D), F32)], axis=0)
    mods = _mod_call(cond, w_ada, b_ada).reshape(DEPTH, 16, 1, 6 * D)

    kc, vc = _ctxkv_call(cache_mla_ckv, _pad_last(cache_mla_krope, MLA_NOPE, LANES - MLA_QK), w)
    cdk = cache_diff_k.reshape(LAT_B, DEPTH, PAST, BW)
    cdv = cache_diff_v.reshape(LAT_B, DEPTH, PAST, BW)

    xs = [x_prompt.reshape(CTX_ROWS, D), x_sample.reshape(LAT_ROWS, D)]
    new = [[], [], [], []]
    for l in range(DEPTH):
        pa, qd, kd, vd, oc, qm, km, vm, *ctx_outs = _inproj_call(l, xs, mods, w, consts)
        ob, od = _attn_call(l, qd, kd, vd, cdk, cdv, qm, km, vm, kc, vc, w)
        x = _merge_call(l, xs, mods, w, pa, ob, oc, od)
        xs = _ffn_call(l, x, mods, w, split_out=(l == DEPTH - 1))
        for acc, a in zip(new, ctx_outs):
            acc.append(a)

    stack = lambda arrs, shape: jnp.stack(arrs, axis=1).reshape((CTX_B, DEPTH, CTX_T) + shape)
    return (xs[0].reshape(CTX_B, CTX_T, D), xs[1].reshape(LAT_B, LAT_T, D),
            stack([a.reshape(CTX_B, CTX_T, BW) for a in new[0]], (HEADS, DA_HD)),
            stack([a.reshape(CTX_B, CTX_T, BW) for a in new[1]], (HEADS, DA_HD)),
            stack([a.reshape(CTX_B, CTX_T, KV_LORA) for a in new[2]], (KV_LORA,)),
            stack([a.reshape(CTX_B, CTX_T, MLA_ROPE) for a in new[3]], (MLA_ROPE,)))
```

```python
import functools
import math

import jax
import jax.numpy as jnp
import numpy as np
from jax import lax
from jax.experimental import pallas as pl
from jax.experimental.pallas import tpu as pltpu

F32 = jnp.float32
BF16 = jnp.bfloat16

D = 1024
DEPTH = 4
CTX_B, CTX_T = 16, 256
LAT_B, LAT_T = 8, 1024
PAST = 512
TILE = 1024
CTX_TILES = CTX_B * CTX_T // TILE
LAT_TILES = LAT_B * LAT_T // TILE
TILES = CTX_TILES + LAT_TILES
ROWS = TILES * TILE
CTX_ROWS = CTX_TILES * TILE
LAT_ROWS = LAT_TILES * TILE
GRID_W = 64
BW = 256
DA_D = 32
DA_HD = 64
HEADS = 4
MLA_NOPE, MLA_ROPE, MLA_V, MLA_QK = 64, 32, 64, 96
Q_LORA, KV_LORA = 192, 128
FFN = 2816
FFN_CHUNK = 256
FFN_CHUNKS = FFN // FFN_CHUNK
FFN_EDGE = 8
POOL_HALF = (1, 2, 4, 8)
POOL_PAD = 16
ROPE_BASE = 10000.0
EPS = 1e-6
LOG2E = 1.4426950408889634
TQ = 256
LANES = 128
VMEM_LIMIT = 56 * 1024 * 1024

W_BLOCKS = 4 * D // BW
W_MLA = 4 * D + 5 * BW
C_CQ, C_CKV, C_KR, C_END = 0, 256, 384, 512


def _lambda_init(l):
    return 0.8 - 0.6 * math.exp(-0.3 * l)


def _sigmoid(x):
    return 0.5 * jnp.tanh(0.5 * x) + 0.5


def _rms(x, n):
    ms = jnp.sum(x * x, axis=-1, keepdims=True) * (1.0 / n)
    return x * lax.rsqrt(ms + EPS)


def _dot(a, b):
    return jnp.dot(a, b, preferred_element_type=F32)


def _dot_nt(a, b):
    return lax.dot_general(a, b, (((1,), (1,)), ((), ())), preferred_element_type=F32)


def _const(shape):
    n = len(shape)
    return pl.BlockSpec(shape, lambda *_: (0,) * n, pipeline_mode=pl.Buffered(1))


def _layer(shape, l):
    n = len(shape)
    return pl.BlockSpec((None,) + shape, lambda *_: (l,) + (0,) * n, pipeline_mode=pl.Buffered(1))


def _params(n_grid):
    return pltpu.CompilerParams(dimension_semantics=("arbitrary",) * n_grid, vmem_limit_bytes=VMEM_LIMIT)


def _row_spec(width):
    return pl.BlockSpec((TILE, width), lambda i: (i, 0))


def _ctx_row_spec(width):
    return pl.BlockSpec((TILE, width), lambda i: (jnp.minimum(i, CTX_TILES - 1), 0))


def _lat_row_spec(width):
    return pl.BlockSpec((TILE, width), lambda i: (jnp.maximum(i - CTX_TILES, 0), 0))


def _x_specs(split):
    return [_ctx_row_spec(D), _lat_row_spec(D)] if split else [_row_spec(D)]


def _read_x(x_refs, is_lat):
    if len(x_refs) == 2:
        return jnp.where(is_lat, x_refs[1][...], x_refs[0][...])
    return x_refs[0][...]


def _in_cols(l, n):
    return pl.BlockSpec((None, D, BW), lambda *_: (l, 0, n), pipeline_mode=pl.Buffered(1))


def _mod_spec(l):
    return pl.BlockSpec((None, 1, 1, 6 * D), lambda i: (l, jnp.maximum(i - (CTX_TILES - 1), 0), 0, 0))


def _group_spec(shape):
    n = len(shape)
    return pl.BlockSpec((None,) + shape, lambda i: (jnp.minimum(i // CTX_TILES, 1),) + (0,) * n,
                        pipeline_mode=pl.Buffered(1))


def _mod_kernel(c_ref, w_ref, b_ref, o_ref):
    c = c_ref[...]
    s = (c * _sigmoid(c)).astype(BF16)
    o_ref[0] = _dot(s, w_ref[0].astype(BF16)) + b_ref[0]


def _mod_call(cond, w_ada, b_ada):
    nc = 4
    cw = 6 * D // nc
    return pl.pallas_call(
        _mod_kernel,
        grid=(DEPTH, nc),
        in_specs=[
            pl.BlockSpec((16, D), lambda l, j: (0, 0)),
            pl.BlockSpec((1, D, cw), lambda l, j: (l, 0, j)),
            pl.BlockSpec((1, 1, cw), lambda l, j: (l, 0, j)),
        ],
        out_specs=pl.BlockSpec((1, 16, cw), lambda l, j: (l, 0, j)),
        out_shape=jax.ShapeDtypeStruct((DEPTH, 16, 6 * D), F32),
        compiler_params=_params(2),
        name="adaln_mod",
    )(cond, w_ada, b_ada.reshape(DEPTH, 1, 6 * D))


def _mla_key_heads(knope, kr_pad, gkn):
    out = []
    for h in range(HEADS):
        kh = knope[:, h * LANES:(h + 1) * LANES] + kr_pad
        out.append(_rms(kh, MLA_QK) * gkn)
    return out


def _ctxkv_kernel(ckv_ref, kr_ref, wuk_ref, wuv_ref, gkn_ref, kc_ref, vc_ref):
    cb = ckv_ref[...].astype(BF16)
    knope = _dot(cb, wuk_ref[...])
    heads = _mla_key_heads(knope, kr_ref[...], gkn_ref[...])
    for h in range(HEADS):
        kc_ref[:, h * LANES:(h + 1) * LANES] = heads[h].astype(BF16)
    vc_ref[...] = _dot(cb, wuv_ref[...]).astype(BF16)


def _ctxkv_call(ckv, kr_pad, w):
    cache = lambda width: pl.BlockSpec((None, None, PAST, width), lambda b, l: (b, l, 0, 0))
    wspec = lambda r, c: pl.BlockSpec((None, r, c), lambda b, l: (l, 0, 0))
    return pl.pallas_call(
        _ctxkv_kernel,
        grid=(LAT_B, DEPTH),
        in_specs=[cache(KV_LORA), cache(LANES), wspec(KV_LORA, 512), wspec(KV_LORA, BW), wspec(1, LANES)],
        out_specs=[cache(512), cache(BW)],
        out_shape=[
            jax.ShapeDtypeStruct((LAT_B, DEPTH, PAST, 512), BF16),
            jax.ShapeDtypeStruct((LAT_B, DEPTH, PAST, BW), BF16),
        ],
        compiler_params=_params(2),
        name="mla_ctx_keys",
    )(ckv, kr_pad, w["wuk"], w["wuv"], w["gkn"])


def _norm_mod(x, g, sc, sh):
    return (_rms(x, D) * g) * (1.0 + sc) + sh


def _rope(x, tab_ref, rot):
    hi = x.astype(BF16)
    pieces = jnp.concatenate([hi, (x - hi.astype(F32)).astype(BF16)], axis=1)
    return x * tab_ref[0] + _dot(pieces, rot) * tab_ref[1]


def _group_inv_rms(x, ones, n):
    sq = (x * x).astype(BF16)
    ms = jnp.concatenate([_dot(sq[:, c:c + BW], ones) for c in range(0, x.shape[1], BW)], axis=1)
    return lax.rsqrt(ms * (1.0 / n) + EPS)


def _inproj_kernel(n_x, *refs):
    x_refs = refs[:n_x]
    (mod_ref, g1_ref, wxa_ref, wq_ref, wk_ref, wv_ref, wxc_ref, wmla_ref,
     poolw_ref, pools_ref, seg_ref, ones_ref, rotd_ref, rotm_ref, dgq_ref, dgk_ref,
     ropd_ref, ropm_ref, cc_ref, ss_ref, ft_ref, gqa_ref, wuq_ref, gqn_ref, gkv_ref,
     wuk_ref, wuv_ref, gkn_ref,
     pa_ref, qd_ref, kd_ref, vd_ref, oc_ref, qm_ref, km_ref, vm_ref,
     kcf_ref, vf_ref, ckv_ref, kr_ref, pad_ref) = refs[n_x:]
    is_lat = pl.program_id(0) >= CTX_TILES
    mod = mod_ref[0]
    hb = _norm_mod(_read_x(x_refs, is_lat), g1_ref[...], mod[:, D:2 * D], mod[:, 0:D]).astype(BF16)

    xa = _dot(hb, wxa_ref[...])
    pad_ref[0:POOL_PAD, :] = jnp.zeros((POOL_PAD, BW), F32)
    pad_ref[POOL_PAD + TILE:, :] = jnp.zeros((POOL_PAD, BW), F32)
    pad_ref[POOL_PAD:POOL_PAD + TILE, :] = xa
    tm1 = jnp.where(is_lat, LAT_T - 1, CTX_T - 1)
    pos = lax.broadcasted_iota(jnp.int32, (TILE, BW), 0) & tm1
    lane = lax.broadcasted_iota(jnp.int32, (TILE, BW), 1)

    def term(s):
        v = pad_ref[POOL_PAD + s:POOL_PAD + s + TILE, :]
        if s < 0:
            return jnp.where(pos >= -s, v, 0.0)
        return jnp.where(pos + s <= tm1, v, 0.0)

    acc = xa
    wins, cnts = [], []
    lo_s, hi_s = 0, 0
    for half in POOL_HALF:
        for s in list(range(-half, lo_s)) + list(range(hi_s + 1, half)):
            acc = acc + term(s)
        lo_s, hi_s = -half, half - 1
        wins.append(acc)
        cnts.append((jnp.minimum(pos + half, tm1 + 1) - jnp.maximum(pos - half, 0)).astype(F32))
    win = jnp.where(lane < 64, wins[0], jnp.where(lane < 128, wins[1], jnp.where(lane < 192, wins[2], wins[3])))
    cnt = jnp.where(lane < 64, cnts[0], jnp.where(lane < 128, cnts[1], jnp.where(lane < 192, cnts[2], cnts[3])))
    pooled = (win / cnt - xa).astype(BF16)
    pa_ref[...] = (_dot(pooled, poolw_ref[...]) * pools_ref[...]).astype(BF16)

    qscale = DA_D ** -0.5 * LOG2E
    q = _dot(hb, wq_ref[...])
    k = _dot(hb, wk_ref[...])
    v = _dot(hb, wv_ref[...])
    rq = _group_inv_rms(q, seg_ref[...], DA_D) * qscale
    rk = _group_inv_rms(k, seg_ref[...], DA_D)
    kns = []
    for s in range(BW // LANES):
        sl = slice(s * LANES, (s + 1) * LANES)
        kg = k[:, sl] * dgk_ref[...]
        kns.append(kg * rk[:, sl])
        qd_ref[:, sl] = (_rope(q[:, sl] * dgq_ref[...], ropd_ref, rotd_ref[...]) * rq[:, sl]).astype(BF16)
        kd_ref[:, sl] = (_rope(kg, ropd_ref, rotd_ref[...]) * rk[:, sl]).astype(BF16)
    vd_ref[...] = v.astype(BF16)

    xcb = _dot(hb, wxc_ref[...]).astype(BF16)
    xcc = _dot(xcb, cc_ref[...]).astype(BF16)
    xcs = _dot(xcb, ss_ref[...]).astype(BF16)
    y = _dot(ft_ref[0], xcc) + _dot(ft_ref[1], xcs)
    fnorm = jnp.where(is_lat, (LAT_T * 64.0) ** -0.5, (CTX_T * 64.0) ** -0.5)
    oc_ref[...] = (y * fnorm).astype(BF16)

    cq = _dot(hb, wmla_ref[:, C_CQ:C_CQ + BW])
    qa = (_rms(cq, Q_LORA) * gqa_ref[...]).astype(BF16)
    qf = _dot(qa, wuq_ref[...])
    mscale = MLA_QK ** -0.5 * LOG2E
    rqm = _group_inv_rms(qf, ones_ref[...], MLA_QK) * mscale
    for h in range(HEADS):
        sl = slice(h * LANES, (h + 1) * LANES)
        qm_ref[:, sl] = (_rope(qf[:, sl] * gqn_ref[...], ropm_ref, rotm_ref[...]) * rqm[:, sl]).astype(BF16)
    ckv = _dot(hb, wmla_ref[:, C_CKV:C_CKV + KV_LORA])
    ckv_n = _rms(ckv, KV_LORA) * gkv_ref[...]
    cb = ckv_n.astype(BF16)
    kr_pad = _dot(hb, wmla_ref[:, C_KR:C_KR + LANES])
    knope = _dot(cb, wuk_ref[...])
    kr_rot = _rope(kr_pad * gkn_ref[...], ropm_ref, rotm_ref[...])
    kr_ssq = _dot((kr_pad * kr_pad).astype(BF16), ones_ref[0:LANES, 0:LANES])
    kn_ssq = jnp.concatenate([_dot((knope[:, c:c + BW] * knope[:, c:c + BW]).astype(BF16), ones_ref[...])
                              for c in range(0, HEADS * LANES, BW)], axis=1)
    for h in range(HEADS):
        sl = slice(h * LANES, (h + 1) * LANES)
        rkm = lax.rsqrt((kn_ssq[:, sl] + kr_ssq) * (1.0 / MLA_QK) + EPS)
        km_ref[:, sl] = ((knope[:, sl] * gkn_ref[...] + kr_rot) * rkm).astype(BF16)
    vm_ref[...] = _dot(cb, wuv_ref[...]).astype(BF16)

    @pl.when(jnp.logical_not(is_lat))
    def _():
        for s, kn in enumerate(kns):
            kcf_ref[:, s * LANES:(s + 1) * LANES] = kn
        vf_ref[...] = v
        ckv_ref[...] = ckv_n
        kr_ref[...] = kr_pad[:, MLA_NOPE:MLA_NOPE + MLA_ROPE]


def _inproj_call(l, xs, mods, w, consts):
    row_outs = [(BW, BF16), (BW, BF16), (BW, BF16), (BW, BF16), (BW, BF16), (512, BF16), (512, BF16), (BW, BF16)]
    ctx_outs = [(BW, F32), (BW, F32), (KV_LORA, F32), (MLA_ROPE, F32)]
    return pl.pallas_call(
        functools.partial(_inproj_kernel, len(xs)),
        grid=(TILES,),
        in_specs=_x_specs(len(xs) == 2) + [
            _mod_spec(l), _layer((1, D), l), *[_in_cols(l, W_BLOCKS + n) for n in range(5)],
            _layer((D, C_END), l), _layer((BW, BW), l), _layer((1, BW), l),
            _const((BW, BW)), _const((BW, BW)), _const((BW, LANES)), _const((BW, LANES)),
            _layer((1, LANES), l), _layer((1, LANES), l),
            _group_spec((2, TILE, LANES)), _group_spec((2, TILE, LANES)),
            _const((BW, BW)), _const((BW, BW)), _group_spec((2, TILE, TILE)),
            _layer((1, BW), l), _layer((BW, 512), l), _layer((1, LANES), l), _layer((1, KV_LORA), l),
            _layer((KV_LORA, 512), l), _layer((KV_LORA, BW), l), _layer((1, LANES), l),
        ],
        out_specs=[_row_spec(wd) for wd, _ in row_outs] + [_ctx_row_spec(wd) for wd, _ in ctx_outs],
        out_shape=([jax.ShapeDtypeStruct((ROWS, wd), dt) for wd, dt in row_outs]
                   + [jax.ShapeDtypeStruct((CTX_ROWS, wd), dt) for wd, dt in ctx_outs]),
        scratch_shapes=[pltpu.VMEM((TILE + 2 * POOL_PAD, BW), F32)],
        compiler_params=_params(1),
        name="inproj",
    )(*xs, mods, w["g1"], *[w["win"]] * 5, w["wmla"], w["poolw"], w["pools"], consts["seg"],
      consts["ones"], consts["rotd"], consts["rotm"], w["dgq"], w["dgk"],
      consts["ropd"], consts["ropm"], consts["cc"], consts["ss"], consts["ft"],
      w["gqa"], w["wuq"], w["gqn"], w["gkv"], w["wuk"], w["wuv"], w["gkn"])


def _softmax_parts(q, ks):
    ss = [_dot_nt(q, k) for k in ks]
    m = ss[0].max(axis=-1, keepdims=True)
    for s in ss[1:]:
        m = jnp.maximum(m, s.max(axis=-1, keepdims=True))
    ps = [jnp.exp2(s - m) for s in ss]
    l = ps[0].sum(axis=-1, keepdims=True)
    for p in ps[1:]:
        l = l + p.sum(axis=-1, keepdims=True)
    return [p.astype(BF16) for p in ps], 1.0 / l


def _attend(qd, kds, vds, qm, kms, vms, lam, gsub, lam_init, ob_ref, od_ref):
    n = qd.shape[0]
    lane = lax.broadcasted_iota(jnp.int32, (n, BW), 1)
    zero = jnp.zeros((), BF16)
    out_b = jnp.zeros((n, BW), F32)
    out_d = jnp.zeros((n, BW), F32)
    for h in range(HEADS):
        head = (lane >= h * DA_HD) & (lane < (h + 1) * DA_HD)
        o = None
        for half in range(2):
            lo = h * DA_HD + half * DA_D
            qc = jnp.where((lane >= lo) & (lane < lo + DA_D), qd, zero)
            ps, rl = _softmax_parts(qc, kds)
            pv = sum(_dot(p, v) for p, v in zip(ps, vds)) * rl
            o = pv if half == 0 else o - lam * pv
        o = jnp.where(head, o, 0.0)
        ms = jnp.sum(o * o, axis=-1, keepdims=True) * (1.0 / DA_HD)
        out_b = out_b + o * lax.rsqrt(ms + EPS)
        sl = slice(h * LANES, (h + 1) * LANES)
        ps, rl = _softmax_parts(qm[:, sl], [k[:, sl] for k in kms])
        pv = sum(_dot(p, v) for p, v in zip(ps, vms)) * rl
        out_d = out_d + jnp.where(head, pv, 0.0)
    ob_ref[...] = (out_b * gsub * (1.0 - lam_init)).astype(BF16)
    od_ref[...] = out_d.astype(BF16)


def _attn_kernel(lam_init, qd_ref, kd_ref, vd_ref, cdk_ref, cdv_ref, qm_ref, km_ref, vm_ref, kc_ref, vc_ref,
                 dl_ref, gsub_ref, ob_ref, od_ref):
    i = pl.program_id(0)
    j = pl.program_id(1)
    dl = dl_ref[...]
    lam = (jnp.exp(jnp.sum(dl[0:1] * dl[1:2], axis=-1, keepdims=True))
           - jnp.exp(jnp.sum(dl[2:3] * dl[3:4], axis=-1, keepdims=True)) + lam_init)
    rows = pl.ds(pl.multiple_of(j * TQ, TQ), TQ)
    qd = qd_ref[rows, :]
    qm = qm_ref[rows, :]

    @pl.when(i < CTX_TILES)
    def _():
        _attend(qd, [kd_ref[rows, :]], [vd_ref[rows, :]], qm, [km_ref[rows, :]], [vm_ref[rows, :]],
                lam, gsub_ref[...], lam_init, ob_ref, od_ref)

    @pl.when(i >= CTX_TILES)
    def _():
        _attend(qd, [cdk_ref[...].astype(BF16), kd_ref[...]], [cdv_ref[...].astype(BF16), vd_ref[...]],
                qm, [kc_ref[...], km_ref[...]], [vc_ref[...], vm_ref[...]],
                lam, gsub_ref[...], lam_init, ob_ref, od_ref)


def _attn_call(l, qd, kd, vd, cdk, cdv, qm, km, vm, kc, vc, w):
    tile = lambda width: pl.BlockSpec((TILE, width), lambda i, j: (i, 0))
    cache = lambda width: pl.BlockSpec((None, None, PAST, width),
                                       lambda i, j: (jnp.maximum(i - CTX_TILES, 0), l, 0, 0))
    blk = pl.BlockSpec((TQ, BW), lambda i, j: (i * (TILE // TQ) + j, 0))
    return pl.pallas_call(
        functools.partial(_attn_kernel, _lambda_init(l)),
        grid=(TILES, TILE // TQ),
        in_specs=[tile(BW), tile(BW), tile(BW), cache(BW), cache(BW), tile(512), tile(512), tile(BW),
                  cache(512), cache(BW), _layer((4, DA_D), l), _layer((1, BW), l)],
        out_specs=[blk, blk],
        out_shape=[jax.ShapeDtypeStruct((ROWS, BW), BF16)] * 2,
        compiler_params=_params(2),
        name="attention",
    )(qd, kd, vd, cdk, cdv, qm, km, vm, kc, vc, w["dl"], w["gsub"])


def _merge_kernel(n_x, *refs):
    x_refs = refs[:n_x]
    mod_ref, g1_ref, wg_ref, wb_ref, wo_ref, pa_ref, pb_ref, pc_ref, pd_ref, o_ref = refs[n_x:]
    mod = mod_ref[0]
    x = _read_x(x_refs, pl.program_id(0) >= CTX_TILES)
    hb = _norm_mod(x, g1_ref[...], mod[:, D:2 * D], mod[:, 0:D]).astype(BF16)
    merged = None
    for n, br_ref in enumerate((pa_ref, pb_ref, pc_ref, pd_ref)):
        gate = _sigmoid(_dot(hb, wg_ref[:, n * D:(n + 1) * D]))
        t = gate * _dot(br_ref[...], wb_ref[n])
        merged = t if merged is None else merged + t
    o_ref[...] = x + mod[:, 2 * D:3 * D] * _dot(merged.astype(BF16), wo_ref[...])


def _merge_call(l, xs, mods, w, pa, pb, pc, pd):
    return pl.pallas_call(
        functools.partial(_merge_kernel, len(xs)),
        grid=(TILES,),
        in_specs=_x_specs(len(xs) == 2) + [
            _mod_spec(l), _layer((1, D), l), _layer((D, 4 * D), l), _layer((4, BW, D), l), _layer((D, D), l),
            _row_spec(BW), _row_spec(BW), _row_spec(BW), _row_spec(BW)],
        out_specs=_row_spec(D),
        out_shape=jax.ShapeDtypeStruct((ROWS, D), F32),
        compiler_params=_params(1),
        name="merge",
    )(*xs, mods, w["g1"], w["win"], w["wbr"], w["wout"], pa, pb, pc, pd)


def _ffn_kernel(n_o, x_ref, mod_ref, g2_ref, wu_ref, cv_ref, wd_ref, *refs):
    o_refs = refs[:n_o]
    hb_ref, acc_ref = refs[n_o:]
    is_lat = pl.program_id(0) >= CTX_TILES
    is_ctx = jnp.logical_not(is_lat)
    mod = mod_ref[0]
    hb_ref[...] = _norm_mod(x_ref[...], g2_ref[...], mod[:, 4 * D:5 * D], mod[:, 3 * D:4 * D]).astype(BF16)
    acc_ref[...] = jnp.zeros((TILE, D), F32)
    edge = jnp.zeros((FFN_EDGE, FFN_CHUNK), F32)

    def body(c, carry):
        c0 = pl.multiple_of(c * FFN_CHUNK, FFN_CHUNK)
        gate, val = pl.ds(c0, FFN_CHUNK), pl.ds(FFN + c0, FFN_CHUNK)
        cvs = cv_ref[:, gate], cv_ref[:, val]

        def up(r):
            hb = hb_ref[r * CTX_T:(r + 1) * CTX_T]
            return _dot(hb, wu_ref[:, gate]), _dot(hb, wu_ref[:, val])

        def conv(r, i, us):
            above = jnp.where(is_ctx, 0.0, us[r - 1][i][-FFN_EDGE:]) if r > 0 else edge
            below = jnp.where(is_ctx, 0.0, us[r + 1][i][:FFN_EDGE]) if r + 1 < TILE // CTX_T else edge
            ext = jnp.concatenate([above, us[r][i], below], axis=0)
            own = slice(FFN_EDGE, FFN_EDGE + CTX_T)
            prev = pltpu.roll(ext, 1, 0)[own]
            nxt = pltpu.roll(ext, ext.shape[0] - 1, 0)[own]
            cv = cvs[i]
            return prev * cv[0:1] + us[r][i] * cv[1:2] + nxt * cv[2:3] + cv[3:4]

        us = {0: up(0)}
        for r in range(TILE // CTX_T):
            if r + 1 < TILE // CTX_T:
                us[r + 1] = up(r + 1)
            h, v = conv(r, 0, us), conv(r, 1, us)
            act = (h * (1.0 + jnp.tanh(h)) * v).astype(BF16)
            acc_ref[r * CTX_T:(r + 1) * CTX_T] += _dot(act, wd_ref[gate, :])
        return carry

    lax.fori_loop(0, FFN_CHUNKS, body, 0)
    y = x_ref[...] + mod[:, 5 * D:6 * D] * acc_ref[...]
    if n_o == 1:
        o_refs[0][...] = y
    else:
        @pl.when(is_ctx)
        def _():
            o_refs[0][...] = y

        @pl.when(is_lat)
        def _():
            o_refs[1][...] = y


def _ffn_call(l, x, mods, w, split_out):
    if split_out:
        out_specs = _x_specs(True)
        out_shape = [jax.ShapeDtypeStruct((CTX_ROWS, D), F32), jax.ShapeDtypeStruct((LAT_ROWS, D), F32)]
    else:
        out_specs = _x_specs(False)
        out_shape = [jax.ShapeDtypeStruct((ROWS, D), F32)]
    return pl.pallas_call(
        functools.partial(_ffn_kernel, len(out_shape)),
        grid=(TILES,),
        in_specs=[_row_spec(D), _mod_spec(l), _layer((1, D), l),
                  _layer((D, 2 * FFN), l), _layer((8, 2 * FFN), l), _layer((FFN, D), l)],
        out_specs=out_specs,
        out_shape=out_shape,
        scratch_shapes=[pltpu.VMEM((TILE, D), BF16), pltpu.VMEM((TILE, D), F32)],
        compiler_params=_params(1),
        name="ffn",
    )(x, mods, w["g2"], w["wup"], w["cv"], w["wdown"])


def _rope_tables():
    half = 8
    inv = ROPE_BASE ** (-np.arange(half, dtype=np.float64) / half)
    t = np.arange(LAT_T)
    rows, cols = (t // GRID_W).astype(np.float64), (t % GRID_W).astype(np.float64)
    ang = np.concatenate([rows[:, None] * inv[None, :]] * 2 + [cols[:, None] * inv[None, :]] * 2, axis=1)
    cos, sin = np.cos(ang), np.sin(ang)

    def diff_layout(a):
        return np.tile(a, (1, LANES // 32))

    def mla_layout(a, fill):
        out = np.full((LAT_T, LANES), fill)
        out[:, MLA_NOPE:MLA_QK] = a
        return out

    ident = lambda layout: np.stack([layout(np.ones_like(cos)), layout(np.zeros_like(cos))])
    ropd = np.stack([ident(diff_layout), np.stack([diff_layout(cos), diff_layout(sin)])])
    ropm = np.stack([np.stack([mla_layout(np.ones_like(cos), 1.0), mla_layout(np.zeros_like(cos), 0.0)]),
                     np.stack([mla_layout(cos, 1.0), mla_layout(sin, 0.0)])])
    return jnp.asarray(ropd, F32), jnp.asarray(ropm, F32)


def _rotate_half_matrix(lanes):
    r = np.zeros((LANES, LANES))
    for j in lanes:
        if j % 16 < 8:
            r[j + 8, j] = -1.0
        else:
            r[j - 8, j] = 1.0
    return np.concatenate([r, r], axis=0)


def _matrix_tables():
    def cs(n):
        k = np.arange(n)
        ang = 2.0 * np.pi * ((k[:, None] * k[None, :]) % n) / n
        return np.cos(ang), np.sin(ang)

    c64, s64 = cs(64)
    eye4 = np.eye(4)
    cl, sl = cs(LAT_T)
    cc_, sc_ = cs(CTX_T)
    ft = np.stack([np.stack([np.kron(eye4, cc_), -np.kron(eye4, sc_)]), np.stack([cl, -sl])])
    as_bf16 = lambda a: jnp.asarray(a, F32).astype(BF16)
    return dict(
        cc=as_bf16(np.kron(eye4, c64)), ss=as_bf16(np.kron(eye4, s64)), ft=as_bf16(ft),
        seg=as_bf16(np.kron(np.eye(BW // DA_D), np.ones((DA_D, DA_D)))),
        ones=as_bf16(np.kron(np.eye(BW // LANES), np.ones((LANES, LANES)))),
        rotd=as_bf16(_rotate_half_matrix(range(LANES))),
        rotm=as_bf16(_rotate_half_matrix(range(MLA_NOPE, MLA_QK))))


def _pad_last(a, lo, hi):
    return jnp.pad(a, ((0, 0),) * (a.ndim - 1) + ((lo, hi),))


def _prep_weights(p):
    w_in = p["w_in"]
    mla = w_in[:, :, W_MLA:]
    cq, ckv, kr = mla[..., :Q_LORA], mla[..., Q_LORA:Q_LORA + KV_LORA], mla[..., Q_LORA + KV_LORA:]
    wmla = jnp.concatenate([_pad_last(cq, 0, BW - Q_LORA), ckv, _pad_last(kr, MLA_NOPE, LANES - MLA_QK)],
                           axis=-1).astype(BF16)
    poolw = jnp.einsum("lgce,gh->lgche", p["pool_w"], jnp.eye(4, dtype=F32)).reshape(DEPTH, BW, BW).astype(BF16)
    wuq = p["w_uq"].reshape(DEPTH, Q_LORA, HEADS, MLA_QK)
    wuq = jnp.pad(wuq, ((0, 0), (0, BW - Q_LORA), (0, 0), (0, LANES - MLA_QK))).reshape(DEPTH, BW, HEADS * LANES)
    wukv = p["w_ukv"].reshape(DEPTH, KV_LORA, HEADS, MLA_NOPE + MLA_V)
    wuk = _pad_last(wukv[..., :MLA_NOPE], 0, LANES - MLA_NOPE).reshape(DEPTH, KV_LORA, HEADS * LANES)
    wuv = wukv[..., MLA_NOPE:].reshape(DEPTH, KV_LORA, HEADS * MLA_V)
    cv = jnp.concatenate([p["conv_w"], p["conv_b"][:, None, :], jnp.zeros((DEPTH, 4, 2 * FFN), F32)], axis=1)
    cv = cv * jnp.where(jnp.arange(2 * FFN) < FFN, 0.5, 1.0)

    row = lambda a, width: _pad_last(a, 0, width - a.shape[-1])[:, None, :]
    tiled = lambda a, reps: jnp.tile(a, (1, reps))[:, None, :]
    return {
        "g1": row(p["g_norm1"], D), "g2": row(p["g_norm2"], D),
        "win": w_in.astype(BF16), "wmla": wmla,
        "poolw": poolw, "pools": row(p["pool_scale"], BW),
        "dgq": tiled(p["diff_qnorm"], LANES // DA_D), "dgk": tiled(p["diff_knorm"], LANES // DA_D),
        "dl": p["diff_lambda"], "gsub": tiled(p["diff_subln"], HEADS),
        "gqa": row(p["mla_qa_norm"], BW), "wuq": wuq.astype(BF16), "gqn": row(p["mla_qnorm"], LANES),
        "gkv": row(p["mla_kv_norm"], KV_LORA), "wuk": wuk.astype(BF16), "wuv": wuv.astype(BF16),
        "gkn": row(p["mla_knorm"], LANES),
        "wbr": p["w_branch"].astype(BF16), "wout": p["w_out"].astype(BF16),
        "wup": p["w_up"].astype(BF16), "cv": cv, "wdown": p["w_down"].astype(BF16),
    }


def kernel(x_prompt, x_sample, c, c_ctx, cache_diff_k, cache_diff_v, cache_mla_ckv, cache_mla_krope, w_ada, b_ada, g_norm1, g_norm2, w_in, pool_w, pool_scale, diff_qnorm, diff_knorm, diff_lambda, diff_subln, mla_qa_norm, w_uq, mla_kv_norm, w_ukv, mla_qnorm, mla_knorm, w_branch, w_out, w_up, conv_w, conv_b, w_down):
    w = _prep_weights(dict(
        w_in=w_in, pool_w=pool_w, pool_scale=pool_scale, diff_qnorm=diff_qnorm, diff_knorm=diff_knorm,
        diff_lambda=diff_lambda, diff_subln=diff_subln, mla_qa_norm=mla_qa_norm, w_uq=w_uq,
        mla_kv_norm=mla_kv_norm, w_ukv=w_ukv, mla_qnorm=mla_qnorm, mla_knorm=mla_knorm, w_branch=w_branch,
        w_out=w_out, w_up=w_up, conv_w=conv_w, conv_b=conv_b, w_down=w_down, g_norm1=g_norm1,
        g_norm2=g_norm2))
    ropd, ropm = _rope_tables()
    consts = dict(_matrix_tables(), ropd=ropd, ropm=ropm)

    cond = jnp.concatenate([c_ctx[None, :], c, jnp.zeros((16 - 1 - LAT_B, D), F32)], axis=0)
    mods = _mod_call(cond, w_ada, b_ada).reshape(DEPTH, 16, 1, 6 * D)

    kc, vc = _ctxkv_call(cache_mla_ckv, _pad_last(cache_mla_krope, MLA_NOPE, LANES - MLA_QK), w)
    cdk = cache_diff_k.reshape(LAT_B, DEPTH, PAST, BW)
    cdv = cache_diff_v.reshape(LAT_B, DEPTH, PAST, BW)

    xs = [x_prompt.reshape(CTX_ROWS, D), x_sample.reshape(LAT_ROWS, D)]
    new = [[], [], [], []]
    for l in range(DEPTH):
        pa, qd, kd, vd, oc, qm, km, vm, *ctx_outs = _inproj_call(l, xs, mods, w, consts)
        ob, od = _attn_call(l, qd, kd, vd, cdk, cdv, qm, km, vm, kc, vc, w)
        x = _merge_call(l, xs, mods, w, pa, ob, oc, od)
        xs = _ffn_call(l, x, mods, w, split_out=(l == DEPTH - 1))
        for acc, a in zip(new, ctx_outs):
            acc.append(a)

    stack = lambda arrs, shape: jnp.stack(arrs, axis=1).reshape((CTX_B, DEPTH, CTX_T) + shape)
    return (xs[0].reshape(CTX_B, CTX_T, D), xs[1].reshape(LAT_B, LAT_T, D),
            stack([a.reshape(CTX_B, CTX_T, BW) for a in new[0]], (HEADS, DA_HD)),
            stack([a.reshape(CTX_B, CTX_T, BW) for a in new[1]], (HEADS, DA_HD)),
            stack([a.reshape(CTX_B, CTX_T, KV_LORA) for a in new[2]], (KV_LORA,)),
            stack([a.reshape(CTX_B, CTX_T, MLA_ROPE) for a in new[3]], (MLA_ROPE,)))
```

```python
import functools
import math

import jax
import jax.numpy as jnp
import numpy as np
from jax import lax
from jax.experimental import pallas as pl
from jax.experimental.pallas import tpu as pltpu

F32 = jnp.float32
BF16 = jnp.bfloat16

D = 1024
DEPTH = 4
CTX_B, CTX_T = 16, 256
LAT_B, LAT_T = 8, 1024
PAST = 512
TILE = 1024
CTX_TILES = CTX_B * CTX_T // TILE
LAT_TILES = LAT_B * LAT_T // TILE
TILES = CTX_TILES + LAT_TILES
ROWS = TILES * TILE
CTX_ROWS = CTX_TILES * TILE
LAT_ROWS = LAT_TILES * TILE
GRID_W = 64
BW = 256
DA_D = 32
DA_HD = 64
HEADS = 4
MLA_NOPE, MLA_ROPE, MLA_V, MLA_QK = 64, 32, 64, 96
Q_LORA, KV_LORA = 192, 128
FFN = 2816
FFN_CHUNK = 256
FFN_CHUNKS = FFN // FFN_CHUNK
FFN_EDGE = 8
POOL_HALF = (1, 2, 4, 8)
POOL_PAD = 16
ROPE_BASE = 10000.0
EPS = 1e-6
LOG2E = 1.4426950408889634
TQ = 256
LANES = 128
VMEM_LIMIT = 56 * 1024 * 1024

W_BLOCKS = 4 * D // BW
W_MLA = 4 * D + 5 * BW
C_CQ, C_CKV, C_KR, C_END = 0, 256, 384, 512


def _lambda_init(l):
    return 0.8 - 0.6 * math.exp(-0.3 * l)


def _sigmoid(x):
    return 0.5 * jnp.tanh(0.5 * x) + 0.5


def _rms(x, n):
    ms = jnp.sum(x * x, axis=-1, keepdims=True) * (1.0 / n)
    return x * lax.rsqrt(ms + EPS)


def _dot(a, b):
    return jnp.dot(a, b, preferred_element_type=F32)


def _dot_nt(a, b):
    return lax.dot_general(a, b, (((1,), (1,)), ((), ())), preferred_element_type=F32)


def _const(shape):
    n = len(shape)
    return pl.BlockSpec(shape, lambda *_: (0,) * n, pipeline_mode=pl.Buffered(1))


def _layer(shape, l):
    n = len(shape)
    return pl.BlockSpec((None,) + shape, lambda *_: (l,) + (0,) * n, pipeline_mode=pl.Buffered(1))


def _params(n_grid):
    return pltpu.CompilerParams(dimension_semantics=("arbitrary",) * n_grid, vmem_limit_bytes=VMEM_LIMIT)


def _row_spec(width):
    return pl.BlockSpec((TILE, width), lambda i: (i, 0))


def _ctx_row_spec(width):
    return pl.BlockSpec((TILE, width), lambda i: (jnp.minimum(i, CTX_TILES - 1), 0))


def _lat_row_spec(width):
    return pl.BlockSpec((TILE, width), lambda i: (jnp.maximum(i - CTX_TILES, 0), 0))


def _x_specs(split):
    return [_ctx_row_spec(D), _lat_row_spec(D)] if split else [_row_spec(D)]


def _read_x(x_refs, is_lat):
    if len(x_refs) == 2:
        return jnp.where(is_lat, x_refs[1][...], x_refs[0][...])
    return x_refs[0][...]


def _in_cols(l, n):
    return pl.BlockSpec((None, BW, D), lambda *_: (l, n, 0), pipeline_mode=pl.Buffered(1))


def _mod_spec(l):
    return pl.BlockSpec((None, 1, 1, 6 * D), lambda i: (l, jnp.maximum(i - (CTX_TILES - 1), 0), 0, 0))


def _group_spec(shape):
    n = len(shape)
    return pl.BlockSpec((None,) + shape, lambda i: (jnp.minimum(i // CTX_TILES, 1),) + (0,) * n,
                        pipeline_mode=pl.Buffered(1))


def _mod_kernel(c_ref, w_ref, b_ref, o_ref):
    c = c_ref[...]
    s = (c * _sigmoid(c)).astype(BF16)
    o_ref[0] = _dot(s, w_ref[0].astype(BF16)) + b_ref[0]


def _mod_call(cond, w_ada, b_ada):
    nc = 4
    cw = 6 * D // nc
    return pl.pallas_call(
        _mod_kernel,
        grid=(DEPTH, nc),
        in_specs=[
            pl.BlockSpec((16, D), lambda l, j: (0, 0)),
            pl.BlockSpec((1, D, cw), lambda l, j: (l, 0, j)),
            pl.BlockSpec((1, 1, cw), lambda l, j: (l, 0, j)),
        ],
        out_specs=pl.BlockSpec((1, 16, cw), lambda l, j: (l, 0, j)),
        out_shape=jax.ShapeDtypeStruct((DEPTH, 16, 6 * D), F32),
        compiler_params=_params(2),
        name="adaln_mod",
    )(cond, w_ada, b_ada.reshape(DEPTH, 1, 6 * D))


def _mla_key_heads(knope, kr_pad, gkn):
    out = []
    for h in range(HEADS):
        kh = knope[:, h * LANES:(h + 1) * LANES] + kr_pad
        out.append(_rms(kh, MLA_QK) * gkn)
    return out


def _ctxkv_kernel(ckv_ref, kr_ref, wuk_ref, wuv_ref, gkn_ref, kc_ref, vc_ref):
    cb = ckv_ref[...].astype(BF16)
    knope = _dot(cb, wuk_ref[...])
    heads = _mla_key_heads(knope, kr_ref[...], gkn_ref[...])
    for h in range(HEADS):
        kc_ref[:, h * LANES:(h + 1) * LANES] = heads[h].astype(BF16)
    vc_ref[...] = _dot(cb, wuv_ref[...]).astype(BF16)


def _ctxkv_call(ckv, kr_pad, w):
    cache = lambda width: pl.BlockSpec((None, None, PAST, width), lambda b, l: (b, l, 0, 0))
    wspec = lambda r, c: pl.BlockSpec((None, r, c), lambda b, l: (l, 0, 0))
    return pl.pallas_call(
        _ctxkv_kernel,
        grid=(LAT_B, DEPTH),
        in_specs=[cache(KV_LORA), cache(LANES), wspec(KV_LORA, 512), wspec(KV_LORA, BW), wspec(1, LANES)],
        out_specs=[cache(512), cache(BW)],
        out_shape=[
            jax.ShapeDtypeStruct((LAT_B, DEPTH, PAST, 512), BF16),
            jax.ShapeDtypeStruct((LAT_B, DEPTH, PAST, BW), BF16),
        ],
        compiler_params=_params(2),
        name="mla_ctx_keys",
    )(ckv, kr_pad, w["wuk"], w["wuv"], w["gkn"])


def _norm_mod(x, g, sc, sh):
    return (_rms(x, D) * g) * (1.0 + sc) + sh


def _rope(x, tab_ref, rot):
    hi = x.astype(BF16)
    pieces = jnp.concatenate([hi, (x - hi.astype(F32)).astype(BF16)], axis=1)
    return x * tab_ref[0] + _dot(pieces, rot) * tab_ref[1]


def _group_inv_rms(x, ones, n):
    sq = (x * x).astype(BF16)
    ms = jnp.concatenate([_dot(sq[:, c:c + BW], ones) for c in range(0, x.shape[1], BW)], axis=1)
    return lax.rsqrt(ms * (1.0 / n) + EPS)


def _inproj_kernel(n_x, *refs):
    x_refs = refs[:n_x]
    (mod_ref, g1_ref, wxa_ref, wq_ref, wk_ref, wv_ref, wxc_ref, wmla_ref,
     poolw_ref, pools_ref, seg_ref, ones_ref, rotd_ref, rotm_ref, dgq_ref, dgk_ref,
     ropd_ref, ropm_ref, cc_ref, ss_ref, ft_ref, gqa_ref, wuq_ref, gqn_ref, gkv_ref,
     wuk_ref, wuv_ref, gkn_ref,
     pa_ref, qd_ref, kd_ref, vd_ref, oc_ref, qm_ref, km_ref, vm_ref,
     kcf_ref, vf_ref, ckv_ref, kr_ref, pad_ref) = refs[n_x:]
    is_lat = pl.program_id(0) >= CTX_TILES
    mod = mod_ref[0]
    hb = _norm_mod(_read_x(x_refs, is_lat), g1_ref[...], mod[:, D:2 * D], mod[:, 0:D]).astype(BF16)

    xa = _dot_nt(hb, wxa_ref[...])
    pad_ref[0:POOL_PAD, :] = jnp.zeros((POOL_PAD, BW), F32)
    pad_ref[POOL_PAD + TILE:, :] = jnp.zeros((POOL_PAD, BW), F32)
    pad_ref[POOL_PAD:POOL_PAD + TILE, :] = xa
    tm1 = jnp.where(is_lat, LAT_T - 1, CTX_T - 1)
    pos = lax.broadcasted_iota(jnp.int32, (TILE, BW), 0) & tm1
    lane = lax.broadcasted_iota(jnp.int32, (TILE, BW), 1)

    def term(s):
        v = pad_ref[POOL_PAD + s:POOL_PAD + s + TILE, :]
        if s < 0:
            return jnp.where(pos >= -s, v, 0.0)
        return jnp.where(pos + s <= tm1, v, 0.0)

    acc = xa
    wins, cnts = [], []
    lo_s, hi_s = 0, 0
    for half in POOL_HALF:
        for s in list(range(-half, lo_s)) + list(range(hi_s + 1, half)):
            acc = acc + term(s)
        lo_s, hi_s = -half, half - 1
        wins.append(acc)
        cnts.append((jnp.minimum(pos + half, tm1 + 1) - jnp.maximum(pos - half, 0)).astype(F32))
    win = jnp.where(lane < 64, wins[0], jnp.where(lane < 128, wins[1], jnp.where(lane < 192, wins[2], wins[3])))
    cnt = jnp.where(lane < 64, cnts[0], jnp.where(lane < 128, cnts[1], jnp.where(lane < 192, cnts[2], cnts[3])))
    pooled = (win / cnt - xa).astype(BF16)
    pa_ref[...] = (_dot(pooled, poolw_ref[...]) * pools_ref[...]).astype(BF16)

    qscale = DA_D ** -0.5 * LOG2E
    q = _dot_nt(hb, wq_ref[...])
    k = _dot_nt(hb, wk_ref[...])
    v = _dot_nt(hb, wv_ref[...])
    rq = _group_inv_rms(q, seg_ref[...], DA_D) * qscale
    rk = _group_inv_rms(k, seg_ref[...], DA_D)
    kns = []
    for s in range(BW // LANES):
        sl = slice(s * LANES, (s + 1) * LANES)
        kg = k[:, sl] * dgk_ref[...]
        kns.append(kg * rk[:, sl])
        qd_ref[:, sl] = (_rope(q[:, sl] * dgq_ref[...], ropd_ref, rotd_ref[...]) * rq[:, sl]).astype(BF16)
        kd_ref[:, sl] = (_rope(kg, ropd_ref, rotd_ref[...]) * rk[:, sl]).astype(BF16)
    vd_ref[...] = v.astype(BF16)

    xcb = _dot_nt(hb, wxc_ref[...]).astype(BF16)
    xcc = _dot(xcb, cc_ref[...]).astype(BF16)
    xcs = _dot(xcb, ss_ref[...]).astype(BF16)
    y = _dot(ft_ref[0], xcc) + _dot(ft_ref[1], xcs)
    fnorm = jnp.where(is_lat, (LAT_T * 64.0) ** -0.5, (CTX_T * 64.0) ** -0.5)
    oc_ref[...] = (y * fnorm).astype(BF16)

    cq = _dot(hb, wmla_ref[:, C_CQ:C_CQ + BW])
    qa = (_rms(cq, Q_LORA) * gqa_ref[...]).astype(BF16)
    qf = _dot(qa, wuq_ref[...])
    mscale = MLA_QK ** -0.5 * LOG2E
    rqm = _group_inv_rms(qf, ones_ref[...], MLA_QK) * mscale
    for h in range(HEADS):
        sl = slice(h * LANES, (h + 1) * LANES)
        qm_ref[:, sl] = (_rope(qf[:, sl] * gqn_ref[...], ropm_ref, rotm_ref[...]) * rqm[:, sl]).astype(BF16)
    ckv = _dot(hb, wmla_ref[:, C_CKV:C_CKV + KV_LORA])
    ckv_n = _rms(ckv, KV_LORA) * gkv_ref[...]
    cb = ckv_n.astype(BF16)
    kr_pad = _dot(hb, wmla_ref[:, C_KR:C_KR + LANES])
    knope = _dot(cb, wuk_ref[...])
    kr_rot = _rope(kr_pad * gkn_ref[...], ropm_ref, rotm_ref[...])
    kr_ssq = _dot((kr_pad * kr_pad).astype(BF16), ones_ref[0:LANES, 0:LANES])
    kn_ssq = jnp.concatenate([_dot((knope[:, c:c + BW] * knope[:, c:c + BW]).astype(BF16), ones_ref[...])
                              for c in range(0, HEADS * LANES, BW)], axis=1)
    for h in range(HEADS):
        sl = slice(h * LANES, (h + 1) * LANES)
        rkm = lax.rsqrt((kn_ssq[:, sl] + kr_ssq) * (1.0 / MLA_QK) + EPS)
        km_ref[:, sl] = ((knope[:, sl] * gkn_ref[...] + kr_rot) * rkm).astype(BF16)
    vm_ref[...] = _dot(cb, wuv_ref[...]).astype(BF16)

    @pl.when(jnp.logical_not(is_lat))
    def _():
        for s, kn in enumerate(kns):
            kcf_ref[:, s * LANES:(s + 1) * LANES] = kn
        vf_ref[...] = v
        ckv_ref[...] = ckv_n
        kr_ref[...] = kr_pad[:, MLA_NOPE:MLA_NOPE + MLA_ROPE]


def _inproj_call(l, xs, mods, w, consts):
    row_outs = [(BW, BF16), (BW, BF16), (BW, BF16), (BW, BF16), (BW, BF16), (512, BF16), (512, BF16), (BW, BF16)]
    ctx_outs = [(BW, F32), (BW, F32), (KV_LORA, F32), (MLA_ROPE, F32)]
    return pl.pallas_call(
        functools.partial(_inproj_kernel, len(xs)),
        grid=(TILES,),
        in_specs=_x_specs(len(xs) == 2) + [
            _mod_spec(l), _layer((1, D), l), *[_in_cols(l, W_BLOCKS + n) for n in range(5)],
            _layer((D, C_END), l), _layer((BW, BW), l), _layer((1, BW), l),
            _const((BW, BW)), _const((BW, BW)), _const((BW, LANES)), _const((BW, LANES)),
            _layer((1, LANES), l), _layer((1, LANES), l),
            _group_spec((2, TILE, LANES)), _group_spec((2, TILE, LANES)),
            _const((BW, BW)), _const((BW, BW)), _group_spec((2, TILE, TILE)),
            _layer((1, BW), l), _layer((BW, 512), l), _layer((1, LANES), l), _layer((1, KV_LORA), l),
            _layer((KV_LORA, 512), l), _layer((KV_LORA, BW), l), _layer((1, LANES), l),
        ],
        out_specs=[_row_spec(wd) for wd, _ in row_outs] + [_ctx_row_spec(wd) for wd, _ in ctx_outs],
        out_shape=([jax.ShapeDtypeStruct((ROWS, wd), dt) for wd, dt in row_outs]
                   + [jax.ShapeDtypeStruct((CTX_ROWS, wd), dt) for wd, dt in ctx_outs]),
        scratch_shapes=[pltpu.VMEM((TILE + 2 * POOL_PAD, BW), F32)],
        compiler_params=_params(1),
        name="inproj",
    )(*xs, mods, w["g1"], *[w["win"]] * 5, w["wmla"], w["poolw"], w["pools"], consts["seg"],
      consts["ones"], consts["rotd"], consts["rotm"], w["dgq"], w["dgk"],
      consts["ropd"], consts["ropm"], consts["cc"], consts["ss"], consts["ft"],
      w["gqa"], w["wuq"], w["gqn"], w["gkv"], w["wuk"], w["wuv"], w["gkn"])


def _softmax_parts(q, ks):
    ss = [_dot_nt(q, k) for k in ks]
    m = ss[0].max(axis=-1, keepdims=True)
    for s in ss[1:]:
        m = jnp.maximum(m, s.max(axis=-1, keepdims=True))
    ps = [jnp.exp2(s - m) for s in ss]
    l = ps[0].sum(axis=-1, keepdims=True)
    for p in ps[1:]:
        l = l + p.sum(axis=-1, keepdims=True)
    return [p.astype(BF16) for p in ps], 1.0 / l


def _attend(qd, kds, vds, qm, kms, vms, lam, gsub, lam_init, ob_ref, od_ref):
    n = qd.shape[0]
    lane = lax.broadcasted_iota(jnp.int32, (n, BW), 1)
    zero = jnp.zeros((), BF16)
    out_b = jnp.zeros((n, BW), F32)
    out_d = jnp.zeros((n, BW), F32)
    for h in range(HEADS):
        head = (lane >= h * DA_HD) & (lane < (h + 1) * DA_HD)
        o = None
        for half in range(2):
            lo = h * DA_HD + half * DA_D
            qc = jnp.where((lane >= lo) & (lane < lo + DA_D), qd, zero)
            ps, rl = _softmax_parts(qc, kds)
            pv = sum(_dot(p, v) for p, v in zip(ps, vds)) * rl
            o = pv if half == 0 else o - lam * pv
        o = jnp.where(head, o, 0.0)
        ms = jnp.sum(o * o, axis=-1, keepdims=True) * (1.0 / DA_HD)
        out_b = out_b + o * lax.rsqrt(ms + EPS)
        sl = slice(h * LANES, (h + 1) * LANES)
        ps, rl = _softmax_parts(qm[:, sl], [k[:, sl] for k in kms])
        pv = sum(_dot(p, v) for p, v in zip(ps, vms)) * rl
        out_d = out_d + jnp.where(head, pv, 0.0)
    ob_ref[...] = (out_b * gsub * (1.0 - lam_init)).astype(BF16)
    od_ref[...] = out_d.astype(BF16)


def _attn_kernel(lam_init, qd_ref, kd_ref, vd_ref, cdk_ref, cdv_ref, qm_ref, km_ref, vm_ref, kc_ref, vc_ref,
                 dl_ref, gsub_ref, ob_ref, od_ref):
    i = pl.program_id(0)
    j = pl.program_id(1)
    dl = dl_ref[...]
    lam = (jnp.exp(jnp.sum(dl[0:1] * dl[1:2], axis=-1, keepdims=True))
           - jnp.exp(jnp.sum(dl[2:3] * dl[3:4], axis=-1, keepdims=True)) + lam_init)
    rows = pl.ds(pl.multiple_of(j * TQ, TQ), TQ)
    qd = qd_ref[rows, :]
    qm = qm_ref[rows, :]

    @pl.when(i < CTX_TILES)
    def _():
        _attend(qd, [kd_ref[rows, :]], [vd_ref[rows, :]], qm, [km_ref[rows, :]], [vm_ref[rows, :]],
                lam, gsub_ref[...], lam_init, ob_ref, od_ref)

    @pl.when(i >= CTX_TILES)
    def _():
        _attend(qd, [cdk_ref[...], kd_ref[...]], [cdv_ref[...], vd_ref[...]],
                qm, [kc_ref[...], km_ref[...]], [vc_ref[...], vm_ref[...]],
                lam, gsub_ref[...], lam_init, ob_ref, od_ref)


def _attn_call(l, qd, kd, vd, cdk, cdv, qm, km, vm, kc, vc, w):
    tile = lambda width: pl.BlockSpec((TILE, width), lambda i, j: (i, 0))
    cache = lambda width: pl.BlockSpec((None, None, PAST, width),
                                       lambda i, j: (jnp.maximum(i - CTX_TILES, 0), l, 0, 0))
    blk = pl.BlockSpec((TQ, BW), lambda i, j: (i * (TILE // TQ) + j, 0))
    return pl.pallas_call(
        functools.partial(_attn_kernel, _lambda_init(l)),
        grid=(TILES, TILE // TQ),
        in_specs=[tile(BW), tile(BW), tile(BW), cache(BW), cache(BW), tile(512), tile(512), tile(BW),
                  cache(512), cache(BW), _layer((4, DA_D), l), _layer((1, BW), l)],
        out_specs=[blk, blk],
        out_shape=[jax.ShapeDtypeStruct((ROWS, BW), BF16)] * 2,
        compiler_params=_params(2),
        name="attention",
    )(qd, kd, vd, cdk, cdv, qm, km, vm, kc, vc, w["dl"], w["gsub"])


def _merge_kernel(n_x, *refs):
    x_refs = refs[:n_x]
    mod_ref, g1_ref, wg_ref, wb_ref, wo_ref, pa_ref, pb_ref, pc_ref, pd_ref, o_ref = refs[n_x:]
    mod = mod_ref[0]
    x = _read_x(x_refs, pl.program_id(0) >= CTX_TILES)
    hb = _norm_mod(x, g1_ref[...], mod[:, D:2 * D], mod[:, 0:D]).astype(BF16)
    merged = None
    for n, br_ref in enumerate((pa_ref, pb_ref, pc_ref, pd_ref)):
        gate = _sigmoid(_dot_nt(hb, wg_ref[n * D:(n + 1) * D, :]))
        t = gate * _dot(br_ref[...], wb_ref[n])
        merged = t if merged is None else merged + t
    o_ref[...] = x + mod[:, 2 * D:3 * D] * _dot(merged.astype(BF16), wo_ref[...])


def _merge_call(l, xs, mods, w, pa, pb, pc, pd):
    return pl.pallas_call(
        functools.partial(_merge_kernel, len(xs)),
        grid=(TILES,),
        in_specs=_x_specs(len(xs) == 2) + [
            _mod_spec(l), _layer((1, D), l), _layer((4 * D, D), l), _layer((4, BW, D), l), _layer((D, D), l),
            _row_spec(BW), _row_spec(BW), _row_spec(BW), _row_spec(BW)],
        out_specs=_row_spec(D),
        out_shape=jax.ShapeDtypeStruct((ROWS, D), F32),
        compiler_params=_params(1),
        name="merge",
    )(*xs, mods, w["g1"], w["win"], w["wbr"], w["wout"], pa, pb, pc, pd)


def _ffn_kernel(n_o, x_ref, mod_ref, g2_ref, wu_ref, cv_ref, wd_ref, *refs):
    o_refs = refs[:n_o]
    hb_ref, acc_ref = refs[n_o:]
    is_lat = pl.program_id(0) >= CTX_TILES
    is_ctx = jnp.logical_not(is_lat)
    mod = mod_ref[0]
    hb_ref[...] = _norm_mod(x_ref[...], g2_ref[...], mod[:, 4 * D:5 * D], mod[:, 3 * D:4 * D]).astype(BF16)
    acc_ref[...] = jnp.zeros((TILE, D), F32)
    row8 = lax.broadcasted_iota(jnp.int32, (FFN_EDGE, FFN_CHUNK), 0)
    tile_start, tile_end = row8 == 0, row8 == FFN_EDGE - 1
    seq_start = row8 == jnp.where(is_ctx, 0, -1)
    seq_end = row8 == jnp.where(is_ctx, FFN_EDGE - 1, -1)

    def conv(cols):
        u = _dot(hb_ref[...], wu_ref[:, cols])
        cv = cv_ref[:, cols]
        prev, nxt = pltpu.roll(u, 1, 0), pltpu.roll(u, TILE - 1, 0)
        prev_parts, nxt_parts = [], []
        for r0 in range(0, TILE, CTX_T):
            r1 = r0 + CTX_T
            start = tile_start if r0 == 0 else seq_start
            end = tile_end if r1 == TILE else seq_end
            prev_parts += [jnp.where(start, 0.0, prev[r0:r0 + FFN_EDGE]), prev[r0 + FFN_EDGE:r1]]
            nxt_parts += [nxt[r0:r1 - FFN_EDGE], jnp.where(end, 0.0, nxt[r1 - FFN_EDGE:r1])]
        prev, nxt = jnp.concatenate(prev_parts, axis=0), jnp.concatenate(nxt_parts, axis=0)
        return prev * cv[0:1] + u * cv[1:2] + nxt * cv[2:3] + cv[3:4]

    def body(c, carry):
        c0 = pl.multiple_of(c * FFN_CHUNK, FFN_CHUNK)
        h = conv(pl.ds(c0, FFN_CHUNK))
        v = conv(pl.ds(FFN + c0, FFN_CHUNK))
        act = (h * (1.0 + jnp.tanh(h)) * v).astype(BF16)
        acc_ref[...] += _dot(act, wd_ref[pl.ds(c0, FFN_CHUNK), :])
        return carry

    lax.fori_loop(0, FFN_CHUNKS, body, 0)
    y = x_ref[...] + mod[:, 5 * D:6 * D] * acc_ref[...]
    if n_o == 1:
        o_refs[0][...] = y
    else:
        @pl.when(is_ctx)
        def _():
            o_refs[0][...] = y

        @pl.when(is_lat)
        def _():
            o_refs[1][...] = y


def _ffn_call(l, x, mods, w, split_out):
    if split_out:
        out_specs = _x_specs(True)
        out_shape = [jax.ShapeDtypeStruct((CTX_ROWS, D), F32), jax.ShapeDtypeStruct((LAT_ROWS, D), F32)]
    else:
        out_specs = _x_specs(False)
        out_shape = [jax.ShapeDtypeStruct((ROWS, D), F32)]
    return pl.pallas_call(
        functools.partial(_ffn_kernel, len(out_shape)),
        grid=(TILES,),
        in_specs=[_row_spec(D), _mod_spec(l), _layer((1, D), l),
                  _layer((D, 2 * FFN), l), _layer((8, 2 * FFN), l), _layer((FFN, D), l)],
        out_specs=out_specs,
        out_shape=out_shape,
        scratch_shapes=[pltpu.VMEM((TILE, D), BF16), pltpu.VMEM((TILE, D), F32)],
        compiler_params=_params(1),
        name="ffn",
    )(x, mods, w["g2"], w["wup"], w["cv"], w["wdown"])


def _rope_tables():
    half = 8
    inv = ROPE_BASE ** (-np.arange(half, dtype=np.float64) / half)
    t = np.arange(LAT_T)
    rows, cols = (t // GRID_W).astype(np.float64), (t % GRID_W).astype(np.float64)
    ang = np.concatenate([rows[:, None] * inv[None, :]] * 2 + [cols[:, None] * inv[None, :]] * 2, axis=1)
    cos, sin = np.cos(ang), np.sin(ang)

    def diff_layout(a):
        return np.tile(a, (1, LANES // 32))

    def mla_layout(a, fill):
        out = np.full((LAT_T, LANES), fill)
        out[:, MLA_NOPE:MLA_QK] = a
        return out

    ident = lambda layout: np.stack([layout(np.ones_like(cos)), layout(np.zeros_like(cos))])
    ropd = np.stack([ident(diff_layout), np.stack([diff_layout(cos), diff_layout(sin)])])
    ropm = np.stack([np.stack([mla_layout(np.ones_like(cos), 1.0), mla_layout(np.zeros_like(cos), 0.0)]),
                     np.stack([mla_layout(cos, 1.0), mla_layout(sin, 0.0)])])
    return jnp.asarray(ropd, F32), jnp.asarray(ropm, F32)


def _rotate_half_matrix(lanes):
    r = np.zeros((LANES, LANES))
    for j in lanes:
        if j % 16 < 8:
            r[j + 8, j] = -1.0
        else:
            r[j - 8, j] = 1.0
    return np.concatenate([r, r], axis=0)


def _matrix_tables():
    def cs(n):
        k = np.arange(n)
        ang = 2.0 * np.pi * ((k[:, None] * k[None, :]) % n) / n
        return np.cos(ang), np.sin(ang)

    c64, s64 = cs(64)
    eye4 = np.eye(4)
    cl, sl = cs(LAT_T)
    cc_, sc_ = cs(CTX_T)
    ft = np.stack([np.stack([np.kron(eye4, cc_), -np.kron(eye4, sc_)]), np.stack([cl, -sl])])
    as_bf16 = lambda a: jnp.asarray(a, F32).astype(BF16)
    return dict(
        cc=as_bf16(np.kron(eye4, c64)), ss=as_bf16(np.kron(eye4, s64)), ft=as_bf16(ft),
        seg=as_bf16(np.kron(np.eye(BW // DA_D), np.ones((DA_D, DA_D)))),
        ones=as_bf16(np.kron(np.eye(BW // LANES), np.ones((LANES, LANES)))),
        rotd=as_bf16(_rotate_half_matrix(range(LANES))),
        rotm=as_bf16(_rotate_half_matrix(range(MLA_NOPE, MLA_QK))))


def _pad_last(a, lo, hi):
    return jnp.pad(a, ((0, 0),) * (a.ndim - 1) + ((lo, hi),))


def _prep_weights(p):
    w_in = p["w_in"]
    mla = w_in[:, :, W_MLA:]
    cq, ckv, kr = mla[..., :Q_LORA], mla[..., Q_LORA:Q_LORA + KV_LORA], mla[..., Q_LORA + KV_LORA:]
    wmla = jnp.concatenate([_pad_last(cq, 0, BW - Q_LORA), ckv, _pad_last(kr, MLA_NOPE, LANES - MLA_QK)],
                           axis=-1).astype(BF16)
    poolw = jnp.einsum("lgce,gh->lgche", p["pool_w"], jnp.eye(4, dtype=F32)).reshape(DEPTH, BW, BW).astype(BF16)
    wuq = p["w_uq"].reshape(DEPTH, Q_LORA, HEADS, MLA_QK)
    wuq = jnp.pad(wuq, ((0, 0), (0, BW - Q_LORA), (0, 0), (0, LANES - MLA_QK))).reshape(DEPTH, BW, HEADS * LANES)
    wukv = p["w_ukv"].reshape(DEPTH, KV_LORA, HEADS, MLA_NOPE + MLA_V)
    wuk = _pad_last(wukv[..., :MLA_NOPE], 0, LANES - MLA_NOPE).reshape(DEPTH, KV_LORA, HEADS * LANES)
    wuv = wukv[..., MLA_NOPE:].reshape(DEPTH, KV_LORA, HEADS * MLA_V)
    cv = jnp.concatenate([p["conv_w"], p["conv_b"][:, None, :], jnp.zeros((DEPTH, 4, 2 * FFN), F32)], axis=1)
    cv = cv * jnp.where(jnp.arange(2 * FFN) < FFN, 0.5, 1.0)

    row = lambda a, width: _pad_last(a, 0, width - a.shape[-1])[:, None, :]
    tiled = lambda a, reps: jnp.tile(a, (1, reps))[:, None, :]
    return {
        "g1": row(p["g_norm1"], D), "g2": row(p["g_norm2"], D),
        "win": jnp.swapaxes(w_in, 1, 2).astype(BF16), "wmla": wmla,
        "poolw": poolw, "pools": row(p["pool_scale"], BW),
        "dgq": tiled(p["diff_qnorm"], LANES // DA_D), "dgk": tiled(p["diff_knorm"], LANES // DA_D),
        "dl": p["diff_lambda"], "gsub": tiled(p["diff_subln"], HEADS),
        "gqa": row(p["mla_qa_norm"], BW), "wuq": wuq.astype(BF16), "gqn": row(p["mla_qnorm"], LANES),
        "gkv": row(p["mla_kv_norm"], KV_LORA), "wuk": wuk.astype(BF16), "wuv": wuv.astype(BF16),
        "gkn": row(p["mla_knorm"], LANES),
        "wbr": p["w_branch"].astype(BF16), "wout": p["w_out"].astype(BF16),
        "wup": p["w_up"].astype(BF16), "cv": cv, "wdown": p["w_down"].astype(BF16),
    }


def kernel(x_prompt, x_sample, c, c_ctx, cache_diff_k, cache_diff_v, cache_mla_ckv, cache_mla_krope, w_ada, b_ada, g_norm1, g_norm2, w_in, pool_w, pool_scale, diff_qnorm, diff_knorm, diff_lambda, diff_subln, mla_qa_norm, w_uq, mla_kv_norm, w_ukv, mla_qnorm, mla_knorm, w_branch, w_out, w_up, conv_w, conv_b, w_down):
    w = _prep_weights(dict(
        w_in=w_in, pool_w=pool_w, pool_scale=pool_scale, diff_qnorm=diff_qnorm, diff_knorm=diff_knorm,
        diff_lambda=diff_lambda, diff_subln=diff_subln, mla_qa_norm=mla_qa_norm, w_uq=w_uq,
        mla_kv_norm=mla_kv_norm, w_ukv=w_ukv, mla_qnorm=mla_qnorm, mla_knorm=mla_knorm, w_branch=w_branch,
        w_out=w_out, w_up=w_up, conv_w=conv_w, conv_b=conv_b, w_down=w_down, g_norm1=g_norm1,
        g_norm2=g_norm2))
    ropd, ropm = _rope_tables()
    consts = dict(_matrix_tables(), ropd=ropd, ropm=ropm)

    cond = jnp.concatenate([c_ctx[None, :], c, jnp.zeros((16 - 1 - LAT_B, D), F32)], axis=0)
    mods = _mod_call(cond, w_ada, b_ada).reshape(DEPTH, 16, 1, 6 * D)

    kc, vc = _ctxkv_call(cache_mla_ckv, _pad_last(cache_mla_krope, MLA_NOPE, LANES - MLA_QK), w)
    cdk = cache_diff_k.reshape(LAT_B, DEPTH, PAST, BW).astype(BF16)
    cdv = cache_diff_v.reshape(LAT_B, DEPTH, PAST, BW).astype(BF16)

    xs = [x_prompt.reshape(CTX_ROWS, D), x_sample.reshape(LAT_ROWS, D)]
    new = [[], [], [], []]
    for l in range(DEPTH):
        pa, qd, kd, vd, oc, qm, km, vm, *ctx_outs = _inproj_call(l, xs, mods, w, consts)
        ob, od = _attn_call(l, qd, kd, vd, cdk, cdv, qm, km, vm, kc, vc, w)
        x = _merge_call(l, xs, mods, w, pa, ob, oc, od)
        xs = _ffn_call(l, x, mods, w, split_out=(l == DEPTH - 1))
        for acc, a in zip(new, ctx_outs):
            acc.append(a)

    stack = lambda arrs, shape: jnp.stack(arrs, axis=1).reshape((CTX_B, DEPTH, CTX_T) + shape)
    return (xs[0].reshape(CTX_B, CTX_T, D), xs[1].reshape(LAT_B, LAT_T, D),
            stack([a.reshape(CTX_B, CTX_T, BW) for a in new[0]], (HEADS, DA_HD)),
            stack([a.reshape(CTX_B, CTX_T, BW) for a in new[1]], (HEADS, DA_HD)),
            stack([a.reshape(CTX_B, CTX_T, KV_LORA) for a in new[2]], (KV_LORA,)),
            stack([a.reshape(CTX_B, CTX_T, MLA_ROPE) for a in new[3]], (MLA_ROPE,)))
```

```python
import functools
import math

import jax
import jax.numpy as jnp
import numpy as np
from jax import lax
from jax.experimental import pallas as pl
from jax.experimental.pallas import tpu as pltpu

F32 = jnp.float32
BF16 = jnp.bfloat16

D = 1024
DEPTH = 4
CTX_B, CTX_T = 16, 256
LAT_B, LAT_T = 8, 1024
PAST = 512
TILE = 1024
CTX_TILES = CTX_B * CTX_T // TILE
LAT_TILES = LAT_B * LAT_T // TILE
TILES = CTX_TILES + LAT_TILES
ROWS = TILES * TILE
CTX_ROWS = CTX_TILES * TILE
LAT_ROWS = LAT_TILES * TILE
GRID_W = 64
BW = 256
DA_D = 32
DA_HD = 64
HEADS = 4
MLA_NOPE, MLA_ROPE, MLA_V, MLA_QK = 64, 32, 64, 96
Q_LORA, KV_LORA = 192, 128
FFN = 2816
FFN_CHUNK = 256
FFN_CHUNKS = FFN // FFN_CHUNK
SUBLANES = 8
POOL_HALF = (1, 2, 4, 8)
POOL_PAD = 16
ROPE_BASE = 10000.0
EPS = 1e-6
LOG2E = 1.4426950408889634
TQ = 256
LANES = 128
VMEM_LIMIT = 56 * 1024 * 1024

W_BLOCKS = 4 * D // BW
W_MLA = 4 * D + 5 * BW
C_CQ, C_CKV, C_END = 0, 256, 512


def _lambda_init(l):
    return 0.8 - 0.6 * math.exp(-0.3 * l)


def _sigmoid(x):
    return 0.5 * jnp.tanh(0.5 * x) + 0.5


def _rms(x, n):
    ms = jnp.sum(x * x, axis=-1, keepdims=True) * (1.0 / n)
    return x * lax.rsqrt(ms + EPS)


def _dot(a, b):
    return jnp.dot(a, b, preferred_element_type=F32)


def _dot_nt(a, b):
    return lax.dot_general(a, b, (((1,), (1,)), ((), ())), preferred_element_type=F32)


def _const(shape):
    n = len(shape)
    return pl.BlockSpec(shape, lambda *_: (0,) * n, pipeline_mode=pl.Buffered(1))


def _layer(shape, l):
    n = len(shape)
    return pl.BlockSpec((None,) + shape, lambda *_: (l,) + (0,) * n, pipeline_mode=pl.Buffered(1))


def _params(n_grid):
    return pltpu.CompilerParams(dimension_semantics=("arbitrary",) * n_grid, vmem_limit_bytes=VMEM_LIMIT)


def _row_spec(width):
    return pl.BlockSpec((TILE, width), lambda i: (i, 0))


def _ctx_row_spec(width):
    return pl.BlockSpec((TILE, width), lambda i: (jnp.minimum(i, CTX_TILES - 1), 0))


def _lat_row_spec(width):
    return pl.BlockSpec((TILE, width), lambda i: (jnp.maximum(i - CTX_TILES, 0), 0))


def _x_specs(split):
    return [_ctx_row_spec(D), _lat_row_spec(D)] if split else [_row_spec(D)]


def _read_x(x_refs, is_lat):
    if len(x_refs) == 2:
        return jnp.where(is_lat, x_refs[1][...], x_refs[0][...])
    return x_refs[0][...]


def _in_cols(l, n):
    return pl.BlockSpec((None, BW, D), lambda *_: (l, n, 0), pipeline_mode=pl.Buffered(1))


def _mod_spec(l):
    return pl.BlockSpec((None, 1, 1, 6 * D), lambda i: (l, jnp.maximum(i - (CTX_TILES - 1), 0), 0, 0))


def _group_spec(shape):
    n = len(shape)
    return pl.BlockSpec((None,) + shape, lambda i: (jnp.minimum(i // CTX_TILES, 1),) + (0,) * n,
                        pipeline_mode=pl.Buffered(1))


def _mod_kernel(c_ref, w_ref, b_ref, o_ref):
    c = c_ref[...]
    s = (c * _sigmoid(c)).astype(BF16)
    o_ref[0] = _dot(s, w_ref[0].astype(BF16)) + b_ref[0]


def _mod_call(cond, w_ada, b_ada):
    nc = 4
    cw = 6 * D // nc
    return pl.pallas_call(
        _mod_kernel,
        grid=(DEPTH, nc),
        in_specs=[
            pl.BlockSpec((16, D), lambda l, j: (0, 0)),
            pl.BlockSpec((1, D, cw), lambda l, j: (l, 0, j)),
            pl.BlockSpec((1, 1, cw), lambda l, j: (l, 0, j)),
        ],
        out_specs=pl.BlockSpec((1, 16, cw), lambda l, j: (l, 0, j)),
        out_shape=jax.ShapeDtypeStruct((DEPTH, 16, 6 * D), F32),
        compiler_params=_params(2),
        name="adaln_mod",
    )(cond, w_ada, b_ada.reshape(DEPTH, 1, 6 * D))


def _mla_key_heads(knope, kr_pad, gkn):
    out = []
    for h in range(HEADS):
        kh = knope[:, h * LANES:(h + 1) * LANES] + kr_pad
        out.append(_rms(kh, MLA_QK) * gkn)
    return out


def _ctxkv_kernel(ckv_ref, kr_ref, wuk_ref, wuv_ref, gkn_ref, kc_ref, vc_ref):
    cb = ckv_ref[...].astype(BF16)
    knope = _dot(cb, wuk_ref[...])
    heads = _mla_key_heads(knope, kr_ref[...], gkn_ref[...])
    for h in range(HEADS):
        kc_ref[:, h * LANES:(h + 1) * LANES] = heads[h].astype(BF16)
    vc_ref[...] = _dot(cb, wuv_ref[...]).astype(BF16)


def _ctxkv_call(ckv, kr_pad, w):
    cache = lambda width: pl.BlockSpec((None, None, PAST, width), lambda b, l: (b, l, 0, 0))
    wspec = lambda r, c: pl.BlockSpec((None, r, c), lambda b, l: (l, 0, 0))
    return pl.pallas_call(
        _ctxkv_kernel,
        grid=(LAT_B, DEPTH),
        in_specs=[cache(KV_LORA), cache(LANES), wspec(KV_LORA, 512), wspec(KV_LORA, BW), wspec(1, LANES)],
        out_specs=[cache(512), cache(BW)],
        out_shape=[
            jax.ShapeDtypeStruct((LAT_B, DEPTH, PAST, 512), BF16),
            jax.ShapeDtypeStruct((LAT_B, DEPTH, PAST, BW), BF16),
        ],
        compiler_params=_params(2),
        name="mla_ctx_keys",
    )(ckv, kr_pad, w["wuk"], w["wuv"], w["gkn"])


def _norm_mod(x, g, sc, sh):
    return (_rms(x, D) * g) * (1.0 + sc) + sh


def _rope(x, tab_ref, rot):
    hi = x.astype(BF16)
    pieces = jnp.concatenate([hi, (x - hi.astype(F32)).astype(BF16)], axis=1)
    return x * tab_ref[0] + _dot(pieces, rot) * tab_ref[1]


def _group_inv_rms(x, ones, n):
    sq = (x * x).astype(BF16)
    ms = jnp.concatenate([_dot(sq[:, c:c + BW], ones) for c in range(0, x.shape[1], BW)], axis=1)
    return lax.rsqrt(ms * (1.0 / n) + EPS)


def _inproj_kernel(n_x, *refs):
    x_refs = refs[:n_x]
    (mod_ref, g1_ref, wxa_ref, wq_ref, wk_ref, wv_ref, wxc_ref, wmla_ref,
     icnt_ref, poolw_ref, pools_ref, seg_ref, ones_ref, rotd_ref, rotm_ref, dgq_ref, dgk_ref,
     ropd_ref, ropm_ref, cc_ref, ss_ref, ft_ref, gqa_ref, wuq_ref, gqn_ref, gkv_ref,
     wuk_ref, wuv_ref, gkn_ref,
     pa_ref, qd_ref, kd_ref, vd_ref, oc_ref, qm_ref, km_ref, vm_ref,
     kcf_ref, vf_ref, ckv_ref, kr_ref, pad_ref) = refs[n_x:]
    is_lat = pl.program_id(0) >= CTX_TILES
    mod = mod_ref[0]
    hb = _norm_mod(_read_x(x_refs, is_lat), g1_ref[...], mod[:, D:2 * D], mod[:, 0:D]).astype(BF16)

    xa = _dot_nt(hb, wxa_ref[...])
    pad_ref[0:POOL_PAD, :] = jnp.zeros((POOL_PAD, BW), F32)
    pad_ref[POOL_PAD + TILE:, :] = jnp.zeros((POOL_PAD, BW), F32)
    pad_ref[POOL_PAD:POOL_PAD + TILE, :] = xa
    lane = lax.broadcasted_iota(jnp.int32, (TILE, BW), 1)
    row8 = lax.broadcasted_iota(jnp.int32, (SUBLANES, BW), 0)

    def term(s):
        v = pad_ref[POOL_PAD + s:POOL_PAD + s + TILE, :]
        kill = row8 < jnp.where(is_lat, 0, -s) if s < 0 else row8 >= jnp.where(is_lat, SUBLANES, SUBLANES - s)
        parts = []
        for r0 in range(0, TILE, CTX_T):
            r1 = r0 + CTX_T
            if s < 0:
                head = v[r0:r0 + SUBLANES]
                parts += [head if r0 == 0 else jnp.where(kill, 0.0, head), v[r0 + SUBLANES:r1]]
            else:
                tail = v[r1 - SUBLANES:r1]
                parts += [v[r0:r1 - SUBLANES], tail if r1 == TILE else jnp.where(kill, 0.0, tail)]
        return jnp.concatenate(parts, axis=0)

    acc = xa
    wins = []
    lo_s, hi_s = 0, 0
    for half in POOL_HALF:
        for s in list(range(-half, lo_s)) + list(range(hi_s + 1, half)):
            acc = acc + term(s)
        lo_s, hi_s = -half, half - 1
        wins.append(acc)
    win = jnp.where(lane < 64, wins[0], jnp.where(lane < 128, wins[1], jnp.where(lane < 192, wins[2], wins[3])))
    pooled = (win * icnt_ref[...] - xa).astype(BF16)
    pa_ref[...] = (_dot(pooled, poolw_ref[...]) * pools_ref[...]).astype(BF16)

    qscale = DA_D ** -0.5 * LOG2E
    q = _dot_nt(hb, wq_ref[...])
    k = _dot_nt(hb, wk_ref[...])
    v = _dot_nt(hb, wv_ref[...])
    rq = _group_inv_rms(q, seg_ref[...], DA_D) * qscale
    rk = _group_inv_rms(k, seg_ref[...], DA_D)
    kns = []
    for s in range(BW // LANES):
        sl = slice(s * LANES, (s + 1) * LANES)
        kg = k[:, sl] * dgk_ref[...]
        kns.append(kg * rk[:, sl])
        qd_ref[:, sl] = (_rope(q[:, sl] * dgq_ref[...], ropd_ref, rotd_ref[...]) * rq[:, sl]).astype(BF16)
        kd_ref[:, sl] = (_rope(kg, ropd_ref, rotd_ref[...]) * rk[:, sl]).astype(BF16)
    vd_ref[...] = v.astype(BF16)

    xcb = _dot_nt(hb, wxc_ref[...]).astype(BF16)
    xcc = _dot(xcb, cc_ref[...]).astype(BF16)
    xcs = _dot(xcb, ss_ref[...]).astype(BF16)
    y = _dot(ft_ref[0], xcc) + _dot(ft_ref[1], xcs)
    fnorm = jnp.where(is_lat, (LAT_T * 64.0) ** -0.5, (CTX_T * 64.0) ** -0.5)
    oc_ref[...] = (y * fnorm).astype(BF16)

    cq = _dot_nt(hb, wmla_ref[C_CQ:C_CQ + BW, :])
    qa = (_rms(cq, Q_LORA) * gqa_ref[...]).astype(BF16)
    qf = _dot(qa, wuq_ref[...])
    mscale = MLA_QK ** -0.5 * LOG2E
    rqm = _group_inv_rms(qf, ones_ref[...], MLA_QK) * mscale
    for h in range(HEADS):
        sl = slice(h * LANES, (h + 1) * LANES)
        qm_ref[:, sl] = (_rope(qf[:, sl] * gqn_ref[...], ropm_ref, rotm_ref[...]) * rqm[:, sl]).astype(BF16)
    ckv_kr = _dot_nt(hb, wmla_ref[C_CKV:C_END, :])
    ckv, kr_pad = ckv_kr[:, :KV_LORA], ckv_kr[:, KV_LORA:]
    ckv_n = _rms(ckv, KV_LORA) * gkv_ref[...]
    cb = ckv_n.astype(BF16)
    knope = _dot(cb, wuk_ref[...])
    kr_rot = _rope(kr_pad * gkn_ref[...], ropm_ref, rotm_ref[...])
    kr_ssq = _dot((kr_pad * kr_pad).astype(BF16), ones_ref[0:LANES, 0:LANES])
    kn_ssq = jnp.concatenate([_dot((knope[:, c:c + BW] * knope[:, c:c + BW]).astype(BF16), ones_ref[...])
                              for c in range(0, HEADS * LANES, BW)], axis=1)
    for h in range(HEADS):
        sl = slice(h * LANES, (h + 1) * LANES)
        rkm = lax.rsqrt((kn_ssq[:, sl] + kr_ssq) * (1.0 / MLA_QK) + EPS)
        km_ref[:, sl] = ((knope[:, sl] * gkn_ref[...] + kr_rot) * rkm).astype(BF16)
    vm_ref[...] = _dot(cb, wuv_ref[...]).astype(BF16)

    @pl.when(jnp.logical_not(is_lat))
    def _():
        for s, kn in enumerate(kns):
            kcf_ref[:, s * LANES:(s + 1) * LANES] = kn
        vf_ref[...] = v
        ckv_ref[...] = ckv_n
        kr_ref[...] = kr_pad[:, MLA_NOPE:MLA_NOPE + MLA_ROPE]


def _inproj_call(l, xs, mods, w, consts):
    row_outs = [(BW, BF16), (BW, BF16), (BW, BF16), (BW, BF16), (BW, BF16), (512, BF16), (512, BF16), (BW, BF16)]
    ctx_outs = [(BW, F32), (BW, F32), (KV_LORA, F32), (MLA_ROPE, F32)]
    return pl.pallas_call(
        functools.partial(_inproj_kernel, len(xs)),
        grid=(TILES,),
        in_specs=_x_specs(len(xs) == 2) + [
            _mod_spec(l), _layer((1, D), l), *[_in_cols(l, W_BLOCKS + n) for n in range(5)],
            _layer((C_END, D), l), _group_spec((TILE, BW)), _layer((BW, BW), l), _layer((1, BW), l),
            _const((BW, BW)), _const((BW, BW)), _const((BW, LANES)), _const((BW, LANES)),
            _layer((1, LANES), l), _layer((1, LANES), l),
            _group_spec((2, TILE, LANES)), _group_spec((2, TILE, LANES)),
            _const((BW, BW)), _const((BW, BW)), _group_spec((2, TILE, TILE)),
            _layer((1, BW), l), _layer((BW, 512), l), _layer((1, LANES), l), _layer((1, KV_LORA), l),
            _layer((KV_LORA, 512), l), _layer((KV_LORA, BW), l), _layer((1, LANES), l),
        ],
        out_specs=[_row_spec(wd) for wd, _ in row_outs] + [_ctx_row_spec(wd) for wd, _ in ctx_outs],
        out_shape=([jax.ShapeDtypeStruct((ROWS, wd), dt) for wd, dt in row_outs]
                   + [jax.ShapeDtypeStruct((CTX_ROWS, wd), dt) for wd, dt in ctx_outs]),
        scratch_shapes=[pltpu.VMEM((TILE + 2 * POOL_PAD, BW), F32)],
        compiler_params=_params(1),
        name="inproj",
    )(*xs, mods, w["g1"], *[w["win"]] * 5, w["wmla"], consts["icnt"], w["poolw"], w["pools"], consts["seg"],
      consts["ones"], consts["rotd"], consts["rotm"], w["dgq"], w["dgk"],
      consts["ropd"], consts["ropm"], consts["cc"], consts["ss"], consts["ft"],
      w["gqa"], w["wuq"], w["gqn"], w["gkv"], w["wuk"], w["wuv"], w["gkn"])


def _softmax_parts(q, ks):
    ss = [_dot_nt(q, k) for k in ks]
    m = ss[0].max(axis=-1, keepdims=True)
    for s in ss[1:]:
        m = jnp.maximum(m, s.max(axis=-1, keepdims=True))
    ps = [jnp.exp2(s - m) for s in ss]
    l = ps[0].sum(axis=-1, keepdims=True)
    for p in ps[1:]:
        l = l + p.sum(axis=-1, keepdims=True)
    return [p.astype(BF16) for p in ps], 1.0 / l


def _attend(qd, kds, vds, qm, kms, vms, lam, gsub, lam_init, ob_ref, od_ref):
    n = qd.shape[0]
    lane = lax.broadcasted_iota(jnp.int32, (n, BW), 1)
    zero = jnp.zeros((), BF16)
    out_b = jnp.zeros((n, BW), F32)
    out_d = jnp.zeros((n, BW), F32)
    for h in range(HEADS):
        head = (lane >= h * DA_HD) & (lane < (h + 1) * DA_HD)
        o = None
        for half in range(2):
            lo = h * DA_HD + half * DA_D
            qc = jnp.where((lane >= lo) & (lane < lo + DA_D), qd, zero)
            ps, rl = _softmax_parts(qc, kds)
            pv = sum(_dot(p, v) for p, v in zip(ps, vds)) * rl
            o = pv if half == 0 else o - lam * pv
        o = jnp.where(head, o, 0.0)
        ms = jnp.sum(o * o, axis=-1, keepdims=True) * (1.0 / DA_HD)
        out_b = out_b + o * lax.rsqrt(ms + EPS)
        sl = slice(h * LANES, (h + 1) * LANES)
        ps, rl = _softmax_parts(qm[:, sl], [k[:, sl] for k in kms])
        pv = sum(_dot(p, v) for p, v in zip(ps, vms)) * rl
        out_d = out_d + jnp.where(head, pv, 0.0)
    ob_ref[...] = (out_b * gsub * (1.0 - lam_init)).astype(BF16)
    od_ref[...] = out_d.astype(BF16)


def _attn_kernel(lam_init, qd_ref, kd_ref, vd_ref, cdk_ref, cdv_ref, qm_ref, km_ref, vm_ref, kc_ref, vc_ref,
                 dl_ref, gsub_ref, ob_ref, od_ref):
    i = pl.program_id(0)
    j = pl.program_id(1)
    dl = dl_ref[...]
    lam = (jnp.exp(jnp.sum(dl[0:1] * dl[1:2], axis=-1, keepdims=True))
           - jnp.exp(jnp.sum(dl[2:3] * dl[3:4], axis=-1, keepdims=True)) + lam_init)
    rows = pl.ds(pl.multiple_of(j * TQ, TQ), TQ)
    qd = qd_ref[rows, :]
    qm = qm_ref[rows, :]

    @pl.when(i < CTX_TILES)
    def _():
        _attend(qd, [kd_ref[rows, :]], [vd_ref[rows, :]], qm, [km_ref[rows, :]], [vm_ref[rows, :]],
                lam, gsub_ref[...], lam_init, ob_ref, od_ref)

    @pl.when(i >= CTX_TILES)
    def _():
        _attend(qd, [cdk_ref[...], kd_ref[...]], [cdv_ref[...], vd_ref[...]],
                qm, [kc_ref[...], km_ref[...]], [vc_ref[...], vm_ref[...]],
                lam, gsub_ref[...], lam_init, ob_ref, od_ref)


def _attn_call(l, qd, kd, vd, cdk, cdv, qm, km, vm, kc, vc, w):
    tile = lambda width: pl.BlockSpec((TILE, width), lambda i, j: (i, 0))
    cache = lambda width: pl.BlockSpec((None, None, PAST, width),
                                       lambda i, j: (jnp.maximum(i - CTX_TILES, 0), l, 0, 0))
    blk = pl.BlockSpec((TQ, BW), lambda i, j: (i * (TILE // TQ) + j, 0))
    return pl.pallas_call(
        functools.partial(_attn_kernel, _lambda_init(l)),
        grid=(TILES, TILE // TQ),
        in_specs=[tile(BW), tile(BW), tile(BW), cache(BW), cache(BW), tile(512), tile(512), tile(BW),
                  cache(512), cache(BW), _layer((4, DA_D), l), _layer((1, BW), l)],
        out_specs=[blk, blk],
        out_shape=[jax.ShapeDtypeStruct((ROWS, BW), BF16)] * 2,
        compiler_params=_params(2),
        name="attention",
    )(qd, kd, vd, cdk, cdv, qm, km, vm, kc, vc, w["dl"], w["gsub"])


def _merge_kernel(n_x, *refs):
    x_refs = refs[:n_x]
    mod_ref, g1_ref, wg_ref, wb_ref, wo_ref, pa_ref, pb_ref, pc_ref, pd_ref, o_ref = refs[n_x:]
    mod = mod_ref[0]
    x = _read_x(x_refs, pl.program_id(0) >= CTX_TILES)
    hb = _norm_mod(x, g1_ref[...], mod[:, D:2 * D], mod[:, 0:D]).astype(BF16)
    merged = None
    for n, br_ref in enumerate((pa_ref, pb_ref, pc_ref, pd_ref)):
        gate = _sigmoid(_dot_nt(hb, wg_ref[n * D:(n + 1) * D, :]))
        t = gate * _dot(br_ref[...], wb_ref[n])
        merged = t if merged is None else merged + t
    o_ref[...] = x + mod[:, 2 * D:3 * D] * _dot(merged.astype(BF16), wo_ref[...])


def _merge_call(l, xs, mods, w, pa, pb, pc, pd):
    return pl.pallas_call(
        functools.partial(_merge_kernel, len(xs)),
        grid=(TILES,),
        in_specs=_x_specs(len(xs) == 2) + [
            _mod_spec(l), _layer((1, D), l), _layer((4 * D, D), l), _layer((4, BW, D), l), _layer((D, D), l),
            _row_spec(BW), _row_spec(BW), _row_spec(BW), _row_spec(BW)],
        out_specs=_row_spec(D),
        out_shape=jax.ShapeDtypeStruct((ROWS, D), F32),
        compiler_params=_params(1),
        name="merge",
    )(*xs, mods, w["g1"], w["win"], w["wbr"], w["wout"], pa, pb, pc, pd)


def _ffn_kernel(n_o, x_ref, mod_ref, g2_ref, wu_ref, cv_ref, wd_ref, *refs):
    o_refs = refs[:n_o]
    hb_ref, acc_ref = refs[n_o:]
    is_lat = pl.program_id(0) >= CTX_TILES
    is_ctx = jnp.logical_not(is_lat)
    mod = mod_ref[0]
    hb_ref[...] = _norm_mod(x_ref[...], g2_ref[...], mod[:, 4 * D:5 * D], mod[:, 3 * D:4 * D]).astype(BF16)
    acc_ref[...] = jnp.zeros((TILE, D), F32)
    row8 = lax.broadcasted_iota(jnp.int32, (SUBLANES, FFN_CHUNK), 0)
    tile_start, tile_end = row8 == 0, row8 == SUBLANES - 1
    seq_start = row8 == jnp.where(is_ctx, 0, -1)
    seq_end = row8 == jnp.where(is_ctx, SUBLANES - 1, -1)

    def conv(cols):
        u = _dot(hb_ref[...], wu_ref[:, cols])
        cv = cv_ref[:, cols]
        prev, nxt = pltpu.roll(u, 1, 0), pltpu.roll(u, TILE - 1, 0)
        prev_parts, nxt_parts = [], []
        for r0 in range(0, TILE, CTX_T):
            r1 = r0 + CTX_T
            start = tile_start if r0 == 0 else seq_start
            end = tile_end if r1 == TILE else seq_end
            prev_parts += [jnp.where(start, 0.0, prev[r0:r0 + SUBLANES]), prev[r0 + SUBLANES:r1]]
            nxt_parts += [nxt[r0:r1 - SUBLANES], jnp.where(end, 0.0, nxt[r1 - SUBLANES:r1])]
        prev, nxt = jnp.concatenate(prev_parts, axis=0), jnp.concatenate(nxt_parts, axis=0)
        return prev * cv[0:1] + u * cv[1:2] + nxt * cv[2:3] + cv[3:4]

    def body(c, carry):
        c0 = pl.multiple_of(c * FFN_CHUNK, FFN_CHUNK)
        h = conv(pl.ds(c0, FFN_CHUNK))
        v = conv(pl.ds(FFN + c0, FFN_CHUNK))
        act = (h * (1.0 + jnp.tanh(h)) * v).astype(BF16)
        acc_ref[...] += _dot(act, wd_ref[pl.ds(c0, FFN_CHUNK), :])
        return carry

    lax.fori_loop(0, FFN_CHUNKS, body, 0)
    y = x_ref[...] + mod[:, 5 * D:6 * D] * acc_ref[...]
    if n_o == 1:
        o_refs[0][...] = y
    else:
        @pl.when(is_ctx)
        def _():
            o_refs[0][...] = y

        @pl.when(is_lat)
        def _():
            o_refs[1][...] = y


def _ffn_call(l, x, mods, w, split_out):
    if split_out:
        out_specs = _x_specs(True)
        out_shape = [jax.ShapeDtypeStruct((CTX_ROWS, D), F32), jax.ShapeDtypeStruct((LAT_ROWS, D), F32)]
    else:
        out_specs = _x_specs(False)
        out_shape = [jax.ShapeDtypeStruct((ROWS, D), F32)]
    return pl.pallas_call(
        functools.partial(_ffn_kernel, len(out_shape)),
        grid=(TILES,),
        in_specs=[_row_spec(D), _mod_spec(l), _layer((1, D), l),
                  _layer((D, 2 * FFN), l), _layer((8, 2 * FFN), l), _layer((FFN, D), l)],
        out_specs=out_specs,
        out_shape=out_shape,
        scratch_shapes=[pltpu.VMEM((TILE, D), BF16), pltpu.VMEM((TILE, D), F32)],
        compiler_params=_params(1),
        name="ffn",
    )(x, mods, w["g2"], w["wup"], w["cv"], w["wdown"])


def _rope_tables():
    half = 8
    inv = ROPE_BASE ** (-np.arange(half, dtype=np.float64) / half)
    t = np.arange(LAT_T)
    rows, cols = (t // GRID_W).astype(np.float64), (t % GRID_W).astype(np.float64)
    ang = np.concatenate([rows[:, None] * inv[None, :]] * 2 + [cols[:, None] * inv[None, :]] * 2, axis=1)
    cos, sin = np.cos(ang), np.sin(ang)

    def diff_layout(a):
        return np.tile(a, (1, LANES // 32))

    def mla_layout(a, fill):
        out = np.full((LAT_T, LANES), fill)
        out[:, MLA_NOPE:MLA_QK] = a
        return out

    ident = lambda layout: np.stack([layout(np.ones_like(cos)), layout(np.zeros_like(cos))])
    ropd = np.stack([ident(diff_layout), np.stack([diff_layout(cos), diff_layout(sin)])])
    ropm = np.stack([np.stack([mla_layout(np.ones_like(cos), 1.0), mla_layout(np.zeros_like(cos), 0.0)]),
                     np.stack([mla_layout(cos, 1.0), mla_layout(sin, 0.0)])])
    return jnp.asarray(ropd, F32), jnp.asarray(ropm, F32)


def _pool_inv_count():
    out = []
    for seq in (CTX_T, LAT_T):
        t = np.arange(TILE) % seq
        cols = [np.repeat((1.0 / (np.minimum(t + half, seq) - np.maximum(t - half, 0)))[:, None], BW // 4, axis=1)
                for half in POOL_HALF]
        out.append(np.concatenate(cols, axis=1))
    return jnp.asarray(np.stack(out), F32)


def _rotate_half_matrix(lanes):
    r = np.zeros((LANES, LANES))
    for j in lanes:
        if j % 16 < 8:
            r[j + 8, j] = -1.0
        else:
            r[j - 8, j] = 1.0
    return np.concatenate([r, r], axis=0)


def _matrix_tables():
    def cs(n):
        k = np.arange(n)
        ang = 2.0 * np.pi * ((k[:, None] * k[None, :]) % n) / n
        return np.cos(ang), np.sin(ang)

    c64, s64 = cs(64)
    eye4 = np.eye(4)
    cl, sl = cs(LAT_T)
    cc_, sc_ = cs(CTX_T)
    ft = np.stack([np.stack([np.kron(eye4, cc_), -np.kron(eye4, sc_)]), np.stack([cl, -sl])])
    as_bf16 = lambda a: jnp.asarray(a, F32).astype(BF16)
    return dict(
        cc=as_bf16(np.kron(eye4, c64)), ss=as_bf16(np.kron(eye4, s64)), ft=as_bf16(ft),
        seg=as_bf16(np.kron(np.eye(BW // DA_D), np.ones((DA_D, DA_D)))),
        ones=as_bf16(np.kron(np.eye(BW // LANES), np.ones((LANES, LANES)))),
        rotd=as_bf16(_rotate_half_matrix(range(LANES))),
        rotm=as_bf16(_rotate_half_matrix(range(MLA_NOPE, MLA_QK))))


def _pad_last(a, lo, hi):
    return jnp.pad(a, ((0, 0),) * (a.ndim - 1) + ((lo, hi),))


def _prep_weights(p):
    win = jnp.swapaxes(p["w_in"], 1, 2).astype(BF16)
    mla = win[:, W_MLA:, :]
    cq, ckv, kr = mla[:, :Q_LORA], mla[:, Q_LORA:Q_LORA + KV_LORA], mla[:, Q_LORA + KV_LORA:]
    pad_rows = lambda a, lo, hi: jnp.pad(a, ((0, 0), (lo, hi), (0, 0)))
    wmla = jnp.concatenate([pad_rows(cq, 0, BW - Q_LORA), ckv, pad_rows(kr, MLA_NOPE, LANES - MLA_QK)], axis=1)
    poolw = jnp.einsum("lgce,gh->lgche", p["pool_w"], jnp.eye(4, dtype=F32)).reshape(DEPTH, BW, BW).astype(BF16)
    wuq = p["w_uq"].reshape(DEPTH, Q_LORA, HEADS, MLA_QK)
    wuq = jnp.pad(wuq, ((0, 0), (0, BW - Q_LORA), (0, 0), (0, LANES - MLA_QK))).reshape(DEPTH, BW, HEADS * LANES)
    wukv = p["w_ukv"].reshape(DEPTH, KV_LORA, HEADS, MLA_NOPE + MLA_V)
    wuk = _pad_last(wukv[..., :MLA_NOPE], 0, LANES - MLA_NOPE).reshape(DEPTH, KV_LORA, HEADS * LANES)
    wuv = wukv[..., MLA_NOPE:].reshape(DEPTH, KV_LORA, HEADS * MLA_V)
    cv = jnp.concatenate([p["conv_w"], p["conv_b"][:, None, :], jnp.zeros((DEPTH, 4, 2 * FFN), F32)], axis=1)
    cv = cv * jnp.where(jnp.arange(2 * FFN) < FFN, 0.5, 1.0)

    row = lambda a, width: _pad_last(a, 0, width - a.shape[-1])[:, None, :]
    tiled = lambda a, reps: jnp.tile(a, (1, reps))[:, None, :]
    return {
        "g1": row(p["g_norm1"], D), "g2": row(p["g_norm2"], D),
        "win": win, "wmla": wmla,
        "poolw": poolw, "pools": row(p["pool_scale"], BW),
        "dgq": tiled(p["diff_qnorm"], LANES // DA_D), "dgk": tiled(p["diff_knorm"], LANES // DA_D),
        "dl": p["diff_lambda"], "gsub": tiled(p["diff_subln"], HEADS),
        "gqa": row(p["mla_qa_norm"], BW), "wuq": wuq.astype(BF16), "gqn": row(p["mla_qnorm"], LANES),
        "gkv": row(p["mla_kv_norm"], KV_LORA), "wuk": wuk.astype(BF16), "wuv": wuv.astype(BF16),
        "gkn": row(p["mla_knorm"], LANES),
        "wbr": p["w_branch"].astype(BF16), "wout": p["w_out"].astype(BF16),
        "wup": p["w_up"].astype(BF16), "cv": cv, "wdown": p["w_down"].astype(BF16),
    }


def kernel(x_prompt, x_sample, c, c_ctx, cache_diff_k, cache_diff_v, cache_mla_ckv, cache_mla_krope, w_ada, b_ada, g_norm1, g_norm2, w_in, pool_w, pool_scale, diff_qnorm, diff_knorm, diff_lambda, diff_subln, mla_qa_norm, w_uq, mla_kv_norm, w_ukv, mla_qnorm, mla_knorm, w_branch, w_out, w_up, conv_w, conv_b, w_down):
    w = _prep_weights(dict(
        w_in=w_in, pool_w=pool_w, pool_scale=pool_scale, diff_qnorm=diff_qnorm, diff_knorm=diff_knorm,
        diff_lambda=diff_lambda, diff_subln=diff_subln, mla_qa_norm=mla_qa_norm, w_uq=w_uq,
        mla_kv_norm=mla_kv_norm, w_ukv=w_ukv, mla_qnorm=mla_qnorm, mla_knorm=mla_knorm, w_branch=w_branch,
        w_out=w_out, w_up=w_up, conv_w=conv_w, conv_b=conv_b, w_down=w_down, g_norm1=g_norm1,
        g_norm2=g_norm2))
    ropd, ropm = _rope_tables()
    consts = dict(_matrix_tables(), ropd=ropd, ropm=ropm, icnt=_pool_inv_count())

    cond = jnp.concatenate([c_ctx[None, :], c, jnp.zeros((16 - 1 - LAT_B, D), F32)], axis=0)
    mods = _mod_call(cond, w_ada, b_ada).reshape(DEPTH, 16, 1, 6 * D)

    kc, vc = _ctxkv_call(cache_mla_ckv, _pad_last(cache_mla_krope, MLA_NOPE, LANES - MLA_QK), w)
    cdk = cache_diff_k.reshape(LAT_B, DEPTH, PAST, BW).astype(BF16)
    cdv = cache_diff_v.reshape(LAT_B, DEPTH, PAST, BW).astype(BF16)

    xs = [x_prompt.reshape(CTX_ROWS, D), x_sample.reshape(LAT_ROWS, D)]
    new = [[], [], [], []]
    for l in range(DEPTH):
        pa, qd, kd, vd, oc, qm, km, vm, *ctx_outs = _inproj_call(l, xs, mods, w, consts)
        ob, od = _attn_call(l, qd, kd, vd, cdk, cdv, qm, km, vm, kc, vc, w)
        x = _merge_call(l, xs, mods, w, pa, ob, oc, od)
        xs = _ffn_call(l, x, mods, w, split_out=(l == DEPTH - 1))
        for acc, a in zip(new, ctx_outs):
            acc.append(a)

    stack = lambda arrs, shape: jnp.stack(arrs, axis=1).reshape((CTX_B, DEPTH, CTX_T) + shape)
    return (xs[0].reshape(CTX_B, CTX_T, D), xs[1].reshape(LAT_B, LAT_T, D),
            stack([a.reshape(CTX_B, CTX_T, BW) for a in new[0]], (HEADS, DA_HD)),
            stack([a.reshape(CTX_B, CTX_T, BW) for a in new[1]], (HEADS, DA_HD)),
            stack([a.reshape(CTX_B, CTX_T, KV_LORA) for a in new[2]], (KV_LORA,)),
            stack([a.reshape(CTX_B, CTX_T, MLA_ROPE) for a in new[3]], (MLA_ROPE,)))
```

```python
import functools
import math

import jax
import jax.numpy as jnp
import numpy as np
from jax import lax
from jax.experimental import pallas as pl
from jax.experimental.pallas import tpu as pltpu

F32 = jnp.float32
BF16 = jnp.bfloat16

D = 1024
DEPTH = 4
CTX_B, CTX_T = 16, 256
LAT_B, LAT_T = 8, 1024
PAST = 512
TILE = 1024
CTX_TILES = CTX_B * CTX_T // TILE
LAT_TILES = LAT_B * LAT_T // TILE
TILES = CTX_TILES + LAT_TILES
ROWS = TILES * TILE
CTX_ROWS = CTX_TILES * TILE
LAT_ROWS = LAT_TILES * TILE
GRID_W = 64
BW = 256
DA_D = 32
DA_HD = 64
HEADS = 4
MLA_NOPE, MLA_ROPE, MLA_V, MLA_QK = 64, 32, 64, 96
Q_LORA, KV_LORA = 192, 128
FFN = 2816
FFN_CHUNK = 256
FFN_CHUNKS = FFN // FFN_CHUNK
SUBLANES = 8
POOL_HALF = (1, 2, 4, 8)
POOL_PAD = 16
ROPE_BASE = 10000.0
EPS = 1e-6
LOG2E = 1.4426950408889634
TQ = 512
LANES = 128
VMEM_LIMIT = 56 * 1024 * 1024

W_BLOCKS = 4 * D // BW
W_MLA = 4 * D + 5 * BW
C_CQ, C_CKV, C_END = 0, 256, 512


def _lambda_init(l):
    return 0.8 - 0.6 * math.exp(-0.3 * l)


def _sigmoid(x):
    return 0.5 * jnp.tanh(0.5 * x) + 0.5


def _rms(x, n):
    ms = jnp.sum(x * x, axis=-1, keepdims=True) * (1.0 / n)
    return x * lax.rsqrt(ms + EPS)


def _dot(a, b):
    return jnp.dot(a, b, preferred_element_type=F32)


def _dot_nt(a, b):
    return lax.dot_general(a, b, (((1,), (1,)), ((), ())), preferred_element_type=F32)


def _const(shape):
    n = len(shape)
    return pl.BlockSpec(shape, lambda *_: (0,) * n, pipeline_mode=pl.Buffered(1))


def _layer(shape, l):
    n = len(shape)
    return pl.BlockSpec((None,) + shape, lambda *_: (l,) + (0,) * n, pipeline_mode=pl.Buffered(1))


def _params(n_grid):
    return pltpu.CompilerParams(dimension_semantics=("arbitrary",) * n_grid, vmem_limit_bytes=VMEM_LIMIT)


def _row_spec(width):
    return pl.BlockSpec((TILE, width), lambda i: (i, 0))


def _ctx_row_spec(width):
    return pl.BlockSpec((TILE, width), lambda i: (jnp.minimum(i, CTX_TILES - 1), 0))


def _lat_row_spec(width):
    return pl.BlockSpec((TILE, width), lambda i: (jnp.maximum(i - CTX_TILES, 0), 0))


def _x_specs(split):
    return [_ctx_row_spec(D), _lat_row_spec(D)] if split else [_row_spec(D)]


def _read_x(x_refs, is_lat):
    if len(x_refs) == 2:
        return jnp.where(is_lat, x_refs[1][...], x_refs[0][...])
    return x_refs[0][...]


def _in_cols(l, n):
    return pl.BlockSpec((None, BW, D), lambda *_: (l, n, 0), pipeline_mode=pl.Buffered(1))


def _mod_spec(l):
    return pl.BlockSpec((None, 1, 1, 6 * D), lambda i: (l, jnp.maximum(i - (CTX_TILES - 1), 0), 0, 0))


def _group_spec(shape):
    n = len(shape)
    return pl.BlockSpec((None,) + shape, lambda i: (jnp.minimum(i // CTX_TILES, 1),) + (0,) * n,
                        pipeline_mode=pl.Buffered(1))


def _mod_kernel(c_ref, w_ref, b_ref, o_ref):
    c = c_ref[...]
    s = (c * _sigmoid(c)).astype(BF16)
    o_ref[0] = _dot(s, w_ref[0].astype(BF16)) + b_ref[0]


def _mod_call(cond, w_ada, b_ada):
    nc = 4
    cw = 6 * D // nc
    return pl.pallas_call(
        _mod_kernel,
        grid=(DEPTH, nc),
        in_specs=[
            pl.BlockSpec((16, D), lambda l, j: (0, 0)),
            pl.BlockSpec((1, D, cw), lambda l, j: (l, 0, j)),
            pl.BlockSpec((1, 1, cw), lambda l, j: (l, 0, j)),
        ],
        out_specs=pl.BlockSpec((1, 16, cw), lambda l, j: (l, 0, j)),
        out_shape=jax.ShapeDtypeStruct((DEPTH, 16, 6 * D), F32),
        compiler_params=_params(2),
        name="adaln_mod",
    )(cond, w_ada, b_ada.reshape(DEPTH, 1, 6 * D))


def _mla_key_heads(knope, kr_pad, gkn):
    out = []
    for h in range(HEADS):
        kh = knope[:, h * LANES:(h + 1) * LANES] + kr_pad
        out.append(_rms(kh, MLA_QK) * gkn)
    return out


def _ctxkv_kernel(ckv_ref, kr_ref, wuk_ref, wuv_ref, gkn_ref, kc_ref, vc_ref):
    cb = ckv_ref[...].astype(BF16)
    knope = _dot(cb, wuk_ref[...])
    heads = _mla_key_heads(knope, kr_ref[...], gkn_ref[...])
    for h in range(HEADS):
        kc_ref[:, h * LANES:(h + 1) * LANES] = heads[h].astype(BF16)
    vc_ref[...] = _dot(cb, wuv_ref[...]).astype(BF16)


def _ctxkv_call(ckv, kr_pad, w):
    cache = lambda width: pl.BlockSpec((None, None, PAST, width), lambda b, l: (b, l, 0, 0))
    wspec = lambda r, c: pl.BlockSpec((None, r, c), lambda b, l: (l, 0, 0))
    return pl.pallas_call(
        _ctxkv_kernel,
        grid=(LAT_B, DEPTH),
        in_specs=[cache(KV_LORA), cache(LANES), wspec(KV_LORA, 512), wspec(KV_LORA, BW), wspec(1, LANES)],
        out_specs=[cache(512), cache(BW)],
        out_shape=[
            jax.ShapeDtypeStruct((LAT_B, DEPTH, PAST, 512), BF16),
            jax.ShapeDtypeStruct((LAT_B, DEPTH, PAST, BW), BF16),
        ],
        compiler_params=_params(2),
        name="mla_ctx_keys",
    )(ckv, kr_pad, w["wuk"], w["wuv"], w["gkn"])


def _norm_mod(x, g, sc, sh):
    return _rms(x, D) * (g * (1.0 + sc)) + sh


def _rope(x, tab_ref, rot):
    hi = x.astype(BF16)
    pieces = jnp.concatenate([hi, (x - hi.astype(F32)).astype(BF16)], axis=1)
    return x * tab_ref[0] + _dot(pieces, rot) * tab_ref[1]


def _group_inv_rms(x, ones, n):
    sq = (x * x).astype(BF16)
    ms = jnp.concatenate([_dot(sq[:, c:c + BW], ones) for c in range(0, x.shape[1], BW)], axis=1)
    return lax.rsqrt(ms * (1.0 / n) + EPS)


def _inproj_kernel(n_x, *refs):
    x_refs = refs[:n_x]
    (mod_ref, g1_ref, wxa_ref, wq_ref, wk_ref, wv_ref, wxc_ref, wmla_ref,
     icnt_ref, poolw_ref, pools_ref, seg_ref, ones_ref, rotd_ref, rotm_ref, dgq_ref, dgk_ref,
     ropd_ref, ropm_ref, cc_ref, ss_ref, ft_ref, gqa_ref, wuq_ref, gqn_ref, gkv_ref,
     wuk_ref, wuv_ref, gkn_ref,
     pa_ref, qd_ref, kd_ref, vd_ref, oc_ref, qm_ref, km_ref, vm_ref,
     kcf_ref, vf_ref, ckv_ref, kr_ref, pad_ref) = refs[n_x:]
    is_lat = pl.program_id(0) >= CTX_TILES
    mod = mod_ref[0]
    hb = _norm_mod(_read_x(x_refs, is_lat), g1_ref[...], mod[:, D:2 * D], mod[:, 0:D]).astype(BF16)

    xa = _dot_nt(hb, wxa_ref[...])
    pad_ref[0:POOL_PAD, :] = jnp.zeros((POOL_PAD, BW), F32)
    pad_ref[POOL_PAD + TILE:, :] = jnp.zeros((POOL_PAD, BW), F32)
    pad_ref[POOL_PAD:POOL_PAD + TILE, :] = xa
    lane = lax.broadcasted_iota(jnp.int32, (TILE, BW), 1)
    row8 = lax.broadcasted_iota(jnp.int32, (SUBLANES, BW), 0)

    def term(s):
        v = pad_ref[POOL_PAD + s:POOL_PAD + s + TILE, :]
        kill = row8 < jnp.where(is_lat, 0, -s) if s < 0 else row8 >= jnp.where(is_lat, SUBLANES, SUBLANES - s)
        parts = []
        for r0 in range(0, TILE, CTX_T):
            r1 = r0 + CTX_T
            if s < 0:
                head = v[r0:r0 + SUBLANES]
                parts += [head if r0 == 0 else jnp.where(kill, 0.0, head), v[r0 + SUBLANES:r1]]
            else:
                tail = v[r1 - SUBLANES:r1]
                parts += [v[r0:r1 - SUBLANES], tail if r1 == TILE else jnp.where(kill, 0.0, tail)]
        return jnp.concatenate(parts, axis=0)

    acc = xa
    wins = []
    lo_s, hi_s = 0, 0
    for half in POOL_HALF:
        for s in list(range(-half, lo_s)) + list(range(hi_s + 1, half)):
            acc = acc + term(s)
        lo_s, hi_s = -half, half - 1
        wins.append(acc)
    win = jnp.where(lane < 64, wins[0], jnp.where(lane < 128, wins[1], jnp.where(lane < 192, wins[2], wins[3])))
    pooled = (win * icnt_ref[...] - xa).astype(BF16)
    pa_ref[...] = (_dot(pooled, poolw_ref[...]) * pools_ref[...]).astype(BF16)

    qscale = DA_D ** -0.5 * LOG2E
    q = _dot_nt(hb, wq_ref[...])
    k = _dot_nt(hb, wk_ref[...])
    v = _dot_nt(hb, wv_ref[...])
    rq = _group_inv_rms(q, seg_ref[...], DA_D) * qscale
    rk = _group_inv_rms(k, seg_ref[...], DA_D)
    kns = []
    for s in range(BW // LANES):
        sl = slice(s * LANES, (s + 1) * LANES)
        kg = k[:, sl] * dgk_ref[...]
        kns.append(kg * rk[:, sl])
        qd_ref[:, sl] = (_rope(q[:, sl] * dgq_ref[...], ropd_ref, rotd_ref[...]) * rq[:, sl]).astype(BF16)
        kd_ref[:, sl] = (_rope(kg, ropd_ref, rotd_ref[...]) * rk[:, sl]).astype(BF16)
    vd_ref[...] = v.astype(BF16)

    xcb = _dot_nt(hb, wxc_ref[...]).astype(BF16)
    xcc = _dot(xcb, cc_ref[...]).astype(BF16)
    xcs = _dot(xcb, ss_ref[...]).astype(BF16)
    y = _dot(ft_ref[0], xcc) + _dot(ft_ref[1], xcs)
    fnorm = jnp.where(is_lat, (LAT_T * 64.0) ** -0.5, (CTX_T * 64.0) ** -0.5)
    oc_ref[...] = (y * fnorm).astype(BF16)

    cq = _dot_nt(hb, wmla_ref[C_CQ:C_CQ + BW, :])
    qa = (_rms(cq, Q_LORA) * gqa_ref[...]).astype(BF16)
    qf = _dot(qa, wuq_ref[...])
    mscale = MLA_QK ** -0.5 * LOG2E
    rqm = _group_inv_rms(qf, ones_ref[...], MLA_QK) * mscale
    for h in range(HEADS):
        sl = slice(h * LANES, (h + 1) * LANES)
        qm_ref[:, sl] = (_rope(qf[:, sl] * gqn_ref[...], ropm_ref, rotm_ref[...]) * rqm[:, sl]).astype(BF16)
    ckv_kr = _dot_nt(hb, wmla_ref[C_CKV:C_END, :])
    ckv, kr_pad = ckv_kr[:, :KV_LORA], ckv_kr[:, KV_LORA:]
    ckv_n = _rms(ckv, KV_LORA) * gkv_ref[...]
    cb = ckv_n.astype(BF16)
    knope = _dot(cb, wuk_ref[...])
    kr_rot = _rope(kr_pad * gkn_ref[...], ropm_ref, rotm_ref[...])
    kr_ssq = _dot((kr_pad * kr_pad).astype(BF16), ones_ref[0:LANES, 0:LANES])
    kn_ssq = jnp.concatenate([_dot((knope[:, c:c + BW] * knope[:, c:c + BW]).astype(BF16), ones_ref[...])
                              for c in range(0, HEADS * LANES, BW)], axis=1)
    for h in range(HEADS):
        sl = slice(h * LANES, (h + 1) * LANES)
        rkm = lax.rsqrt((kn_ssq[:, sl] + kr_ssq) * (1.0 / MLA_QK) + EPS)
        km_ref[:, sl] = ((knope[:, sl] * gkn_ref[...] + kr_rot) * rkm).astype(BF16)
    vm_ref[...] = _dot(cb, wuv_ref[...]).astype(BF16)

    @pl.when(jnp.logical_not(is_lat))
    def _():
        for s, kn in enumerate(kns):
            kcf_ref[:, s * LANES:(s + 1) * LANES] = kn
        vf_ref[...] = v
        ckv_ref[...] = ckv_n
        kr_ref[...] = kr_pad[:, MLA_NOPE:MLA_NOPE + MLA_ROPE]


def _inproj_call(l, xs, mods, w, consts):
    row_outs = [(BW, BF16), (BW, BF16), (BW, BF16), (BW, BF16), (BW, BF16), (512, BF16), (512, BF16), (BW, BF16)]
    ctx_outs = [(BW, F32), (BW, F32), (KV_LORA, F32), (MLA_ROPE, F32)]
    return pl.pallas_call(
        functools.partial(_inproj_kernel, len(xs)),
        grid=(TILES,),
        in_specs=_x_specs(len(xs) == 2) + [
            _mod_spec(l), _layer((1, D), l), *[_in_cols(l, W_BLOCKS + n) for n in range(5)],
            _layer((C_END, D), l), _group_spec((TILE, BW)), _layer((BW, BW), l), _layer((1, BW), l),
            _const((BW, BW)), _const((BW, BW)), _const((BW, LANES)), _const((BW, LANES)),
            _layer((1, LANES), l), _layer((1, LANES), l),
            _group_spec((2, TILE, LANES)), _group_spec((2, TILE, LANES)),
            _const((BW, BW)), _const((BW, BW)), _group_spec((2, TILE, TILE)),
            _layer((1, BW), l), _layer((BW, 512), l), _layer((1, LANES), l), _layer((1, KV_LORA), l),
            _layer((KV_LORA, 512), l), _layer((KV_LORA, BW), l), _layer((1, LANES), l),
        ],
        out_specs=[_row_spec(wd) for wd, _ in row_outs] + [_ctx_row_spec(wd) for wd, _ in ctx_outs],
        out_shape=([jax.ShapeDtypeStruct((ROWS, wd), dt) for wd, dt in row_outs]
                   + [jax.ShapeDtypeStruct((CTX_ROWS, wd), dt) for wd, dt in ctx_outs]),
        scratch_shapes=[pltpu.VMEM((TILE + 2 * POOL_PAD, BW), F32)],
        compiler_params=_params(1),
        name="inproj",
    )(*xs, mods, w["g1"], *[w["win"]] * 5, w["wmla"], consts["icnt"], w["poolw"], w["pools"], consts["seg"],
      consts["ones"], consts["rotd"], consts["rotm"], w["dgq"], w["dgk"],
      consts["ropd"], consts["ropm"], consts["cc"], consts["ss"], consts["ft"],
      w["gqa"], w["wuq"], w["gqn"], w["gkv"], w["wuk"], w["wuv"], w["gkn"])


def _softmax_parts(q, ks):
    ss = [_dot_nt(q, k) for k in ks]
    m = ss[0].max(axis=-1, keepdims=True)
    for s in ss[1:]:
        m = jnp.maximum(m, s.max(axis=-1, keepdims=True))
    ps = [jnp.exp2(s - m) for s in ss]
    l = ps[0].sum(axis=-1, keepdims=True)
    for p in ps[1:]:
        l = l + p.sum(axis=-1, keepdims=True)
    return [p.astype(BF16) for p in ps], 1.0 / l


def _attend(qd, kds, vds, qm, kms, vms, lam, gsub, lam_init, ob_ref, od_ref, out_rows):
    n = qd.shape[0]
    lane = lax.broadcasted_iota(jnp.int32, (n, BW), 1)
    zero = jnp.zeros((), BF16)
    out_b = jnp.zeros((n, BW), F32)
    out_d = jnp.zeros((n, BW), F32)
    for h in range(HEADS):
        head = (lane >= h * DA_HD) & (lane < (h + 1) * DA_HD)
        o = None
        for half in range(2):
            lo = h * DA_HD + half * DA_D
            qc = jnp.where((lane >= lo) & (lane < lo + DA_D), qd, zero)
            ps, rl = _softmax_parts(qc, kds)
            pv = sum(_dot(p, v) for p, v in zip(ps, vds)) * rl
            o = pv if half == 0 else o - lam * pv
        o = jnp.where(head, o, 0.0)
        ms = jnp.sum(o * o, axis=-1, keepdims=True) * (1.0 / DA_HD)
        out_b = out_b + o * lax.rsqrt(ms + EPS)
        sl = slice(h * LANES, (h + 1) * LANES)
        ps, rl = _softmax_parts(qm[:, sl], [k[:, sl] for k in kms])
        pv = sum(_dot(p, v) for p, v in zip(ps, vms)) * rl
        out_d = out_d + jnp.where(head, pv, 0.0)
    ob_ref[out_rows, :] = (out_b * gsub * (1.0 - lam_init)).astype(BF16)
    od_ref[out_rows, :] = out_d.astype(BF16)


def _attn_kernel(lam_init, qd_ref, kd_ref, vd_ref, cdk_ref, cdv_ref, qm_ref, km_ref, vm_ref, kc_ref, vc_ref,
                 dl_ref, gsub_ref, ob_ref, od_ref):
    i = pl.program_id(0)
    j = pl.program_id(1)
    dl = dl_ref[...]
    lam = (jnp.exp(jnp.sum(dl[0:1] * dl[1:2], axis=-1, keepdims=True))
           - jnp.exp(jnp.sum(dl[2:3] * dl[3:4], axis=-1, keepdims=True)) + lam_init)
    @pl.when(i < CTX_TILES)
    def _():
        for sub in range(TQ // CTX_T):
            rows = pl.ds(pl.multiple_of(j * TQ + sub * CTX_T, CTX_T), CTX_T)
            _attend(qd_ref[rows, :], [kd_ref[rows, :]], [vd_ref[rows, :]],
                    qm_ref[rows, :], [km_ref[rows, :]], [vm_ref[rows, :]],
                    lam, gsub_ref[...], lam_init, ob_ref, od_ref, slice(sub * CTX_T, (sub + 1) * CTX_T))

    @pl.when(i >= CTX_TILES)
    def _():
        rows = pl.ds(pl.multiple_of(j * TQ, TQ), TQ)
        _attend(qd_ref[rows, :], [cdk_ref[...], kd_ref[...]], [cdv_ref[...], vd_ref[...]],
                qm_ref[rows, :], [kc_ref[...], km_ref[...]], [vc_ref[...], vm_ref[...]],
                lam, gsub_ref[...], lam_init, ob_ref, od_ref, slice(None))


def _attn_call(l, qd, kd, vd, cdk, cdv, qm, km, vm, kc, vc, w):
    tile = lambda width: pl.BlockSpec((TILE, width), lambda i, j: (i, 0))
    cache = lambda width: pl.BlockSpec((None, None, PAST, width),
                                       lambda i, j: (jnp.maximum(i - CTX_TILES, 0), l, 0, 0))
    blk = pl.BlockSpec((TQ, BW), lambda i, j: (i * (TILE // TQ) + j, 0))
    return pl.pallas_call(
        functools.partial(_attn_kernel, _lambda_init(l)),
        grid=(TILES, TILE // TQ),
        in_specs=[tile(BW), tile(BW), tile(BW), cache(BW), cache(BW), tile(512), tile(512), tile(BW),
                  cache(512), cache(BW), _layer((4, DA_D), l), _layer((1, BW), l)],
        out_specs=[blk, blk],
        out_shape=[jax.ShapeDtypeStruct((ROWS, BW), BF16)] * 2,
        compiler_params=_params(2),
        name="attention",
    )(qd, kd, vd, cdk, cdv, qm, km, vm, kc, vc, w["dl"], w["gsub"])


def _merge_kernel(n_x, *refs):
    x_refs = refs[:n_x]
    mod_ref, g1_ref, wg_ref, wb_ref, wo_ref, pa_ref, pb_ref, pc_ref, pd_ref, o_ref = refs[n_x:]
    mod = mod_ref[0]
    x = _read_x(x_refs, pl.program_id(0) >= CTX_TILES)
    hb = _norm_mod(x, g1_ref[...], mod[:, D:2 * D], mod[:, 0:D]).astype(BF16)
    merged = None
    for n, br_ref in enumerate((pa_ref, pb_ref, pc_ref, pd_ref)):
        gate = _sigmoid(_dot_nt(hb, wg_ref[n * D:(n + 1) * D, :]))
        t = gate * _dot(br_ref[...], wb_ref[n])
        merged = t if merged is None else merged + t
    o_ref[...] = x + mod[:, 2 * D:3 * D] * _dot(merged.astype(BF16), wo_ref[...])


def _merge_call(l, xs, mods, w, pa, pb, pc, pd):
    return pl.pallas_call(
        functools.partial(_merge_kernel, len(xs)),
        grid=(TILES,),
        in_specs=_x_specs(len(xs) == 2) + [
            _mod_spec(l), _layer((1, D), l), _layer((4 * D, D), l), _layer((4, BW, D), l), _layer((D, D), l),
            _row_spec(BW), _row_spec(BW), _row_spec(BW), _row_spec(BW)],
        out_specs=_row_spec(D),
        out_shape=jax.ShapeDtypeStruct((ROWS, D), F32),
        compiler_params=_params(1),
        name="merge",
    )(*xs, mods, w["g1"], w["win"], w["wbr"], w["wout"], pa, pb, pc, pd)


def _ffn_kernel(n_o, x_ref, mod_ref, g2_ref, wu_ref, cv_ref, wd_ref, *refs):
    o_refs = refs[:n_o]
    hb_ref, acc_ref = refs[n_o:]
    is_lat = pl.program_id(0) >= CTX_TILES
    is_ctx = jnp.logical_not(is_lat)
    mod = mod_ref[0]
    hb_ref[...] = _norm_mod(x_ref[...], g2_ref[...], mod[:, 4 * D:5 * D], mod[:, 3 * D:4 * D]).astype(BF16)
    acc_ref[...] = jnp.zeros((TILE, D), F32)
    row8 = lax.broadcasted_iota(jnp.int32, (SUBLANES, FFN_CHUNK), 0)
    tile_start, tile_end = row8 == 0, row8 == SUBLANES - 1
    seq_start = row8 == jnp.where(is_ctx, 0, -1)
    seq_end = row8 == jnp.where(is_ctx, SUBLANES - 1, -1)

    def conv(cols):
        u = _dot(hb_ref[...], wu_ref[:, cols])
        cv = cv_ref[:, cols]
        prev, nxt = pltpu.roll(u, 1, 0), pltpu.roll(u, TILE - 1, 0)
        prev_parts, nxt_parts = [], []
        for r0 in range(0, TILE, CTX_T):
            r1 = r0 + CTX_T
            start = tile_start if r0 == 0 else seq_start
            end = tile_end if r1 == TILE else seq_end
            prev_parts += [jnp.where(start, 0.0, prev[r0:r0 + SUBLANES]), prev[r0 + SUBLANES:r1]]
            nxt_parts += [nxt[r0:r1 - SUBLANES], jnp.where(end, 0.0, nxt[r1 - SUBLANES:r1])]
        prev, nxt = jnp.concatenate(prev_parts, axis=0), jnp.concatenate(nxt_parts, axis=0)
        return prev * cv[0:1] + u * cv[1:2] + nxt * cv[2:3] + cv[3:4]

    def body(c, carry):
        c0 = pl.multiple_of(c * FFN_CHUNK, FFN_CHUNK)
        h = conv(pl.ds(c0, FFN_CHUNK))
        v = conv(pl.ds(FFN + c0, FFN_CHUNK))
        act = (h * (1.0 + jnp.tanh(h)) * v).astype(BF16)
        acc_ref[...] += _dot(act, wd_ref[pl.ds(c0, FFN_CHUNK), :])
        return carry

    lax.fori_loop(0, FFN_CHUNKS, body, 0)
    y = x_ref[...] + mod[:, 5 * D:6 * D] * acc_ref[...]
    if n_o == 1:
        o_refs[0][...] = y
    else:
        @pl.when(is_ctx)
        def _():
            o_refs[0][...] = y

        @pl.when(is_lat)
        def _():
            o_refs[1][...] = y


def _ffn_call(l, x, mods, w, split_out):
    if split_out:
        out_specs = _x_specs(True)
        out_shape = [jax.ShapeDtypeStruct((CTX_ROWS, D), F32), jax.ShapeDtypeStruct((LAT_ROWS, D), F32)]
    else:
        out_specs = _x_specs(False)
        out_shape = [jax.ShapeDtypeStruct((ROWS, D), F32)]
    return pl.pallas_call(
        functools.partial(_ffn_kernel, len(out_shape)),
        grid=(TILES,),
        in_specs=[_row_spec(D), _mod_spec(l), _layer((1, D), l),
                  _layer((D, 2 * FFN), l), _layer((8, 2 * FFN), l), _layer((FFN, D), l)],
        out_specs=out_specs,
        out_shape=out_shape,
        scratch_shapes=[pltpu.VMEM((TILE, D), BF16), pltpu.VMEM((TILE, D), F32)],
        compiler_params=_params(1),
        name="ffn",
    )(x, mods, w["g2"], w["wup"], w["cv"], w["wdown"])


def _rope_tables():
    half = 8
    inv = ROPE_BASE ** (-np.arange(half, dtype=np.float64) / half)
    t = np.arange(LAT_T)
    rows, cols = (t // GRID_W).astype(np.float64), (t % GRID_W).astype(np.float64)
    ang = np.concatenate([rows[:, None] * inv[None, :]] * 2 + [cols[:, None] * inv[None, :]] * 2, axis=1)
    cos, sin = np.cos(ang), np.sin(ang)

    def diff_layout(a):
        return np.tile(a, (1, LANES // 32))

    def mla_layout(a, fill):
        out = np.full((LAT_T, LANES), fill)
        out[:, MLA_NOPE:MLA_QK] = a
        return out

    ident = lambda layout: np.stack([layout(np.ones_like(cos)), layout(np.zeros_like(cos))])
    ropd = np.stack([ident(diff_layout), np.stack([diff_layout(cos), diff_layout(sin)])])
    ropm = np.stack([np.stack([mla_layout(np.ones_like(cos), 1.0), mla_layout(np.zeros_like(cos), 0.0)]),
                     np.stack([mla_layout(cos, 1.0), mla_layout(sin, 0.0)])])
    return jnp.asarray(ropd, F32), jnp.asarray(ropm, F32)


def _pool_inv_count():
    out = []
    for seq in (CTX_T, LAT_T):
        t = np.arange(TILE) % seq
        cols = [np.repeat((1.0 / (np.minimum(t + half, seq) - np.maximum(t - half, 0)))[:, None], BW // 4, axis=1)
                for half in POOL_HALF]
        out.append(np.concatenate(cols, axis=1))
    return jnp.asarray(np.stack(out), F32)


def _rotate_half_matrix(lanes):
    r = np.zeros((LANES, LANES))
    for j in lanes:
        if j % 16 < 8:
            r[j + 8, j] = -1.0
        else:
            r[j - 8, j] = 1.0
    return np.concatenate([r, r], axis=0)


def _matrix_tables():
    def cs(n):
        k = np.arange(n)
        ang = 2.0 * np.pi * ((k[:, None] * k[None, :]) % n) / n
        return np.cos(ang), np.sin(ang)

    c64, s64 = cs(64)
    eye4 = np.eye(4)
    cl, sl = cs(LAT_T)
    cc_, sc_ = cs(CTX_T)
    ft = np.stack([np.stack([np.kron(eye4, cc_), -np.kron(eye4, sc_)]), np.stack([cl, -sl])])
    as_bf16 = lambda a: jnp.asarray(a, F32).astype(BF16)
    return dict(
        cc=as_bf16(np.kron(eye4, c64)), ss=as_bf16(np.kron(eye4, s64)), ft=as_bf16(ft),
        seg=as_bf16(np.kron(np.eye(BW // DA_D), np.ones((DA_D, DA_D)))),
        ones=as_bf16(np.kron(np.eye(BW // LANES), np.ones((LANES, LANES)))),
        rotd=as_bf16(_rotate_half_matrix(range(LANES))),
        rotm=as_bf16(_rotate_half_matrix(range(MLA_NOPE, MLA_QK))))


def _pad_last(a, lo, hi):
    return jnp.pad(a, ((0, 0),) * (a.ndim - 1) + ((lo, hi),))


def _prep_weights(p):
    win = jnp.swapaxes(p["w_in"], 1, 2).astype(BF16)
    mla = win[:, W_MLA:, :]
    cq, ckv, kr = mla[:, :Q_LORA], mla[:, Q_LORA:Q_LORA + KV_LORA], mla[:, Q_LORA + KV_LORA:]
    pad_rows = lambda a, lo, hi: jnp.pad(a, ((0, 0), (lo, hi), (0, 0)))
    wmla = jnp.concatenate([pad_rows(cq, 0, BW - Q_LORA), ckv, pad_rows(kr, MLA_NOPE, LANES - MLA_QK)], axis=1)
    poolw = jnp.einsum("lgce,gh->lgche", p["pool_w"], jnp.eye(4, dtype=F32)).reshape(DEPTH, BW, BW).astype(BF16)
    wuq = p["w_uq"].reshape(DEPTH, Q_LORA, HEADS, MLA_QK)
    wuq = jnp.pad(wuq, ((0, 0), (0, BW - Q_LORA), (0, 0), (0, LANES - MLA_QK))).reshape(DEPTH, BW, HEADS * LANES)
    wukv = p["w_ukv"].reshape(DEPTH, KV_LORA, HEADS, MLA_NOPE + MLA_V)
    wuk = _pad_last(wukv[..., :MLA_NOPE], 0, LANES - MLA_NOPE).reshape(DEPTH, KV_LORA, HEADS * LANES)
    wuv = wukv[..., MLA_NOPE:].reshape(DEPTH, KV_LORA, HEADS * MLA_V)
    cv = jnp.concatenate([p["conv_w"], p["conv_b"][:, None, :], jnp.zeros((DEPTH, 4, 2 * FFN), F32)], axis=1)
    cv = cv * jnp.where(jnp.arange(2 * FFN) < FFN, 0.5, 1.0)

    row = lambda a, width: _pad_last(a, 0, width - a.shape[-1])[:, None, :]
    tiled = lambda a, reps: jnp.tile(a, (1, reps))[:, None, :]
    return {
        "g1": row(p["g_norm1"], D), "g2": row(p["g_norm2"], D),
        "win": win, "wmla": wmla,
        "poolw": poolw, "pools": row(p["pool_scale"], BW),
        "dgq": tiled(p["diff_qnorm"], LANES // DA_D), "dgk": tiled(p["diff_knorm"], LANES // DA_D),
        "dl": p["diff_lambda"], "gsub": tiled(p["diff_subln"], HEADS),
        "gqa": row(p["mla_qa_norm"], BW), "wuq": wuq.astype(BF16), "gqn": row(p["mla_qnorm"], LANES),
        "gkv": row(p["mla_kv_norm"], KV_LORA), "wuk": wuk.astype(BF16), "wuv": wuv.astype(BF16),
        "gkn": row(p["mla_knorm"], LANES),
        "wbr": p["w_branch"].astype(BF16), "wout": p["w_out"].astype(BF16),
        "wup": p["w_up"].astype(BF16), "cv": cv, "wdown": p["w_down"].astype(BF16),
    }


def kernel(x_prompt, x_sample, c, c_ctx, cache_diff_k, cache_diff_v, cache_mla_ckv, cache_mla_krope, w_ada, b_ada, g_norm1, g_norm2, w_in, pool_w, pool_scale, diff_qnorm, diff_knorm, diff_lambda, diff_subln, mla_qa_norm, w_uq, mla_kv_norm, w_ukv, mla_qnorm, mla_knorm, w_branch, w_out, w_up, conv_w, conv_b, w_down):
    w = _prep_weights(dict(
        w_in=w_in, pool_w=pool_w, pool_scale=pool_scale, diff_qnorm=diff_qnorm, diff_knorm=diff_knorm,
        diff_lambda=diff_lambda, diff_subln=diff_subln, mla_qa_norm=mla_qa_norm, w_uq=w_uq,
        mla_kv_norm=mla_kv_norm, w_ukv=w_ukv, mla_qnorm=mla_qnorm, mla_knorm=mla_knorm, w_branch=w_branch,
        w_out=w_out, w_up=w_up, conv_w=conv_w, conv_b=conv_b, w_down=w_down, g_norm1=g_norm1,
        g_norm2=g_norm2))
    ropd, ropm = _rope_tables()
    consts = dict(_matrix_tables(), ropd=ropd, ropm=ropm, icnt=_pool_inv_count())

    cond = jnp.concatenate([c_ctx[None, :], c, jnp.zeros((16 - 1 - LAT_B, D), F32)], axis=0)
    mods = _mod_call(cond, w_ada, b_ada).reshape(DEPTH, 16, 1, 6 * D)

    kc, vc = _ctxkv_call(cache_mla_ckv, _pad_last(cache_mla_krope, MLA_NOPE, LANES - MLA_QK), w)
    cdk = cache_diff_k.reshape(LAT_B, DEPTH, PAST, BW).astype(BF16)
    cdv = cache_diff_v.reshape(LAT_B, DEPTH, PAST, BW).astype(BF16)

    xs = [x_prompt.reshape(CTX_ROWS, D), x_sample.reshape(LAT_ROWS, D)]
    new = [[], [], [], []]
    for l in range(DEPTH):
        pa, qd, kd, vd, oc, qm, km, vm, *ctx_outs = _inproj_call(l, xs, mods, w, consts)
        ob, od = _attn_call(l, qd, kd, vd, cdk, cdv, qm, km, vm, kc, vc, w)
        x = _merge_call(l, xs, mods, w, pa, ob, oc, od)
        xs = _ffn_call(l, x, mods, w, split_out=(l == DEPTH - 1))
        for acc, a in zip(new, ctx_outs):
            acc.append(a)

    stack = lambda arrs, shape: jnp.stack(arrs, axis=1).reshape((CTX_B, DEPTH, CTX_T) + shape)
    return (xs[0].reshape(CTX_B, CTX_T, D), xs[1].reshape(LAT_B, LAT_T, D),
            stack([a.reshape(CTX_B, CTX_T, BW) for a in new[0]], (HEADS, DA_HD)),
            stack([a.reshape(CTX_B, CTX_T, BW) for a in new[1]], (HEADS, DA_HD)),
            stack([a.reshape(CTX_B, CTX_T, KV_LORA) for a in new[2]], (KV_LORA,)),
            stack([a.reshape(CTX_B, CTX_T, MLA_ROPE) for a in new[3]], (MLA_ROPE,)))
```

```python
import functools
import math

import jax
import jax.numpy as jnp
import numpy as np
from jax import lax
from jax.experimental import pallas as pl
from jax.experimental.pallas import tpu as pltpu

F32 = jnp.float32
BF16 = jnp.bfloat16

D = 1024
DEPTH = 4
CTX_B, CTX_T = 16, 256
LAT_B, LAT_T = 8, 1024
PAST = 512
TILE = 1024
CTX_TILES = CTX_B * CTX_T // TILE
LAT_TILES = LAT_B * LAT_T // TILE
TILES = CTX_TILES + LAT_TILES
ROWS = TILES * TILE
CTX_ROWS = CTX_TILES * TILE
LAT_ROWS = LAT_TILES * TILE
GRID_W = 64
BW = 256
DA_D = 32
DA_HD = 64
HEADS = 4
MLA_NOPE, MLA_ROPE, MLA_V, MLA_QK = 64, 32, 64, 96
Q_LORA, KV_LORA = 192, 128
FFN = 2816
FFN_CHUNK = 512
SUBLANES = 8
POOL_HALF = (1, 2, 4, 8)
POOL_PAD = 16
ROPE_BASE = 10000.0
EPS = 1e-6
LOG2E = 1.4426950408889634
TQ = 512
LANES = 128
VMEM_LIMIT = 56 * 1024 * 1024

W_BLOCKS = 4 * D // BW
W_MLA = 4 * D + 5 * BW
C_CQ, C_CKV, C_END = 0, 256, 512


def _lambda_init(l):
    return 0.8 - 0.6 * math.exp(-0.3 * l)


def _sigmoid(x):
    return 0.5 * jnp.tanh(0.5 * x) + 0.5


def _rms(x, n):
    ms = jnp.sum(x * x, axis=-1, keepdims=True) * (1.0 / n)
    return x * lax.rsqrt(ms + EPS)


def _dot(a, b):
    return jnp.dot(a, b, preferred_element_type=F32)


def _dot_nt(a, b):
    return lax.dot_general(a, b, (((1,), (1,)), ((), ())), preferred_element_type=F32)


def _const(shape):
    n = len(shape)
    return pl.BlockSpec(shape, lambda *_: (0,) * n, pipeline_mode=pl.Buffered(1))


def _layer(shape, l):
    n = len(shape)
    return pl.BlockSpec((None,) + shape, lambda *_: (l,) + (0,) * n, pipeline_mode=pl.Buffered(1))


def _params(n_grid):
    return pltpu.CompilerParams(dimension_semantics=("arbitrary",) * n_grid, vmem_limit_bytes=VMEM_LIMIT)


def _row_spec(width):
    return pl.BlockSpec((TILE, width), lambda i: (i, 0))


def _ctx_row_spec(width):
    return pl.BlockSpec((TILE, width), lambda i: (jnp.minimum(i, CTX_TILES - 1), 0))


def _lat_row_spec(width):
    return pl.BlockSpec((TILE, width), lambda i: (jnp.maximum(i - CTX_TILES, 0), 0))


def _x_specs(split):
    return [_ctx_row_spec(D), _lat_row_spec(D)] if split else [_row_spec(D)]


def _read_x(x_refs, is_lat):
    if len(x_refs) == 2:
        return jnp.where(is_lat, x_refs[1][...], x_refs[0][...])
    return x_refs[0][...]


def _in_cols(l, n):
    return pl.BlockSpec((None, BW, D), lambda *_: (l, n, 0), pipeline_mode=pl.Buffered(1))


def _mod_spec(l):
    return pl.BlockSpec((None, 1, 1, 6 * D), lambda i: (l, jnp.maximum(i - (CTX_TILES - 1), 0), 0, 0))


def _group_spec(shape):
    n = len(shape)
    return pl.BlockSpec((None,) + shape, lambda i: (jnp.minimum(i // CTX_TILES, 1),) + (0,) * n,
                        pipeline_mode=pl.Buffered(1))


def _mod_kernel(c_ref, w_ref, b_ref, o_ref):
    c = c_ref[...]
    s = (c * _sigmoid(c)).astype(BF16)
    o_ref[0] = _dot(s, w_ref[0].astype(BF16)) + b_ref[0]


def _mod_call(cond, w_ada, b_ada):
    nc = 4
    cw = 6 * D // nc
    return pl.pallas_call(
        _mod_kernel,
        grid=(DEPTH, nc),
        in_specs=[
            pl.BlockSpec((16, D), lambda l, j: (0, 0)),
            pl.BlockSpec((1, D, cw), lambda l, j: (l, 0, j)),
            pl.BlockSpec((1, 1, cw), lambda l, j: (l, 0, j)),
        ],
        out_specs=pl.BlockSpec((1, 16, cw), lambda l, j: (l, 0, j)),
        out_shape=jax.ShapeDtypeStruct((DEPTH, 16, 6 * D), F32),
        compiler_params=_params(2),
        name="adaln_mod",
    )(cond, w_ada, b_ada.reshape(DEPTH, 1, 6 * D))


def _mla_key_heads(knope, kr_pad, gkn):
    out = []
    for h in range(HEADS):
        kh = knope[:, h * LANES:(h + 1) * LANES] + kr_pad
        out.append(_rms(kh, MLA_QK) * gkn)
    return out


def _ctxkv_kernel(ckv_ref, kr_ref, wuk_ref, wuv_ref, gkn_ref, kc_ref, vc_ref):
    cb = ckv_ref[...].astype(BF16)
    knope = _dot(cb, wuk_ref[...])
    heads = _mla_key_heads(knope, kr_ref[...], gkn_ref[...])
    for h in range(HEADS):
        kc_ref[:, h * LANES:(h + 1) * LANES] = heads[h].astype(BF16)
    vc_ref[...] = _dot(cb, wuv_ref[...]).astype(BF16)


def _ctxkv_call(ckv, kr_pad, w):
    cache = lambda width: pl.BlockSpec((None, None, PAST, width), lambda b, l: (b, l, 0, 0))
    wspec = lambda r, c: pl.BlockSpec((None, r, c), lambda b, l: (l, 0, 0))
    return pl.pallas_call(
        _ctxkv_kernel,
        grid=(LAT_B, DEPTH),
        in_specs=[cache(KV_LORA), cache(LANES), wspec(KV_LORA, 512), wspec(KV_LORA, BW), wspec(1, LANES)],
        out_specs=[cache(512), cache(BW)],
        out_shape=[
            jax.ShapeDtypeStruct((LAT_B, DEPTH, PAST, 512), BF16),
            jax.ShapeDtypeStruct((LAT_B, DEPTH, PAST, BW), BF16),
        ],
        compiler_params=_params(2),
        name="mla_ctx_keys",
    )(ckv, kr_pad, w["wuk"], w["wuv"], w["gkn"])


def _norm_mod(x, g, sc, sh):
    return _rms(x, D) * (g * (1.0 + sc)) + sh


def _rope(x, tab_ref, rot):
    hi = x.astype(BF16)
    pieces = jnp.concatenate([hi, (x - hi.astype(F32)).astype(BF16)], axis=1)
    return x * tab_ref[0] + _dot(pieces, rot) * tab_ref[1]


def _group_inv_rms(x, ones, n):
    sq = (x * x).astype(BF16)
    ms = jnp.concatenate([_dot(sq[:, c:c + BW], ones) for c in range(0, x.shape[1], BW)], axis=1)
    return lax.rsqrt(ms * (1.0 / n) + EPS)


def _inproj_kernel(n_x, *refs):
    x_refs = refs[:n_x]
    (mod_ref, g1_ref, wxa_ref, wq_ref, wk_ref, wv_ref, wxc_ref, wmla_ref,
     icnt_ref, poolw_ref, pools_ref, seg_ref, ones_ref, rotd_ref, rotm_ref, dgq_ref, dgk_ref,
     ropd_ref, ropm_ref, cc_ref, ss_ref, ft_ref, gqa_ref, wuq_ref, gqn_ref, gkv_ref,
     wuk_ref, wuv_ref, gkn_ref,
     pa_ref, qd_ref, kd_ref, vd_ref, oc_ref, qm_ref, km_ref, vm_ref,
     kcf_ref, vf_ref, ckv_ref, kr_ref, pad_ref) = refs[n_x:]
    is_lat = pl.program_id(0) >= CTX_TILES
    mod = mod_ref[0]
    hb = _norm_mod(_read_x(x_refs, is_lat), g1_ref[...], mod[:, D:2 * D], mod[:, 0:D]).astype(BF16)

    xa = _dot_nt(hb, wxa_ref[...])
    pad_ref[0:POOL_PAD, :] = jnp.zeros((POOL_PAD, BW), F32)
    pad_ref[POOL_PAD + TILE:, :] = jnp.zeros((POOL_PAD, BW), F32)
    pad_ref[POOL_PAD:POOL_PAD + TILE, :] = xa
    lane = lax.broadcasted_iota(jnp.int32, (TILE, BW), 1)
    row8 = lax.broadcasted_iota(jnp.int32, (SUBLANES, BW), 0)

    def term(s):
        v = pad_ref[POOL_PAD + s:POOL_PAD + s + TILE, :]
        kill = row8 < jnp.where(is_lat, 0, -s) if s < 0 else row8 >= jnp.where(is_lat, SUBLANES, SUBLANES - s)
        parts = []
        for r0 in range(0, TILE, CTX_T):
            r1 = r0 + CTX_T
            if s < 0:
                head = v[r0:r0 + SUBLANES]
                parts += [head if r0 == 0 else jnp.where(kill, 0.0, head), v[r0 + SUBLANES:r1]]
            else:
                tail = v[r1 - SUBLANES:r1]
                parts += [v[r0:r1 - SUBLANES], tail if r1 == TILE else jnp.where(kill, 0.0, tail)]
        return jnp.concatenate(parts, axis=0)

    acc = xa
    wins = []
    lo_s, hi_s = 0, 0
    for half in POOL_HALF:
        for s in list(range(-half, lo_s)) + list(range(hi_s + 1, half)):
            acc = acc + term(s)
        lo_s, hi_s = -half, half - 1
        wins.append(acc)
    win = jnp.where(lane < 64, wins[0], jnp.where(lane < 128, wins[1], jnp.where(lane < 192, wins[2], wins[3])))
    pooled = (win * icnt_ref[...] - xa).astype(BF16)
    pa_ref[...] = (_dot(pooled, poolw_ref[...]) * pools_ref[...]).astype(BF16)

    qscale = DA_D ** -0.5 * LOG2E
    q = _dot_nt(hb, wq_ref[...])
    k = _dot_nt(hb, wk_ref[...])
    v = _dot_nt(hb, wv_ref[...])
    rq = _group_inv_rms(q, seg_ref[...], DA_D) * qscale
    rk = _group_inv_rms(k, seg_ref[...], DA_D)
    kns = []
    for s in range(BW // LANES):
        sl = slice(s * LANES, (s + 1) * LANES)
        kg = k[:, sl] * dgk_ref[...]
        kns.append(kg * rk[:, sl])
        qd_ref[:, sl] = (_rope(q[:, sl] * dgq_ref[...], ropd_ref, rotd_ref[...]) * rq[:, sl]).astype(BF16)
        kd_ref[:, sl] = (_rope(kg, ropd_ref, rotd_ref[...]) * rk[:, sl]).astype(BF16)
    vd_ref[...] = v.astype(BF16)

    xcb = _dot_nt(hb, wxc_ref[...]).astype(BF16)
    xcc = _dot(xcb, cc_ref[...]).astype(BF16)
    xcs = _dot(xcb, ss_ref[...]).astype(BF16)
    y = _dot(ft_ref[0], xcc) + _dot(ft_ref[1], xcs)
    fnorm = jnp.where(is_lat, (LAT_T * 64.0) ** -0.5, (CTX_T * 64.0) ** -0.5)
    oc_ref[...] = (y * fnorm).astype(BF16)

    cq = _dot_nt(hb, wmla_ref[C_CQ:C_CQ + BW, :])
    qa = (_rms(cq, Q_LORA) * gqa_ref[...]).astype(BF16)
    qf = _dot(qa, wuq_ref[...])
    mscale = MLA_QK ** -0.5 * LOG2E
    rqm = _group_inv_rms(qf, ones_ref[...], MLA_QK) * mscale
    for h in range(HEADS):
        sl = slice(h * LANES, (h + 1) * LANES)
        qm_ref[:, sl] = (_rope(qf[:, sl] * gqn_ref[...], ropm_ref, rotm_ref[...]) * rqm[:, sl]).astype(BF16)
    ckv_kr = _dot_nt(hb, wmla_ref[C_CKV:C_END, :])
    ckv, kr_pad = ckv_kr[:, :KV_LORA], ckv_kr[:, KV_LORA:]
    ckv_n = _rms(ckv, KV_LORA) * gkv_ref[...]
    cb = ckv_n.astype(BF16)
    knope = _dot(cb, wuk_ref[...])
    kr_rot = _rope(kr_pad * gkn_ref[...], ropm_ref, rotm_ref[...])
    kr_ssq = _dot((kr_pad * kr_pad).astype(BF16), ones_ref[0:LANES, 0:LANES])
    kn_ssq = jnp.concatenate([_dot((knope[:, c:c + BW] * knope[:, c:c + BW]).astype(BF16), ones_ref[...])
                              for c in range(0, HEADS * LANES, BW)], axis=1)
    for h in range(HEADS):
        sl = slice(h * LANES, (h + 1) * LANES)
        rkm = lax.rsqrt((kn_ssq[:, sl] + kr_ssq) * (1.0 / MLA_QK) + EPS)
        km_ref[:, sl] = ((knope[:, sl] * gkn_ref[...] + kr_rot) * rkm).astype(BF16)
    vm_ref[...] = _dot(cb, wuv_ref[...]).astype(BF16)

    @pl.when(jnp.logical_not(is_lat))
    def _():
        for s, kn in enumerate(kns):
            kcf_ref[:, s * LANES:(s + 1) * LANES] = kn
        vf_ref[...] = v
        ckv_ref[...] = ckv_n
        kr_ref[...] = kr_pad[:, MLA_NOPE:MLA_NOPE + MLA_ROPE]


def _inproj_call(l, xs, mods, w, consts):
    row_outs = [(BW, BF16), (BW, BF16), (BW, BF16), (BW, BF16), (BW, BF16), (512, BF16), (512, BF16), (BW, BF16)]
    ctx_outs = [(BW, F32), (BW, F32), (KV_LORA, F32), (MLA_ROPE, F32)]
    return pl.pallas_call(
        functools.partial(_inproj_kernel, len(xs)),
        grid=(TILES,),
        in_specs=_x_specs(len(xs) == 2) + [
            _mod_spec(l), _layer((1, D), l), *[_in_cols(l, W_BLOCKS + n) for n in range(5)],
            _layer((C_END, D), l), _group_spec((TILE, BW)), _layer((BW, BW), l), _layer((1, BW), l),
            _const((BW, BW)), _const((BW, BW)), _const((BW, LANES)), _const((BW, LANES)),
            _layer((1, LANES), l), _layer((1, LANES), l),
            _group_spec((2, TILE, LANES)), _group_spec((2, TILE, LANES)),
            _const((BW, BW)), _const((BW, BW)), _group_spec((2, TILE, TILE)),
            _layer((1, BW), l), _layer((BW, 512), l), _layer((1, LANES), l), _layer((1, KV_LORA), l),
            _layer((KV_LORA, 512), l), _layer((KV_LORA, BW), l), _layer((1, LANES), l),
        ],
        out_specs=[_row_spec(wd) for wd, _ in row_outs] + [_ctx_row_spec(wd) for wd, _ in ctx_outs],
        out_shape=([jax.ShapeDtypeStruct((ROWS, wd), dt) for wd, dt in row_outs]
                   + [jax.ShapeDtypeStruct((CTX_ROWS, wd), dt) for wd, dt in ctx_outs]),
        scratch_shapes=[pltpu.VMEM((TILE + 2 * POOL_PAD, BW), F32)],
        compiler_params=_params(1),
        name="inproj",
    )(*xs, mods, w["g1"], *[w["win"]] * 5, w["wmla"], consts["icnt"], w["poolw"], w["pools"], consts["seg"],
      consts["ones"], consts["rotd"], consts["rotm"], w["dgq"], w["dgk"],
      consts["ropd"], consts["ropm"], consts["cc"], consts["ss"], consts["ft"],
      w["gqa"], w["wuq"], w["gqn"], w["gkv"], w["wuk"], w["wuv"], w["gkn"])


def _softmax_parts(q, ks):
    ss = [_dot_nt(q, k) for k in ks]
    m = ss[0].max(axis=-1, keepdims=True)
    for s in ss[1:]:
        m = jnp.maximum(m, s.max(axis=-1, keepdims=True))
    ps = [jnp.exp2(s - m) for s in ss]
    l = ps[0].sum(axis=-1, keepdims=True)
    for p in ps[1:]:
        l = l + p.sum(axis=-1, keepdims=True)
    return [p.astype(BF16) for p in ps], 1.0 / l


def _attend(qd, kds, vds, qm, kms, vms, lam, gsub, lam_init, ob_ref, od_ref, out_rows):
    n = qd.shape[0]
    lane = lax.broadcasted_iota(jnp.int32, (n, BW), 1)
    zero = jnp.zeros((), BF16)
    out_b = jnp.zeros((n, BW), F32)
    out_d = jnp.zeros((n, BW), F32)
    for h in range(HEADS):
        head = (lane >= h * DA_HD) & (lane < (h + 1) * DA_HD)
        o = None
        for half in range(2):
            lo = h * DA_HD + half * DA_D
            qc = jnp.where((lane >= lo) & (lane < lo + DA_D), qd, zero)
            ps, rl = _softmax_parts(qc, kds)
            pv = sum(_dot(p, v) for p, v in zip(ps, vds)) * rl
            o = pv if half == 0 else o - lam * pv
        o = jnp.where(head, o, 0.0)
        ms = jnp.sum(o * o, axis=-1, keepdims=True) * (1.0 / DA_HD)
        out_b = out_b + o * lax.rsqrt(ms + EPS)
        sl = slice(h * LANES, (h + 1) * LANES)
        ps, rl = _softmax_parts(qm[:, sl], [k[:, sl] for k in kms])
        pv = sum(_dot(p, v) for p, v in zip(ps, vms)) * rl
        out_d = out_d + jnp.where(head, pv, 0.0)
    ob_ref[out_rows, :] = (out_b * gsub * (1.0 - lam_init)).astype(BF16)
    od_ref[out_rows, :] = out_d.astype(BF16)


def _attn_kernel(lam_init, qd_ref, kd_ref, vd_ref, cdk_ref, cdv_ref, qm_ref, km_ref, vm_ref, kc_ref, vc_ref,
                 dl_ref, gsub_ref, ob_ref, od_ref):
    i = pl.program_id(0)
    j = pl.program_id(1)
    dl = dl_ref[...]
    lam = (jnp.exp(jnp.sum(dl[0:1] * dl[1:2], axis=-1, keepdims=True))
           - jnp.exp(jnp.sum(dl[2:3] * dl[3:4], axis=-1, keepdims=True)) + lam_init)
    @pl.when(i < CTX_TILES)
    def _():
        for sub in range(TQ // CTX_T):
            rows = pl.ds(pl.multiple_of(j * TQ + sub * CTX_T, CTX_T), CTX_T)
            _attend(qd_ref[rows, :], [kd_ref[rows, :]], [vd_ref[rows, :]],
                    qm_ref[rows, :], [km_ref[rows, :]], [vm_ref[rows, :]],
                    lam, gsub_ref[...], lam_init, ob_ref, od_ref, slice(sub * CTX_T, (sub + 1) * CTX_T))

    @pl.when(i >= CTX_TILES)
    def _():
        rows = pl.ds(pl.multiple_of(j * TQ, TQ), TQ)
        _attend(qd_ref[rows, :], [cdk_ref[...], kd_ref[...]], [cdv_ref[...], vd_ref[...]],
                qm_ref[rows, :], [kc_ref[...], km_ref[...]], [vc_ref[...], vm_ref[...]],
                lam, gsub_ref[...], lam_init, ob_ref, od_ref, slice(None))


def _attn_call(l, qd, kd, vd, cdk, cdv, qm, km, vm, kc, vc, w):
    tile = lambda width: pl.BlockSpec((TILE, width), lambda i, j: (i, 0))
    cache = lambda width: pl.BlockSpec((None, None, PAST, width),
                                       lambda i, j: (jnp.maximum(i - CTX_TILES, 0), l, 0, 0))
    blk = pl.BlockSpec((TQ, BW), lambda i, j: (i * (TILE // TQ) + j, 0))
    return pl.pallas_call(
        functools.partial(_attn_kernel, _lambda_init(l)),
        grid=(TILES, TILE // TQ),
        in_specs=[tile(BW), tile(BW), tile(BW), cache(BW), cache(BW), tile(512), tile(512), tile(BW),
                  cache(512), cache(BW), _layer((4, DA_D), l), _layer((1, BW), l)],
        out_specs=[blk, blk],
        out_shape=[jax.ShapeDtypeStruct((ROWS, BW), BF16)] * 2,
        compiler_params=_params(2),
        name="attention",
    )(qd, kd, vd, cdk, cdv, qm, km, vm, kc, vc, w["dl"], w["gsub"])


def _merge_kernel(n_x, *refs):
    x_refs = refs[:n_x]
    mod_ref, g1_ref, wg_ref, wb_ref, wo_ref, pa_ref, pb_ref, pc_ref, pd_ref, o_ref = refs[n_x:]
    mod = mod_ref[0]
    x = _read_x(x_refs, pl.program_id(0) >= CTX_TILES)
    hb = _norm_mod(x, g1_ref[...], mod[:, D:2 * D], mod[:, 0:D]).astype(BF16)
    merged = None
    for n, br_ref in enumerate((pa_ref, pb_ref, pc_ref, pd_ref)):
        gate = _sigmoid(_dot_nt(hb, wg_ref[n * D:(n + 1) * D, :]))
        t = gate * _dot(br_ref[...], wb_ref[n])
        merged = t if merged is None else merged + t
    o_ref[...] = x + mod[:, 2 * D:3 * D] * _dot(merged.astype(BF16), wo_ref[...])


def _merge_call(l, xs, mods, w, pa, pb, pc, pd):
    return pl.pallas_call(
        functools.partial(_merge_kernel, len(xs)),
        grid=(TILES,),
        in_specs=_x_specs(len(xs) == 2) + [
            _mod_spec(l), _layer((1, D), l), _layer((4 * D, D), l), _layer((4, BW, D), l), _layer((D, D), l),
            _row_spec(BW), _row_spec(BW), _row_spec(BW), _row_spec(BW)],
        out_specs=_row_spec(D),
        out_shape=jax.ShapeDtypeStruct((ROWS, D), F32),
        compiler_params=_params(1),
        name="merge",
    )(*xs, mods, w["g1"], w["win"], w["wbr"], w["wout"], pa, pb, pc, pd)


def _ffn_kernel(n_o, x_ref, mod_ref, g2_ref, wu_ref, cv_ref, wd_ref, *refs):
    o_refs = refs[:n_o]
    hb_ref, acc_ref = refs[n_o:]
    is_lat = pl.program_id(0) >= CTX_TILES
    is_ctx = jnp.logical_not(is_lat)
    mod = mod_ref[0]
    hb_ref[...] = _norm_mod(x_ref[...], g2_ref[...], mod[:, 4 * D:5 * D], mod[:, 3 * D:4 * D]).astype(BF16)
    acc_ref[...] = jnp.zeros((TILE, D), F32)

    def edge_masks(width):
        row8 = lax.broadcasted_iota(jnp.int32, (SUBLANES, width), 0)
        return (row8 == 0, row8 == SUBLANES - 1,
                row8 == jnp.where(is_ctx, 0, -1), row8 == jnp.where(is_ctx, SUBLANES - 1, -1))

    def conv(cols, masks):
        tile_start, tile_end, seq_start, seq_end = masks
        u = _dot(hb_ref[...], wu_ref[:, cols])
        cv = cv_ref[:, cols]
        prev, nxt = pltpu.roll(u, 1, 0), pltpu.roll(u, TILE - 1, 0)
        prev_parts, nxt_parts = [], []
        for r0 in range(0, TILE, CTX_T):
            r1 = r0 + CTX_T
            start = tile_start if r0 == 0 else seq_start
            end = tile_end if r1 == TILE else seq_end
            prev_parts += [jnp.where(start, 0.0, prev[r0:r0 + SUBLANES]), prev[r0 + SUBLANES:r1]]
            nxt_parts += [nxt[r0:r1 - SUBLANES], jnp.where(end, 0.0, nxt[r1 - SUBLANES:r1])]
        prev, nxt = jnp.concatenate(prev_parts, axis=0), jnp.concatenate(nxt_parts, axis=0)
        return prev * cv[0:1] + u * cv[1:2] + nxt * cv[2:3] + cv[3:4]

    def chunk(c0, width, masks):
        h = conv(pl.ds(c0, width), masks)
        v = conv(pl.ds(FFN + c0, width), masks)
        act = (h * (1.0 + jnp.tanh(h)) * v).astype(BF16)
        acc_ref[...] += _dot(act, wd_ref[pl.ds(c0, width), :])

    masks = edge_masks(FFN_CHUNK)

    def body(c, carry):
        chunk(pl.multiple_of(c * FFN_CHUNK, FFN_CHUNK), FFN_CHUNK, masks)
        return carry

    lax.fori_loop(0, FFN // FFN_CHUNK, body, 0)
    if FFN % FFN_CHUNK:
        chunk(FFN - FFN % FFN_CHUNK, FFN % FFN_CHUNK, edge_masks(FFN % FFN_CHUNK))
    y = x_ref[...] + mod[:, 5 * D:6 * D] * acc_ref[...]
    if n_o == 1:
        o_refs[0][...] = y
    else:
        @pl.when(is_ctx)
        def _():
            o_refs[0][...] = y

        @pl.when(is_lat)
        def _():
            o_refs[1][...] = y


def _ffn_call(l, x, mods, w, split_out):
    if split_out:
        out_specs = _x_specs(True)
        out_shape = [jax.ShapeDtypeStruct((CTX_ROWS, D), F32), jax.ShapeDtypeStruct((LAT_ROWS, D), F32)]
    else:
        out_specs = _x_specs(False)
        out_shape = [jax.ShapeDtypeStruct((ROWS, D), F32)]
    return pl.pallas_call(
        functools.partial(_ffn_kernel, len(out_shape)),
        grid=(TILES,),
        in_specs=[_row_spec(D), _mod_spec(l), _layer((1, D), l),
                  _layer((D, 2 * FFN), l), _layer((8, 2 * FFN), l), _layer((FFN, D), l)],
        out_specs=out_specs,
        out_shape=out_shape,
        scratch_shapes=[pltpu.VMEM((TILE, D), BF16), pltpu.VMEM((TILE, D), F32)],
        compiler_params=_params(1),
        name="ffn",
    )(x, mods, w["g2"], w["wup"], w["cv"], w["wdown"])


def _rope_tables():
    half = 8
    inv = ROPE_BASE ** (-np.arange(half, dtype=np.float64) / half)
    t = np.arange(LAT_T)
    rows, cols = (t // GRID_W).astype(np.float64), (t % GRID_W).astype(np.float64)
    ang = np.concatenate([rows[:, None] * inv[None, :]] * 2 + [cols[:, None] * inv[None, :]] * 2, axis=1)
    cos, sin = np.cos(ang), np.sin(ang)

    def diff_layout(a):
        return np.tile(a, (1, LANES // 32))

    def mla_layout(a, fill):
        out = np.full((LAT_T, LANES), fill)
        out[:, MLA_NOPE:MLA_QK] = a
        return out

    ident = lambda layout: np.stack([layout(np.ones_like(cos)), layout(np.zeros_like(cos))])
    ropd = np.stack([ident(diff_layout), np.stack([diff_layout(cos), diff_layout(sin)])])
    ropm = np.stack([np.stack([mla_layout(np.ones_like(cos), 1.0), mla_layout(np.zeros_like(cos), 0.0)]),
                     np.stack([mla_layout(cos, 1.0), mla_layout(sin, 0.0)])])
    return jnp.asarray(ropd, F32), jnp.asarray(ropm, F32)


def _pool_inv_count():
    out = []
    for seq in (CTX_T, LAT_T):
        t = np.arange(TILE) % seq
        cols = [np.repeat((1.0 / (np.minimum(t + half, seq) - np.maximum(t - half, 0)))[:, None], BW // 4, axis=1)
                for half in POOL_HALF]
        out.append(np.concatenate(cols, axis=1))
    return jnp.asarray(np.stack(out), F32)


def _rotate_half_matrix(lanes):
    r = np.zeros((LANES, LANES))
    for j in lanes:
        if j % 16 < 8:
            r[j + 8, j] = -1.0
        else:
            r[j - 8, j] = 1.0
    return np.concatenate([r, r], axis=0)


def _matrix_tables():
    def cs(n):
        k = np.arange(n)
        ang = 2.0 * np.pi * ((k[:, None] * k[None, :]) % n) / n
        return np.cos(ang), np.sin(ang)

    c64, s64 = cs(64)
    eye4 = np.eye(4)
    cl, sl = cs(LAT_T)
    cc_, sc_ = cs(CTX_T)
    ft = np.stack([np.stack([np.kron(eye4, cc_), -np.kron(eye4, sc_)]), np.stack([cl, -sl])])
    as_bf16 = lambda a: jnp.asarray(a, F32).astype(BF16)
    return dict(
        cc=as_bf16(np.kron(eye4, c64)), ss=as_bf16(np.kron(eye4, s64)), ft=as_bf16(ft),
        seg=as_bf16(np.kron(np.eye(BW // DA_D), np.ones((DA_D, DA_D)))),
        ones=as_bf16(np.kron(np.eye(BW // LANES), np.ones((LANES, LANES)))),
        rotd=as_bf16(_rotate_half_matrix(range(LANES))),
        rotm=as_bf16(_rotate_half_matrix(range(MLA_NOPE, MLA_QK))))


def _pad_last(a, lo, hi):
    return jnp.pad(a, ((0, 0),) * (a.ndim - 1) + ((lo, hi),))


def _prep_weights(p):
    win = jnp.swapaxes(p["w_in"], 1, 2).astype(BF16)
    mla = win[:, W_MLA:, :]
    cq, ckv, kr = mla[:, :Q_LORA], mla[:, Q_LORA:Q_LORA + KV_LORA], mla[:, Q_LORA + KV_LORA:]
    pad_rows = lambda a, lo, hi: jnp.pad(a, ((0, 0), (lo, hi), (0, 0)))
    wmla = jnp.concatenate([pad_rows(cq, 0, BW - Q_LORA), ckv, pad_rows(kr, MLA_NOPE, LANES - MLA_QK)], axis=1)
    poolw = jnp.einsum("lgce,gh->lgche", p["pool_w"], jnp.eye(4, dtype=F32)).reshape(DEPTH, BW, BW).astype(BF16)
    wuq = p["w_uq"].reshape(DEPTH, Q_LORA, HEADS, MLA_QK)
    wuq = jnp.pad(wuq, ((0, 0), (0, BW - Q_LORA), (0, 0), (0, LANES - MLA_QK))).reshape(DEPTH, BW, HEADS * LANES)
    wukv = p["w_ukv"].reshape(DEPTH, KV_LORA, HEADS, MLA_NOPE + MLA_V)
    wuk = _pad_last(wukv[..., :MLA_NOPE], 0, LANES - MLA_NOPE).reshape(DEPTH, KV_LORA, HEADS * LANES)
    wuv = wukv[..., MLA_NOPE:].reshape(DEPTH, KV_LORA, HEADS * MLA_V)
    cv = jnp.concatenate([p["conv_w"], p["conv_b"][:, None, :], jnp.zeros((DEPTH, 4, 2 * FFN), F32)], axis=1)
    cv = cv * jnp.where(jnp.arange(2 * FFN) < FFN, 0.5, 1.0)

    row = lambda a, width: _pad_last(a, 0, width - a.shape[-1])[:, None, :]
    tiled = lambda a, reps: jnp.tile(a, (1, reps))[:, None, :]
    return {
        "g1": row(p["g_norm1"], D), "g2": row(p["g_norm2"], D),
        "win": win, "wmla": wmla,
        "poolw": poolw, "pools": row(p["pool_scale"], BW),
        "dgq": tiled(p["diff_qnorm"], LANES // DA_D), "dgk": tiled(p["diff_knorm"], LANES // DA_D),
        "dl": p["diff_lambda"], "gsub": tiled(p["diff_subln"], HEADS),
        "gqa": row(p["mla_qa_norm"], BW), "wuq": wuq.astype(BF16), "gqn": row(p["mla_qnorm"], LANES),
        "gkv": row(p["mla_kv_norm"], KV_LORA), "wuk": wuk.astype(BF16), "wuv": wuv.astype(BF16),
        "gkn": row(p["mla_knorm"], LANES),
        "wbr": p["w_branch"].astype(BF16), "wout": p["w_out"].astype(BF16),
        "wup": p["w_up"].astype(BF16), "cv": cv, "wdown": p["w_down"].astype(BF16),
    }


def kernel(x_prompt, x_sample, c, c_ctx, cache_diff_k, cache_diff_v, cache_mla_ckv, cache_mla_krope, w_ada, b_ada, g_norm1, g_norm2, w_in, pool_w, pool_scale, diff_qnorm, diff_knorm, diff_lambda, diff_subln, mla_qa_norm, w_uq, mla_kv_norm, w_ukv, mla_qnorm, mla_knorm, w_branch, w_out, w_up, conv_w, conv_b, w_down):
    w = _prep_weights(dict(
        w_in=w_in, pool_w=pool_w, pool_scale=pool_scale, diff_qnorm=diff_qnorm, diff_knorm=diff_knorm,
        diff_lambda=diff_lambda, diff_subln=diff_subln, mla_qa_norm=mla_qa_norm, w_uq=w_uq,
        mla_kv_norm=mla_kv_norm, w_ukv=w_ukv, mla_qnorm=mla_qnorm, mla_knorm=mla_knorm, w_branch=w_branch,
        w_out=w_out, w_up=w_up, conv_w=conv_w, conv_b=conv_b, w_down=w_down, g_norm1=g_norm1,
        g_norm2=g_norm2))
    ropd, ropm = _rope_tables()
    consts = dict(_matrix_tables(), ropd=ropd, ropm=ropm, icnt=_pool_inv_count())

    cond = jnp.concatenate([c_ctx[None, :], c, jnp.zeros((16 - 1 - LAT_B, D), F32)], axis=0)
    mods = _mod_call(cond, w_ada, b_ada).reshape(DEPTH, 16, 1, 6 * D)

    kc, vc = _ctxkv_call(cache_mla_ckv, _pad_last(cache_mla_krope, MLA_NOPE, LANES - MLA_QK), w)
    cdk = cache_diff_k.reshape(LAT_B, DEPTH, PAST, BW).astype(BF16)
    cdv = cache_diff_v.reshape(LAT_B, DEPTH, PAST, BW).astype(BF16)

    xs = [x_prompt.reshape(CTX_ROWS, D), x_sample.reshape(LAT_ROWS, D)]
    new = [[], [], [], []]
    for l in range(DEPTH):
        pa, qd, kd, vd, oc, qm, km, vm, *ctx_outs = _inproj_call(l, xs, mods, w, consts)
        ob, od = _attn_call(l, qd, kd, vd, cdk, cdv, qm, km, vm, kc, vc, w)
        x = _merge_call(l, xs, mods, w, pa, ob, oc, od)
        xs = _ffn_call(l, x, mods, w, split_out=(l == DEPTH - 1))
        for acc, a in zip(new, ctx_outs):
            acc.append(a)

    stack = lambda arrs, shape: jnp.stack(arrs, axis=1).reshape((CTX_B, DEPTH, CTX_T) + shape)
    return (xs[0].reshape(CTX_B, CTX_T, D), xs[1].reshape(LAT_B, LAT_T, D),
            stack([a.reshape(CTX_B, CTX_T, BW) for a in new[0]], (HEADS, DA_HD)),
            stack([a.reshape(CTX_B, CTX_T, BW) for a in new[1]], (HEADS, DA_HD)),
            stack([a.reshape(CTX_B, CTX_T, KV_LORA) for a in new[2]], (KV_LORA,)),
            stack([a.reshape(CTX_B, CTX_T, MLA_ROPE) for a in new[3]], (MLA_ROPE,)))
```

```python
import functools
import math

import jax
import jax.numpy as jnp
import numpy as np
from jax import lax
from jax.experimental import pallas as pl
from jax.experimental.pallas import tpu as pltpu

F32 = jnp.float32
BF16 = jnp.bfloat16

D = 1024
DEPTH = 4
CTX_B, CTX_T = 16, 256
LAT_B, LAT_T = 8, 1024
PAST = 512
TILE = 1024
CTX_TILES = CTX_B * CTX_T // TILE
LAT_TILES = LAT_B * LAT_T // TILE
TILES = CTX_TILES + LAT_TILES
ROWS = TILES * TILE
CTX_ROWS = CTX_TILES * TILE
LAT_ROWS = LAT_TILES * TILE
GRID_W = 64
BW = 256
DA_D = 32
DA_HD = 64
HEADS = 4
MLA_NOPE, MLA_ROPE, MLA_V, MLA_QK = 64, 32, 64, 96
Q_LORA, KV_LORA = 192, 128
FFN = 2816
FFN_CHUNK = 768
SUBLANES = 8
POOL_HALF = (1, 2, 4, 8)
POOL_PAD = 16
ROPE_BASE = 10000.0
EPS = 1e-6
LOG2E = 1.4426950408889634
TQ = 512
LANES = 128
VMEM_LIMIT = 56 * 1024 * 1024

W_BLOCKS = 4 * D // BW
W_MLA = 4 * D + 5 * BW
C_CQ, C_CKV, C_END = 0, 256, 512


def _lambda_init(l):
    return 0.8 - 0.6 * math.exp(-0.3 * l)


def _sigmoid(x):
    return 0.5 * jnp.tanh(0.5 * x) + 0.5


def _rms(x, n):
    ms = jnp.sum(x * x, axis=-1, keepdims=True) * (1.0 / n)
    return x * lax.rsqrt(ms + EPS)


def _dot(a, b):
    return jnp.dot(a, b, preferred_element_type=F32)


def _dot_nt(a, b):
    return lax.dot_general(a, b, (((1,), (1,)), ((), ())), preferred_element_type=F32)


def _const(shape):
    n = len(shape)
    return pl.BlockSpec(shape, lambda *_: (0,) * n, pipeline_mode=pl.Buffered(1))


def _layer(shape, l):
    n = len(shape)
    return pl.BlockSpec((None,) + shape, lambda *_: (l,) + (0,) * n, pipeline_mode=pl.Buffered(1))


def _params(n_grid):
    return pltpu.CompilerParams(dimension_semantics=("arbitrary",) * n_grid, vmem_limit_bytes=VMEM_LIMIT)


def _row_spec(width):
    return pl.BlockSpec((TILE, width), lambda i: (i, 0))


def _ctx_row_spec(width):
    return pl.BlockSpec((TILE, width), lambda i: (jnp.minimum(i, CTX_TILES - 1), 0))


def _lat_row_spec(width):
    return pl.BlockSpec((TILE, width), lambda i: (jnp.maximum(i - CTX_TILES, 0), 0))


def _x_specs(split):
    return [_ctx_row_spec(D), _lat_row_spec(D)] if split else [_row_spec(D)]


def _read_x(x_refs, is_lat):
    if len(x_refs) == 2:
        return jnp.where(is_lat, x_refs[1][...], x_refs[0][...])
    return x_refs[0][...]


def _in_cols(l, n):
    return pl.BlockSpec((None, BW, D), lambda *_: (l, n, 0), pipeline_mode=pl.Buffered(1))


def _mod_spec(l):
    return pl.BlockSpec((None, 1, 1, 6 * D), lambda i: (l, jnp.maximum(i - (CTX_TILES - 1), 0), 0, 0))


def _group_spec(shape):
    n = len(shape)
    return pl.BlockSpec((None,) + shape, lambda i: (jnp.minimum(i // CTX_TILES, 1),) + (0,) * n,
                        pipeline_mode=pl.Buffered(1))


def _mod_kernel(c_ref, w_ref, b_ref, o_ref):
    c = c_ref[...]
    s = (c * _sigmoid(c)).astype(BF16)
    o_ref[0] = _dot(s, w_ref[0].astype(BF16)) + b_ref[0]


def _mod_call(cond, w_ada, b_ada):
    nc = 4
    cw = 6 * D // nc
    return pl.pallas_call(
        _mod_kernel,
        grid=(DEPTH, nc),
        in_specs=[
            pl.BlockSpec((16, D), lambda l, j: (0, 0)),
            pl.BlockSpec((1, D, cw), lambda l, j: (l, 0, j)),
            pl.BlockSpec((1, 1, cw), lambda l, j: (l, 0, j)),
        ],
        out_specs=pl.BlockSpec((1, 16, cw), lambda l, j: (l, 0, j)),
        out_shape=jax.ShapeDtypeStruct((DEPTH, 16, 6 * D), F32),
        compiler_params=_params(2),
        name="adaln_mod",
    )(cond, w_ada, b_ada.reshape(DEPTH, 1, 6 * D))


def _mla_key_heads(knope, kr_pad, gkn):
    out = []
    for h in range(HEADS):
        kh = knope[:, h * LANES:(h + 1) * LANES] + kr_pad
        out.append(_rms(kh, MLA_QK) * gkn)
    return out


def _ctxkv_kernel(ckv_ref, kr_ref, wuk_ref, wuv_ref, gkn_ref, kc_ref, vc_ref):
    cb = ckv_ref[...].astype(BF16)
    knope = _dot(cb, wuk_ref[...])
    heads = _mla_key_heads(knope, kr_ref[...], gkn_ref[...])
    for h in range(HEADS):
        kc_ref[:, h * LANES:(h + 1) * LANES] = heads[h].astype(BF16)
    vc_ref[...] = _dot(cb, wuv_ref[...]).astype(BF16)


def _ctxkv_call(ckv, kr_pad, w):
    cache = lambda width: pl.BlockSpec((None, None, PAST, width), lambda b, l: (b, l, 0, 0))
    wspec = lambda r, c: pl.BlockSpec((None, r, c), lambda b, l: (l, 0, 0))
    return pl.pallas_call(
        _ctxkv_kernel,
        grid=(LAT_B, DEPTH),
        in_specs=[cache(KV_LORA), cache(LANES), wspec(KV_LORA, 512), wspec(KV_LORA, BW), wspec(1, LANES)],
        out_specs=[cache(512), cache(BW)],
        out_shape=[
            jax.ShapeDtypeStruct((LAT_B, DEPTH, PAST, 512), BF16),
            jax.ShapeDtypeStruct((LAT_B, DEPTH, PAST, BW), BF16),
        ],
        compiler_params=_params(2),
        name="mla_ctx_keys",
    )(ckv, kr_pad, w["wuk"], w["wuv"], w["gkn"])


def _norm_mod(x, g, sc, sh):
    return _rms(x, D) * (g * (1.0 + sc)) + sh


def _rope(x, tab_ref, rot):
    hi = x.astype(BF16)
    pieces = jnp.concatenate([hi, (x - hi.astype(F32)).astype(BF16)], axis=1)
    return x * tab_ref[0] + _dot(pieces, rot) * tab_ref[1]


def _group_inv_rms(x, ones, n):
    sq = (x * x).astype(BF16)
    ms = jnp.concatenate([_dot(sq[:, c:c + BW], ones) for c in range(0, x.shape[1], BW)], axis=1)
    return lax.rsqrt(ms * (1.0 / n) + EPS)


def _inproj_kernel(n_x, *refs):
    x_refs = refs[:n_x]
    (mod_ref, g1_ref, wxa_ref, wq_ref, wk_ref, wv_ref, wxc_ref, wmla_ref,
     icnt_ref, poolw_ref, pools_ref, seg_ref, ones_ref, rotd_ref, rotm_ref, dgq_ref, dgk_ref,
     ropd_ref, ropm_ref, cc_ref, ss_ref, ft_ref, gqa_ref, wuq_ref, gqn_ref, gkv_ref,
     wuk_ref, wuv_ref, gkn_ref,
     pa_ref, qd_ref, kd_ref, vd_ref, oc_ref, qm_ref, km_ref, vm_ref,
     kcf_ref, vf_ref, ckv_ref, kr_ref, pad_ref) = refs[n_x:]
    is_lat = pl.program_id(0) >= CTX_TILES
    mod = mod_ref[0]
    hb = _norm_mod(_read_x(x_refs, is_lat), g1_ref[...], mod[:, D:2 * D], mod[:, 0:D]).astype(BF16)

    xa = _dot_nt(hb, wxa_ref[...])
    pad_ref[0:POOL_PAD, :] = jnp.zeros((POOL_PAD, BW), F32)
    pad_ref[POOL_PAD + TILE:, :] = jnp.zeros((POOL_PAD, BW), F32)
    pad_ref[POOL_PAD:POOL_PAD + TILE, :] = xa
    lane = lax.broadcasted_iota(jnp.int32, (TILE, BW), 1)
    row8 = lax.broadcasted_iota(jnp.int32, (SUBLANES, BW), 0)

    def term(s):
        v = pad_ref[POOL_PAD + s:POOL_PAD + s + TILE, :]
        kill = row8 < jnp.where(is_lat, 0, -s) if s < 0 else row8 >= jnp.where(is_lat, SUBLANES, SUBLANES - s)
        parts = []
        for r0 in range(0, TILE, CTX_T):
            r1 = r0 + CTX_T
            if s < 0:
                head = v[r0:r0 + SUBLANES]
                parts += [head if r0 == 0 else jnp.where(kill, 0.0, head), v[r0 + SUBLANES:r1]]
            else:
                tail = v[r1 - SUBLANES:r1]
                parts += [v[r0:r1 - SUBLANES], tail if r1 == TILE else jnp.where(kill, 0.0, tail)]
        return jnp.concatenate(parts, axis=0)

    acc = xa
    wins = []
    lo_s, hi_s = 0, 0
    for half in POOL_HALF:
        for s in list(range(-half, lo_s)) + list(range(hi_s + 1, half)):
            acc = acc + term(s)
        lo_s, hi_s = -half, half - 1
        wins.append(acc)
    win = jnp.where(lane < 64, wins[0], jnp.where(lane < 128, wins[1], jnp.where(lane < 192, wins[2], wins[3])))
    pooled = (win * icnt_ref[...] - xa).astype(BF16)
    pa_ref[...] = (_dot(pooled, poolw_ref[...]) * pools_ref[...]).astype(BF16)

    qscale = DA_D ** -0.5 * LOG2E
    q = _dot_nt(hb, wq_ref[...])
    k = _dot_nt(hb, wk_ref[...])
    v = _dot_nt(hb, wv_ref[...])
    rq = _group_inv_rms(q, seg_ref[...], DA_D) * qscale
    rk = _group_inv_rms(k, seg_ref[...], DA_D)
    kns = []
    for s in range(BW // LANES):
        sl = slice(s * LANES, (s + 1) * LANES)
        kg = k[:, sl] * dgk_ref[...]
        kns.append(kg * rk[:, sl])
        qd_ref[:, sl] = (_rope(q[:, sl] * dgq_ref[...], ropd_ref, rotd_ref[...]) * rq[:, sl]).astype(BF16)
        kd_ref[:, sl] = (_rope(kg, ropd_ref, rotd_ref[...]) * rk[:, sl]).astype(BF16)
    vd_ref[...] = v.astype(BF16)

    xcb = _dot_nt(hb, wxc_ref[...]).astype(BF16)
    xcc = _dot(xcb, cc_ref[...]).astype(BF16)
    xcs = _dot(xcb, ss_ref[...]).astype(BF16)
    y = _dot(ft_ref[0], xcc) + _dot(ft_ref[1], xcs)
    fnorm = jnp.where(is_lat, (LAT_T * 64.0) ** -0.5, (CTX_T * 64.0) ** -0.5)
    oc_ref[...] = (y * fnorm).astype(BF16)

    cq = _dot_nt(hb, wmla_ref[C_CQ:C_CQ + BW, :])
    qa = (_rms(cq, Q_LORA) * gqa_ref[...]).astype(BF16)
    qf = _dot(qa, wuq_ref[...])
    mscale = MLA_QK ** -0.5 * LOG2E
    rqm = _group_inv_rms(qf, ones_ref[...], MLA_QK) * mscale
    for h in range(HEADS):
        sl = slice(h * LANES, (h + 1) * LANES)
        qm_ref[:, sl] = (_rope(qf[:, sl] * gqn_ref[...], ropm_ref, rotm_ref[...]) * rqm[:, sl]).astype(BF16)
    ckv_kr = _dot_nt(hb, wmla_ref[C_CKV:C_END, :])
    ckv, kr_pad = ckv_kr[:, :KV_LORA], ckv_kr[:, KV_LORA:]
    ckv_n = _rms(ckv, KV_LORA) * gkv_ref[...]
    cb = ckv_n.astype(BF16)
    knope = _dot(cb, wuk_ref[...])
    kr_rot = _rope(kr_pad * gkn_ref[...], ropm_ref, rotm_ref[...])
    kr_ssq = _dot((kr_pad * kr_pad).astype(BF16), ones_ref[0:LANES, 0:LANES])
    kn_ssq = jnp.concatenate([_dot((knope[:, c:c + BW] * knope[:, c:c + BW]).astype(BF16), ones_ref[...])
                              for c in range(0, HEADS * LANES, BW)], axis=1)
    for h in range(HEADS):
        sl = slice(h * LANES, (h + 1) * LANES)
        rkm = lax.rsqrt((kn_ssq[:, sl] + kr_ssq) * (1.0 / MLA_QK) + EPS)
        km_ref[:, sl] = ((knope[:, sl] * gkn_ref[...] + kr_rot) * rkm).astype(BF16)
    vm_ref[...] = _dot(cb, wuv_ref[...]).astype(BF16)

    @pl.when(jnp.logical_not(is_lat))
    def _():
        for s, kn in enumerate(kns):
            kcf_ref[:, s * LANES:(s + 1) * LANES] = kn
        vf_ref[...] = v
        ckv_ref[...] = ckv_n
        kr_ref[...] = kr_pad[:, MLA_NOPE:MLA_NOPE + MLA_ROPE]


def _inproj_call(l, xs, mods, w, consts):
    row_outs = [(BW, BF16), (BW, BF16), (BW, BF16), (BW, BF16), (BW, BF16), (512, BF16), (512, BF16), (BW, BF16)]
    ctx_outs = [(BW, F32), (BW, F32), (KV_LORA, F32), (MLA_ROPE, F32)]
    return pl.pallas_call(
        functools.partial(_inproj_kernel, len(xs)),
        grid=(TILES,),
        in_specs=_x_specs(len(xs) == 2) + [
            _mod_spec(l), _layer((1, D), l), *[_in_cols(l, W_BLOCKS + n) for n in range(5)],
            _layer((C_END, D), l), _group_spec((TILE, BW)), _layer((BW, BW), l), _layer((1, BW), l),
            _const((BW, BW)), _const((BW, BW)), _const((BW, LANES)), _const((BW, LANES)),
            _layer((1, LANES), l), _layer((1, LANES), l),
            _group_spec((2, TILE, LANES)), _group_spec((2, TILE, LANES)),
            _const((BW, BW)), _const((BW, BW)), _group_spec((2, TILE, TILE)),
            _layer((1, BW), l), _layer((BW, 512), l), _layer((1, LANES), l), _layer((1, KV_LORA), l),
            _layer((KV_LORA, 512), l), _layer((KV_LORA, BW), l), _layer((1, LANES), l),
        ],
        out_specs=[_row_spec(wd) for wd, _ in row_outs] + [_ctx_row_spec(wd) for wd, _ in ctx_outs],
        out_shape=([jax.ShapeDtypeStruct((ROWS, wd), dt) for wd, dt in row_outs]
                   + [jax.ShapeDtypeStruct((CTX_ROWS, wd), dt) for wd, dt in ctx_outs]),
        scratch_shapes=[pltpu.VMEM((TILE + 2 * POOL_PAD, BW), F32)],
        compiler_params=_params(1),
        name="inproj",
    )(*xs, mods, w["g1"], *[w["win"]] * 5, w["wmla"], consts["icnt"], w["poolw"], w["pools"], consts["seg"],
      consts["ones"], consts["rotd"], consts["rotm"], w["dgq"], w["dgk"],
      consts["ropd"], consts["ropm"], consts["cc"], consts["ss"], consts["ft"],
      w["gqa"], w["wuq"], w["gqn"], w["gkv"], w["wuk"], w["wuv"], w["gkn"])


def _softmax_parts(q, ks):
    ss = [_dot_nt(q, k) for k in ks]
    m = ss[0].max(axis=-1, keepdims=True)
    for s in ss[1:]:
        m = jnp.maximum(m, s.max(axis=-1, keepdims=True))
    ps = [jnp.exp2(s - m) for s in ss]
    l = ps[0].sum(axis=-1, keepdims=True)
    for p in ps[1:]:
        l = l + p.sum(axis=-1, keepdims=True)
    return [p.astype(BF16) for p in ps], 1.0 / l


def _attend(qd, kds, vds, qm, kms, vms, lam, gsub, lam_init, ob_ref, od_ref, out_rows):
    n = qd.shape[0]
    lane = lax.broadcasted_iota(jnp.int32, (n, BW), 1)
    zero = jnp.zeros((), BF16)
    out_b = jnp.zeros((n, BW), F32)
    out_d = jnp.zeros((n, BW), F32)
    for h in range(HEADS):
        head = (lane >= h * DA_HD) & (lane < (h + 1) * DA_HD)
        o = None
        for half in range(2):
            lo = h * DA_HD + half * DA_D
            qc = jnp.where((lane >= lo) & (lane < lo + DA_D), qd, zero)
            ps, rl = _softmax_parts(qc, kds)
            pv = sum(_dot(p, v) for p, v in zip(ps, vds)) * rl
            o = pv if half == 0 else o - lam * pv
        o = jnp.where(head, o, 0.0)
        ms = jnp.sum(o * o, axis=-1, keepdims=True) * (1.0 / DA_HD)
        out_b = out_b + o * lax.rsqrt(ms + EPS)
        sl = slice(h * LANES, (h + 1) * LANES)
        ps, rl = _softmax_parts(qm[:, sl], [k[:, sl] for k in kms])
        pv = sum(_dot(p, v) for p, v in zip(ps, vms)) * rl
        out_d = out_d + jnp.where(head, pv, 0.0)
    ob_ref[out_rows, :] = (out_b * gsub * (1.0 - lam_init)).astype(BF16)
    od_ref[out_rows, :] = out_d.astype(BF16)


def _attn_kernel(lam_init, qd_ref, kd_ref, vd_ref, cdk_ref, cdv_ref, qm_ref, km_ref, vm_ref, kc_ref, vc_ref,
                 dl_ref, gsub_ref, ob_ref, od_ref):
    i = pl.program_id(0)
    j = pl.program_id(1)
    dl = dl_ref[...]
    lam = (jnp.exp(jnp.sum(dl[0:1] * dl[1:2], axis=-1, keepdims=True))
           - jnp.exp(jnp.sum(dl[2:3] * dl[3:4], axis=-1, keepdims=True)) + lam_init)
    @pl.when(i < CTX_TILES)
    def _():
        for sub in range(TQ // CTX_T):
            rows = pl.ds(pl.multiple_of(j * TQ + sub * CTX_T, CTX_T), CTX_T)
            _attend(qd_ref[rows, :], [kd_ref[rows, :]], [vd_ref[rows, :]],
                    qm_ref[rows, :], [km_ref[rows, :]], [vm_ref[rows, :]],
                    lam, gsub_ref[...], lam_init, ob_ref, od_ref, slice(sub * CTX_T, (sub + 1) * CTX_T))

    @pl.when(i >= CTX_TILES)
    def _():
        rows = pl.ds(pl.multiple_of(j * TQ, TQ), TQ)
        _attend(qd_ref[rows, :], [cdk_ref[...], kd_ref[...]], [cdv_ref[...], vd_ref[...]],
                qm_ref[rows, :], [kc_ref[...], km_ref[...]], [vc_ref[...], vm_ref[...]],
                lam, gsub_ref[...], lam_init, ob_ref, od_ref, slice(None))


def _attn_call(l, qd, kd, vd, cdk, cdv, qm, km, vm, kc, vc, w):
    tile = lambda width: pl.BlockSpec((TILE, width), lambda i, j: (i, 0))
    cache = lambda width: pl.BlockSpec((None, None, PAST, width),
                                       lambda i, j: (jnp.maximum(i - CTX_TILES, 0), l, 0, 0))
    blk = pl.BlockSpec((TQ, BW), lambda i, j: (i * (TILE // TQ) + j, 0))
    return pl.pallas_call(
        functools.partial(_attn_kernel, _lambda_init(l)),
        grid=(TILES, TILE // TQ),
        in_specs=[tile(BW), tile(BW), tile(BW), cache(BW), cache(BW), tile(512), tile(512), tile(BW),
                  cache(512), cache(BW), _layer((4, DA_D), l), _layer((1, BW), l)],
        out_specs=[blk, blk],
        out_shape=[jax.ShapeDtypeStruct((ROWS, BW), BF16)] * 2,
        compiler_params=_params(2),
        name="attention",
    )(qd, kd, vd, cdk, cdv, qm, km, vm, kc, vc, w["dl"], w["gsub"])


def _merge_kernel(n_x, *refs):
    x_refs = refs[:n_x]
    mod_ref, g1_ref, wg_ref, wb_ref, wo_ref, pa_ref, pb_ref, pc_ref, pd_ref, o_ref = refs[n_x:]
    mod = mod_ref[0]
    x = _read_x(x_refs, pl.program_id(0) >= CTX_TILES)
    hb = _norm_mod(x, g1_ref[...], mod[:, D:2 * D], mod[:, 0:D]).astype(BF16)
    merged = None
    for n, br_ref in enumerate((pa_ref, pb_ref, pc_ref, pd_ref)):
        gate = _sigmoid(_dot_nt(hb, wg_ref[n * D:(n + 1) * D, :]))
        t = gate * _dot(br_ref[...], wb_ref[n])
        merged = t if merged is None else merged + t
    o_ref[...] = x + mod[:, 2 * D:3 * D] * _dot(merged.astype(BF16), wo_ref[...])


def _merge_call(l, xs, mods, w, pa, pb, pc, pd):
    return pl.pallas_call(
        functools.partial(_merge_kernel, len(xs)),
        grid=(TILES,),
        in_specs=_x_specs(len(xs) == 2) + [
            _mod_spec(l), _layer((1, D), l), _layer((4 * D, D), l), _layer((4, BW, D), l), _layer((D, D), l),
            _row_spec(BW), _row_spec(BW), _row_spec(BW), _row_spec(BW)],
        out_specs=_row_spec(D),
        out_shape=jax.ShapeDtypeStruct((ROWS, D), F32),
        compiler_params=_params(1),
        name="merge",
    )(*xs, mods, w["g1"], w["win"], w["wbr"], w["wout"], pa, pb, pc, pd)


def _ffn_kernel(n_o, x_ref, mod_ref, g2_ref, wu_ref, cv_ref, wd_ref, *refs):
    o_refs = refs[:n_o]
    hb_ref, acc_ref = refs[n_o:]
    is_lat = pl.program_id(0) >= CTX_TILES
    is_ctx = jnp.logical_not(is_lat)
    mod = mod_ref[0]
    hb_ref[...] = _norm_mod(x_ref[...], g2_ref[...], mod[:, 4 * D:5 * D], mod[:, 3 * D:4 * D]).astype(BF16)
    acc_ref[...] = jnp.zeros((TILE, D), F32)

    def edge_masks(width):
        row8 = lax.broadcasted_iota(jnp.int32, (SUBLANES, width), 0)
        return (row8 == 0, row8 == SUBLANES - 1,
                row8 == jnp.where(is_ctx, 0, -1), row8 == jnp.where(is_ctx, SUBLANES - 1, -1))

    def conv(cols, masks):
        tile_start, tile_end, seq_start, seq_end = masks
        u = _dot(hb_ref[...], wu_ref[:, cols])
        cv = cv_ref[:, cols]
        prev, nxt = pltpu.roll(u, 1, 0), pltpu.roll(u, TILE - 1, 0)
        prev_parts, nxt_parts = [], []
        for r0 in range(0, TILE, CTX_T):
            r1 = r0 + CTX_T
            start = tile_start if r0 == 0 else seq_start
            end = tile_end if r1 == TILE else seq_end
            prev_parts += [jnp.where(start, 0.0, prev[r0:r0 + SUBLANES]), prev[r0 + SUBLANES:r1]]
            nxt_parts += [nxt[r0:r1 - SUBLANES], jnp.where(end, 0.0, nxt[r1 - SUBLANES:r1])]
        prev, nxt = jnp.concatenate(prev_parts, axis=0), jnp.concatenate(nxt_parts, axis=0)
        return prev * cv[0:1] + u * cv[1:2] + nxt * cv[2:3] + cv[3:4]

    def chunk(c0, width, masks):
        h = conv(pl.ds(c0, width), masks)
        v = conv(pl.ds(FFN + c0, width), masks)
        act = (h * (1.0 + jnp.tanh(h)) * v).astype(BF16)
        acc_ref[...] += _dot(act, wd_ref[pl.ds(c0, width), :])

    masks = edge_masks(FFN_CHUNK)

    def body(c, carry):
        chunk(pl.multiple_of(c * FFN_CHUNK, FFN_CHUNK), FFN_CHUNK, masks)
        return carry

    lax.fori_loop(0, FFN // FFN_CHUNK, body, 0)
    if FFN % FFN_CHUNK:
        chunk(FFN - FFN % FFN_CHUNK, FFN % FFN_CHUNK, edge_masks(FFN % FFN_CHUNK))
    y = x_ref[...] + mod[:, 5 * D:6 * D] * acc_ref[...]
    if n_o == 1:
        o_refs[0][...] = y
    else:
        @pl.when(is_ctx)
        def _():
            o_refs[0][...] = y

        @pl.when(is_lat)
        def _():
            o_refs[1][...] = y


def _ffn_call(l, x, mods, w, split_out):
    if split_out:
        out_specs = [pl.BlockSpec(s.block_shape, s.index_map, pipeline_mode=pl.Buffered(1)) for s in _x_specs(True)]
        out_shape = [jax.ShapeDtypeStruct((CTX_ROWS, D), F32), jax.ShapeDtypeStruct((LAT_ROWS, D), F32)]
    else:
        out_specs = _x_specs(False)
        out_shape = [jax.ShapeDtypeStruct((ROWS, D), F32)]
    return pl.pallas_call(
        functools.partial(_ffn_kernel, len(out_shape)),
        grid=(TILES,),
        in_specs=[_row_spec(D), _mod_spec(l), _layer((1, D), l),
                  _layer((D, 2 * FFN), l), _layer((8, 2 * FFN), l), _layer((FFN, D), l)],
        out_specs=out_specs,
        out_shape=out_shape,
        scratch_shapes=[pltpu.VMEM((TILE, D), BF16), pltpu.VMEM((TILE, D), F32)],
        compiler_params=_params(1),
        name="ffn",
    )(x, mods, w["g2"], w["wup"], w["cv"], w["wdown"])


def _rope_tables():
    half = 8
    inv = ROPE_BASE ** (-np.arange(half, dtype=np.float64) / half)
    t = np.arange(LAT_T)
    rows, cols = (t // GRID_W).astype(np.float64), (t % GRID_W).astype(np.float64)
    ang = np.concatenate([rows[:, None] * inv[None, :]] * 2 + [cols[:, None] * inv[None, :]] * 2, axis=1)
    cos, sin = np.cos(ang), np.sin(ang)

    def diff_layout(a):
        return np.tile(a, (1, LANES // 32))

    def mla_layout(a, fill):
        out = np.full((LAT_T, LANES), fill)
        out[:, MLA_NOPE:MLA_QK] = a
        return out

    ident = lambda layout: np.stack([layout(np.ones_like(cos)), layout(np.zeros_like(cos))])
    ropd = np.stack([ident(diff_layout), np.stack([diff_layout(cos), diff_layout(sin)])])
    ropm = np.stack([np.stack([mla_layout(np.ones_like(cos), 1.0), mla_layout(np.zeros_like(cos), 0.0)]),
                     np.stack([mla_layout(cos, 1.0), mla_layout(sin, 0.0)])])
    return jnp.asarray(ropd, F32), jnp.asarray(ropm, F32)


def _pool_inv_count():
    out = []
    for seq in (CTX_T, LAT_T):
        t = np.arange(TILE) % seq
        cols = [np.repeat((1.0 / (np.minimum(t + half, seq) - np.maximum(t - half, 0)))[:, None], BW // 4, axis=1)
                for half in POOL_HALF]
        out.append(np.concatenate(cols, axis=1))
    return jnp.asarray(np.stack(out), F32)


def _rotate_half_matrix(lanes):
    r = np.zeros((LANES, LANES))
    for j in lanes:
        if j % 16 < 8:
            r[j + 8, j] = -1.0
        else:
            r[j - 8, j] = 1.0
    return np.concatenate([r, r], axis=0)


def _matrix_tables():
    def cs(n):
        k = np.arange(n)
        ang = 2.0 * np.pi * ((k[:, None] * k[None, :]) % n) / n
        return np.cos(ang), np.sin(ang)

    c64, s64 = cs(64)
    eye4 = np.eye(4)
    cl, sl = cs(LAT_T)
    cc_, sc_ = cs(CTX_T)
    ft = np.stack([np.stack([np.kron(eye4, cc_), -np.kron(eye4, sc_)]), np.stack([cl, -sl])])
    as_bf16 = lambda a: jnp.asarray(a, F32).astype(BF16)
    return dict(
        cc=as_bf16(np.kron(eye4, c64)), ss=as_bf16(np.kron(eye4, s64)), ft=as_bf16(ft),
        seg=as_bf16(np.kron(np.eye(BW // DA_D), np.ones((DA_D, DA_D)))),
        ones=as_bf16(np.kron(np.eye(BW // LANES), np.ones((LANES, LANES)))),
        rotd=as_bf16(_rotate_half_matrix(range(LANES))),
        rotm=as_bf16(_rotate_half_matrix(range(MLA_NOPE, MLA_QK))))


def _pad_last(a, lo, hi):
    return jnp.pad(a, ((0, 0),) * (a.ndim - 1) + ((lo, hi),))


def _prep_weights(p):
    win = jnp.swapaxes(p["w_in"], 1, 2).astype(BF16)
    mla = win[:, W_MLA:, :]
    cq, ckv, kr = mla[:, :Q_LORA], mla[:, Q_LORA:Q_LORA + KV_LORA], mla[:, Q_LORA + KV_LORA:]
    pad_rows = lambda a, lo, hi: jnp.pad(a, ((0, 0), (lo, hi), (0, 0)))
    wmla = jnp.concatenate([pad_rows(cq, 0, BW - Q_LORA), ckv, pad_rows(kr, MLA_NOPE, LANES - MLA_QK)], axis=1)
    poolw = jnp.einsum("lgce,gh->lgche", p["pool_w"], jnp.eye(4, dtype=F32)).reshape(DEPTH, BW, BW).astype(BF16)
    wuq = p["w_uq"].reshape(DEPTH, Q_LORA, HEADS, MLA_QK)
    wuq = jnp.pad(wuq, ((0, 0), (0, BW - Q_LORA), (0, 0), (0, LANES - MLA_QK))).reshape(DEPTH, BW, HEADS * LANES)
    wukv = p["w_ukv"].reshape(DEPTH, KV_LORA, HEADS, MLA_NOPE + MLA_V)
    wuk = _pad_last(wukv[..., :MLA_NOPE], 0, LANES - MLA_NOPE).reshape(DEPTH, KV_LORA, HEADS * LANES)
    wuv = wukv[..., MLA_NOPE:].reshape(DEPTH, KV_LORA, HEADS * MLA_V)
    cv = jnp.concatenate([p["conv_w"], p["conv_b"][:, None, :], jnp.zeros((DEPTH, 4, 2 * FFN), F32)], axis=1)
    cv = cv * jnp.where(jnp.arange(2 * FFN) < FFN, 0.5, 1.0)

    row = lambda a, width: _pad_last(a, 0, width - a.shape[-1])[:, None, :]
    tiled = lambda a, reps: jnp.tile(a, (1, reps))[:, None, :]
    return {
        "g1": row(p["g_norm1"], D), "g2": row(p["g_norm2"], D),
        "win": win, "wmla": wmla,
        "poolw": poolw, "pools": row(p["pool_scale"], BW),
        "dgq": tiled(p["diff_qnorm"], LANES // DA_D), "dgk": tiled(p["diff_knorm"], LANES // DA_D),
        "dl": p["diff_lambda"], "gsub": tiled(p["diff_subln"], HEADS),
        "gqa": row(p["mla_qa_norm"], BW), "wuq": wuq.astype(BF16), "gqn": row(p["mla_qnorm"], LANES),
        "gkv": row(p["mla_kv_norm"], KV_LORA), "wuk": wuk.astype(BF16), "wuv": wuv.astype(BF16),
        "gkn": row(p["mla_knorm"], LANES),
        "wbr": p["w_branch"].astype(BF16), "wout": p["w_out"].astype(BF16),
        "wup": p["w_up"].astype(BF16), "cv": cv, "wdown": p["w_down"].astype(BF16),
    }


def kernel(x_prompt, x_sample, c, c_ctx, cache_diff_k, cache_diff_v, cache_mla_ckv, cache_mla_krope, w_ada, b_ada, g_norm1, g_norm2, w_in, pool_w, pool_scale, diff_qnorm, diff_knorm, diff_lambda, diff_subln, mla_qa_norm, w_uq, mla_kv_norm, w_ukv, mla_qnorm, mla_knorm, w_branch, w_out, w_up, conv_w, conv_b, w_down):
    w = _prep_weights(dict(
        w_in=w_in, pool_w=pool_w, pool_scale=pool_scale, diff_qnorm=diff_qnorm, diff_knorm=diff_knorm,
        diff_lambda=diff_lambda, diff_subln=diff_subln, mla_qa_norm=mla_qa_norm, w_uq=w_uq,
        mla_kv_norm=mla_kv_norm, w_ukv=w_ukv, mla_qnorm=mla_qnorm, mla_knorm=mla_knorm, w_branch=w_branch,
        w_out=w_out, w_up=w_up, conv_w=conv_w, conv_b=conv_b, w_down=w_down, g_norm1=g_norm1,
        g_norm2=g_norm2))
    ropd, ropm = _rope_tables()
    consts = dict(_matrix_tables(), ropd=ropd, ropm=ropm, icnt=_pool_inv_count())

    cond = jnp.concatenate([c_ctx[None, :], c, jnp.zeros((16 - 1 - LAT_B, D), F32)], axis=0)
    mods = _mod_call(cond, w_ada, b_ada).reshape(DEPTH, 16, 1, 6 * D)

    kc, vc = _ctxkv_call(cache_mla_ckv, _pad_last(cache_mla_krope, MLA_NOPE, LANES - MLA_QK), w)
    cdk = cache_diff_k.reshape(LAT_B, DEPTH, PAST, BW).astype(BF16)
    cdv = cache_diff_v.reshape(LAT_B, DEPTH, PAST, BW).astype(BF16)

    xs = [x_prompt.reshape(CTX_ROWS, D), x_sample.reshape(LAT_ROWS, D)]
    new = [[], [], [], []]
    for l in range(DEPTH):
        pa, qd, kd, vd, oc, qm, km, vm, *ctx_outs = _inproj_call(l, xs, mods, w, consts)
        ob, od = _attn_call(l, qd, kd, vd, cdk, cdv, qm, km, vm, kc, vc, w)
        x = _merge_call(l, xs, mods, w, pa, ob, oc, od)
        xs = _ffn_call(l, x, mods, w, split_out=(l == DEPTH - 1))
        for acc, a in zip(new, ctx_outs):
            acc.append(a)

    stack = lambda arrs, shape: jnp.stack(arrs, axis=1).reshape((CTX_B, DEPTH, CTX_T) + shape)
    return (xs[0].reshape(CTX_B, CTX_T, D), xs[1].reshape(LAT_B, LAT_T, D),
            stack([a.reshape(CTX_B, CTX_T, BW) for a in new[0]], (HEADS, DA_HD)),
            stack([a.reshape(CTX_B, CTX_T, BW) for a in new[1]], (HEADS, DA_HD)),
            stack([a.reshape(CTX_B, CTX_T, KV_LORA) for a in new[2]], (KV_LORA,)),
            stack([a.reshape(CTX_B, CTX_T, MLA_ROPE) for a in new[3]], (MLA_ROPE,)))
```

```python
import functools
import math

import jax
import jax.numpy as jnp
import numpy as np
from jax import lax
from jax.experimental import pallas as pl
from jax.experimental.pallas import tpu as pltpu

F32 = jnp.float32
BF16 = jnp.bfloat16

D = 1024
DEPTH = 4
CTX_B, CTX_T = 16, 256
LAT_B, LAT_T = 8, 1024
PAST = 512
TILE = 1024
CTX_TILES = CTX_B * CTX_T // TILE
LAT_TILES = LAT_B * LAT_T // TILE
TILES = CTX_TILES + LAT_TILES
ROWS = TILES * TILE
CTX_ROWS = CTX_TILES * TILE
LAT_ROWS = LAT_TILES * TILE
GRID_W = 64
BW = 256
DA_D = 32
DA_HD = 64
HEADS = 4
MLA_NOPE, MLA_ROPE, MLA_V, MLA_QK = 64, 32, 64, 96
Q_LORA, KV_LORA = 192, 128
FFN = 2816
FFN_CHUNK = 768
SUBLANES = 8
POOL_HALF = (1, 2, 4, 8)
POOL_PAD = 16
ROPE_BASE = 10000.0
EPS = 1e-6
LOG2E = 1.4426950408889634
TQ = 512
LANES = 128
VMEM_LIMIT = 56 * 1024 * 1024

W_BLOCKS = 4 * D // BW
W_MLA = 4 * D + 5 * BW
C_CQ, C_CKV, C_END = 0, 256, 512


def _lambda_init(l):
    return 0.8 - 0.6 * math.exp(-0.3 * l)


def _sigmoid(x):
    return 0.5 * jnp.tanh(0.5 * x) + 0.5


def _rms(x, n):
    ms = jnp.sum(x * x, axis=-1, keepdims=True) * (1.0 / n)
    return x * lax.rsqrt(ms + EPS)


def _dot(a, b):
    return jnp.dot(a, b, preferred_element_type=F32)


def _dot_nt(a, b):
    return lax.dot_general(a, b, (((1,), (1,)), ((), ())), preferred_element_type=F32)


def _const(shape):
    n = len(shape)
    return pl.BlockSpec(shape, lambda *_: (0,) * n, pipeline_mode=pl.Buffered(1))


def _layer(shape, l):
    n = len(shape)
    return pl.BlockSpec((None,) + shape, lambda *_: (l,) + (0,) * n, pipeline_mode=pl.Buffered(1))


def _params(n_grid):
    return pltpu.CompilerParams(dimension_semantics=("arbitrary",) * n_grid, vmem_limit_bytes=VMEM_LIMIT)


def _row_spec(width):
    return pl.BlockSpec((TILE, width), lambda i: (i, 0))


def _ctx_row_spec(width):
    return pl.BlockSpec((TILE, width), lambda i: (jnp.minimum(i, CTX_TILES - 1), 0))


def _lat_row_spec(width):
    return pl.BlockSpec((TILE, width), lambda i: (jnp.maximum(i - CTX_TILES, 0), 0))


def _x_specs(split):
    return [_ctx_row_spec(D), _lat_row_spec(D)] if split else [_row_spec(D)]


def _read_x(x_refs, is_lat):
    if len(x_refs) == 2:
        return jnp.where(is_lat, x_refs[1][...], x_refs[0][...])
    return x_refs[0][...]


def _in_cols(l, n):
    return pl.BlockSpec((None, BW, D), lambda *_: (l, n, 0), pipeline_mode=pl.Buffered(1))


def _mod_spec(l):
    return pl.BlockSpec((None, 1, 1, 6 * D), lambda i: (l, jnp.maximum(i - (CTX_TILES - 1), 0), 0, 0))


def _group_spec(shape):
    n = len(shape)
    return pl.BlockSpec((None,) + shape, lambda i: (jnp.minimum(i // CTX_TILES, 1),) + (0,) * n,
                        pipeline_mode=pl.Buffered(1))


def _mod_kernel(c_ref, w_ref, b_ref, o_ref):
    c = c_ref[...]
    s = (c * _sigmoid(c)).astype(BF16)
    o_ref[0] = _dot(s, w_ref[0].astype(BF16)) + b_ref[0]


def _mod_call(cond, w_ada, b_ada):
    nc = 4
    cw = 6 * D // nc
    return pl.pallas_call(
        _mod_kernel,
        grid=(DEPTH, nc),
        in_specs=[
            pl.BlockSpec((16, D), lambda l, j: (0, 0)),
            pl.BlockSpec((1, D, cw), lambda l, j: (l, 0, j)),
            pl.BlockSpec((1, 1, cw), lambda l, j: (l, 0, j)),
        ],
        out_specs=pl.BlockSpec((1, 16, cw), lambda l, j: (l, 0, j)),
        out_shape=jax.ShapeDtypeStruct((DEPTH, 16, 6 * D), F32),
        compiler_params=_params(2),
        name="adaln_mod",
    )(cond, w_ada, b_ada.reshape(DEPTH, 1, 6 * D))


def _mla_key_heads(knope, kr_pad, gkn):
    out = []
    for h in range(HEADS):
        kh = knope[:, h * LANES:(h + 1) * LANES] + kr_pad
        out.append(_rms(kh, MLA_QK) * gkn)
    return out


def _ctxkv_kernel(ckv_ref, kr_ref, wuk_ref, wuv_ref, gkn_ref, kc_ref, vc_ref):
    cb = ckv_ref[...].astype(BF16)
    knope = _dot(cb, wuk_ref[...])
    heads = _mla_key_heads(knope, kr_ref[...], gkn_ref[...])
    for h in range(HEADS):
        kc_ref[:, h * LANES:(h + 1) * LANES] = heads[h].astype(BF16)
    vc_ref[...] = _dot(cb, wuv_ref[...]).astype(BF16)


def _ctxkv_call(ckv, kr_pad, w):
    cache = lambda width: pl.BlockSpec((None, None, PAST, width), lambda b, l: (b, l, 0, 0))
    wspec = lambda r, c: pl.BlockSpec((None, r, c), lambda b, l: (l, 0, 0))
    return pl.pallas_call(
        _ctxkv_kernel,
        grid=(LAT_B, DEPTH),
        in_specs=[cache(KV_LORA), cache(LANES), wspec(KV_LORA, 512), wspec(KV_LORA, BW), wspec(1, LANES)],
        out_specs=[cache(512), cache(BW)],
        out_shape=[
            jax.ShapeDtypeStruct((LAT_B, DEPTH, PAST, 512), BF16),
            jax.ShapeDtypeStruct((LAT_B, DEPTH, PAST, BW), BF16),
        ],
        compiler_params=_params(2),
        name="mla_ctx_keys",
    )(ckv, kr_pad, w["wuk"], w["wuv"], w["gkn"])


def _norm_mod(x, g, sc, sh):
    return _rms(x, D) * (g * (1.0 + sc)) + sh


def _rope(x, tab_ref, rot):
    hi = x.astype(BF16)
    pieces = jnp.concatenate([hi, (x - hi.astype(F32)).astype(BF16)], axis=1)
    return x * tab_ref[0] + _dot(pieces, rot) * tab_ref[1]


def _group_inv_rms(x, ones, n):
    sq = (x * x).astype(BF16)
    ms = jnp.concatenate([_dot(sq[:, c:c + BW], ones) for c in range(0, x.shape[1], BW)], axis=1)
    return lax.rsqrt(ms * (1.0 / n) + EPS)


def _inproj_kernel(n_x, *refs):
    x_refs = refs[:n_x]
    (mod_ref, g1_ref, wxa_ref, wq_ref, wk_ref, wv_ref, wxc_ref, wmla_ref,
     icnt_ref, poolw_ref, pools_ref, seg_ref, ones_ref, rotd_ref, rotm_ref, dgq_ref, dgk_ref,
     ropd_ref, ropm_ref, cc_ref, ss_ref, ft_ref, gqa_ref, wuq_ref, gqn_ref, gkv_ref,
     wuk_ref, wuv_ref, gkn_ref,
     pa_ref, qd_ref, kd_ref, vd_ref, oc_ref, qm_ref, km_ref, vm_ref,
     kcf_ref, vf_ref, ckv_ref, kr_ref, pad_ref) = refs[n_x:]
    is_lat = pl.program_id(0) >= CTX_TILES
    mod = mod_ref[0]
    hb = _norm_mod(_read_x(x_refs, is_lat), g1_ref[...], mod[:, D:2 * D], mod[:, 0:D]).astype(BF16)

    xa = _dot_nt(hb, wxa_ref[...])
    q = _dot_nt(hb, wq_ref[...])
    k = _dot_nt(hb, wk_ref[...])
    v = _dot_nt(hb, wv_ref[...])
    xcb = _dot_nt(hb, wxc_ref[...]).astype(BF16)
    cq = _dot_nt(hb, wmla_ref[C_CQ:C_CQ + BW, :])
    ckv_kr = _dot_nt(hb, wmla_ref[C_CKV:C_END, :])

    pad_ref[0:POOL_PAD, :] = jnp.zeros((POOL_PAD, BW), F32)
    pad_ref[POOL_PAD + TILE:, :] = jnp.zeros((POOL_PAD, BW), F32)
    pad_ref[POOL_PAD:POOL_PAD + TILE, :] = xa
    lane = lax.broadcasted_iota(jnp.int32, (TILE, BW), 1)
    row8 = lax.broadcasted_iota(jnp.int32, (SUBLANES, BW), 0)

    def term(s):
        v = pad_ref[POOL_PAD + s:POOL_PAD + s + TILE, :]
        kill = row8 < jnp.where(is_lat, 0, -s) if s < 0 else row8 >= jnp.where(is_lat, SUBLANES, SUBLANES - s)
        parts = []
        for r0 in range(0, TILE, CTX_T):
            r1 = r0 + CTX_T
            if s < 0:
                head = v[r0:r0 + SUBLANES]
                parts += [head if r0 == 0 else jnp.where(kill, 0.0, head), v[r0 + SUBLANES:r1]]
            else:
                tail = v[r1 - SUBLANES:r1]
                parts += [v[r0:r1 - SUBLANES], tail if r1 == TILE else jnp.where(kill, 0.0, tail)]
        return jnp.concatenate(parts, axis=0)

    acc = xa
    wins = []
    lo_s, hi_s = 0, 0
    for half in POOL_HALF:
        for s in list(range(-half, lo_s)) + list(range(hi_s + 1, half)):
            acc = acc + term(s)
        lo_s, hi_s = -half, half - 1
        wins.append(acc)
    win = jnp.where(lane < 64, wins[0], jnp.where(lane < 128, wins[1], jnp.where(lane < 192, wins[2], wins[3])))
    pooled = (win * icnt_ref[...] - xa).astype(BF16)
    pa_ref[...] = (_dot(pooled, poolw_ref[...]) * pools_ref[...]).astype(BF16)

    qscale = DA_D ** -0.5 * LOG2E
    rq = _group_inv_rms(q, seg_ref[...], DA_D) * qscale
    rk = _group_inv_rms(k, seg_ref[...], DA_D)
    kns = []
    for s in range(BW // LANES):
        sl = slice(s * LANES, (s + 1) * LANES)
        kg = k[:, sl] * dgk_ref[...]
        kns.append(kg * rk[:, sl])
        qd_ref[:, sl] = (_rope(q[:, sl] * dgq_ref[...], ropd_ref, rotd_ref[...]) * rq[:, sl]).astype(BF16)
        kd_ref[:, sl] = (_rope(kg, ropd_ref, rotd_ref[...]) * rk[:, sl]).astype(BF16)
    vd_ref[...] = v.astype(BF16)

    xcc = _dot(xcb, cc_ref[...]).astype(BF16)
    xcs = _dot(xcb, ss_ref[...]).astype(BF16)
    y = _dot(ft_ref[0], xcc) + _dot(ft_ref[1], xcs)
    fnorm = jnp.where(is_lat, (LAT_T * 64.0) ** -0.5, (CTX_T * 64.0) ** -0.5)
    oc_ref[...] = (y * fnorm).astype(BF16)

    qa = (_rms(cq, Q_LORA) * gqa_ref[...]).astype(BF16)
    qf = _dot(qa, wuq_ref[...])
    mscale = MLA_QK ** -0.5 * LOG2E
    rqm = _group_inv_rms(qf, ones_ref[...], MLA_QK) * mscale
    for h in range(HEADS):
        sl = slice(h * LANES, (h + 1) * LANES)
        qm_ref[:, sl] = (_rope(qf[:, sl] * gqn_ref[...], ropm_ref, rotm_ref[...]) * rqm[:, sl]).astype(BF16)
    ckv, kr_pad = ckv_kr[:, :KV_LORA], ckv_kr[:, KV_LORA:]
    ckv_n = _rms(ckv, KV_LORA) * gkv_ref[...]
    cb = ckv_n.astype(BF16)
    knope = _dot(cb, wuk_ref[...])
    kr_rot = _rope(kr_pad * gkn_ref[...], ropm_ref, rotm_ref[...])
    kr_ssq = _dot((kr_pad * kr_pad).astype(BF16), ones_ref[0:LANES, 0:LANES])
    kn_ssq = jnp.concatenate([_dot((knope[:, c:c + BW] * knope[:, c:c + BW]).astype(BF16), ones_ref[...])
                              for c in range(0, HEADS * LANES, BW)], axis=1)
    for h in range(HEADS):
        sl = slice(h * LANES, (h + 1) * LANES)
        rkm = lax.rsqrt((kn_ssq[:, sl] + kr_ssq) * (1.0 / MLA_QK) + EPS)
        km_ref[:, sl] = ((knope[:, sl] * gkn_ref[...] + kr_rot) * rkm).astype(BF16)
    vm_ref[...] = _dot(cb, wuv_ref[...]).astype(BF16)

    @pl.when(jnp.logical_not(is_lat))
    def _():
        for s, kn in enumerate(kns):
            kcf_ref[:, s * LANES:(s + 1) * LANES] = kn
        vf_ref[...] = v
        ckv_ref[...] = ckv_n
        kr_ref[...] = kr_pad[:, MLA_NOPE:MLA_NOPE + MLA_ROPE]


def _inproj_call(l, xs, mods, w, consts):
    row_outs = [(BW, BF16), (BW, BF16), (BW, BF16), (BW, BF16), (BW, BF16), (512, BF16), (512, BF16), (BW, BF16)]
    ctx_outs = [(BW, F32), (BW, F32), (KV_LORA, F32), (MLA_ROPE, F32)]
    return pl.pallas_call(
        functools.partial(_inproj_kernel, len(xs)),
        grid=(TILES,),
        in_specs=_x_specs(len(xs) == 2) + [
            _mod_spec(l), _layer((1, D), l), *[_in_cols(l, W_BLOCKS + n) for n in range(5)],
            _layer((C_END, D), l), _group_spec((TILE, BW)), _layer((BW, BW), l), _layer((1, BW), l),
            _const((BW, BW)), _const((BW, BW)), _const((BW, LANES)), _const((BW, LANES)),
            _layer((1, LANES), l), _layer((1, LANES), l),
            _group_spec((2, TILE, LANES)), _group_spec((2, TILE, LANES)),
            _const((BW, BW)), _const((BW, BW)), _group_spec((2, TILE, TILE)),
            _layer((1, BW), l), _layer((BW, 512), l), _layer((1, LANES), l), _layer((1, KV_LORA), l),
            _layer((KV_LORA, 512), l), _layer((KV_LORA, BW), l), _layer((1, LANES), l),
        ],
        out_specs=[_row_spec(wd) for wd, _ in row_outs] + [_ctx_row_spec(wd) for wd, _ in ctx_outs],
        out_shape=([jax.ShapeDtypeStruct((ROWS, wd), dt) for wd, dt in row_outs]
                   + [jax.ShapeDtypeStruct((CTX_ROWS, wd), dt) for wd, dt in ctx_outs]),
        scratch_shapes=[pltpu.VMEM((TILE + 2 * POOL_PAD, BW), F32)],
        compiler_params=_params(1),
        name="inproj",
    )(*xs, mods, w["g1"], *[w["win"]] * 5, w["wmla"], consts["icnt"], w["poolw"], w["pools"], consts["seg"],
      consts["ones"], consts["rotd"], consts["rotm"], w["dgq"], w["dgk"],
      consts["ropd"], consts["ropm"], consts["cc"], consts["ss"], consts["ft"],
      w["gqa"], w["wuq"], w["gqn"], w["gkv"], w["wuk"], w["wuv"], w["gkn"])


def _softmax_pv(ss, vs):
    m = ss[0].max(axis=-1, keepdims=True)
    for s in ss[1:]:
        m = jnp.maximum(m, s.max(axis=-1, keepdims=True))
    ps = [jnp.exp2(s - m) for s in ss]
    l = ps[0].sum(axis=-1, keepdims=True)
    for p in ps[1:]:
        l = l + p.sum(axis=-1, keepdims=True)
    return sum(_dot(p.astype(BF16), v) for p, v in zip(ps, vs)) * (1.0 / l)


def _attend(qd, kds, vds, qm, kms, vms, lam, gsub, lam_init, ob_ref, od_ref, out_rows):
    n = qd.shape[0]
    lane = lax.broadcasted_iota(jnp.int32, (n, BW), 1)
    zero = jnp.zeros((), BF16)

    jobs = []
    for h in range(HEADS):
        for half in range(2):
            lo = h * DA_HD + half * DA_D
            jobs.append((jnp.where((lane >= lo) & (lane < lo + DA_D), qd, zero), kds, vds))
        sl = slice(h * LANES, (h + 1) * LANES)
        jobs.append((qm[:, sl], [k[:, sl] for k in kms], vms))

    score = lambda job: [_dot_nt(job[0], k) for k in job[1]]
    out_b = jnp.zeros((n, BW), F32)
    out_d = jnp.zeros((n, BW), F32)
    pvs, nxt = [], score(jobs[0])
    for c, job in enumerate(jobs):
        ss, nxt = nxt, score(jobs[c + 1]) if c + 1 < len(jobs) else None
        pvs.append(_softmax_pv(ss, job[2]))
        if len(pvs) == 3:
            h = c // 3
            head = (lane >= h * DA_HD) & (lane < (h + 1) * DA_HD)
            o = jnp.where(head, pvs[0] - lam * pvs[1], 0.0)
            ms = jnp.sum(o * o, axis=-1, keepdims=True) * (1.0 / DA_HD)
            out_b = out_b + o * lax.rsqrt(ms + EPS)
            out_d = out_d + jnp.where(head, pvs[2], 0.0)
            pvs = []
    ob_ref[out_rows, :] = (out_b * gsub * (1.0 - lam_init)).astype(BF16)
    od_ref[out_rows, :] = out_d.astype(BF16)


def _attn_kernel(lam_init, qd_ref, kd_ref, vd_ref, cdk_ref, cdv_ref, qm_ref, km_ref, vm_ref, kc_ref, vc_ref,
                 dl_ref, gsub_ref, ob_ref, od_ref):
    i = pl.program_id(0)
    j = pl.program_id(1)
    dl = dl_ref[...]
    lam = (jnp.exp(jnp.sum(dl[0:1] * dl[1:2], axis=-1, keepdims=True))
           - jnp.exp(jnp.sum(dl[2:3] * dl[3:4], axis=-1, keepdims=True)) + lam_init)
    @pl.when(i < CTX_TILES)
    def _():
        for sub in range(TQ // CTX_T):
            rows = pl.ds(pl.multiple_of(j * TQ + sub * CTX_T, CTX_T), CTX_T)
            _attend(qd_ref[rows, :], [kd_ref[rows, :]], [vd_ref[rows, :]],
                    qm_ref[rows, :], [km_ref[rows, :]], [vm_ref[rows, :]],
                    lam, gsub_ref[...], lam_init, ob_ref, od_ref, slice(sub * CTX_T, (sub + 1) * CTX_T))

    @pl.when(i >= CTX_TILES)
    def _():
        rows = pl.ds(pl.multiple_of(j * TQ, TQ), TQ)
        _attend(qd_ref[rows, :], [cdk_ref[...], kd_ref[...]], [cdv_ref[...], vd_ref[...]],
                qm_ref[rows, :], [kc_ref[...], km_ref[...]], [vc_ref[...], vm_ref[...]],
                lam, gsub_ref[...], lam_init, ob_ref, od_ref, slice(None))


def _attn_call(l, qd, kd, vd, cdk, cdv, qm, km, vm, kc, vc, w):
    tile = lambda width: pl.BlockSpec((TILE, width), lambda i, j: (i, 0))
    cache = lambda width: pl.BlockSpec((None, None, PAST, width),
                                       lambda i, j: (jnp.maximum(i - CTX_TILES, 0), l, 0, 0))
    blk = pl.BlockSpec((TQ, BW), lambda i, j: (i * (TILE // TQ) + j, 0))
    return pl.pallas_call(
        functools.partial(_attn_kernel, _lambda_init(l)),
        grid=(TILES, TILE // TQ),
        in_specs=[tile(BW), tile(BW), tile(BW), cache(BW), cache(BW), tile(512), tile(512), tile(BW),
                  cache(512), cache(BW), _layer((4, DA_D), l), _layer((1, BW), l)],
        out_specs=[blk, blk],
        out_shape=[jax.ShapeDtypeStruct((ROWS, BW), BF16)] * 2,
        compiler_params=_params(2),
        name="attention",
    )(qd, kd, vd, cdk, cdv, qm, km, vm, kc, vc, w["dl"], w["gsub"])


def _merge_kernel(n_x, *refs):
    x_refs = refs[:n_x]
    mod_ref, g1_ref, wg_ref, wb_ref, wo_ref, pa_ref, pb_ref, pc_ref, pd_ref, o_ref = refs[n_x:]
    mod = mod_ref[0]
    x = _read_x(x_refs, pl.program_id(0) >= CTX_TILES)
    hb = _norm_mod(x, g1_ref[...], mod[:, D:2 * D], mod[:, 0:D]).astype(BF16)
    merged = None
    for n, br_ref in enumerate((pa_ref, pb_ref, pc_ref, pd_ref)):
        gate = _sigmoid(_dot_nt(hb, wg_ref[n * D:(n + 1) * D, :]))
        t = gate * _dot(br_ref[...], wb_ref[n])
        merged = t if merged is None else merged + t
    o_ref[...] = x + mod[:, 2 * D:3 * D] * _dot(merged.astype(BF16), wo_ref[...])


def _merge_call(l, xs, mods, w, pa, pb, pc, pd):
    return pl.pallas_call(
        functools.partial(_merge_kernel, len(xs)),
        grid=(TILES,),
        in_specs=_x_specs(len(xs) == 2) + [
            _mod_spec(l), _layer((1, D), l), _layer((4 * D, D), l), _layer((4, BW, D), l), _layer((D, D), l),
            _row_spec(BW), _row_spec(BW), _row_spec(BW), _row_spec(BW)],
        out_specs=_row_spec(D),
        out_shape=jax.ShapeDtypeStruct((ROWS, D), F32),
        compiler_params=_params(1),
        name="merge",
    )(*xs, mods, w["g1"], w["win"], w["wbr"], w["wout"], pa, pb, pc, pd)


def _ffn_kernel(n_o, x_ref, mod_ref, g2_ref, wu_ref, cv_ref, wd_ref, *refs):
    o_refs = refs[:n_o]
    hb_ref, acc_ref = refs[n_o:]
    is_lat = pl.program_id(0) >= CTX_TILES
    is_ctx = jnp.logical_not(is_lat)
    mod = mod_ref[0]
    hb_ref[...] = _norm_mod(x_ref[...], g2_ref[...], mod[:, 4 * D:5 * D], mod[:, 3 * D:4 * D]).astype(BF16)
    acc_ref[...] = jnp.zeros((TILE, D), F32)

    def edge_masks(width):
        row8 = lax.broadcasted_iota(jnp.int32, (SUBLANES, width), 0)
        return (row8 == 0, row8 == SUBLANES - 1,
                row8 == jnp.where(is_ctx, 0, -1), row8 == jnp.where(is_ctx, SUBLANES - 1, -1))

    def conv(cols, masks):
        tile_start, tile_end, seq_start, seq_end = masks
        u = _dot(hb_ref[...], wu_ref[:, cols])
        cv = cv_ref[:, cols]
        prev, nxt = pltpu.roll(u, 1, 0), pltpu.roll(u, TILE - 1, 0)
        prev_parts, nxt_parts = [], []
        for r0 in range(0, TILE, CTX_T):
            r1 = r0 + CTX_T
            start = tile_start if r0 == 0 else seq_start
            end = tile_end if r1 == TILE else seq_end
            prev_parts += [jnp.where(start, 0.0, prev[r0:r0 + SUBLANES]), prev[r0 + SUBLANES:r1]]
            nxt_parts += [nxt[r0:r1 - SUBLANES], jnp.where(end, 0.0, nxt[r1 - SUBLANES:r1])]
        prev, nxt = jnp.concatenate(prev_parts, axis=0), jnp.concatenate(nxt_parts, axis=0)
        return prev * cv[0:1] + u * cv[1:2] + nxt * cv[2:3] + cv[3:4]

    def chunk(c0, width, masks):
        h = conv(pl.ds(c0, width), masks)
        v = conv(pl.ds(FFN + c0, width), masks)
        act = (h * (1.0 + jnp.tanh(h)) * v).astype(BF16)
        acc_ref[...] += _dot(act, wd_ref[pl.ds(c0, width), :])

    masks = edge_masks(FFN_CHUNK)

    def body(c, carry):
        chunk(pl.multiple_of(c * FFN_CHUNK, FFN_CHUNK), FFN_CHUNK, masks)
        return carry

    lax.fori_loop(0, FFN // FFN_CHUNK, body, 0)
    if FFN % FFN_CHUNK:
        chunk(FFN - FFN % FFN_CHUNK, FFN % FFN_CHUNK, edge_masks(FFN % FFN_CHUNK))
    y = x_ref[...] + mod[:, 5 * D:6 * D] * acc_ref[...]
    if n_o == 1:
        o_refs[0][...] = y
    else:
        @pl.when(is_ctx)
        def _():
            o_refs[0][...] = y

        @pl.when(is_lat)
        def _():
            o_refs[1][...] = y


def _ffn_call(l, x, mods, w, split_out):
    if split_out:
        out_specs = [pl.BlockSpec(s.block_shape, s.index_map, pipeline_mode=pl.Buffered(1)) for s in _x_specs(True)]
        out_shape = [jax.ShapeDtypeStruct((CTX_ROWS, D), F32), jax.ShapeDtypeStruct((LAT_ROWS, D), F32)]
    else:
        out_specs = _x_specs(False)
        out_shape = [jax.ShapeDtypeStruct((ROWS, D), F32)]
    return pl.pallas_call(
        functools.partial(_ffn_kernel, len(out_shape)),
        grid=(TILES,),
        in_specs=[_row_spec(D), _mod_spec(l), _layer((1, D), l),
                  _layer((D, 2 * FFN), l), _layer((8, 2 * FFN), l), _layer((FFN, D), l)],
        out_specs=out_specs,
        out_shape=out_shape,
        scratch_shapes=[pltpu.VMEM((TILE, D), BF16), pltpu.VMEM((TILE, D), F32)],
        compiler_params=_params(1),
        name="ffn",
    )(x, mods, w["g2"], w["wup"], w["cv"], w["wdown"])


def _rope_tables():
    half = 8
    inv = ROPE_BASE ** (-np.arange(half, dtype=np.float64) / half)
    t = np.arange(LAT_T)
    rows, cols = (t // GRID_W).astype(np.float64), (t % GRID_W).astype(np.float64)
    ang = np.concatenate([rows[:, None] * inv[None, :]] * 2 + [cols[:, None] * inv[None, :]] * 2, axis=1)
    cos, sin = np.cos(ang), np.sin(ang)

    def diff_layout(a):
        return np.tile(a, (1, LANES // 32))

    def mla_layout(a, fill):
        out = np.full((LAT_T, LANES), fill)
        out[:, MLA_NOPE:MLA_QK] = a
        return out

    ident = lambda layout: np.stack([layout(np.ones_like(cos)), layout(np.zeros_like(cos))])
    ropd = np.stack([ident(diff_layout), np.stack([diff_layout(cos), diff_layout(sin)])])
    ropm = np.stack([np.stack([mla_layout(np.ones_like(cos), 1.0), mla_layout(np.zeros_like(cos), 0.0)]),
                     np.stack([mla_layout(cos, 1.0), mla_layout(sin, 0.0)])])
    return jnp.asarray(ropd, F32), jnp.asarray(ropm, F32)


def _pool_inv_count():
    out = []
    for seq in (CTX_T, LAT_T):
        t = np.arange(TILE) % seq
        cols = [np.repeat((1.0 / (np.minimum(t + half, seq) - np.maximum(t - half, 0)))[:, None], BW // 4, axis=1)
                for half in POOL_HALF]
        out.append(np.concatenate(cols, axis=1))
    return jnp.asarray(np.stack(out), F32)


def _rotate_half_matrix(lanes):
    r = np.zeros((LANES, LANES))
    for j in lanes:
        if j % 16 < 8:
            r[j + 8, j] = -1.0
        else:
            r[j - 8, j] = 1.0
    return np.concatenate([r, r], axis=0)


def _matrix_tables():
    def cs(n):
        k = np.arange(n)
        ang = 2.0 * np.pi * ((k[:, None] * k[None, :]) % n) / n
        return np.cos(ang), np.sin(ang)

    c64, s64 = cs(64)
    eye4 = np.eye(4)
    cl, sl = cs(LAT_T)
    cc_, sc_ = cs(CTX_T)
    ft = np.stack([np.stack([np.kron(eye4, cc_), -np.kron(eye4, sc_)]), np.stack([cl, -sl])])
    as_bf16 = lambda a: jnp.asarray(a, F32).astype(BF16)
    return dict(
        cc=as_bf16(np.kron(eye4, c64)), ss=as_bf16(np.kron(eye4, s64)), ft=as_bf16(ft),
        seg=as_bf16(np.kron(np.eye(BW // DA_D), np.ones((DA_D, DA_D)))),
        ones=as_bf16(np.kron(np.eye(BW // LANES), np.ones((LANES, LANES)))),
        rotd=as_bf16(_rotate_half_matrix(range(LANES))),
        rotm=as_bf16(_rotate_half_matrix(range(MLA_NOPE, MLA_QK))))


def _pad_last(a, lo, hi):
    return jnp.pad(a, ((0, 0),) * (a.ndim - 1) + ((lo, hi),))


def _prep_weights(p):
    win = jnp.swapaxes(p["w_in"], 1, 2).astype(BF16)
    mla = win[:, W_MLA:, :]
    cq, ckv, kr = mla[:, :Q_LORA], mla[:, Q_LORA:Q_LORA + KV_LORA], mla[:, Q_LORA + KV_LORA:]
    pad_rows = lambda a, lo, hi: jnp.pad(a, ((0, 0), (lo, hi), (0, 0)))
    wmla = jnp.concatenate([pad_rows(cq, 0, BW - Q_LORA), ckv, pad_rows(kr, MLA_NOPE, LANES - MLA_QK)], axis=1)
    poolw = jnp.einsum("lgce,gh->lgche", p["pool_w"], jnp.eye(4, dtype=F32)).reshape(DEPTH, BW, BW).astype(BF16)
    wuq = p["w_uq"].reshape(DEPTH, Q_LORA, HEADS, MLA_QK)
    wuq = jnp.pad(wuq, ((0, 0), (0, BW - Q_LORA), (0, 0), (0, LANES - MLA_QK))).reshape(DEPTH, BW, HEADS * LANES)
    wukv = p["w_ukv"].reshape(DEPTH, KV_LORA, HEADS, MLA_NOPE + MLA_V)
    wuk = _pad_last(wukv[..., :MLA_NOPE], 0, LANES - MLA_NOPE).reshape(DEPTH, KV_LORA, HEADS * LANES)
    wuv = wukv[..., MLA_NOPE:].reshape(DEPTH, KV_LORA, HEADS * MLA_V)
    cv = jnp.concatenate([p["conv_w"], p["conv_b"][:, None, :], jnp.zeros((DEPTH, 4, 2 * FFN), F32)], axis=1)
    cv = cv * jnp.where(jnp.arange(2 * FFN) < FFN, 0.5, 1.0)

    row = lambda a, width: _pad_last(a, 0, width - a.shape[-1])[:, None, :]
    tiled = lambda a, reps: jnp.tile(a, (1, reps))[:, None, :]
    return {
        "g1": row(p["g_norm1"], D), "g2": row(p["g_norm2"], D),
        "win": win, "wmla": wmla,
        "poolw": poolw, "pools": row(p["pool_scale"], BW),
        "dgq": tiled(p["diff_qnorm"], LANES // DA_D), "dgk": tiled(p["diff_knorm"], LANES // DA_D),
        "dl": p["diff_lambda"], "gsub": tiled(p["diff_subln"], HEADS),
        "gqa": row(p["mla_qa_norm"], BW), "wuq": wuq.astype(BF16), "gqn": row(p["mla_qnorm"], LANES),
        "gkv": row(p["mla_kv_norm"], KV_LORA), "wuk": wuk.astype(BF16), "wuv": wuv.astype(BF16),
        "gkn": row(p["mla_knorm"], LANES),
        "wbr": p["w_branch"].astype(BF16), "wout": p["w_out"].astype(BF16),
        "wup": p["w_up"].astype(BF16), "cv": cv, "wdown": p["w_down"].astype(BF16),
    }


def kernel(x_prompt, x_sample, c, c_ctx, cache_diff_k, cache_diff_v, cache_mla_ckv, cache_mla_krope, w_ada, b_ada, g_norm1, g_norm2, w_in, pool_w, pool_scale, diff_qnorm, diff_knorm, diff_lambda, diff_subln, mla_qa_norm, w_uq, mla_kv_norm, w_ukv, mla_qnorm, mla_knorm, w_branch, w_out, w_up, conv_w, conv_b, w_down):
    w = _prep_weights(dict(
        w_in=w_in, pool_w=pool_w, pool_scale=pool_scale, diff_qnorm=diff_qnorm, diff_knorm=diff_knorm,
        diff_lambda=diff_lambda, diff_subln=diff_subln, mla_qa_norm=mla_qa_norm, w_uq=w_uq,
        mla_kv_norm=mla_kv_norm, w_ukv=w_ukv, mla_qnorm=mla_qnorm, mla_knorm=mla_knorm, w_branch=w_branch,
        w_out=w_out, w_up=w_up, conv_w=conv_w, conv_b=conv_b, w_down=w_down, g_norm1=g_norm1,
        g_norm2=g_norm2))
    ropd, ropm = _rope_tables()
    consts = dict(_matrix_tables(), ropd=ropd, ropm=ropm, icnt=_pool_inv_count())

    cond = jnp.concatenate([c_ctx[None, :], c, jnp.zeros((16 - 1 - LAT_B, D), F32)], axis=0)
    mods = _mod_call(cond, w_ada, b_ada).reshape(DEPTH, 16, 1, 6 * D)

    kc, vc = _ctxkv_call(cache_mla_ckv, _pad_last(cache_mla_krope, MLA_NOPE, LANES - MLA_QK), w)
    cdk = cache_diff_k.reshape(LAT_B, DEPTH, PAST, BW).astype(BF16)
    cdv = cache_diff_v.reshape(LAT_B, DEPTH, PAST, BW).astype(BF16)

    xs = [x_prompt.reshape(CTX_ROWS, D), x_sample.reshape(LAT_ROWS, D)]
    new = [[], [], [], []]
    for l in range(DEPTH):
        pa, qd, kd, vd, oc, qm, km, vm, *ctx_outs = _inproj_call(l, xs, mods, w, consts)
        ob, od = _attn_call(l, qd, kd, vd, cdk, cdv, qm, km, vm, kc, vc, w)
        x = _merge_call(l, xs, mods, w, pa, ob, oc, od)
        xs = _ffn_call(l, x, mods, w, split_out=(l == DEPTH - 1))
        for acc, a in zip(new, ctx_outs):
            acc.append(a)

    stack = lambda arrs, shape: jnp.stack(arrs, axis=1).reshape((CTX_B, DEPTH, CTX_T) + shape)
    return (xs[0].reshape(CTX_B, CTX_T, D), xs[1].reshape(LAT_B, LAT_T, D),
            stack([a.reshape(CTX_B, CTX_T, BW) for a in new[0]], (HEADS, DA_HD)),
            stack([a.reshape(CTX_B, CTX_T, BW) for a in new[1]], (HEADS, DA_HD)),
            stack([a.reshape(CTX_B, CTX_T, KV_LORA) for a in new[2]], (KV_LORA,)),
            stack([a.reshape(CTX_B, CTX_T, MLA_ROPE) for a in new[3]], (MLA_ROPE,)))
```

```python
import functools
import math

import jax
import jax.numpy as jnp
import numpy as np
from jax import lax
from jax.experimental import pallas as pl
from jax.experimental.pallas import tpu as pltpu

F32 = jnp.float32
BF16 = jnp.bfloat16

D = 1024
DEPTH = 4
CTX_B, CTX_T = 16, 256
LAT_B, LAT_T = 8, 1024
PAST = 512
TILE = 1024
CTX_TILES = CTX_B * CTX_T // TILE
LAT_TILES = LAT_B * LAT_T // TILE
TILES = CTX_TILES + LAT_TILES
ROWS = TILES * TILE
CTX_ROWS = CTX_TILES * TILE
LAT_ROWS = LAT_TILES * TILE
GRID_W = 64
BW = 256
DA_D = 32
DA_HD = 64
HEADS = 4
MLA_NOPE, MLA_ROPE, MLA_V, MLA_QK = 64, 32, 64, 96
Q_LORA, KV_LORA = 192, 128
FFN = 2816
FFN_CHUNK = 768
SUBLANES = 8
POOL_HALF = (1, 2, 4, 8)
POOL_PAD = 16
ROPE_BASE = 10000.0
EPS = 1e-6
LOG2E = 1.4426950408889634
TQ = 512
LANES = 128
VMEM_LIMIT = 56 * 1024 * 1024

W_BLOCKS = 4 * D // BW
W_MLA = 4 * D + 5 * BW
C_CQ, C_CKV, C_END = 0, 256, 512


def _lambda_init(l):
    return 0.8 - 0.6 * math.exp(-0.3 * l)


def _sigmoid(x):
    return 0.5 * jnp.tanh(0.5 * x) + 0.5


def _rms(x, n):
    ms = jnp.sum(x * x, axis=-1, keepdims=True) * (1.0 / n)
    return x * lax.rsqrt(ms + EPS)


def _dot(a, b):
    return jnp.dot(a, b, preferred_element_type=F32)


def _dot_nt(a, b):
    return lax.dot_general(a, b, (((1,), (1,)), ((), ())), preferred_element_type=F32)


def _const(shape):
    n = len(shape)
    return pl.BlockSpec(shape, lambda *_: (0,) * n, pipeline_mode=pl.Buffered(1))


def _layer(shape, l):
    n = len(shape)
    return pl.BlockSpec((None,) + shape, lambda *_: (l,) + (0,) * n, pipeline_mode=pl.Buffered(1))


def _params(n_grid):
    return pltpu.CompilerParams(dimension_semantics=("arbitrary",) * n_grid, vmem_limit_bytes=VMEM_LIMIT)


def _row_spec(width):
    return pl.BlockSpec((TILE, width), lambda i: (i, 0))


def _ctx_row_spec(width):
    return pl.BlockSpec((TILE, width), lambda i: (jnp.minimum(i, CTX_TILES - 1), 0))


def _lat_row_spec(width):
    return pl.BlockSpec((TILE, width), lambda i: (jnp.maximum(i - CTX_TILES, 0), 0))


def _x_specs(split):
    return [_ctx_row_spec(D), _lat_row_spec(D)] if split else [_row_spec(D)]


def _read_x(x_refs, is_lat):
    if len(x_refs) == 2:
        return jnp.where(is_lat, x_refs[1][...], x_refs[0][...])
    return x_refs[0][...]


def _in_cols(l, n):
    return pl.BlockSpec((None, BW, D), lambda *_: (l, n, 0), pipeline_mode=pl.Buffered(1))


def _mod_spec(l):
    return pl.BlockSpec((None, 1, 1, 6 * D), lambda i: (l, jnp.maximum(i - (CTX_TILES - 1), 0), 0, 0))


def _group_spec(shape):
    n = len(shape)
    return pl.BlockSpec((None,) + shape, lambda i: (jnp.minimum(i // CTX_TILES, 1),) + (0,) * n,
                        pipeline_mode=pl.Buffered(1))


def _mod_kernel(c_ref, w_ref, b_ref, o_ref):
    c = c_ref[...]
    s = (c * _sigmoid(c)).astype(BF16)
    o_ref[0] = _dot(s, w_ref[0].astype(BF16)) + b_ref[0]


def _mod_call(cond, w_ada, b_ada):
    nc = 4
    cw = 6 * D // nc
    return pl.pallas_call(
        _mod_kernel,
        grid=(DEPTH, nc),
        in_specs=[
            pl.BlockSpec((16, D), lambda l, j: (0, 0)),
            pl.BlockSpec((1, D, cw), lambda l, j: (l, 0, j)),
            pl.BlockSpec((1, 1, cw), lambda l, j: (l, 0, j)),
        ],
        out_specs=pl.BlockSpec((1, 16, cw), lambda l, j: (l, 0, j)),
        out_shape=jax.ShapeDtypeStruct((DEPTH, 16, 6 * D), F32),
        compiler_params=_params(2),
        name="adaln_mod",
    )(cond, w_ada, b_ada.reshape(DEPTH, 1, 6 * D))


def _mla_key_heads(knope, kr_pad, gkn):
    out = []
    for h in range(HEADS):
        kh = knope[:, h * LANES:(h + 1) * LANES] + kr_pad
        out.append(_rms(kh, MLA_QK) * gkn)
    return out


def _ctxkv_kernel(ckv_ref, kr_ref, wuk_ref, wuv_ref, gkn_ref, kc_ref, vc_ref):
    rows = LAT_B * PAST
    cb = ckv_ref[...].reshape(rows, KV_LORA).astype(BF16)
    knope = _dot(cb, wuk_ref[...])
    heads = _mla_key_heads(knope, kr_ref[...].reshape(rows, LANES), gkn_ref[...])
    for h in range(HEADS):
        kc_ref[:, :, h * LANES:(h + 1) * LANES] = heads[h].astype(BF16).reshape(LAT_B, PAST, LANES)
    vc_ref[...] = _dot(cb, wuv_ref[...]).astype(BF16).reshape(LAT_B, PAST, BW)


def _ctxkv_call(ckv, kr_pad, w):
    cache = lambda width: pl.BlockSpec((LAT_B, None, PAST, width), lambda l: (0, l, 0, 0))
    wspec = lambda r, c: pl.BlockSpec((None, r, c), lambda l: (l, 0, 0))
    return pl.pallas_call(
        _ctxkv_kernel,
        grid=(DEPTH,),
        in_specs=[cache(KV_LORA), cache(LANES), wspec(KV_LORA, 512), wspec(KV_LORA, BW), wspec(1, LANES)],
        out_specs=[cache(512), cache(BW)],
        out_shape=[
            jax.ShapeDtypeStruct((LAT_B, DEPTH, PAST, 512), BF16),
            jax.ShapeDtypeStruct((LAT_B, DEPTH, PAST, BW), BF16),
        ],
        compiler_params=_params(1),
        name="mla_ctx_keys",
    )(ckv, kr_pad, w["wuk"], w["wuv"], w["gkn"])


def _norm_mod(x, g, sc, sh):
    return _rms(x, D) * (g * (1.0 + sc)) + sh


def _rope(x, tab_ref, rot):
    hi = x.astype(BF16)
    pieces = jnp.concatenate([hi, (x - hi.astype(F32)).astype(BF16)], axis=1)
    return x * tab_ref[0] + _dot(pieces, rot) * tab_ref[1]


def _group_inv_rms(x, ones, n):
    sq = (x * x).astype(BF16)
    ms = jnp.concatenate([_dot(sq[:, c:c + BW], ones) for c in range(0, x.shape[1], BW)], axis=1)
    return lax.rsqrt(ms * (1.0 / n) + EPS)


def _inproj_kernel(n_x, *refs):
    x_refs = refs[:n_x]
    (mod_ref, g1_ref, wxa_ref, wq_ref, wk_ref, wv_ref, wxc_ref, wmla_ref,
     icnt_ref, poolw_ref, pools_ref, seg_ref, ones_ref, rotd_ref, rotm_ref, dgq_ref, dgk_ref,
     ropd_ref, ropm_ref, cc_ref, ss_ref, ft_ref, gqa_ref, wuq_ref, gqn_ref, gkv_ref,
     wuk_ref, wuv_ref, gkn_ref,
     pa_ref, qd_ref, kd_ref, vd_ref, oc_ref, qm_ref, km_ref, vm_ref,
     kcf_ref, vf_ref, ckv_ref, kr_ref, pad_ref) = refs[n_x:]
    is_lat = pl.program_id(0) >= CTX_TILES
    mod = mod_ref[0]
    hb = _norm_mod(_read_x(x_refs, is_lat), g1_ref[...], mod[:, D:2 * D], mod[:, 0:D]).astype(BF16)

    xa = _dot_nt(hb, wxa_ref[...])
    q = _dot_nt(hb, wq_ref[...])
    k = _dot_nt(hb, wk_ref[...])
    v = _dot_nt(hb, wv_ref[...])
    xcb = _dot_nt(hb, wxc_ref[...]).astype(BF16)
    cq = _dot_nt(hb, wmla_ref[C_CQ:C_CQ + BW, :])
    ckv_kr = _dot_nt(hb, wmla_ref[C_CKV:C_END, :])

    pad_ref[0:POOL_PAD, :] = jnp.zeros((POOL_PAD, BW), F32)
    pad_ref[POOL_PAD + TILE:, :] = jnp.zeros((POOL_PAD, BW), F32)
    pad_ref[POOL_PAD:POOL_PAD + TILE, :] = xa
    lane = lax.broadcasted_iota(jnp.int32, (TILE, BW), 1)
    row8 = lax.broadcasted_iota(jnp.int32, (SUBLANES, BW), 0)

    def term(s):
        v = pad_ref[POOL_PAD + s:POOL_PAD + s + TILE, :]
        kill = row8 < jnp.where(is_lat, 0, -s) if s < 0 else row8 >= jnp.where(is_lat, SUBLANES, SUBLANES - s)
        parts = []
        for r0 in range(0, TILE, CTX_T):
            r1 = r0 + CTX_T
            if s < 0:
                head = v[r0:r0 + SUBLANES]
                parts += [head if r0 == 0 else jnp.where(kill, 0.0, head), v[r0 + SUBLANES:r1]]
            else:
                tail = v[r1 - SUBLANES:r1]
                parts += [v[r0:r1 - SUBLANES], tail if r1 == TILE else jnp.where(kill, 0.0, tail)]
        return jnp.concatenate(parts, axis=0)

    acc = xa
    wins = []
    lo_s, hi_s = 0, 0
    for half in POOL_HALF:
        for s in list(range(-half, lo_s)) + list(range(hi_s + 1, half)):
            acc = acc + term(s)
        lo_s, hi_s = -half, half - 1
        wins.append(acc)
    win = jnp.where(lane < 64, wins[0], jnp.where(lane < 128, wins[1], jnp.where(lane < 192, wins[2], wins[3])))
    pooled = (win * icnt_ref[...] - xa).astype(BF16)
    pa_ref[...] = (_dot(pooled, poolw_ref[...]) * pools_ref[...]).astype(BF16)

    qscale = DA_D ** -0.5 * LOG2E
    rq = _group_inv_rms(q, seg_ref[...], DA_D) * qscale
    rk = _group_inv_rms(k, seg_ref[...], DA_D)
    kns = []
    for s in range(BW // LANES):
        sl = slice(s * LANES, (s + 1) * LANES)
        kg = k[:, sl] * dgk_ref[...]
        kns.append(kg * rk[:, sl])
        qd_ref[:, sl] = (_rope(q[:, sl] * dgq_ref[...], ropd_ref, rotd_ref[...]) * rq[:, sl]).astype(BF16)
        kd_ref[:, sl] = (_rope(kg, ropd_ref, rotd_ref[...]) * rk[:, sl]).astype(BF16)
    vd_ref[...] = v.astype(BF16)

    xcc = _dot(xcb, cc_ref[...]).astype(BF16)
    xcs = _dot(xcb, ss_ref[...]).astype(BF16)
    y = _dot(ft_ref[0], xcc) + _dot(ft_ref[1], xcs)
    fnorm = jnp.where(is_lat, (LAT_T * 64.0) ** -0.5, (CTX_T * 64.0) ** -0.5)
    oc_ref[...] = (y * fnorm).astype(BF16)

    qa = (_rms(cq, Q_LORA) * gqa_ref[...]).astype(BF16)
    qf = _dot(qa, wuq_ref[...])
    mscale = MLA_QK ** -0.5 * LOG2E
    rqm = _group_inv_rms(qf, ones_ref[...], MLA_QK) * mscale
    for h in range(HEADS):
        sl = slice(h * LANES, (h + 1) * LANES)
        qm_ref[:, sl] = (_rope(qf[:, sl] * gqn_ref[...], ropm_ref, rotm_ref[...]) * rqm[:, sl]).astype(BF16)
    ckv, kr_pad = ckv_kr[:, :KV_LORA], ckv_kr[:, KV_LORA:]
    ckv_n = _rms(ckv, KV_LORA) * gkv_ref[...]
    cb = ckv_n.astype(BF16)
    knope = _dot(cb, wuk_ref[...])
    kr_rot = _rope(kr_pad * gkn_ref[...], ropm_ref, rotm_ref[...])
    kr_ssq = _dot((kr_pad * kr_pad).astype(BF16), ones_ref[0:LANES, 0:LANES])
    kn_ssq = jnp.concatenate([_dot((knope[:, c:c + BW] * knope[:, c:c + BW]).astype(BF16), ones_ref[...])
                              for c in range(0, HEADS * LANES, BW)], axis=1)
    for h in range(HEADS):
        sl = slice(h * LANES, (h + 1) * LANES)
        rkm = lax.rsqrt((kn_ssq[:, sl] + kr_ssq) * (1.0 / MLA_QK) + EPS)
        km_ref[:, sl] = ((knope[:, sl] * gkn_ref[...] + kr_rot) * rkm).astype(BF16)
    vm_ref[...] = _dot(cb, wuv_ref[...]).astype(BF16)

    @pl.when(jnp.logical_not(is_lat))
    def _():
        for s, kn in enumerate(kns):
            kcf_ref[:, s * LANES:(s + 1) * LANES] = kn
        vf_ref[...] = v
        ckv_ref[...] = ckv_n
        kr_ref[...] = kr_pad[:, MLA_NOPE:MLA_NOPE + MLA_ROPE]


def _inproj_call(l, xs, mods, w, consts):
    row_outs = [(BW, BF16), (BW, BF16), (BW, BF16), (BW, BF16), (BW, BF16), (512, BF16), (512, BF16), (BW, BF16)]
    ctx_outs = [(BW, F32), (BW, F32), (KV_LORA, F32), (MLA_ROPE, F32)]
    return pl.pallas_call(
        functools.partial(_inproj_kernel, len(xs)),
        grid=(TILES,),
        in_specs=_x_specs(len(xs) == 2) + [
            _mod_spec(l), _layer((1, D), l), *[_in_cols(l, W_BLOCKS + n) for n in range(5)],
            _layer((C_END, D), l), _group_spec((TILE, BW)), _layer((BW, BW), l), _layer((1, BW), l),
            _const((BW, BW)), _const((BW, BW)), _const((BW, LANES)), _const((BW, LANES)),
            _layer((1, LANES), l), _layer((1, LANES), l),
            _group_spec((2, TILE, LANES)), _group_spec((2, TILE, LANES)),
            _const((BW, BW)), _const((BW, BW)), _group_spec((2, TILE, TILE)),
            _layer((1, BW), l), _layer((BW, 512), l), _layer((1, LANES), l), _layer((1, KV_LORA), l),
            _layer((KV_LORA, 512), l), _layer((KV_LORA, BW), l), _layer((1, LANES), l),
        ],
        out_specs=[_row_spec(wd) for wd, _ in row_outs] + [_ctx_row_spec(wd) for wd, _ in ctx_outs],
        out_shape=([jax.ShapeDtypeStruct((ROWS, wd), dt) for wd, dt in row_outs]
                   + [jax.ShapeDtypeStruct((CTX_ROWS, wd), dt) for wd, dt in ctx_outs]),
        scratch_shapes=[pltpu.VMEM((TILE + 2 * POOL_PAD, BW), F32)],
        compiler_params=_params(1),
        name="inproj",
    )(*xs, mods, w["g1"], *[w["win"]] * 5, w["wmla"], consts["icnt"], w["poolw"], w["pools"], consts["seg"],
      consts["ones"], consts["rotd"], consts["rotm"], w["dgq"], w["dgk"],
      consts["ropd"], consts["ropm"], consts["cc"], consts["ss"], consts["ft"],
      w["gqa"], w["wuq"], w["gqn"], w["gkv"], w["wuk"], w["wuv"], w["gkn"])


def _softmax_pv(ss, vs):
    m = ss[0].max(axis=-1, keepdims=True)
    for s in ss[1:]:
        m = jnp.maximum(m, s.max(axis=-1, keepdims=True))
    ps = [jnp.exp2(s - m) for s in ss]
    l = ps[0].sum(axis=-1, keepdims=True)
    for p in ps[1:]:
        l = l + p.sum(axis=-1, keepdims=True)
    return sum(_dot(p.astype(BF16), v) for p, v in zip(ps, vs)) * (1.0 / l)


def _attend(qd, kds, vds, qm, kms, vms, lam, gsub, lam_init, ob_ref, od_ref, out_rows):
    n = qd.shape[0]
    lane = lax.broadcasted_iota(jnp.int32, (n, BW), 1)
    zero = jnp.zeros((), BF16)

    jobs = []
    for h in range(HEADS):
        for half in range(2):
            lo = h * DA_HD + half * DA_D
            jobs.append((jnp.where((lane >= lo) & (lane < lo + DA_D), qd, zero), kds, vds))
        sl = slice(h * LANES, (h + 1) * LANES)
        jobs.append((qm[:, sl], [k[:, sl] for k in kms], vms))

    score = lambda job: [_dot_nt(job[0], k) for k in job[1]]
    out_b = jnp.zeros((n, BW), F32)
    out_d = jnp.zeros((n, BW), F32)
    pvs, nxt = [], score(jobs[0])
    for c, job in enumerate(jobs):
        ss, nxt = nxt, score(jobs[c + 1]) if c + 1 < len(jobs) else None
        pvs.append(_softmax_pv(ss, job[2]))
        if len(pvs) == 3:
            h = c // 3
            head = (lane >= h * DA_HD) & (lane < (h + 1) * DA_HD)
            o = jnp.where(head, pvs[0] - lam * pvs[1], 0.0)
            ms = jnp.sum(o * o, axis=-1, keepdims=True) * (1.0 / DA_HD)
            out_b = out_b + o * lax.rsqrt(ms + EPS)
            out_d = out_d + jnp.where(head, pvs[2], 0.0)
            pvs = []
    ob_ref[out_rows, :] = (out_b * gsub * (1.0 - lam_init)).astype(BF16)
    od_ref[out_rows, :] = out_d.astype(BF16)


def _attn_kernel(lam_init, qd_ref, kd_ref, vd_ref, cdk_ref, cdv_ref, qm_ref, km_ref, vm_ref, kc_ref, vc_ref,
                 dl_ref, gsub_ref, ob_ref, od_ref):
    i = pl.program_id(0)
    j = pl.program_id(1)
    dl = dl_ref[...]
    lam = (jnp.exp(jnp.sum(dl[0:1] * dl[1:2], axis=-1, keepdims=True))
           - jnp.exp(jnp.sum(dl[2:3] * dl[3:4], axis=-1, keepdims=True)) + lam_init)
    @pl.when(i < CTX_TILES)
    def _():
        for sub in range(TQ // CTX_T):
            rows = pl.ds(pl.multiple_of(j * TQ + sub * CTX_T, CTX_T), CTX_T)
            _attend(qd_ref[rows, :], [kd_ref[rows, :]], [vd_ref[rows, :]],
                    qm_ref[rows, :], [km_ref[rows, :]], [vm_ref[rows, :]],
                    lam, gsub_ref[...], lam_init, ob_ref, od_ref, slice(sub * CTX_T, (sub + 1) * CTX_T))

    @pl.when(i >= CTX_TILES)
    def _():
        rows = pl.ds(pl.multiple_of(j * TQ, TQ), TQ)
        _attend(qd_ref[rows, :], [cdk_ref[...], kd_ref[...]], [cdv_ref[...], vd_ref[...]],
                qm_ref[rows, :], [kc_ref[...], km_ref[...]], [vc_ref[...], vm_ref[...]],
                lam, gsub_ref[...], lam_init, ob_ref, od_ref, slice(None))


def _attn_call(l, qd, kd, vd, cdk, cdv, qm, km, vm, kc, vc, w):
    tile = lambda width: pl.BlockSpec((TILE, width), lambda i, j: (i, 0))
    cache = lambda width: pl.BlockSpec((None, None, PAST, width),
                                       lambda i, j: (jnp.maximum(i - CTX_TILES, 0), l, 0, 0))
    blk = pl.BlockSpec((TQ, BW), lambda i, j: (i * (TILE // TQ) + j, 0))
    return pl.pallas_call(
        functools.partial(_attn_kernel, _lambda_init(l)),
        grid=(TILES, TILE // TQ),
        in_specs=[tile(BW), tile(BW), tile(BW), cache(BW), cache(BW), tile(512), tile(512), tile(BW),
                  cache(512), cache(BW), _layer((4, DA_D), l), _layer((1, BW), l)],
        out_specs=[blk, blk],
        out_shape=[jax.ShapeDtypeStruct((ROWS, BW), BF16)] * 2,
        compiler_params=_params(2),
        name="attention",
    )(qd, kd, vd, cdk, cdv, qm, km, vm, kc, vc, w["dl"], w["gsub"])


def _merge_kernel(n_x, *refs):
    x_refs = refs[:n_x]
    mod_ref, g1_ref, wg_ref, wb_ref, wo_ref, pa_ref, pb_ref, pc_ref, pd_ref, o_ref = refs[n_x:]
    mod = mod_ref[0]
    x = _read_x(x_refs, pl.program_id(0) >= CTX_TILES)
    hb = _norm_mod(x, g1_ref[...], mod[:, D:2 * D], mod[:, 0:D]).astype(BF16)
    merged = None
    for n, br_ref in enumerate((pa_ref, pb_ref, pc_ref, pd_ref)):
        gate = _sigmoid(_dot_nt(hb, wg_ref[n * D:(n + 1) * D, :]))
        t = gate * _dot(br_ref[...], wb_ref[n])
        merged = t if merged is None else merged + t
    o_ref[...] = x + mod[:, 2 * D:3 * D] * _dot(merged.astype(BF16), wo_ref[...])


def _merge_call(l, xs, mods, w, pa, pb, pc, pd):
    return pl.pallas_call(
        functools.partial(_merge_kernel, len(xs)),
        grid=(TILES,),
        in_specs=_x_specs(len(xs) == 2) + [
            _mod_spec(l), _layer((1, D), l), _layer((4 * D, D), l), _layer((4, BW, D), l), _layer((D, D), l),
            _row_spec(BW), _row_spec(BW), _row_spec(BW), _row_spec(BW)],
        out_specs=_row_spec(D),
        out_shape=jax.ShapeDtypeStruct((ROWS, D), F32),
        compiler_params=_params(1),
        name="merge",
    )(*xs, mods, w["g1"], w["win"], w["wbr"], w["wout"], pa, pb, pc, pd)


def _ffn_kernel(n_o, x_ref, mod_ref, g2_ref, wu_ref, cv_ref, wd_ref, *refs):
    o_refs = refs[:n_o]
    hb_ref, acc_ref = refs[n_o:]
    is_lat = pl.program_id(0) >= CTX_TILES
    is_ctx = jnp.logical_not(is_lat)
    mod = mod_ref[0]
    hb_ref[...] = _norm_mod(x_ref[...], g2_ref[...], mod[:, 4 * D:5 * D], mod[:, 3 * D:4 * D]).astype(BF16)
    acc_ref[...] = jnp.zeros((TILE, D), F32)

    def edge_masks(width):
        row8 = lax.broadcasted_iota(jnp.int32, (SUBLANES, width), 0)
        return (row8 == 0, row8 == SUBLANES - 1,
                row8 == jnp.where(is_ctx, 0, -1), row8 == jnp.where(is_ctx, SUBLANES - 1, -1))

    def conv(cols, masks):
        tile_start, tile_end, seq_start, seq_end = masks
        u = _dot(hb_ref[...], wu_ref[:, cols])
        cv = cv_ref[:, cols]
        prev, nxt = pltpu.roll(u, 1, 0), pltpu.roll(u, TILE - 1, 0)
        prev_parts, nxt_parts = [], []
        for r0 in range(0, TILE, CTX_T):
            r1 = r0 + CTX_T
            start = tile_start if r0 == 0 else seq_start
            end = tile_end if r1 == TILE else seq_end
            prev_parts += [jnp.where(start, 0.0, prev[r0:r0 + SUBLANES]), prev[r0 + SUBLANES:r1]]
            nxt_parts += [nxt[r0:r1 - SUBLANES], jnp.where(end, 0.0, nxt[r1 - SUBLANES:r1])]
        prev, nxt = jnp.concatenate(prev_parts, axis=0), jnp.concatenate(nxt_parts, axis=0)
        return prev * cv[0:1] + u * cv[1:2] + nxt * cv[2:3] + cv[3:4]

    def chunk(c0, width, masks):
        h = conv(pl.ds(c0, width), masks)
        v = conv(pl.ds(FFN + c0, width), masks)
        act = (h * (1.0 + jnp.tanh(h)) * v).astype(BF16)
        acc_ref[...] += _dot(act, wd_ref[pl.ds(c0, width), :])

    masks = edge_masks(FFN_CHUNK)

    def body(c, carry):
        chunk(pl.multiple_of(c * FFN_CHUNK, FFN_CHUNK), FFN_CHUNK, masks)
        return carry

    lax.fori_loop(0, FFN // FFN_CHUNK, body, 0)
    if FFN % FFN_CHUNK:
        chunk(FFN - FFN % FFN_CHUNK, FFN % FFN_CHUNK, edge_masks(FFN % FFN_CHUNK))
    y = x_ref[...] + mod[:, 5 * D:6 * D] * acc_ref[...]
    if n_o == 1:
        o_refs[0][...] = y
    else:
        @pl.when(is_ctx)
        def _():
            o_refs[0][...] = y

        @pl.when(is_lat)
        def _():
            o_refs[1][...] = y


def _ffn_call(l, x, mods, w, split_out):
    if split_out:
        out_specs = [pl.BlockSpec(s.block_shape, s.index_map, pipeline_mode=pl.Buffered(1)) for s in _x_specs(True)]
        out_shape = [jax.ShapeDtypeStruct((CTX_ROWS, D), F32), jax.ShapeDtypeStruct((LAT_ROWS, D), F32)]
    else:
        out_specs = _x_specs(False)
        out_shape = [jax.ShapeDtypeStruct((ROWS, D), F32)]
    return pl.pallas_call(
        functools.partial(_ffn_kernel, len(out_shape)),
        grid=(TILES,),
        in_specs=[_row_spec(D), _mod_spec(l), _layer((1, D), l),
                  _layer((D, 2 * FFN), l), _layer((8, 2 * FFN), l), _layer((FFN, D), l)],
        out_specs=out_specs,
        out_shape=out_shape,
        scratch_shapes=[pltpu.VMEM((TILE, D), BF16), pltpu.VMEM((TILE, D), F32)],
        compiler_params=_params(1),
        name="ffn",
    )(x, mods, w["g2"], w["wup"], w["cv"], w["wdown"])


def _rope_tables():
    half = 8
    inv = ROPE_BASE ** (-np.arange(half, dtype=np.float64) / half)
    t = np.arange(LAT_T)
    rows, cols = (t // GRID_W).astype(np.float64), (t % GRID_W).astype(np.float64)
    ang = np.concatenate([rows[:, None] * inv[None, :]] * 2 + [cols[:, None] * inv[None, :]] * 2, axis=1)
    cos, sin = np.cos(ang), np.sin(ang)

    def diff_layout(a):
        return np.tile(a, (1, LANES // 32))

    def mla_layout(a, fill):
        out = np.full((LAT_T, LANES), fill)
        out[:, MLA_NOPE:MLA_QK] = a
        return out

    ident = lambda layout: np.stack([layout(np.ones_like(cos)), layout(np.zeros_like(cos))])
    ropd = np.stack([ident(diff_layout), np.stack([diff_layout(cos), diff_layout(sin)])])
    ropm = np.stack([np.stack([mla_layout(np.ones_like(cos), 1.0), mla_layout(np.zeros_like(cos), 0.0)]),
                     np.stack([mla_layout(cos, 1.0), mla_layout(sin, 0.0)])])
    return jnp.asarray(ropd, F32), jnp.asarray(ropm, F32)


def _pool_inv_count():
    out = []
    for seq in (CTX_T, LAT_T):
        t = np.arange(TILE) % seq
        cols = [np.repeat((1.0 / (np.minimum(t + half, seq) - np.maximum(t - half, 0)))[:, None], BW // 4, axis=1)
                for half in POOL_HALF]
        out.append(np.concatenate(cols, axis=1))
    return jnp.asarray(np.stack(out), F32)


def _rotate_half_matrix(lanes):
    r = np.zeros((LANES, LANES))
    for j in lanes:
        if j % 16 < 8:
            r[j + 8, j] = -1.0
        else:
            r[j - 8, j] = 1.0
    return np.concatenate([r, r], axis=0)


def _matrix_tables():
    def cs(n):
        k = np.arange(n)
        ang = 2.0 * np.pi * ((k[:, None] * k[None, :]) % n) / n
        return np.cos(ang), np.sin(ang)

    c64, s64 = cs(64)
    eye4 = np.eye(4)
    cl, sl = cs(LAT_T)
    cc_, sc_ = cs(CTX_T)
    ft = np.stack([np.stack([np.kron(eye4, cc_), -np.kron(eye4, sc_)]), np.stack([cl, -sl])])
    as_bf16 = lambda a: jnp.asarray(a, F32).astype(BF16)
    return dict(
        cc=as_bf16(np.kron(eye4, c64)), ss=as_bf16(np.kron(eye4, s64)), ft=as_bf16(ft),
        seg=as_bf16(np.kron(np.eye(BW // DA_D), np.ones((DA_D, DA_D)))),
        ones=as_bf16(np.kron(np.eye(BW // LANES), np.ones((LANES, LANES)))),
        rotd=as_bf16(_rotate_half_matrix(range(LANES))),
        rotm=as_bf16(_rotate_half_matrix(range(MLA_NOPE, MLA_QK))))


def _pad_last(a, lo, hi):
    return jnp.pad(a, ((0, 0),) * (a.ndim - 1) + ((lo, hi),))


def _prep_weights(p):
    win = jnp.swapaxes(p["w_in"], 1, 2).astype(BF16)
    mla = win[:, W_MLA:, :]
    cq, ckv, kr = mla[:, :Q_LORA], mla[:, Q_LORA:Q_LORA + KV_LORA], mla[:, Q_LORA + KV_LORA:]
    pad_rows = lambda a, lo, hi: jnp.pad(a, ((0, 0), (lo, hi), (0, 0)))
    wmla = jnp.concatenate([pad_rows(cq, 0, BW - Q_LORA), ckv, pad_rows(kr, MLA_NOPE, LANES - MLA_QK)], axis=1)
    poolw = jnp.einsum("lgce,gh->lgche", p["pool_w"], jnp.eye(4, dtype=F32)).reshape(DEPTH, BW, BW).astype(BF16)
    wuq = p["w_uq"].reshape(DEPTH, Q_LORA, HEADS, MLA_QK)
    wuq = jnp.pad(wuq, ((0, 0), (0, BW - Q_LORA), (0, 0), (0, LANES - MLA_QK))).reshape(DEPTH, BW, HEADS * LANES)
    wukv = p["w_ukv"].reshape(DEPTH, KV_LORA, HEADS, MLA_NOPE + MLA_V)
    wuk = _pad_last(wukv[..., :MLA_NOPE], 0, LANES - MLA_NOPE).reshape(DEPTH, KV_LORA, HEADS * LANES)
    wuv = wukv[..., MLA_NOPE:].reshape(DEPTH, KV_LORA, HEADS * MLA_V)
    cv = jnp.concatenate([p["conv_w"], p["conv_b"][:, None, :], jnp.zeros((DEPTH, 4, 2 * FFN), F32)], axis=1)
    cv = cv * jnp.where(jnp.arange(2 * FFN) < FFN, 0.5, 1.0)

    row = lambda a, width: _pad_last(a, 0, width - a.shape[-1])[:, None, :]
    tiled = lambda a, reps: jnp.tile(a, (1, reps))[:, None, :]
    return {
        "g1": row(p["g_norm1"], D), "g2": row(p["g_norm2"], D),
        "win": win, "wmla": wmla,
        "poolw": poolw, "pools": row(p["pool_scale"], BW),
        "dgq": tiled(p["diff_qnorm"], LANES // DA_D), "dgk": tiled(p["diff_knorm"], LANES // DA_D),
        "dl": p["diff_lambda"], "gsub": tiled(p["diff_subln"], HEADS),
        "gqa": row(p["mla_qa_norm"], BW), "wuq": wuq.astype(BF16), "gqn": row(p["mla_qnorm"], LANES),
        "gkv": row(p["mla_kv_norm"], KV_LORA), "wuk": wuk.astype(BF16), "wuv": wuv.astype(BF16),
        "gkn": row(p["mla_knorm"], LANES),
        "wbr": p["w_branch"].astype(BF16), "wout": p["w_out"].astype(BF16),
        "wup": p["w_up"].astype(BF16), "cv": cv, "wdown": p["w_down"].astype(BF16),
    }


def kernel(x_prompt, x_sample, c, c_ctx, cache_diff_k, cache_diff_v, cache_mla_ckv, cache_mla_krope, w_ada, b_ada, g_norm1, g_norm2, w_in, pool_w, pool_scale, diff_qnorm, diff_knorm, diff_lambda, diff_subln, mla_qa_norm, w_uq, mla_kv_norm, w_ukv, mla_qnorm, mla_knorm, w_branch, w_out, w_up, conv_w, conv_b, w_down):
    w = _prep_weights(dict(
        w_in=w_in, pool_w=pool_w, pool_scale=pool_scale, diff_qnorm=diff_qnorm, diff_knorm=diff_knorm,
        diff_lambda=diff_lambda, diff_subln=diff_subln, mla_qa_norm=mla_qa_norm, w_uq=w_uq,
        mla_kv_norm=mla_kv_norm, w_ukv=w_ukv, mla_qnorm=mla_qnorm, mla_knorm=mla_knorm, w_branch=w_branch,
        w_out=w_out, w_up=w_up, conv_w=conv_w, conv_b=conv_b, w_down=w_down, g_norm1=g_norm1,
        g_norm2=g_norm2))
    ropd, ropm = _rope_tables()
    consts = dict(_matrix_tables(), ropd=ropd, ropm=ropm, icnt=_pool_inv_count())

    cond = jnp.concatenate([c_ctx[None, :], c, jnp.zeros((16 - 1 - LAT_B, D), F32)], axis=0)
    mods = _mod_call(cond, w_ada, b_ada).reshape(DEPTH, 16, 1, 6 * D)

    kc, vc = _ctxkv_call(cache_mla_ckv, _pad_last(cache_mla_krope, MLA_NOPE, LANES - MLA_QK), w)
    cdk = cache_diff_k.reshape(LAT_B, DEPTH, PAST, BW).astype(BF16)
    cdv = cache_diff_v.reshape(LAT_B, DEPTH, PAST, BW).astype(BF16)

    xs = [x_prompt.reshape(CTX_ROWS, D), x_sample.reshape(LAT_ROWS, D)]
    new = [[], [], [], []]
    for l in range(DEPTH):
        pa, qd, kd, vd, oc, qm, km, vm, *ctx_outs = _inproj_call(l, xs, mods, w, consts)
        ob, od = _attn_call(l, qd, kd, vd, cdk, cdv, qm, km, vm, kc, vc, w)
        x = _merge_call(l, xs, mods, w, pa, ob, oc, od)
        xs = _ffn_call(l, x, mods, w, split_out=(l == DEPTH - 1))
        for acc, a in zip(new, ctx_outs):
            acc.append(a)

    stack = lambda arrs, shape: jnp.stack(arrs, axis=1).reshape((CTX_B, DEPTH, CTX_T) + shape)
    return (xs[0].reshape(CTX_B, CTX_T, D), xs[1].reshape(LAT_B, LAT_T, D),
            stack([a.reshape(CTX_B, CTX_T, BW) for a in new[0]], (HEADS, DA_HD)),
            stack([a.reshape(CTX_B, CTX_T, BW) for a in new[1]], (HEADS, DA_HD)),
            stack([a.reshape(CTX_B, CTX_T, KV_LORA) for a in new[2]], (KV_LORA,)),
            stack([a.reshape(CTX_B, CTX_T, MLA_ROPE) for a in new[3]], (MLA_ROPE,)))
```

```python
import functools
import math

import jax
import jax.numpy as jnp
import numpy as np
from jax import lax
from jax.experimental import pallas as pl
from jax.experimental.pallas import tpu as pltpu

F32 = jnp.float32
BF16 = jnp.bfloat16

D = 1024
DEPTH = 4
CTX_B, CTX_T = 16, 256
LAT_B, LAT_T = 8, 1024
PAST = 512
TILE = 1024
CTX_TILES = CTX_B * CTX_T // TILE
LAT_TILES = LAT_B * LAT_T // TILE
TILES = CTX_TILES + LAT_TILES
ROWS = TILES * TILE
CTX_ROWS = CTX_TILES * TILE
LAT_ROWS = LAT_TILES * TILE
GRID_W = 64
BW = 256
DA_D = 32
DA_HD = 64
HEADS = 4
MLA_NOPE, MLA_ROPE, MLA_V, MLA_QK = 64, 32, 64, 96
Q_LORA, KV_LORA = 192, 128
FFN = 2816
FFN_CHUNK = 768
SUBLANES = 8
POOL_HALF = (1, 2, 4, 8)
POOL_PAD = 16
ROPE_BASE = 10000.0
EPS = 1e-6
LOG2E = 1.4426950408889634
TQ = 512
LANES = 128
VMEM_LIMIT = 60 * 1024 * 1024

W_BLOCKS = 4 * D // BW
W_MLA = 4 * D + 5 * BW
C_CQ, C_CKV, C_END = 0, 256, 512


def _lambda_init(l):
    return 0.8 - 0.6 * math.exp(-0.3 * l)


def _sigmoid(x):
    return 0.5 * jnp.tanh(0.5 * x) + 0.5


def _rms(x, n):
    ms = jnp.sum(x * x, axis=-1, keepdims=True) * (1.0 / n)
    return x * lax.rsqrt(ms + EPS)


def _dot(a, b):
    return jnp.dot(a, b, preferred_element_type=F32)


def _dot_nt(a, b):
    return lax.dot_general(a, b, (((1,), (1,)), ((), ())), preferred_element_type=F32)


def _const(shape):
    n = len(shape)
    return pl.BlockSpec(shape, lambda *_: (0,) * n, pipeline_mode=pl.Buffered(1))


def _layer(shape, l):
    n = len(shape)
    return pl.BlockSpec((None,) + shape, lambda *_: (l,) + (0,) * n, pipeline_mode=pl.Buffered(1))


def _params(n_grid):
    return pltpu.CompilerParams(dimension_semantics=("arbitrary",) * n_grid, vmem_limit_bytes=VMEM_LIMIT)


def _row_spec(width):
    return pl.BlockSpec((TILE, width), lambda i: (i, 0))


def _ctx_row_spec(width):
    return pl.BlockSpec((TILE, width), lambda i: (jnp.minimum(i, CTX_TILES - 1), 0))


def _lat_row_spec(width):
    return pl.BlockSpec((TILE, width), lambda i: (jnp.maximum(i - CTX_TILES, 0), 0))


def _x_specs(split):
    return [_ctx_row_spec(D), _lat_row_spec(D)] if split else [_row_spec(D)]


def _read_x(x_refs, is_lat):
    if len(x_refs) == 2:
        return jnp.where(is_lat, x_refs[1][...], x_refs[0][...])
    return x_refs[0][...]


def _in_cols(l, n):
    return pl.BlockSpec((None, BW, D), lambda *_: (l, n, 0), pipeline_mode=pl.Buffered(1))


def _mod_spec(l):
    return pl.BlockSpec((None, 1, 1, 6 * D), lambda i: (l, jnp.maximum(i - (CTX_TILES - 1), 0), 0, 0))


def _group_spec(shape):
    n = len(shape)
    return pl.BlockSpec((None,) + shape, lambda i: (jnp.minimum(i // CTX_TILES, 1),) + (0,) * n,
                        pipeline_mode=pl.Buffered(1))


def _mod_kernel(c_ref, w_ref, b_ref, o_ref):
    c = c_ref[...]
    s = (c * _sigmoid(c)).astype(BF16)
    o_ref[0] = _dot(s, w_ref[0].astype(BF16)) + b_ref[0]


def _mod_call(cond, w_ada, b_ada):
    nc = 4
    cw = 6 * D // nc
    return pl.pallas_call(
        _mod_kernel,
        grid=(DEPTH, nc),
        in_specs=[
            pl.BlockSpec((16, D), lambda l, j: (0, 0)),
            pl.BlockSpec((1, D, cw), lambda l, j: (l, 0, j)),
            pl.BlockSpec((1, 1, cw), lambda l, j: (l, 0, j)),
        ],
        out_specs=pl.BlockSpec((1, 16, cw), lambda l, j: (l, 0, j)),
        out_shape=jax.ShapeDtypeStruct((DEPTH, 16, 6 * D), F32),
        compiler_params=_params(2),
        name="adaln_mod",
    )(cond, w_ada, b_ada.reshape(DEPTH, 1, 6 * D))


def _mla_key_heads(knope, kr_pad, gkn):
    out = []
    for h in range(HEADS):
        kh = knope[:, h * LANES:(h + 1) * LANES] + kr_pad
        out.append(_rms(kh, MLA_QK) * gkn)
    return out


def _ctxkv_kernel(ckv_ref, kr_ref, wuk_ref, wuv_ref, gkn_ref, kc_ref, vc_ref):
    rows = LAT_B * PAST
    cb = ckv_ref[...].reshape(rows, KV_LORA).astype(BF16)
    knope = _dot(cb, wuk_ref[...])
    heads = _mla_key_heads(knope, kr_ref[...].reshape(rows, LANES), gkn_ref[...])
    for h in range(HEADS):
        kc_ref[:, :, h * LANES:(h + 1) * LANES] = heads[h].astype(BF16).reshape(LAT_B, PAST, LANES)
    vc_ref[...] = _dot(cb, wuv_ref[...]).astype(BF16).reshape(LAT_B, PAST, BW)


def _ctxkv_call(ckv, kr_pad, w):
    cache = lambda width: pl.BlockSpec((LAT_B, None, PAST, width), lambda l: (0, l, 0, 0))
    wspec = lambda r, c: pl.BlockSpec((None, r, c), lambda l: (l, 0, 0))
    return pl.pallas_call(
        _ctxkv_kernel,
        grid=(DEPTH,),
        in_specs=[cache(KV_LORA), cache(LANES), wspec(KV_LORA, 512), wspec(KV_LORA, BW), wspec(1, LANES)],
        out_specs=[cache(512), cache(BW)],
        out_shape=[
            jax.ShapeDtypeStruct((LAT_B, DEPTH, PAST, 512), BF16),
            jax.ShapeDtypeStruct((LAT_B, DEPTH, PAST, BW), BF16),
        ],
        compiler_params=_params(1),
        name="mla_ctx_keys",
    )(ckv, kr_pad, w["wuk"], w["wuv"], w["gkn"])


def _norm_mod(x, g, sc, sh):
    return _rms(x, D) * (g * (1.0 + sc)) + sh


def _rope(x, tab_ref, rot):
    hi = x.astype(BF16)
    pieces = jnp.concatenate([hi, (x - hi.astype(F32)).astype(BF16)], axis=1)
    return x * tab_ref[0] + _dot(pieces, rot) * tab_ref[1]


def _group_inv_rms(x, ones, n):
    sq = (x * x).astype(BF16)
    ms = jnp.concatenate([_dot(sq[:, c:c + BW], ones) for c in range(0, x.shape[1], BW)], axis=1)
    return lax.rsqrt(ms * (1.0 / n) + EPS)


def _inproj_kernel(n_x, *refs):
    x_refs = refs[:n_x]
    (mod_ref, g1_ref, wxa_ref, wq_ref, wk_ref, wv_ref, wxc_ref, wmla_ref,
     icnt_ref, poolw_ref, pools_ref, seg_ref, ones_ref, rotd_ref, rotm_ref, dgq_ref, dgk_ref,
     ropd_ref, ropm_ref, cc_ref, ss_ref, ft_ref, gqa_ref, wuq_ref, gqn_ref, gkv_ref,
     wuk_ref, wuv_ref, gkn_ref,
     hb_ref, pa_ref, qd_ref, kd_ref, vd_ref, oc_ref, qm_ref, km_ref, vm_ref,
     kcf_ref, vf_ref, ckv_ref, kr_ref, pad_ref) = refs[n_x:]
    is_lat = pl.program_id(0) >= CTX_TILES
    mod = mod_ref[0]
    hb = _norm_mod(_read_x(x_refs, is_lat), g1_ref[...], mod[:, D:2 * D], mod[:, 0:D]).astype(BF16)
    hb_ref[...] = hb

    xa = _dot_nt(hb, wxa_ref[...])
    q = _dot_nt(hb, wq_ref[...])
    k = _dot_nt(hb, wk_ref[...])
    v = _dot_nt(hb, wv_ref[...])
    xcb = _dot_nt(hb, wxc_ref[...]).astype(BF16)
    cq = _dot_nt(hb, wmla_ref[C_CQ:C_CQ + BW, :])
    ckv_kr = _dot_nt(hb, wmla_ref[C_CKV:C_END, :])

    pad_ref[0:POOL_PAD, :] = jnp.zeros((POOL_PAD, BW), F32)
    pad_ref[POOL_PAD + TILE:, :] = jnp.zeros((POOL_PAD, BW), F32)
    pad_ref[POOL_PAD:POOL_PAD + TILE, :] = xa
    lane = lax.broadcasted_iota(jnp.int32, (TILE, BW), 1)
    row8 = lax.broadcasted_iota(jnp.int32, (SUBLANES, BW), 0)

    def term(s):
        v = pad_ref[POOL_PAD + s:POOL_PAD + s + TILE, :]
        kill = row8 < jnp.where(is_lat, 0, -s) if s < 0 else row8 >= jnp.where(is_lat, SUBLANES, SUBLANES - s)
        parts = []
        for r0 in range(0, TILE, CTX_T):
            r1 = r0 + CTX_T
            if s < 0:
                head = v[r0:r0 + SUBLANES]
                parts += [head if r0 == 0 else jnp.where(kill, 0.0, head), v[r0 + SUBLANES:r1]]
            else:
                tail = v[r1 - SUBLANES:r1]
                parts += [v[r0:r1 - SUBLANES], tail if r1 == TILE else jnp.where(kill, 0.0, tail)]
        return jnp.concatenate(parts, axis=0)

    acc = xa
    wins = []
    lo_s, hi_s = 0, 0
    for half in POOL_HALF:
        for s in list(range(-half, lo_s)) + list(range(hi_s + 1, half)):
            acc = acc + term(s)
        lo_s, hi_s = -half, half - 1
        wins.append(acc)
    win = jnp.where(lane < 64, wins[0], jnp.where(lane < 128, wins[1], jnp.where(lane < 192, wins[2], wins[3])))
    pooled = (win * icnt_ref[...] - xa).astype(BF16)
    pa_ref[...] = (_dot(pooled, poolw_ref[...]) * pools_ref[...]).astype(BF16)

    qscale = DA_D ** -0.5 * LOG2E
    rq = _group_inv_rms(q, seg_ref[...], DA_D) * qscale
    rk = _group_inv_rms(k, seg_ref[...], DA_D)
    kns = []
    for s in range(BW // LANES):
        sl = slice(s * LANES, (s + 1) * LANES)
        kg = k[:, sl] * dgk_ref[...]
        kns.append(kg * rk[:, sl])
        qd_ref[:, sl] = (_rope(q[:, sl] * dgq_ref[...], ropd_ref, rotd_ref[...]) * rq[:, sl]).astype(BF16)
        kd_ref[:, sl] = (_rope(kg, ropd_ref, rotd_ref[...]) * rk[:, sl]).astype(BF16)
    vd_ref[...] = v.astype(BF16)

    xcc = _dot(xcb, cc_ref[...]).astype(BF16)
    xcs = _dot(xcb, ss_ref[...]).astype(BF16)
    y = _dot(ft_ref[0], xcc) + _dot(ft_ref[1], xcs)
    fnorm = jnp.where(is_lat, (LAT_T * 64.0) ** -0.5, (CTX_T * 64.0) ** -0.5)
    oc_ref[...] = (y * fnorm).astype(BF16)

    qa = (_rms(cq, Q_LORA) * gqa_ref[...]).astype(BF16)
    qf = _dot(qa, wuq_ref[...])
    mscale = MLA_QK ** -0.5 * LOG2E
    rqm = _group_inv_rms(qf, ones_ref[...], MLA_QK) * mscale
    for h in range(HEADS):
        sl = slice(h * LANES, (h + 1) * LANES)
        qm_ref[:, sl] = (_rope(qf[:, sl] * gqn_ref[...], ropm_ref, rotm_ref[...]) * rqm[:, sl]).astype(BF16)
    ckv, kr_pad = ckv_kr[:, :KV_LORA], ckv_kr[:, KV_LORA:]
    ckv_n = _rms(ckv, KV_LORA) * gkv_ref[...]
    cb = ckv_n.astype(BF16)
    knope = _dot(cb, wuk_ref[...])
    kr_rot = _rope(kr_pad * gkn_ref[...], ropm_ref, rotm_ref[...])
    kr_ssq = _dot((kr_pad * kr_pad).astype(BF16), ones_ref[0:LANES, 0:LANES])
    kn_ssq = jnp.concatenate([_dot((knope[:, c:c + BW] * knope[:, c:c + BW]).astype(BF16), ones_ref[...])
                              for c in range(0, HEADS * LANES, BW)], axis=1)
    for h in range(HEADS):
        sl = slice(h * LANES, (h + 1) * LANES)
        rkm = lax.rsqrt((kn_ssq[:, sl] + kr_ssq) * (1.0 / MLA_QK) + EPS)
        km_ref[:, sl] = ((knope[:, sl] * gkn_ref[...] + kr_rot) * rkm).astype(BF16)
    vm_ref[...] = _dot(cb, wuv_ref[...]).astype(BF16)

    @pl.when(jnp.logical_not(is_lat))
    def _():
        for s, kn in enumerate(kns):
            kcf_ref[:, s * LANES:(s + 1) * LANES] = kn
        vf_ref[...] = v
        ckv_ref[...] = ckv_n
        kr_ref[...] = kr_pad[:, MLA_NOPE:MLA_NOPE + MLA_ROPE]


def _inproj_call(l, xs, mods, w, consts):
    row_outs = [(D, BF16), (BW, BF16), (BW, BF16), (BW, BF16), (BW, BF16), (BW, BF16), (512, BF16), (512, BF16), (BW, BF16)]
    ctx_outs = [(BW, F32), (BW, F32), (KV_LORA, F32), (MLA_ROPE, F32)]
    return pl.pallas_call(
        functools.partial(_inproj_kernel, len(xs)),
        grid=(TILES,),
        in_specs=_x_specs(len(xs) == 2) + [
            _mod_spec(l), _layer((1, D), l), *[_in_cols(l, W_BLOCKS + n) for n in range(5)],
            _layer((C_END, D), l), _group_spec((TILE, BW)), _layer((BW, BW), l), _layer((1, BW), l),
            _const((BW, BW)), _const((BW, BW)), _const((BW, LANES)), _const((BW, LANES)),
            _layer((1, LANES), l), _layer((1, LANES), l),
            _group_spec((2, TILE, LANES)), _group_spec((2, TILE, LANES)),
            _const((BW, BW)), _const((BW, BW)), _group_spec((2, TILE, TILE)),
            _layer((1, BW), l), _layer((BW, 512), l), _layer((1, LANES), l), _layer((1, KV_LORA), l),
            _layer((KV_LORA, 512), l), _layer((KV_LORA, BW), l), _layer((1, LANES), l),
        ],
        out_specs=[_row_spec(wd) for wd, _ in row_outs] + [_ctx_row_spec(wd) for wd, _ in ctx_outs],
        out_shape=([jax.ShapeDtypeStruct((ROWS, wd), dt) for wd, dt in row_outs]
                   + [jax.ShapeDtypeStruct((CTX_ROWS, wd), dt) for wd, dt in ctx_outs]),
        scratch_shapes=[pltpu.VMEM((TILE + 2 * POOL_PAD, BW), F32)],
        compiler_params=_params(1),
        name="inproj",
    )(*xs, mods, w["g1"], *[w["win"]] * 5, w["wmla"], consts["icnt"], w["poolw"], w["pools"], consts["seg"],
      consts["ones"], consts["rotd"], consts["rotm"], w["dgq"], w["dgk"],
      consts["ropd"], consts["ropm"], consts["cc"], consts["ss"], consts["ft"],
      w["gqa"], w["wuq"], w["gqn"], w["gkv"], w["wuk"], w["wuv"], w["gkn"])


def _softmax_pv(ss, vs):
    m = ss[0].max(axis=-1, keepdims=True)
    for s in ss[1:]:
        m = jnp.maximum(m, s.max(axis=-1, keepdims=True))
    ps = [jnp.exp2(s - m) for s in ss]
    l = ps[0].sum(axis=-1, keepdims=True)
    for p in ps[1:]:
        l = l + p.sum(axis=-1, keepdims=True)
    return sum(_dot(p.astype(BF16), v) for p, v in zip(ps, vs)) * (1.0 / l)


def _attend(qd, kds, vds, qm, kms, vms, lam, gsub, lam_init, ob_ref, od_ref, out_rows):
    n = qd.shape[0]
    lane = lax.broadcasted_iota(jnp.int32, (n, BW), 1)
    zero = jnp.zeros((), BF16)

    jobs = []
    for h in range(HEADS):
        for half in range(2):
            lo = h * DA_HD + half * DA_D
            jobs.append((jnp.where((lane >= lo) & (lane < lo + DA_D), qd, zero), kds, vds))
        sl = slice(h * LANES, (h + 1) * LANES)
        jobs.append((qm[:, sl], [k[:, sl] for k in kms], vms))

    score = lambda job: [_dot_nt(job[0], k) for k in job[1]]
    out_b = jnp.zeros((n, BW), F32)
    out_d = jnp.zeros((n, BW), F32)
    pvs, nxt = [], score(jobs[0])
    for c, job in enumerate(jobs):
        ss, nxt = nxt, score(jobs[c + 1]) if c + 1 < len(jobs) else None
        pvs.append(_softmax_pv(ss, job[2]))
        if len(pvs) == 3:
            h = c // 3
            head = (lane >= h * DA_HD) & (lane < (h + 1) * DA_HD)
            o = jnp.where(head, pvs[0] - lam * pvs[1], 0.0)
            ms = jnp.sum(o * o, axis=-1, keepdims=True) * (1.0 / DA_HD)
            out_b = out_b + o * lax.rsqrt(ms + EPS)
            out_d = out_d + jnp.where(head, pvs[2], 0.0)
            pvs = []
    ob_ref[out_rows, :] = (out_b * gsub * (1.0 - lam_init)).astype(BF16)
    od_ref[out_rows, :] = out_d.astype(BF16)


def _attn_kernel(lam_init, qd_ref, kd_ref, vd_ref, cdk_ref, cdv_ref, qm_ref, km_ref, vm_ref, kc_ref, vc_ref,
                 dl_ref, gsub_ref, ob_ref, od_ref):
    i = pl.program_id(0)
    j = pl.program_id(1)
    dl = dl_ref[...]
    lam = (jnp.exp(jnp.sum(dl[0:1] * dl[1:2], axis=-1, keepdims=True))
           - jnp.exp(jnp.sum(dl[2:3] * dl[3:4], axis=-1, keepdims=True)) + lam_init)
    @pl.when(i < CTX_TILES)
    def _():
        for sub in range(TQ // CTX_T):
            rows = pl.ds(pl.multiple_of(j * TQ + sub * CTX_T, CTX_T), CTX_T)
            _attend(qd_ref[rows, :], [kd_ref[rows, :]], [vd_ref[rows, :]],
                    qm_ref[rows, :], [km_ref[rows, :]], [vm_ref[rows, :]],
                    lam, gsub_ref[...], lam_init, ob_ref, od_ref, slice(sub * CTX_T, (sub + 1) * CTX_T))

    @pl.when(i >= CTX_TILES)
    def _():
        rows = pl.ds(pl.multiple_of(j * TQ, TQ), TQ)
        _attend(qd_ref[rows, :], [cdk_ref[...], kd_ref[...]], [cdv_ref[...], vd_ref[...]],
                qm_ref[rows, :], [kc_ref[...], km_ref[...]], [vc_ref[...], vm_ref[...]],
                lam, gsub_ref[...], lam_init, ob_ref, od_ref, slice(None))


def _attn_call(l, qd, kd, vd, cdk, cdv, qm, km, vm, kc, vc, w):
    tile = lambda width: pl.BlockSpec((TILE, width), lambda i, j: (i, 0))
    cache = lambda width: pl.BlockSpec((None, None, PAST, width),
                                       lambda i, j: (jnp.maximum(i - CTX_TILES, 0), l, 0, 0))
    blk = pl.BlockSpec((TQ, BW), lambda i, j: (i * (TILE // TQ) + j, 0))
    return pl.pallas_call(
        functools.partial(_attn_kernel, _lambda_init(l)),
        grid=(TILES, TILE // TQ),
        in_specs=[tile(BW), tile(BW), tile(BW), cache(BW), cache(BW), tile(512), tile(512), tile(BW),
                  cache(512), cache(BW), _layer((4, DA_D), l), _layer((1, BW), l)],
        out_specs=[blk, blk],
        out_shape=[jax.ShapeDtypeStruct((ROWS, BW), BF16)] * 2,
        compiler_params=_params(2),
        name="attention",
    )(qd, kd, vd, cdk, cdv, qm, km, vm, kc, vc, w["dl"], w["gsub"])


def _merge_kernel(n_x, *refs):
    x_refs = refs[:n_x]
    mod_ref, hb_ref, wg_ref, wb_ref, wo_ref, pa_ref, pb_ref, pc_ref, pd_ref, o_ref = refs[n_x:]
    mod = mod_ref[0]
    x = _read_x(x_refs, pl.program_id(0) >= CTX_TILES)
    hb = hb_ref[...]
    merged = None
    for n, br_ref in enumerate((pa_ref, pb_ref, pc_ref, pd_ref)):
        gate = _sigmoid(_dot_nt(hb, wg_ref[n * D:(n + 1) * D, :]))
        t = gate * _dot(br_ref[...], wb_ref[n])
        merged = t if merged is None else merged + t
    o_ref[...] = x + mod[:, 2 * D:3 * D] * _dot(merged.astype(BF16), wo_ref[...])


def _merge_call(l, xs, mods, w, hb, pa, pb, pc, pd):
    return pl.pallas_call(
        functools.partial(_merge_kernel, len(xs)),
        grid=(TILES,),
        in_specs=_x_specs(len(xs) == 2) + [
            _mod_spec(l), _row_spec(D), _layer((4 * D, D), l), _layer((4, BW, D), l), _layer((D, D), l),
            _row_spec(BW), _row_spec(BW), _row_spec(BW), _row_spec(BW)],
        out_specs=_row_spec(D),
        out_shape=jax.ShapeDtypeStruct((ROWS, D), F32),
        compiler_params=_params(1),
        name="merge",
    )(*xs, mods, hb, w["win"], w["wbr"], w["wout"], pa, pb, pc, pd)


def _ffn_kernel(n_o, x_ref, mod_ref, g2_ref, wu_ref, cv_ref, wd_ref, *refs):
    o_refs = refs[:n_o]
    hb_ref, acc_ref = refs[n_o:]
    is_lat = pl.program_id(0) >= CTX_TILES
    is_ctx = jnp.logical_not(is_lat)
    mod = mod_ref[0]
    hb_ref[...] = _norm_mod(x_ref[...], g2_ref[...], mod[:, 4 * D:5 * D], mod[:, 3 * D:4 * D]).astype(BF16)
    acc_ref[...] = jnp.zeros((TILE, D), F32)

    def edge_masks(width):
        row8 = lax.broadcasted_iota(jnp.int32, (SUBLANES, width), 0)
        return (row8 == 0, row8 == SUBLANES - 1,
                row8 == jnp.where(is_ctx, 0, -1), row8 == jnp.where(is_ctx, SUBLANES - 1, -1))

    def conv(cols, masks):
        tile_start, tile_end, seq_start, seq_end = masks
        u = _dot(hb_ref[...], wu_ref[:, cols])
        cv = cv_ref[:, cols]
        prev, nxt = pltpu.roll(u, 1, 0), pltpu.roll(u, TILE - 1, 0)
        prev_parts, nxt_parts = [], []
        for r0 in range(0, TILE, CTX_T):
            r1 = r0 + CTX_T
            start = tile_start if r0 == 0 else seq_start
            end = tile_end if r1 == TILE else seq_end
            prev_parts += [jnp.where(start, 0.0, prev[r0:r0 + SUBLANES]), prev[r0 + SUBLANES:r1]]
            nxt_parts += [nxt[r0:r1 - SUBLANES], jnp.where(end, 0.0, nxt[r1 - SUBLANES:r1])]
        prev, nxt = jnp.concatenate(prev_parts, axis=0), jnp.concatenate(nxt_parts, axis=0)
        return prev * cv[0:1] + u * cv[1:2] + nxt * cv[2:3] + cv[3:4]

    def chunk(c0, width, masks):
        h = conv(pl.ds(c0, width), masks)
        v = conv(pl.ds(FFN + c0, width), masks)
        act = (h * (1.0 + jnp.tanh(h)) * v).astype(BF16)
        acc_ref[...] += _dot(act, wd_ref[pl.ds(c0, width), :])

    masks = edge_masks(FFN_CHUNK)

    def body(c, carry):
        chunk(pl.multiple_of(c * FFN_CHUNK, FFN_CHUNK), FFN_CHUNK, masks)
        return carry

    lax.fori_loop(0, FFN // FFN_CHUNK, body, 0)
    if FFN % FFN_CHUNK:
        chunk(FFN - FFN % FFN_CHUNK, FFN % FFN_CHUNK, edge_masks(FFN % FFN_CHUNK))
    y = x_ref[...] + mod[:, 5 * D:6 * D] * acc_ref[...]
    if n_o == 1:
        o_refs[0][...] = y
    else:
        @pl.when(is_ctx)
        def _():
            o_refs[0][...] = y

        @pl.when(is_lat)
        def _():
            o_refs[1][...] = y


def _ffn_call(l, x, mods, w, split_out):
    if split_out:
        out_specs = [pl.BlockSpec(s.block_shape, s.index_map, pipeline_mode=pl.Buffered(1)) for s in _x_specs(True)]
        out_shape = [jax.ShapeDtypeStruct((CTX_ROWS, D), F32), jax.ShapeDtypeStruct((LAT_ROWS, D), F32)]
    else:
        out_specs = _x_specs(False)
        out_shape = [jax.ShapeDtypeStruct((ROWS, D), F32)]
    return pl.pallas_call(
        functools.partial(_ffn_kernel, len(out_shape)),
        grid=(TILES,),
        in_specs=[_row_spec(D), _mod_spec(l), _layer((1, D), l),
                  _layer((D, 2 * FFN), l), _layer((8, 2 * FFN), l), _layer((FFN, D), l)],
        out_specs=out_specs,
        out_shape=out_shape,
        scratch_shapes=[pltpu.VMEM((TILE, D), BF16), pltpu.VMEM((TILE, D), F32)],
        compiler_params=_params(1),
        name="ffn",
    )(x, mods, w["g2"], w["wup"], w["cv"], w["wdown"])


def _rope_tables():
    half = 8
    inv = ROPE_BASE ** (-np.arange(half, dtype=np.float64) / half)
    t = np.arange(LAT_T)
    rows, cols = (t // GRID_W).astype(np.float64), (t % GRID_W).astype(np.float64)
    ang = np.concatenate([rows[:, None] * inv[None, :]] * 2 + [cols[:, None] * inv[None, :]] * 2, axis=1)
    cos, sin = np.cos(ang), np.sin(ang)

    def diff_layout(a):
        return np.tile(a, (1, LANES // 32))

    def mla_layout(a, fill):
        out = np.full((LAT_T, LANES), fill)
        out[:, MLA_NOPE:MLA_QK] = a
        return out

    ident = lambda layout: np.stack([layout(np.ones_like(cos)), layout(np.zeros_like(cos))])
    ropd = np.stack([ident(diff_layout), np.stack([diff_layout(cos), diff_layout(sin)])])
    ropm = np.stack([np.stack([mla_layout(np.ones_like(cos), 1.0), mla_layout(np.zeros_like(cos), 0.0)]),
                     np.stack([mla_layout(cos, 1.0), mla_layout(sin, 0.0)])])
    return jnp.asarray(ropd, F32), jnp.asarray(ropm, F32)


def _pool_inv_count():
    out = []
    for seq in (CTX_T, LAT_T):
        t = np.arange(TILE) % seq
        cols = [np.repeat((1.0 / (np.minimum(t + half, seq) - np.maximum(t - half, 0)))[:, None], BW // 4, axis=1)
                for half in POOL_HALF]
        out.append(np.concatenate(cols, axis=1))
    return jnp.asarray(np.stack(out), F32)


def _rotate_half_matrix(lanes):
    r = np.zeros((LANES, LANES))
    for j in lanes:
        if j % 16 < 8:
            r[j + 8, j] = -1.0
        else:
            r[j - 8, j] = 1.0
    return np.concatenate([r, r], axis=0)


def _matrix_tables():
    def cs(n):
        k = np.arange(n)
        ang = 2.0 * np.pi * ((k[:, None] * k[None, :]) % n) / n
        return np.cos(ang), np.sin(ang)

    c64, s64 = cs(64)
    eye4 = np.eye(4)
    cl, sl = cs(LAT_T)
    cc_, sc_ = cs(CTX_T)
    ft = np.stack([np.stack([np.kron(eye4, cc_), -np.kron(eye4, sc_)]), np.stack([cl, -sl])])
    as_bf16 = lambda a: jnp.asarray(a, F32).astype(BF16)
    return dict(
        cc=as_bf16(np.kron(eye4, c64)), ss=as_bf16(np.kron(eye4, s64)), ft=as_bf16(ft),
        seg=as_bf16(np.kron(np.eye(BW // DA_D), np.ones((DA_D, DA_D)))),
        ones=as_bf16(np.kron(np.eye(BW // LANES), np.ones((LANES, LANES)))),
        rotd=as_bf16(_rotate_half_matrix(range(LANES))),
        rotm=as_bf16(_rotate_half_matrix(range(MLA_NOPE, MLA_QK))))


def _pad_last(a, lo, hi):
    return jnp.pad(a, ((0, 0),) * (a.ndim - 1) + ((lo, hi),))


def _prep_weights(p):
    win = jnp.swapaxes(p["w_in"], 1, 2).astype(BF16)
    mla = win[:, W_MLA:, :]
    cq, ckv, kr = mla[:, :Q_LORA], mla[:, Q_LORA:Q_LORA + KV_LORA], mla[:, Q_LORA + KV_LORA:]
    pad_rows = lambda a, lo, hi: jnp.pad(a, ((0, 0), (lo, hi), (0, 0)))
    wmla = jnp.concatenate([pad_rows(cq, 0, BW - Q_LORA), ckv, pad_rows(kr, MLA_NOPE, LANES - MLA_QK)], axis=1)
    poolw = jnp.einsum("lgce,gh->lgche", p["pool_w"], jnp.eye(4, dtype=F32)).reshape(DEPTH, BW, BW).astype(BF16)
    wuq = p["w_uq"].reshape(DEPTH, Q_LORA, HEADS, MLA_QK)
    wuq = jnp.pad(wuq, ((0, 0), (0, BW - Q_LORA), (0, 0), (0, LANES - MLA_QK))).reshape(DEPTH, BW, HEADS * LANES)
    wukv = p["w_ukv"].reshape(DEPTH, KV_LORA, HEADS, MLA_NOPE + MLA_V)
    wuk = _pad_last(wukv[..., :MLA_NOPE], 0, LANES - MLA_NOPE).reshape(DEPTH, KV_LORA, HEADS * LANES)
    wuv = wukv[..., MLA_NOPE:].reshape(DEPTH, KV_LORA, HEADS * MLA_V)
    cv = jnp.concatenate([p["conv_w"], p["conv_b"][:, None, :], jnp.zeros((DEPTH, 4, 2 * FFN), F32)], axis=1)
    cv = cv * jnp.where(jnp.arange(2 * FFN) < FFN, 0.5, 1.0)

    row = lambda a, width: _pad_last(a, 0, width - a.shape[-1])[:, None, :]
    tiled = lambda a, reps: jnp.tile(a, (1, reps))[:, None, :]
    return {
        "g1": row(p["g_norm1"], D), "g2": row(p["g_norm2"], D),
        "win": win, "wmla": wmla,
        "poolw": poolw, "pools": row(p["pool_scale"], BW),
        "dgq": tiled(p["diff_qnorm"], LANES // DA_D), "dgk": tiled(p["diff_knorm"], LANES // DA_D),
        "dl": p["diff_lambda"], "gsub": tiled(p["diff_subln"], HEADS),
        "gqa": row(p["mla_qa_norm"], BW), "wuq": wuq.astype(BF16), "gqn": row(p["mla_qnorm"], LANES),
        "gkv": row(p["mla_kv_norm"], KV_LORA), "wuk": wuk.astype(BF16), "wuv": wuv.astype(BF16),
        "gkn": row(p["mla_knorm"], LANES),
        "wbr": p["w_branch"].astype(BF16), "wout": p["w_out"].astype(BF16),
        "wup": p["w_up"].astype(BF16), "cv": cv, "wdown": p["w_down"].astype(BF16),
    }


def kernel(x_prompt, x_sample, c, c_ctx, cache_diff_k, cache_diff_v, cache_mla_ckv, cache_mla_krope, w_ada, b_ada, g_norm1, g_norm2, w_in, pool_w, pool_scale, diff_qnorm, diff_knorm, diff_lambda, diff_subln, mla_qa_norm, w_uq, mla_kv_norm, w_ukv, mla_qnorm, mla_knorm, w_branch, w_out, w_up, conv_w, conv_b, w_down):
    w = _prep_weights(dict(
        w_in=w_in, pool_w=pool_w, pool_scale=pool_scale, diff_qnorm=diff_qnorm, diff_knorm=diff_knorm,
        diff_lambda=diff_lambda, diff_subln=diff_subln, mla_qa_norm=mla_qa_norm, w_uq=w_uq,
        mla_kv_norm=mla_kv_norm, w_ukv=w_ukv, mla_qnorm=mla_qnorm, mla_knorm=mla_knorm, w_branch=w_branch,
        w_out=w_out, w_up=w_up, conv_w=conv_w, conv_b=conv_b, w_down=w_down, g_norm1=g_norm1,
        g_norm2=g_norm2))
    ropd, ropm = _rope_tables()
    consts = dict(_matrix_tables(), ropd=ropd, ropm=ropm, icnt=_pool_inv_count())

    cond = jnp.concatenate([c_ctx[None, :], c, jnp.zeros((16 - 1 - LAT_B, D), F32)], axis=0)
    mods = _mod_call(cond, w_ada, b_ada).reshape(DEPTH, 16, 1, 6 * D)

    kc, vc = _ctxkv_call(cache_mla_ckv, _pad_last(cache_mla_krope, MLA_NOPE, LANES - MLA_QK), w)
    cdk = cache_diff_k.reshape(LAT_B, DEPTH, PAST, BW).astype(BF16)
    cdv = cache_diff_v.reshape(LAT_B, DEPTH, PAST, BW).astype(BF16)

    xs = [x_prompt.reshape(CTX_ROWS, D), x_sample.reshape(LAT_ROWS, D)]
    new = [[], [], [], []]
    for l in range(DEPTH):
        hb, pa, qd, kd, vd, oc, qm, km, vm, *ctx_outs = _inproj_call(l, xs, mods, w, consts)
        ob, od = _attn_call(l, qd, kd, vd, cdk, cdv, qm, km, vm, kc, vc, w)
        x = _merge_call(l, xs, mods, w, hb, pa, ob, oc, od)
        xs = _ffn_call(l, x, mods, w, split_out=(l == DEPTH - 1))
        for acc, a in zip(new, ctx_outs):
            acc.append(a)

    stack = lambda arrs, shape: jnp.stack(arrs, axis=1).reshape((CTX_B, DEPTH, CTX_T) + shape)
    return (xs[0].reshape(CTX_B, CTX_T, D), xs[1].reshape(LAT_B, LAT_T, D),
            stack([a.reshape(CTX_B, CTX_T, BW) for a in new[0]], (HEADS, DA_HD)),
            stack([a.reshape(CTX_B, CTX_T, BW) for a in new[1]], (HEADS, DA_HD)),
            stack([a.reshape(CTX_B, CTX_T, KV_LORA) for a in new[2]], (KV_LORA,)),
            stack([a.reshape(CTX_B, CTX_T, MLA_ROPE) for a in new[3]], (MLA_ROPE,)))
```

```python
import functools
import math

import jax
import jax.numpy as jnp
import numpy as np
from jax import lax
from jax.experimental import pallas as pl
from jax.experimental.pallas import tpu as pltpu

F32 = jnp.float32
BF16 = jnp.bfloat16

D = 1024
DEPTH = 4
CTX_B, CTX_T = 16, 256
LAT_B, LAT_T = 8, 1024
PAST = 512
TILE = 1024
CTX_TILES = CTX_B * CTX_T // TILE
LAT_TILES = LAT_B * LAT_T // TILE
TILES = CTX_TILES + LAT_TILES
ROWS = TILES * TILE
CTX_ROWS = CTX_TILES * TILE
LAT_ROWS = LAT_TILES * TILE
GRID_W = 64
BW = 256
DA_D = 32
DA_HD = 64
HEADS = 4
MLA_NOPE, MLA_ROPE, MLA_V, MLA_QK = 64, 32, 64, 96
Q_LORA, KV_LORA = 192, 128
FFN = 2816
FFN_CHUNK = 768
SUBLANES = 8
POOL_HALF = (1, 2, 4, 8)
POOL_PAD = 16
ROPE_BASE = 10000.0
EPS = 1e-6
LOG2E = 1.4426950408889634
TQ = 512
LANES = 128
VMEM_LIMIT = 60 * 1024 * 1024

W_BLOCKS = 4 * D // BW
W_MLA = 4 * D + 5 * BW
C_CQ, C_CKV, C_END = 0, 256, 512


def _lambda_init(l):
    return 0.8 - 0.6 * math.exp(-0.3 * l)


def _sigmoid(x):
    return 0.5 * jnp.tanh(0.5 * x) + 0.5


def _rms(x, n):
    ms = jnp.sum(x * x, axis=-1, keepdims=True) * (1.0 / n)
    return x * lax.rsqrt(ms + EPS)


def _dot(a, b):
    return jnp.dot(a, b, preferred_element_type=F32)


def _dot_nt(a, b):
    return lax.dot_general(a, b, (((1,), (1,)), ((), ())), preferred_element_type=F32)


def _const(shape):
    n = len(shape)
    return pl.BlockSpec(shape, lambda *_: (0,) * n, pipeline_mode=pl.Buffered(1))


def _layer(shape, l):
    n = len(shape)
    return pl.BlockSpec((None,) + shape, lambda *_: (l,) + (0,) * n, pipeline_mode=pl.Buffered(1))


def _params(n_grid):
    return pltpu.CompilerParams(dimension_semantics=("arbitrary",) * n_grid, vmem_limit_bytes=VMEM_LIMIT)


def _row_spec(width):
    return pl.BlockSpec((TILE, width), lambda i: (i, 0))


def _ctx_row_spec(width):
    return pl.BlockSpec((TILE, width), lambda i: (jnp.minimum(i, CTX_TILES - 1), 0))


def _lat_row_spec(width):
    return pl.BlockSpec((TILE, width), lambda i: (jnp.maximum(i - CTX_TILES, 0), 0))


def _x_specs(split):
    return [_ctx_row_spec(D), _lat_row_spec(D)] if split else [_row_spec(D)]


def _read_x(x_refs, is_lat):
    if len(x_refs) == 2:
        return jnp.where(is_lat, x_refs[1][...], x_refs[0][...])
    return x_refs[0][...]


def _in_cols(l, n):
    return pl.BlockSpec((None, BW, D), lambda *_: (l, n, 0), pipeline_mode=pl.Buffered(1))


def _mod_spec(l):
    return pl.BlockSpec((None, 1, 1, 6 * D), lambda i: (l, jnp.maximum(i - (CTX_TILES - 1), 0), 0, 0))


def _group_spec(shape):
    n = len(shape)
    return pl.BlockSpec((None,) + shape, lambda i: (jnp.minimum(i // CTX_TILES, 1),) + (0,) * n,
                        pipeline_mode=pl.Buffered(1))


def _mod_kernel(c_ref, w_ref, b_ref, o_ref):
    c = c_ref[...]
    s = (c * _sigmoid(c)).astype(BF16)
    o_ref[0] = _dot(s, w_ref[0].astype(BF16)) + b_ref[0]


def _mod_call(cond, w_ada, b_ada):
    nc = 4
    cw = 6 * D // nc
    return pl.pallas_call(
        _mod_kernel,
        grid=(DEPTH, nc),
        in_specs=[
            pl.BlockSpec((16, D), lambda l, j: (0, 0)),
            pl.BlockSpec((1, D, cw), lambda l, j: (l, 0, j)),
            pl.BlockSpec((1, 1, cw), lambda l, j: (l, 0, j)),
        ],
        out_specs=pl.BlockSpec((1, 16, cw), lambda l, j: (l, 0, j)),
        out_shape=jax.ShapeDtypeStruct((DEPTH, 16, 6 * D), F32),
        compiler_params=_params(2),
        name="adaln_mod",
    )(cond, w_ada, b_ada.reshape(DEPTH, 1, 6 * D))


def _mla_key_heads(knope, kr_pad, gkn):
    out = []
    for h in range(HEADS):
        kh = knope[:, h * LANES:(h + 1) * LANES] + kr_pad
        out.append(_rms(kh, MLA_QK) * gkn)
    return out


def _ctxkv_kernel(ckv_ref, kr_ref, wuk_ref, wuv_ref, gkn_ref, kc_ref, vc_ref):
    rows = LAT_B * PAST
    cb = ckv_ref[...].reshape(rows, KV_LORA).astype(BF16)
    knope = _dot(cb, wuk_ref[...])
    heads = _mla_key_heads(knope, kr_ref[...].reshape(rows, LANES), gkn_ref[...])
    for h in range(HEADS):
        kc_ref[:, :, h * LANES:(h + 1) * LANES] = heads[h].astype(BF16).reshape(LAT_B, PAST, LANES)
    vc_ref[...] = _dot(cb, wuv_ref[...]).astype(BF16).reshape(LAT_B, PAST, BW)


def _ctxkv_call(ckv, kr_pad, w):
    cache = lambda width: pl.BlockSpec((LAT_B, None, PAST, width), lambda l: (0, l, 0, 0))
    wspec = lambda r, c: pl.BlockSpec((None, r, c), lambda l: (l, 0, 0))
    return pl.pallas_call(
        _ctxkv_kernel,
        grid=(DEPTH,),
        in_specs=[cache(KV_LORA), cache(LANES), wspec(KV_LORA, 512), wspec(KV_LORA, BW), wspec(1, LANES)],
        out_specs=[cache(512), cache(BW)],
        out_shape=[
            jax.ShapeDtypeStruct((LAT_B, DEPTH, PAST, 512), BF16),
            jax.ShapeDtypeStruct((LAT_B, DEPTH, PAST, BW), BF16),
        ],
        compiler_params=_params(1),
        name="mla_ctx_keys",
    )(ckv, kr_pad, w["wuk"], w["wuv"], w["gkn"])


def _norm_mod(x, g, sc, sh):
    return _rms(x, D) * (g * (1.0 + sc)) + sh


def _rope(x, tab_ref, rot):
    hi = x.astype(BF16)
    pieces = jnp.concatenate([hi, (x - hi.astype(F32)).astype(BF16)], axis=1)
    return x * tab_ref[0] + _dot(pieces, rot) * tab_ref[1]


def _group_inv_rms(x, ones, n):
    sq = (x * x).astype(BF16)
    ms = jnp.concatenate([_dot(sq[:, c:c + BW], ones) for c in range(0, x.shape[1], BW)], axis=1)
    return lax.rsqrt(ms * (1.0 / n) + EPS)


def _inproj_kernel(n_x, *refs):
    x_refs = refs[:n_x]
    (mod_ref, g1_ref, wxa_ref, wq_ref, wk_ref, wv_ref, wxc_ref, wmla_ref,
     icnt_ref, poolw_ref, pools_ref, seg_ref, ones_ref, rotd_ref, rotm_ref, dgq_ref, dgk_ref,
     ropd_ref, ropm_ref, cc_ref, ss_ref, ft_ref, gqa_ref, wuq_ref, gqn_ref, gkv_ref,
     wuk_ref, wuv_ref, gkn_ref,
     pa_ref, qd_ref, kd_ref, vd_ref, oc_ref, qm_ref, km_ref, vm_ref,
     kcf_ref, vf_ref, ckv_ref, kr_ref, pad_ref) = refs[n_x:]
    is_lat = pl.program_id(0) >= CTX_TILES
    mod = mod_ref[0]
    hb = _norm_mod(_read_x(x_refs, is_lat), g1_ref[...], mod[:, D:2 * D], mod[:, 0:D]).astype(BF16)

    xa = _dot_nt(hb, wxa_ref[...])
    q = _dot_nt(hb, wq_ref[...])
    k = _dot_nt(hb, wk_ref[...])
    v = _dot_nt(hb, wv_ref[...])
    xcb = _dot_nt(hb, wxc_ref[...]).astype(BF16)
    cq = _dot_nt(hb, wmla_ref[C_CQ:C_CQ + BW, :])
    ckv_kr = _dot_nt(hb, wmla_ref[C_CKV:C_END, :])

    pad_ref[0:POOL_PAD, :] = jnp.zeros((POOL_PAD, BW), F32)
    pad_ref[POOL_PAD + TILE:, :] = jnp.zeros((POOL_PAD, BW), F32)
    pad_ref[POOL_PAD:POOL_PAD + TILE, :] = xa
    lane = lax.broadcasted_iota(jnp.int32, (TILE, BW), 1)
    row8 = lax.broadcasted_iota(jnp.int32, (SUBLANES, BW), 0)

    def term(s):
        v = pad_ref[POOL_PAD + s:POOL_PAD + s + TILE, :]
        kill = row8 < jnp.where(is_lat, 0, -s) if s < 0 else row8 >= jnp.where(is_lat, SUBLANES, SUBLANES - s)
        parts = []
        for r0 in range(0, TILE, CTX_T):
            r1 = r0 + CTX_T
            if s < 0:
                head = v[r0:r0 + SUBLANES]
                parts += [head if r0 == 0 else jnp.where(kill, 0.0, head), v[r0 + SUBLANES:r1]]
            else:
                tail = v[r1 - SUBLANES:r1]
                parts += [v[r0:r1 - SUBLANES], tail if r1 == TILE else jnp.where(kill, 0.0, tail)]
        return jnp.concatenate(parts, axis=0)

    acc = xa
    wins = []
    lo_s, hi_s = 0, 0
    for half in POOL_HALF:
        for s in list(range(-half, lo_s)) + list(range(hi_s + 1, half)):
            acc = acc + term(s)
        lo_s, hi_s = -half, half - 1
        wins.append(acc)
    group = BW // len(POOL_HALF)
    win = jnp.where(lane < group, wins[0],
                    jnp.where(lane < 2 * group, wins[1], jnp.where(lane < 3 * group, wins[2], wins[3])))
    pooled = (win * icnt_ref[...] - xa).astype(BF16)
    pa_ref[...] = (_dot(pooled, poolw_ref[...]) * pools_ref[...]).astype(BF16)

    qscale = DA_D ** -0.5 * LOG2E
    rq = _group_inv_rms(q, seg_ref[...], DA_D) * qscale
    rk = _group_inv_rms(k, seg_ref[...], DA_D)
    kns = []
    for s in range(BW // LANES):
        sl = slice(s * LANES, (s + 1) * LANES)
        kg = k[:, sl] * dgk_ref[...]
        kns.append(kg * rk[:, sl])
        qd_ref[:, sl] = (_rope(q[:, sl] * dgq_ref[...], ropd_ref, rotd_ref[...]) * rq[:, sl]).astype(BF16)
        kd_ref[:, sl] = (_rope(kg, ropd_ref, rotd_ref[...]) * rk[:, sl]).astype(BF16)
    vd_ref[...] = v.astype(BF16)

    xcc = _dot(xcb, cc_ref[...]).astype(BF16)
    xcs = _dot(xcb, ss_ref[...]).astype(BF16)
    y = _dot(ft_ref[0], xcc) + _dot(ft_ref[1], xcs)
    fnorm = jnp.where(is_lat, (LAT_T * 64.0) ** -0.5, (CTX_T * 64.0) ** -0.5)
    oc_ref[...] = (y * fnorm).astype(BF16)

    qa = (_rms(cq, Q_LORA) * gqa_ref[...]).astype(BF16)
    qf = _dot(qa, wuq_ref[...])
    mscale = MLA_QK ** -0.5 * LOG2E
    rqm = _group_inv_rms(qf, ones_ref[...], MLA_QK) * mscale
    for h in range(HEADS):
        sl = slice(h * LANES, (h + 1) * LANES)
        qm_ref[:, sl] = (_rope(qf[:, sl] * gqn_ref[...], ropm_ref, rotm_ref[...]) * rqm[:, sl]).astype(BF16)
    ckv, kr_pad = ckv_kr[:, :KV_LORA], ckv_kr[:, KV_LORA:]
    ckv_n = _rms(ckv, KV_LORA) * gkv_ref[...]
    cb = ckv_n.astype(BF16)
    knope = _dot(cb, wuk_ref[...])
    kr_rot = _rope(kr_pad * gkn_ref[...], ropm_ref, rotm_ref[...])
    kr_ssq = _dot((kr_pad * kr_pad).astype(BF16), ones_ref[0:LANES, 0:LANES])
    kn_ssq = jnp.concatenate([_dot((knope[:, c:c + BW] * knope[:, c:c + BW]).astype(BF16), ones_ref[...])
                              for c in range(0, HEADS * LANES, BW)], axis=1)
    for h in range(HEADS):
        sl = slice(h * LANES, (h + 1) * LANES)
        rkm = lax.rsqrt((kn_ssq[:, sl] + kr_ssq) * (1.0 / MLA_QK) + EPS)
        km_ref[:, sl] = ((knope[:, sl] * gkn_ref[...] + kr_rot) * rkm).astype(BF16)
    vm_ref[...] = _dot(cb, wuv_ref[...]).astype(BF16)

    @pl.when(jnp.logical_not(is_lat))
    def _():
        for s, kn in enumerate(kns):
            kcf_ref[:, s * LANES:(s + 1) * LANES] = kn
        vf_ref[...] = v
        ckv_ref[...] = ckv_n
        kr_ref[...] = kr_pad[:, MLA_NOPE:MLA_NOPE + MLA_ROPE]


def _inproj_call(l, xs, mods, w, consts):
    row_outs = [(BW, BF16), (BW, BF16), (BW, BF16), (BW, BF16), (BW, BF16), (512, BF16), (512, BF16), (BW, BF16)]
    ctx_outs = [(BW, F32), (BW, F32), (KV_LORA, F32), (MLA_ROPE, F32)]
    return pl.pallas_call(
        functools.partial(_inproj_kernel, len(xs)),
        grid=(TILES,),
        in_specs=_x_specs(len(xs) == 2) + [
            _mod_spec(l), _layer((1, D), l), *[_in_cols(l, W_BLOCKS + n) for n in range(5)],
            _layer((C_END, D), l), _group_spec((TILE, BW)), _layer((BW, BW), l), _layer((1, BW), l),
            _const((BW, BW)), _const((BW, BW)), _const((BW, LANES)), _const((BW, LANES)),
            _layer((1, LANES), l), _layer((1, LANES), l),
            _group_spec((2, TILE, LANES)), _group_spec((2, TILE, LANES)),
            _const((BW, BW)), _const((BW, BW)), _group_spec((2, TILE, TILE)),
            _layer((1, BW), l), _layer((BW, 512), l), _layer((1, LANES), l), _layer((1, KV_LORA), l),
            _layer((KV_LORA, 512), l), _layer((KV_LORA, BW), l), _layer((1, LANES), l),
        ],
        out_specs=[_row_spec(wd) for wd, _ in row_outs] + [_ctx_row_spec(wd) for wd, _ in ctx_outs],
        out_shape=([jax.ShapeDtypeStruct((ROWS, wd), dt) for wd, dt in row_outs]
                   + [jax.ShapeDtypeStruct((CTX_ROWS, wd), dt) for wd, dt in ctx_outs]),
        scratch_shapes=[pltpu.VMEM((TILE + 2 * POOL_PAD, BW), F32)],
        compiler_params=_params(1),
        name="inproj",
    )(*xs, mods, w["g1"], *[w["win"]] * 5, w["wmla"], consts["icnt"], w["poolw"], w["pools"], consts["seg"],
      consts["ones"], consts["rotd"], consts["rotm"], w["dgq"], w["dgk"],
      consts["ropd"], consts["ropm"], consts["cc"], consts["ss"], consts["ft"],
      w["gqa"], w["wuq"], w["gqn"], w["gkv"], w["wuk"], w["wuv"], w["gkn"])


def _softmax_pv(ss, vs):
    m = ss[0].max(axis=-1, keepdims=True)
    for s in ss[1:]:
        m = jnp.maximum(m, s.max(axis=-1, keepdims=True))
    ps = [jnp.exp2(s - m) for s in ss]
    l = ps[0].sum(axis=-1, keepdims=True)
    for p in ps[1:]:
        l = l + p.sum(axis=-1, keepdims=True)
    return sum(_dot(p.astype(BF16), v) for p, v in zip(ps, vs)) * (1.0 / l)


def _attend(qd, kds, vds, qm, kms, vms, lam, gsub, lam_init, ob_ref, od_ref, out_rows):
    n = qd.shape[0]
    lane = lax.broadcasted_iota(jnp.int32, (n, BW), 1)
    zero = jnp.zeros((), BF16)

    jobs = []
    for h in range(HEADS):
        for half in range(2):
            lo = h * DA_HD + half * DA_D
            jobs.append((jnp.where((lane >= lo) & (lane < lo + DA_D), qd, zero), kds, vds))
        sl = slice(h * LANES, (h + 1) * LANES)
        jobs.append((qm[:, sl], [k[:, sl] for k in kms], vms))

    score = lambda job: [_dot_nt(job[0], k) for k in job[1]]
    out_b = jnp.zeros((n, BW), F32)
    out_d = jnp.zeros((n, BW), F32)
    pvs, nxt = [], score(jobs[0])
    for c, job in enumerate(jobs):
        ss, nxt = nxt, score(jobs[c + 1]) if c + 1 < len(jobs) else None
        pvs.append(_softmax_pv(ss, job[2]))
        if len(pvs) == 3:
            h = c // 3
            head = (lane >= h * DA_HD) & (lane < (h + 1) * DA_HD)
            o = jnp.where(head, pvs[0] - lam * pvs[1], 0.0)
            ms = jnp.sum(o * o, axis=-1, keepdims=True) * (1.0 / DA_HD)
            out_b = out_b + o * lax.rsqrt(ms + EPS)
            out_d = out_d + jnp.where(head, pvs[2], 0.0)
            pvs = []
    ob_ref[out_rows, :] = (out_b * gsub * (1.0 - lam_init)).astype(BF16)
    od_ref[out_rows, :] = out_d.astype(BF16)


def _attn_kernel(lam_init, qd_ref, kd_ref, vd_ref, cdk_ref, cdv_ref, qm_ref, km_ref, vm_ref, kc_ref, vc_ref,
                 dl_ref, gsub_ref, ob_ref, od_ref):
    i = pl.program_id(0)
    j = pl.program_id(1)
    dl = dl_ref[...]
    lam = (jnp.exp(jnp.sum(dl[0:1] * dl[1:2], axis=-1, keepdims=True))
           - jnp.exp(jnp.sum(dl[2:3] * dl[3:4], axis=-1, keepdims=True)) + lam_init)
    @pl.when(i < CTX_TILES)
    def _():
        for sub in range(TQ // CTX_T):
            rows = pl.ds(pl.multiple_of(j * TQ + sub * CTX_T, CTX_T), CTX_T)
            _attend(qd_ref[rows, :], [kd_ref[rows, :]], [vd_ref[rows, :]],
                    qm_ref[rows, :], [km_ref[rows, :]], [vm_ref[rows, :]],
                    lam, gsub_ref[...], lam_init, ob_ref, od_ref, slice(sub * CTX_T, (sub + 1) * CTX_T))

    @pl.when(i >= CTX_TILES)
    def _():
        rows = pl.ds(pl.multiple_of(j * TQ, TQ), TQ)
        _attend(qd_ref[rows, :], [cdk_ref[...], kd_ref[...]], [cdv_ref[...], vd_ref[...]],
                qm_ref[rows, :], [kc_ref[...], km_ref[...]], [vc_ref[...], vm_ref[...]],
                lam, gsub_ref[...], lam_init, ob_ref, od_ref, slice(None))


def _attn_call(l, qd, kd, vd, cdk, cdv, qm, km, vm, kc, vc, w):
    tile = lambda width: pl.BlockSpec((TILE, width), lambda i, j: (i, 0))
    cache = lambda width: pl.BlockSpec((None, None, PAST, width),
                                       lambda i, j: (jnp.maximum(i - CTX_TILES, 0), l, 0, 0))
    blk = pl.BlockSpec((TQ, BW), lambda i, j: (i * (TILE // TQ) + j, 0))
    return pl.pallas_call(
        functools.partial(_attn_kernel, _lambda_init(l)),
        grid=(TILES, TILE // TQ),
        in_specs=[tile(BW), tile(BW), tile(BW), cache(BW), cache(BW), tile(512), tile(512), tile(BW),
                  cache(512), cache(BW), _layer((4, DA_D), l), _layer((1, BW), l)],
        out_specs=[blk, blk],
        out_shape=[jax.ShapeDtypeStruct((ROWS, BW), BF16)] * 2,
        compiler_params=_params(2),
        name="attention",
    )(qd, kd, vd, cdk, cdv, qm, km, vm, kc, vc, w["dl"], w["gsub"])


def _merge_kernel(n_x, *refs):
    x_refs = refs[:n_x]
    mod_ref, g1_ref, wg_ref, wb_ref, wo_ref, pa_ref, pb_ref, pc_ref, pd_ref, o_ref = refs[n_x:]
    mod = mod_ref[0]
    x = _read_x(x_refs, pl.program_id(0) >= CTX_TILES)
    hb = _norm_mod(x, g1_ref[...], mod[:, D:2 * D], mod[:, 0:D]).astype(BF16)
    merged = None
    for n, br_ref in enumerate((pa_ref, pb_ref, pc_ref, pd_ref)):
        gate = _sigmoid(_dot_nt(hb, wg_ref[n * D:(n + 1) * D, :]))
        t = gate * _dot(br_ref[...], wb_ref[n])
        merged = t if merged is None else merged + t
    o_ref[...] = x + mod[:, 2 * D:3 * D] * _dot(merged.astype(BF16), wo_ref[...])


def _merge_call(l, xs, mods, w, pa, pb, pc, pd):
    return pl.pallas_call(
        functools.partial(_merge_kernel, len(xs)),
        grid=(TILES,),
        in_specs=_x_specs(len(xs) == 2) + [
            _mod_spec(l), _layer((1, D), l), _layer((4 * D, D), l), _layer((4, BW, D), l), _layer((D, D), l),
            _row_spec(BW), _row_spec(BW), _row_spec(BW), _row_spec(BW)],
        out_specs=_row_spec(D),
        out_shape=jax.ShapeDtypeStruct((ROWS, D), F32),
        compiler_params=_params(1),
        name="merge",
    )(*xs, mods, w["g1"], w["win"], w["wbr"], w["wout"], pa, pb, pc, pd)


def _ffn_kernel(n_o, x_ref, mod_ref, g2_ref, wu_ref, cv_ref, wd_ref, *refs):
    o_refs = refs[:n_o]
    hb_ref, acc_ref = refs[n_o:]
    is_lat = pl.program_id(0) >= CTX_TILES
    is_ctx = jnp.logical_not(is_lat)
    mod = mod_ref[0]
    hb_ref[...] = _norm_mod(x_ref[...], g2_ref[...], mod[:, 4 * D:5 * D], mod[:, 3 * D:4 * D]).astype(BF16)
    acc_ref[...] = jnp.zeros((TILE, D), F32)

    def edge_masks(width):
        row8 = lax.broadcasted_iota(jnp.int32, (SUBLANES, width), 0)
        return (row8 == 0, row8 == SUBLANES - 1,
                row8 == jnp.where(is_ctx, 0, -1), row8 == jnp.where(is_ctx, SUBLANES - 1, -1))

    def conv(cols, masks):
        tile_start, tile_end, seq_start, seq_end = masks
        u = _dot(hb_ref[...], wu_ref[:, cols])
        cv = cv_ref[:, cols]
        prev, nxt = pltpu.roll(u, 1, 0), pltpu.roll(u, TILE - 1, 0)
        prev_parts, nxt_parts = [], []
        for r0 in range(0, TILE, CTX_T):
            r1 = r0 + CTX_T
            start = tile_start if r0 == 0 else seq_start
            end = tile_end if r1 == TILE else seq_end
            prev_parts += [jnp.where(start, 0.0, prev[r0:r0 + SUBLANES]), prev[r0 + SUBLANES:r1]]
            nxt_parts += [nxt[r0:r1 - SUBLANES], jnp.where(end, 0.0, nxt[r1 - SUBLANES:r1])]
        prev, nxt = jnp.concatenate(prev_parts, axis=0), jnp.concatenate(nxt_parts, axis=0)
        return prev * cv[0:1] + u * cv[1:2] + nxt * cv[2:3] + cv[3:4]

    def chunk(c0, width, masks):
        h = conv(pl.ds(c0, width), masks)
        v = conv(pl.ds(FFN + c0, width), masks)
        act = (h * (1.0 + jnp.tanh(h)) * v).astype(BF16)
        acc_ref[...] += _dot(act, wd_ref[pl.ds(c0, width), :])

    masks = edge_masks(FFN_CHUNK)

    def body(c, carry):
        chunk(pl.multiple_of(c * FFN_CHUNK, FFN_CHUNK), FFN_CHUNK, masks)
        return carry

    lax.fori_loop(0, FFN // FFN_CHUNK, body, 0)
    if FFN % FFN_CHUNK:
        chunk(FFN - FFN % FFN_CHUNK, FFN % FFN_CHUNK, edge_masks(FFN % FFN_CHUNK))
    y = x_ref[...] + mod[:, 5 * D:6 * D] * acc_ref[...]
    if n_o == 1:
        o_refs[0][...] = y
    else:
        @pl.when(is_ctx)
        def _():
            o_refs[0][...] = y

        @pl.when(is_lat)
        def _():
            o_refs[1][...] = y


def _ffn_call(l, x, mods, w, split_out):
    if split_out:
        out_specs = _x_specs(True)
        out_shape = [jax.ShapeDtypeStruct((CTX_ROWS, D), F32), jax.ShapeDtypeStruct((LAT_ROWS, D), F32)]
    else:
        out_specs = _x_specs(False)
        out_shape = [jax.ShapeDtypeStruct((ROWS, D), F32)]
    return pl.pallas_call(
        functools.partial(_ffn_kernel, len(out_shape)),
        grid=(TILES,),
        in_specs=[_row_spec(D), _mod_spec(l), _layer((1, D), l),
                  _layer((D, 2 * FFN), l), _layer((8, 2 * FFN), l), _layer((FFN, D), l)],
        out_specs=out_specs,
        out_shape=out_shape,
        scratch_shapes=[pltpu.VMEM((TILE, D), BF16), pltpu.VMEM((TILE, D), F32)],
        compiler_params=_params(1),
        name="ffn",
    )(x, mods, w["g2"], w["wup"], w["cv"], w["wdown"])


def _rope_tables():
    half = 8
    inv = ROPE_BASE ** (-np.arange(half, dtype=np.float64) / half)
    t = np.arange(LAT_T)
    rows, cols = (t // GRID_W).astype(np.float64), (t % GRID_W).astype(np.float64)
    ang = np.concatenate([rows[:, None] * inv[None, :]] * 2 + [cols[:, None] * inv[None, :]] * 2, axis=1)
    cos, sin = np.cos(ang), np.sin(ang)

    def diff_layout(a):
        return np.tile(a, (1, LANES // 32))

    def mla_layout(a, fill):
        out = np.full((LAT_T, LANES), fill)
        out[:, MLA_NOPE:MLA_QK] = a
        return out

    ident = lambda layout: np.stack([layout(np.ones_like(cos)), layout(np.zeros_like(cos))])
    ropd = np.stack([ident(diff_layout), np.stack([diff_layout(cos), diff_layout(sin)])])
    ropm = np.stack([np.stack([mla_layout(np.ones_like(cos), 1.0), mla_layout(np.zeros_like(cos), 0.0)]),
                     np.stack([mla_layout(cos, 1.0), mla_layout(sin, 0.0)])])
    return jnp.asarray(ropd, F32), jnp.asarray(ropm, F32)


def _pool_inv_count():
    out = []
    for seq in (CTX_T, LAT_T):
        t = np.arange(TILE) % seq
        cols = [np.repeat((1.0 / (np.minimum(t + half, seq) - np.maximum(t - half, 0)))[:, None],
                          BW // len(POOL_HALF), axis=1)
                for half in POOL_HALF]
        out.append(np.concatenate(cols, axis=1))
    return jnp.asarray(np.stack(out), F32)


def _rotate_half_matrix(lanes):
    r = np.zeros((LANES, LANES))
    for j in lanes:
        if j % 16 < 8:
            r[j + 8, j] = -1.0
        else:
            r[j - 8, j] = 1.0
    return np.concatenate([r, r], axis=0)


def _matrix_tables():
    def cs(n):
        k = np.arange(n)
        ang = 2.0 * np.pi * ((k[:, None] * k[None, :]) % n) / n
        return np.cos(ang), np.sin(ang)

    c64, s64 = cs(64)
    eye4 = np.eye(4)
    cl, sl = cs(LAT_T)
    cc_, sc_ = cs(CTX_T)
    ft = np.stack([np.stack([np.kron(eye4, cc_), -np.kron(eye4, sc_)]), np.stack([cl, -sl])])
    as_bf16 = lambda a: jnp.asarray(a, F32).astype(BF16)
    return dict(
        cc=as_bf16(np.kron(eye4, c64)), ss=as_bf16(np.kron(eye4, s64)), ft=as_bf16(ft),
        seg=as_bf16(np.kron(np.eye(BW // DA_D), np.ones((DA_D, DA_D)))),
        ones=as_bf16(np.kron(np.eye(BW // LANES), np.ones((LANES, LANES)))),
        rotd=as_bf16(_rotate_half_matrix(range(LANES))),
        rotm=as_bf16(_rotate_half_matrix(range(MLA_NOPE, MLA_QK))))


def _pad_last(a, lo, hi):
    return jnp.pad(a, ((0, 0),) * (a.ndim - 1) + ((lo, hi),))


def _prep_weights(p):
    win = jnp.swapaxes(p["w_in"], 1, 2).astype(BF16)
    mla = win[:, W_MLA:, :]
    cq, ckv, kr = mla[:, :Q_LORA], mla[:, Q_LORA:Q_LORA + KV_LORA], mla[:, Q_LORA + KV_LORA:]
    pad_rows = lambda a, lo, hi: jnp.pad(a, ((0, 0), (lo, hi), (0, 0)))
    wmla = jnp.concatenate([pad_rows(cq, 0, BW - Q_LORA), ckv, pad_rows(kr, MLA_NOPE, LANES - MLA_QK)], axis=1)
    poolw = jnp.einsum("lgce,gh->lgche", p["pool_w"], jnp.eye(4, dtype=F32)).reshape(DEPTH, BW, BW).astype(BF16)
    wuq = p["w_uq"].reshape(DEPTH, Q_LORA, HEADS, MLA_QK)
    wuq = jnp.pad(wuq, ((0, 0), (0, BW - Q_LORA), (0, 0), (0, LANES - MLA_QK))).reshape(DEPTH, BW, HEADS * LANES)
    wukv = p["w_ukv"].reshape(DEPTH, KV_LORA, HEADS, MLA_NOPE + MLA_V)
    wuk = _pad_last(wukv[..., :MLA_NOPE], 0, LANES - MLA_NOPE).reshape(DEPTH, KV_LORA, HEADS * LANES)
    wuv = wukv[..., MLA_NOPE:].reshape(DEPTH, KV_LORA, HEADS * MLA_V)
    cv = jnp.concatenate([p["conv_w"], p["conv_b"][:, None, :], jnp.zeros((DEPTH, 4, 2 * FFN), F32)], axis=1)
    cv = cv * jnp.where(jnp.arange(2 * FFN) < FFN, 0.5, 1.0)

    row = lambda a, width: _pad_last(a, 0, width - a.shape[-1])[:, None, :]
    tiled = lambda a, reps: jnp.tile(a, (1, reps))[:, None, :]
    return {
        "g1": row(p["g_norm1"], D), "g2": row(p["g_norm2"], D),
        "win": win, "wmla": wmla,
        "poolw": poolw, "pools": row(p["pool_scale"], BW),
        "dgq": tiled(p["diff_qnorm"], LANES // DA_D), "dgk": tiled(p["diff_knorm"], LANES // DA_D),
        "dl": p["diff_lambda"], "gsub": tiled(p["diff_subln"], HEADS),
        "gqa": row(p["mla_qa_norm"], BW), "wuq": wuq.astype(BF16), "gqn": row(p["mla_qnorm"], LANES),
        "gkv": row(p["mla_kv_norm"], KV_LORA), "wuk": wuk.astype(BF16), "wuv": wuv.astype(BF16),
        "gkn": row(p["mla_knorm"], LANES),
        "wbr": p["w_branch"].astype(BF16), "wout": p["w_out"].astype(BF16),
        "wup": p["w_up"].astype(BF16), "cv": cv, "wdown": p["w_down"].astype(BF16),
    }


def kernel(x_prompt, x_sample, c, c_ctx, cache_diff_k, cache_diff_v, cache_mla_ckv, cache_mla_krope, w_ada, b_ada, g_norm1, g_norm2, w_in, pool_w, pool_scale, diff_qnorm, diff_knorm, diff_lambda, diff_subln, mla_qa_norm, w_uq, mla_kv_norm, w_ukv, mla_qnorm, mla_knorm, w_branch, w_out, w_up, conv_w, conv_b, w_down):
    w = _prep_weights(dict(
        w_in=w_in, pool_w=pool_w, pool_scale=pool_scale, diff_qnorm=diff_qnorm, diff_knorm=diff_knorm,
        diff_lambda=diff_lambda, diff_subln=diff_subln, mla_qa_norm=mla_qa_norm, w_uq=w_uq,
        mla_kv_norm=mla_kv_norm, w_ukv=w_ukv, mla_qnorm=mla_qnorm, mla_knorm=mla_knorm, w_branch=w_branch,
        w_out=w_out, w_up=w_up, conv_w=conv_w, conv_b=conv_b, w_down=w_down, g_norm1=g_norm1,
        g_norm2=g_norm2))
    ropd, ropm = _rope_tables()
    consts = dict(_matrix_tables(), ropd=ropd, ropm=ropm, icnt=_pool_inv_count())

    cond = jnp.concatenate([c_ctx[None, :], c, jnp.zeros((16 - 1 - LAT_B, D), F32)], axis=0)
    mods = _mod_call(cond, w_ada, b_ada).reshape(DEPTH, 16, 1, 6 * D)

    kc, vc = _ctxkv_call(cache_mla_ckv, _pad_last(cache_mla_krope, MLA_NOPE, LANES - MLA_QK), w)
    cdk = cache_diff_k.reshape(LAT_B, DEPTH, PAST, BW).astype(BF16)
    cdv = cache_diff_v.reshape(LAT_B, DEPTH, PAST, BW).astype(BF16)

    xs = [x_prompt.reshape(CTX_ROWS, D), x_sample.reshape(LAT_ROWS, D)]
    new = [[], [], [], []]
    for l in range(DEPTH):
        pa, qd, kd, vd, oc, qm, km, vm, *ctx_outs = _inproj_call(l, xs, mods, w, consts)
        ob, od = _attn_call(l, qd, kd, vd, cdk, cdv, qm, km, vm, kc, vc, w)
        x = _merge_call(l, xs, mods, w, pa, ob, oc, od)
        xs = _ffn_call(l, x, mods, w, split_out=(l == DEPTH - 1))
        for acc, a in zip(new, ctx_outs):
            acc.append(a)

    stack = lambda arrs, shape: jnp.stack(arrs, axis=1).reshape((CTX_B, DEPTH, CTX_T) + shape)
    return (xs[0].reshape(CTX_B, CTX_T, D), xs[1].reshape(LAT_B, LAT_T, D),
            stack([a.reshape(CTX_B, CTX_T, BW) for a in new[0]], (HEADS, DA_HD)),
            stack([a.reshape(CTX_B, CTX_T, BW) for a in new[1]], (HEADS, DA_HD)),
            stack([a.reshape(CTX_B, CTX_T, KV_LORA) for a in new[2]], (KV_LORA,)),
            stack([a.reshape(CTX_B, CTX_T, MLA_ROPE) for a in new[3]], (MLA_ROPE,)))
```

```python
import functools
import math

import jax
import jax.numpy as jnp
import numpy as np
from jax import lax
from jax.experimental import pallas as pl
from jax.experimental.pallas import tpu as pltpu

F32 = jnp.float32
BF16 = jnp.bfloat16

D = 1024
DEPTH = 4
CTX_B, CTX_T = 16, 256
LAT_B, LAT_T = 8, 1024
PAST = 512
TILE = 1024
CTX_TILES = CTX_B * CTX_T // TILE
LAT_TILES = LAT_B * LAT_T // TILE
TILES = CTX_TILES + LAT_TILES
ROWS = TILES * TILE
CTX_ROWS = CTX_TILES * TILE
LAT_ROWS = LAT_TILES * TILE
GRID_W = 64
BW = 256
DA_D = 32
DA_HD = 64
HEADS = 4
MLA_NOPE, MLA_ROPE, MLA_V, MLA_QK = 64, 32, 64, 96
Q_LORA, KV_LORA = 192, 128
FFN = 2816
FFN_CHUNK = 1024
SUBLANES = 8
POOL_HALF = (1, 2, 4, 8)
POOL_PAD = 16
ROPE_BASE = 10000.0
EPS = 1e-6
LOG2E = 1.4426950408889634
TQ = 512
LANES = 128
VMEM_LIMIT = 60 * 1024 * 1024

W_BLOCKS = 4 * D // BW
W_MLA = 4 * D + 5 * BW
C_CQ, C_CKV, C_END = 0, 256, 512


def _lambda_init(l):
    return 0.8 - 0.6 * math.exp(-0.3 * l)


def _sigmoid(x):
    return 0.5 * jnp.tanh(0.5 * x) + 0.5


def _rms(x, n):
    ms = jnp.sum(x * x, axis=-1, keepdims=True) * (1.0 / n)
    return x * lax.rsqrt(ms + EPS)


def _dot(a, b):
    return jnp.dot(a, b, preferred_element_type=F32)


def _dot_nt(a, b):
    return lax.dot_general(a, b, (((1,), (1,)), ((), ())), preferred_element_type=F32)


def _const(shape):
    n = len(shape)
    return pl.BlockSpec(shape, lambda *_: (0,) * n, pipeline_mode=pl.Buffered(1))


def _layer(shape, l):
    n = len(shape)
    return pl.BlockSpec((None,) + shape, lambda *_: (l,) + (0,) * n, pipeline_mode=pl.Buffered(1))


def _params(n_grid):
    return pltpu.CompilerParams(dimension_semantics=("arbitrary",) * n_grid, vmem_limit_bytes=VMEM_LIMIT)


def _row_spec(width):
    return pl.BlockSpec((TILE, width), lambda i: (i, 0))


def _ctx_row_spec(width):
    return pl.BlockSpec((TILE, width), lambda i: (jnp.minimum(i, CTX_TILES - 1), 0))


def _lat_row_spec(width):
    return pl.BlockSpec((TILE, width), lambda i: (jnp.maximum(i - CTX_TILES, 0), 0))


def _x_specs(split):
    return [_ctx_row_spec(D), _lat_row_spec(D)] if split else [_row_spec(D)]


def _read_x(x_refs, is_lat):
    if len(x_refs) == 2:
        return jnp.where(is_lat, x_refs[1][...], x_refs[0][...])
    return x_refs[0][...]


def _in_cols(l, n):
    return pl.BlockSpec((None, BW, D), lambda *_: (l, n, 0), pipeline_mode=pl.Buffered(1))


def _mod_spec(l):
    return pl.BlockSpec((None, 1, 1, 6 * D), lambda i: (l, jnp.maximum(i - (CTX_TILES - 1), 0), 0, 0))


def _group_spec(shape):
    n = len(shape)
    return pl.BlockSpec((None,) + shape, lambda i: (jnp.minimum(i // CTX_TILES, 1),) + (0,) * n,
                        pipeline_mode=pl.Buffered(1))


def _mod_kernel(c_ref, w_ref, b_ref, o_ref):
    c = c_ref[...]
    s = (c * _sigmoid(c)).astype(BF16)
    o_ref[0] = _dot(s, w_ref[0].astype(BF16)) + b_ref[0]


def _mod_call(cond, w_ada, b_ada):
    nc = 4
    cw = 6 * D // nc
    return pl.pallas_call(
        _mod_kernel,
        grid=(DEPTH, nc),
        in_specs=[
            pl.BlockSpec((16, D), lambda l, j: (0, 0)),
            pl.BlockSpec((1, D, cw), lambda l, j: (l, 0, j)),
            pl.BlockSpec((1, 1, cw), lambda l, j: (l, 0, j)),
        ],
        out_specs=pl.BlockSpec((1, 16, cw), lambda l, j: (l, 0, j)),
        out_shape=jax.ShapeDtypeStruct((DEPTH, 16, 6 * D), F32),
        compiler_params=_params(2),
        name="adaln_mod",
    )(cond, w_ada, b_ada.reshape(DEPTH, 1, 6 * D))


def _mla_key_heads(knope, kr_pad, gkn):
    out = []
    for h in range(HEADS):
        kh = knope[:, h * LANES:(h + 1) * LANES] + kr_pad
        out.append(_rms(kh, MLA_QK) * gkn)
    return out


def _ctxkv_kernel(ckv_ref, kr_ref, wuk_ref, wuv_ref, gkn_ref, kc_ref, vc_ref):
    rows = LAT_B * PAST
    cb = ckv_ref[...].reshape(rows, KV_LORA).astype(BF16)
    knope = _dot(cb, wuk_ref[...])
    heads = _mla_key_heads(knope, kr_ref[...].reshape(rows, LANES), gkn_ref[...])
    for h in range(HEADS):
        kc_ref[:, :, h * LANES:(h + 1) * LANES] = heads[h].astype(BF16).reshape(LAT_B, PAST, LANES)
    vc_ref[...] = _dot(cb, wuv_ref[...]).astype(BF16).reshape(LAT_B, PAST, BW)


def _ctxkv_call(ckv, kr_pad, w):
    cache = lambda width: pl.BlockSpec((LAT_B, None, PAST, width), lambda l: (0, l, 0, 0))
    wspec = lambda r, c: pl.BlockSpec((None, r, c), lambda l: (l, 0, 0))
    return pl.pallas_call(
        _ctxkv_kernel,
        grid=(DEPTH,),
        in_specs=[cache(KV_LORA), cache(LANES), wspec(KV_LORA, 512), wspec(KV_LORA, BW), wspec(1, LANES)],
        out_specs=[cache(512), cache(BW)],
        out_shape=[
            jax.ShapeDtypeStruct((LAT_B, DEPTH, PAST, 512), BF16),
            jax.ShapeDtypeStruct((LAT_B, DEPTH, PAST, BW), BF16),
        ],
        compiler_params=_params(1),
        name="mla_ctx_keys",
    )(ckv, kr_pad, w["wuk"], w["wuv"], w["gkn"])


def _norm_mod(x, g, sc, sh):
    return _rms(x, D) * (g * (1.0 + sc)) + sh


def _rope(x, tab_ref, rot):
    hi = x.astype(BF16)
    pieces = jnp.concatenate([hi, (x - hi.astype(F32)).astype(BF16)], axis=1)
    return x * tab_ref[0] + _dot(pieces, rot) * tab_ref[1]


def _group_inv_rms(x, ones, n):
    sq = (x * x).astype(BF16)
    ms = jnp.concatenate([_dot(sq[:, c:c + BW], ones) for c in range(0, x.shape[1], BW)], axis=1)
    return lax.rsqrt(ms * (1.0 / n) + EPS)


def _inproj_kernel(n_x, *refs):
    x_refs = refs[:n_x]
    (mod_ref, g1_ref, wxa_ref, wq_ref, wk_ref, wv_ref, wxc_ref, wmla_ref,
     icnt_ref, poolw_ref, pools_ref, seg_ref, ones_ref, rotd_ref, rotm_ref, dgq_ref, dgk_ref,
     ropd_ref, ropm_ref, cc_ref, ss_ref, ft_ref, gqa_ref, wuq_ref, gqn_ref, gkv_ref,
     wuk_ref, wuv_ref, gkn_ref,
     pa_ref, qd_ref, kd_ref, vd_ref, oc_ref, qm_ref, km_ref, vm_ref,
     kcf_ref, vf_ref, ckv_ref, kr_ref, pad_ref) = refs[n_x:]
    is_lat = pl.program_id(0) >= CTX_TILES
    mod = mod_ref[0]
    hb = _norm_mod(_read_x(x_refs, is_lat), g1_ref[...], mod[:, D:2 * D], mod[:, 0:D]).astype(BF16)

    xa = _dot_nt(hb, wxa_ref[...])
    q = _dot_nt(hb, wq_ref[...])
    k = _dot_nt(hb, wk_ref[...])
    v = _dot_nt(hb, wv_ref[...])
    xcb = _dot_nt(hb, wxc_ref[...]).astype(BF16)
    cq = _dot_nt(hb, wmla_ref[C_CQ:C_CQ + BW, :])
    ckv_kr = _dot_nt(hb, wmla_ref[C_CKV:C_END, :])

    pad_ref[0:POOL_PAD, :] = jnp.zeros((POOL_PAD, BW), F32)
    pad_ref[POOL_PAD + TILE:, :] = jnp.zeros((POOL_PAD, BW), F32)
    pad_ref[POOL_PAD:POOL_PAD + TILE, :] = xa
    lane = lax.broadcasted_iota(jnp.int32, (TILE, BW), 1)
    row8 = lax.broadcasted_iota(jnp.int32, (SUBLANES, BW), 0)

    def term(s):
        v = pad_ref[POOL_PAD + s:POOL_PAD + s + TILE, :]
        kill = row8 < jnp.where(is_lat, 0, -s) if s < 0 else row8 >= jnp.where(is_lat, SUBLANES, SUBLANES - s)
        parts = []
        for r0 in range(0, TILE, CTX_T):
            r1 = r0 + CTX_T
            if s < 0:
                head = v[r0:r0 + SUBLANES]
                parts += [head if r0 == 0 else jnp.where(kill, 0.0, head), v[r0 + SUBLANES:r1]]
            else:
                tail = v[r1 - SUBLANES:r1]
                parts += [v[r0:r1 - SUBLANES], tail if r1 == TILE else jnp.where(kill, 0.0, tail)]
        return jnp.concatenate(parts, axis=0)

    acc = xa
    wins = []
    lo_s, hi_s = 0, 0
    for half in POOL_HALF:
        for s in list(range(-half, lo_s)) + list(range(hi_s + 1, half)):
            acc = acc + term(s)
        lo_s, hi_s = -half, half - 1
        wins.append(acc)
    group = BW // len(POOL_HALF)
    win = jnp.where(lane < group, wins[0],
                    jnp.where(lane < 2 * group, wins[1], jnp.where(lane < 3 * group, wins[2], wins[3])))
    pooled = (win * icnt_ref[...] - xa).astype(BF16)
    pa_ref[...] = (_dot(pooled, poolw_ref[...]) * pools_ref[...]).astype(BF16)

    qscale = DA_D ** -0.5 * LOG2E
    rq = _group_inv_rms(q, seg_ref[...], DA_D) * qscale
    rk = _group_inv_rms(k, seg_ref[...], DA_D)
    kns = []
    for s in range(BW // LANES):
        sl = slice(s * LANES, (s + 1) * LANES)
        kg = k[:, sl] * dgk_ref[...]
        kns.append(kg * rk[:, sl])
        qd_ref[:, sl] = (_rope(q[:, sl] * dgq_ref[...], ropd_ref, rotd_ref[...]) * rq[:, sl]).astype(BF16)
        kd_ref[:, sl] = (_rope(kg, ropd_ref, rotd_ref[...]) * rk[:, sl]).astype(BF16)
    vd_ref[...] = v.astype(BF16)

    xcc = _dot(xcb, cc_ref[...]).astype(BF16)
    xcs = _dot(xcb, ss_ref[...]).astype(BF16)
    y = _dot(ft_ref[0], xcc) + _dot(ft_ref[1], xcs)
    fnorm = jnp.where(is_lat, (LAT_T * 64.0) ** -0.5, (CTX_T * 64.0) ** -0.5)
    oc_ref[...] = (y * fnorm).astype(BF16)

    qa = (_rms(cq, Q_LORA) * gqa_ref[...]).astype(BF16)
    qf = _dot(qa, wuq_ref[...])
    mscale = MLA_QK ** -0.5 * LOG2E
    rqm = _group_inv_rms(qf, ones_ref[...], MLA_QK) * mscale
    for h in range(HEADS):
        sl = slice(h * LANES, (h + 1) * LANES)
        qm_ref[:, sl] = (_rope(qf[:, sl] * gqn_ref[...], ropm_ref, rotm_ref[...]) * rqm[:, sl]).astype(BF16)
    ckv, kr_pad = ckv_kr[:, :KV_LORA], ckv_kr[:, KV_LORA:]
    ckv_n = _rms(ckv, KV_LORA) * gkv_ref[...]
    cb = ckv_n.astype(BF16)
    knope = _dot(cb, wuk_ref[...])
    kr_rot = _rope(kr_pad * gkn_ref[...], ropm_ref, rotm_ref[...])
    kr_ssq = _dot((kr_pad * kr_pad).astype(BF16), ones_ref[0:LANES, 0:LANES])
    kn_ssq = jnp.concatenate([_dot((knope[:, c:c + BW] * knope[:, c:c + BW]).astype(BF16), ones_ref[...])
                              for c in range(0, HEADS * LANES, BW)], axis=1)
    for h in range(HEADS):
        sl = slice(h * LANES, (h + 1) * LANES)
        rkm = lax.rsqrt((kn_ssq[:, sl] + kr_ssq) * (1.0 / MLA_QK) + EPS)
        km_ref[:, sl] = ((knope[:, sl] * gkn_ref[...] + kr_rot) * rkm).astype(BF16)
    vm_ref[...] = _dot(cb, wuv_ref[...]).astype(BF16)

    @pl.when(jnp.logical_not(is_lat))
    def _():
        for s, kn in enumerate(kns):
            kcf_ref[:, s * LANES:(s + 1) * LANES] = kn
        vf_ref[...] = v
        ckv_ref[...] = ckv_n
        kr_ref[...] = kr_pad[:, MLA_NOPE:MLA_NOPE + MLA_ROPE]


def _inproj_call(l, xs, mods, w, consts):
    row_outs = [(BW, BF16), (BW, BF16), (BW, BF16), (BW, BF16), (BW, BF16), (512, BF16), (512, BF16), (BW, BF16)]
    ctx_outs = [(BW, F32), (BW, F32), (KV_LORA, F32), (MLA_ROPE, F32)]
    return pl.pallas_call(
        functools.partial(_inproj_kernel, len(xs)),
        grid=(TILES,),
        in_specs=_x_specs(len(xs) == 2) + [
            _mod_spec(l), _layer((1, D), l), *[_in_cols(l, W_BLOCKS + n) for n in range(5)],
            _layer((C_END, D), l), _group_spec((TILE, BW)), _layer((BW, BW), l), _layer((1, BW), l),
            _const((BW, BW)), _const((BW, BW)), _const((BW, LANES)), _const((BW, LANES)),
            _layer((1, LANES), l), _layer((1, LANES), l),
            _group_spec((2, TILE, LANES)), _group_spec((2, TILE, LANES)),
            _const((BW, BW)), _const((BW, BW)), _group_spec((2, TILE, TILE)),
            _layer((1, BW), l), _layer((BW, 512), l), _layer((1, LANES), l), _layer((1, KV_LORA), l),
            _layer((KV_LORA, 512), l), _layer((KV_LORA, BW), l), _layer((1, LANES), l),
        ],
        out_specs=[_row_spec(wd) for wd, _ in row_outs] + [_ctx_row_spec(wd) for wd, _ in ctx_outs],
        out_shape=([jax.ShapeDtypeStruct((ROWS, wd), dt) for wd, dt in row_outs]
                   + [jax.ShapeDtypeStruct((CTX_ROWS, wd), dt) for wd, dt in ctx_outs]),
        scratch_shapes=[pltpu.VMEM((TILE + 2 * POOL_PAD, BW), F32)],
        compiler_params=_params(1),
        name="inproj",
    )(*xs, mods, w["g1"], *[w["win"]] * 5, w["wmla"], consts["icnt"], w["poolw"], w["pools"], consts["seg"],
      consts["ones"], consts["rotd"], consts["rotm"], w["dgq"], w["dgk"],
      consts["ropd"], consts["ropm"], consts["cc"], consts["ss"], consts["ft"],
      w["gqa"], w["wuq"], w["gqn"], w["gkv"], w["wuk"], w["wuv"], w["gkn"])


def _softmax_pv(ss, vs):
    m = ss[0].max(axis=-1, keepdims=True)
    for s in ss[1:]:
        m = jnp.maximum(m, s.max(axis=-1, keepdims=True))
    ps = [jnp.exp2(s - m) for s in ss]
    l = ps[0].sum(axis=-1, keepdims=True)
    for p in ps[1:]:
        l = l + p.sum(axis=-1, keepdims=True)
    return sum(_dot(p.astype(BF16), v) for p, v in zip(ps, vs)) * (1.0 / l)


def _attend(qd, kds, vds, qm, kms, vms, lam, gsub, lam_init, ob_ref, od_ref, out_rows):
    n = qd.shape[0]
    lane = lax.broadcasted_iota(jnp.int32, (n, BW), 1)
    zero = jnp.zeros((), BF16)

    jobs = []
    for h in range(HEADS):
        for half in range(2):
            lo = h * DA_HD + half * DA_D
            jobs.append((jnp.where((lane >= lo) & (lane < lo + DA_D), qd, zero), kds, vds))
        sl = slice(h * LANES, (h + 1) * LANES)
        jobs.append((qm[:, sl], [k[:, sl] for k in kms], vms))

    score = lambda job: [_dot_nt(job[0], k) for k in job[1]]
    out_b = jnp.zeros((n, BW), F32)
    out_d = jnp.zeros((n, BW), F32)
    pvs, nxt = [], score(jobs[0])
    for c, job in enumerate(jobs):
        ss, nxt = nxt, score(jobs[c + 1]) if c + 1 < len(jobs) else None
        pvs.append(_softmax_pv(ss, job[2]))
        if len(pvs) == 3:
            h = c // 3
            head = (lane >= h * DA_HD) & (lane < (h + 1) * DA_HD)
            o = jnp.where(head, pvs[0] - lam * pvs[1], 0.0)
            ms = jnp.sum(o * o, axis=-1, keepdims=True) * (1.0 / DA_HD)
            out_b = out_b + o * lax.rsqrt(ms + EPS)
            out_d = out_d + jnp.where(head, pvs[2], 0.0)
            pvs = []
    ob_ref[out_rows, :] = (out_b * gsub * (1.0 - lam_init)).astype(BF16)
    od_ref[out_rows, :] = out_d.astype(BF16)


def _attn_kernel(lam_init, qd_ref, kd_ref, vd_ref, cdk_ref, cdv_ref, qm_ref, km_ref, vm_ref, kc_ref, vc_ref,
                 dl_ref, gsub_ref, ob_ref, od_ref):
    i = pl.program_id(0)
    j = pl.program_id(1)
    dl = dl_ref[...]
    lam = (jnp.exp(jnp.sum(dl[0:1] * dl[1:2], axis=-1, keepdims=True))
           - jnp.exp(jnp.sum(dl[2:3] * dl[3:4], axis=-1, keepdims=True)) + lam_init)
    @pl.when(i < CTX_TILES)
    def _():
        for sub in range(TQ // CTX_T):
            rows = pl.ds(pl.multiple_of(j * TQ + sub * CTX_T, CTX_T), CTX_T)
            _attend(qd_ref[rows, :], [kd_ref[rows, :]], [vd_ref[rows, :]],
                    qm_ref[rows, :], [km_ref[rows, :]], [vm_ref[rows, :]],
                    lam, gsub_ref[...], lam_init, ob_ref, od_ref, slice(sub * CTX_T, (sub + 1) * CTX_T))

    @pl.when(i >= CTX_TILES)
    def _():
        rows = pl.ds(pl.multiple_of(j * TQ, TQ), TQ)
        _attend(qd_ref[rows, :], [cdk_ref[...], kd_ref[...]], [cdv_ref[...], vd_ref[...]],
                qm_ref[rows, :], [kc_ref[...], km_ref[...]], [vc_ref[...], vm_ref[...]],
                lam, gsub_ref[...], lam_init, ob_ref, od_ref, slice(None))


def _attn_call(l, qd, kd, vd, cdk, cdv, qm, km, vm, kc, vc, w):
    tile = lambda width: pl.BlockSpec((TILE, width), lambda i, j: (i, 0))
    cache = lambda width: pl.BlockSpec((None, None, PAST, width),
                                       lambda i, j: (jnp.maximum(i - CTX_TILES, 0), l, 0, 0))
    blk = pl.BlockSpec((TQ, BW), lambda i, j: (i * (TILE // TQ) + j, 0))
    return pl.pallas_call(
        functools.partial(_attn_kernel, _lambda_init(l)),
        grid=(TILES, TILE // TQ),
        in_specs=[tile(BW), tile(BW), tile(BW), cache(BW), cache(BW), tile(512), tile(512), tile(BW),
                  cache(512), cache(BW), _layer((4, DA_D), l), _layer((1, BW), l)],
        out_specs=[blk, blk],
        out_shape=[jax.ShapeDtypeStruct((ROWS, BW), BF16)] * 2,
        compiler_params=_params(2),
        name="attention",
    )(qd, kd, vd, cdk, cdv, qm, km, vm, kc, vc, w["dl"], w["gsub"])


def _merge_kernel(n_x, *refs):
    x_refs = refs[:n_x]
    mod_ref, g1_ref, wg_ref, wb_ref, wo_ref, pa_ref, pb_ref, pc_ref, pd_ref, o_ref = refs[n_x:]
    mod = mod_ref[0]
    x = _read_x(x_refs, pl.program_id(0) >= CTX_TILES)
    hb = _norm_mod(x, g1_ref[...], mod[:, D:2 * D], mod[:, 0:D]).astype(BF16)
    merged = None
    for n, br_ref in enumerate((pa_ref, pb_ref, pc_ref, pd_ref)):
        gate = _sigmoid(_dot_nt(hb, wg_ref[n * D:(n + 1) * D, :]))
        t = gate * _dot(br_ref[...], wb_ref[n])
        merged = t if merged is None else merged + t
    o_ref[...] = x + mod[:, 2 * D:3 * D] * _dot(merged.astype(BF16), wo_ref[...])


def _merge_call(l, xs, mods, w, pa, pb, pc, pd):
    return pl.pallas_call(
        functools.partial(_merge_kernel, len(xs)),
        grid=(TILES,),
        in_specs=_x_specs(len(xs) == 2) + [
            _mod_spec(l), _layer((1, D), l), _layer((4 * D, D), l), _layer((4, BW, D), l), _layer((D, D), l),
            _row_spec(BW), _row_spec(BW), _row_spec(BW), _row_spec(BW)],
        out_specs=_row_spec(D),
        out_shape=jax.ShapeDtypeStruct((ROWS, D), F32),
        compiler_params=_params(1),
        name="merge",
    )(*xs, mods, w["g1"], w["win"], w["wbr"], w["wout"], pa, pb, pc, pd)


def _ffn_kernel(n_o, x_ref, mod_ref, g2_ref, wu_ref, cv_ref, wd_ref, *refs):
    o_refs = refs[:n_o]
    hb_ref, acc_ref = refs[n_o:]
    is_lat = pl.program_id(0) >= CTX_TILES
    is_ctx = jnp.logical_not(is_lat)
    mod = mod_ref[0]
    hb_ref[...] = _norm_mod(x_ref[...], g2_ref[...], mod[:, 4 * D:5 * D], mod[:, 3 * D:4 * D]).astype(BF16)
    acc_ref[...] = jnp.zeros((TILE, D), F32)

    def edge_masks(width):
        row8 = lax.broadcasted_iota(jnp.int32, (SUBLANES, width), 0)
        return (row8 == 0, row8 == SUBLANES - 1,
                row8 == jnp.where(is_ctx, 0, -1), row8 == jnp.where(is_ctx, SUBLANES - 1, -1))

    def conv(cols, masks):
        tile_start, tile_end, seq_start, seq_end = masks
        u = _dot(hb_ref[...], wu_ref[:, cols])
        cv = cv_ref[:, cols]
        prev, nxt = pltpu.roll(u, 1, 0), pltpu.roll(u, TILE - 1, 0)
        prev_parts, nxt_parts = [], []
        for r0 in range(0, TILE, CTX_T):
            r1 = r0 + CTX_T
            start = tile_start if r0 == 0 else seq_start
            end = tile_end if r1 == TILE else seq_end
            prev_parts += [jnp.where(start, 0.0, prev[r0:r0 + SUBLANES]), prev[r0 + SUBLANES:r1]]
            nxt_parts += [nxt[r0:r1 - SUBLANES], jnp.where(end, 0.0, nxt[r1 - SUBLANES:r1])]
        prev, nxt = jnp.concatenate(prev_parts, axis=0), jnp.concatenate(nxt_parts, axis=0)
        return prev * cv[0:1] + u * cv[1:2] + nxt * cv[2:3] + cv[3:4]

    def chunk(c0, width, masks):
        h = conv(pl.ds(c0, width), masks)
        v = conv(pl.ds(FFN + c0, width), masks)
        act = (h * (1.0 + jnp.tanh(h)) * v).astype(BF16)
        acc_ref[...] += _dot(act, wd_ref[pl.ds(c0, width), :])

    masks = edge_masks(FFN_CHUNK)

    def body(c, carry):
        chunk(pl.multiple_of(c * FFN_CHUNK, FFN_CHUNK), FFN_CHUNK, masks)
        return carry

    lax.fori_loop(0, FFN // FFN_CHUNK, body, 0)
    if FFN % FFN_CHUNK:
        chunk(FFN - FFN % FFN_CHUNK, FFN % FFN_CHUNK, edge_masks(FFN % FFN_CHUNK))
    y = x_ref[...] + mod[:, 5 * D:6 * D] * acc_ref[...]
    if n_o == 1:
        o_refs[0][...] = y
    else:
        @pl.when(is_ctx)
        def _():
            o_refs[0][...] = y

        @pl.when(is_lat)
        def _():
            o_refs[1][...] = y


def _ffn_call(l, x, mods, w, split_out):
    if split_out:
        out_specs = _x_specs(True)
        out_shape = [jax.ShapeDtypeStruct((CTX_ROWS, D), F32), jax.ShapeDtypeStruct((LAT_ROWS, D), F32)]
    else:
        out_specs = _x_specs(False)
        out_shape = [jax.ShapeDtypeStruct((ROWS, D), F32)]
    return pl.pallas_call(
        functools.partial(_ffn_kernel, len(out_shape)),
        grid=(TILES,),
        in_specs=[_row_spec(D), _mod_spec(l), _layer((1, D), l),
                  _layer((D, 2 * FFN), l), _layer((8, 2 * FFN), l), _layer((FFN, D), l)],
        out_specs=out_specs,
        out_shape=out_shape,
        scratch_shapes=[pltpu.VMEM((TILE, D), BF16), pltpu.VMEM((TILE, D), F32)],
        compiler_params=_params(1),
        name="ffn",
    )(x, mods, w["g2"], w["wup"], w["cv"], w["wdown"])


def _rope_tables():
    half = 8
    inv = ROPE_BASE ** (-np.arange(half, dtype=np.float64) / half)
    t = np.arange(LAT_T)
    rows, cols = (t // GRID_W).astype(np.float64), (t % GRID_W).astype(np.float64)
    ang = np.concatenate([rows[:, None] * inv[None, :]] * 2 + [cols[:, None] * inv[None, :]] * 2, axis=1)
    cos, sin = np.cos(ang), np.sin(ang)

    def diff_layout(a):
        return np.tile(a, (1, LANES // 32))

    def mla_layout(a, fill):
        out = np.full((LAT_T, LANES), fill)
        out[:, MLA_NOPE:MLA_QK] = a
        return out

    ident = lambda layout: np.stack([layout(np.ones_like(cos)), layout(np.zeros_like(cos))])
    ropd = np.stack([ident(diff_layout), np.stack([diff_layout(cos), diff_layout(sin)])])
    ropm = np.stack([np.stack([mla_layout(np.ones_like(cos), 1.0), mla_layout(np.zeros_like(cos), 0.0)]),
                     np.stack([mla_layout(cos, 1.0), mla_layout(sin, 0.0)])])
    return jnp.asarray(ropd, F32), jnp.asarray(ropm, F32)


def _pool_inv_count():
    out = []
    for seq in (CTX_T, LAT_T):
        t = np.arange(TILE) % seq
        cols = [np.repeat((1.0 / (np.minimum(t + half, seq) - np.maximum(t - half, 0)))[:, None],
                          BW // len(POOL_HALF), axis=1)
                for half in POOL_HALF]
        out.append(np.concatenate(cols, axis=1))
    return jnp.asarray(np.stack(out), F32)


def _rotate_half_matrix(lanes):
    r = np.zeros((LANES, LANES))
    for j in lanes:
        if j % 16 < 8:
            r[j + 8, j] = -1.0
        else:
            r[j - 8, j] = 1.0
    return np.concatenate([r, r], axis=0)


def _matrix_tables():
    def cs(n):
        k = np.arange(n)
        ang = 2.0 * np.pi * ((k[:, None] * k[None, :]) % n) / n
        return np.cos(ang), np.sin(ang)

    c64, s64 = cs(64)
    eye4 = np.eye(4)
    cl, sl = cs(LAT_T)
    cc_, sc_ = cs(CTX_T)
    ft = np.stack([np.stack([np.kron(eye4, cc_), -np.kron(eye4, sc_)]), np.stack([cl, -sl])])
    as_bf16 = lambda a: jnp.asarray(a, F32).astype(BF16)
    return dict(
        cc=as_bf16(np.kron(eye4, c64)), ss=as_bf16(np.kron(eye4, s64)), ft=as_bf16(ft),
        seg=as_bf16(np.kron(np.eye(BW // DA_D), np.ones((DA_D, DA_D)))),
        ones=as_bf16(np.kron(np.eye(BW // LANES), np.ones((LANES, LANES)))),
        rotd=as_bf16(_rotate_half_matrix(range(LANES))),
        rotm=as_bf16(_rotate_half_matrix(range(MLA_NOPE, MLA_QK))))


def _pad_last(a, lo, hi):
    return jnp.pad(a, ((0, 0),) * (a.ndim - 1) + ((lo, hi),))


def _prep_weights(p):
    win = jnp.swapaxes(p["w_in"], 1, 2).astype(BF16)
    mla = win[:, W_MLA:, :]
    cq, ckv, kr = mla[:, :Q_LORA], mla[:, Q_LORA:Q_LORA + KV_LORA], mla[:, Q_LORA + KV_LORA:]
    pad_rows = lambda a, lo, hi: jnp.pad(a, ((0, 0), (lo, hi), (0, 0)))
    wmla = jnp.concatenate([pad_rows(cq, 0, BW - Q_LORA), ckv, pad_rows(kr, MLA_NOPE, LANES - MLA_QK)], axis=1)
    poolw = jnp.einsum("lgce,gh->lgche", p["pool_w"], jnp.eye(4, dtype=F32)).reshape(DEPTH, BW, BW).astype(BF16)
    wuq = p["w_uq"].reshape(DEPTH, Q_LORA, HEADS, MLA_QK)
    wuq = jnp.pad(wuq, ((0, 0), (0, BW - Q_LORA), (0, 0), (0, LANES - MLA_QK))).reshape(DEPTH, BW, HEADS * LANES)
    wukv = p["w_ukv"].reshape(DEPTH, KV_LORA, HEADS, MLA_NOPE + MLA_V)
    wuk = _pad_last(wukv[..., :MLA_NOPE], 0, LANES - MLA_NOPE).reshape(DEPTH, KV_LORA, HEADS * LANES)
    wuv = wukv[..., MLA_NOPE:].reshape(DEPTH, KV_LORA, HEADS * MLA_V)
    cv = jnp.concatenate([p["conv_w"], p["conv_b"][:, None, :], jnp.zeros((DEPTH, 4, 2 * FFN), F32)], axis=1)
    cv = cv * jnp.where(jnp.arange(2 * FFN) < FFN, 0.5, 1.0)

    row = lambda a, width: _pad_last(a, 0, width - a.shape[-1])[:, None, :]
    tiled = lambda a, reps: jnp.tile(a, (1, reps))[:, None, :]
    return {
        "g1": row(p["g_norm1"], D), "g2": row(p["g_norm2"], D),
        "win": win, "wmla": wmla,
        "poolw": poolw, "pools": row(p["pool_scale"], BW),
        "dgq": tiled(p["diff_qnorm"], LANES // DA_D), "dgk": tiled(p["diff_knorm"], LANES // DA_D),
        "dl": p["diff_lambda"], "gsub": tiled(p["diff_subln"], HEADS),
        "gqa": row(p["mla_qa_norm"], BW), "wuq": wuq.astype(BF16), "gqn": row(p["mla_qnorm"], LANES),
        "gkv": row(p["mla_kv_norm"], KV_LORA), "wuk": wuk.astype(BF16), "wuv": wuv.astype(BF16),
        "gkn": row(p["mla_knorm"], LANES),
        "wbr": p["w_branch"].astype(BF16), "wout": p["w_out"].astype(BF16),
        "wup": p["w_up"].astype(BF16), "cv": cv, "wdown": p["w_down"].astype(BF16),
    }


def kernel(x_prompt, x_sample, c, c_ctx, cache_diff_k, cache_diff_v, cache_mla_ckv, cache_mla_krope, w_ada, b_ada, g_norm1, g_norm2, w_in, pool_w, pool_scale, diff_qnorm, diff_knorm, diff_lambda, diff_subln, mla_qa_norm, w_uq, mla_kv_norm, w_ukv, mla_qnorm, mla_knorm, w_branch, w_out, w_up, conv_w, conv_b, w_down):
    w = _prep_weights(dict(
        w_in=w_in, pool_w=pool_w, pool_scale=pool_scale, diff_qnorm=diff_qnorm, diff_knorm=diff_knorm,
        diff_lambda=diff_lambda, diff_subln=diff_subln, mla_qa_norm=mla_qa_norm, w_uq=w_uq,
        mla_kv_norm=mla_kv_norm, w_ukv=w_ukv, mla_qnorm=mla_qnorm, mla_knorm=mla_knorm, w_branch=w_branch,
        w_out=w_out, w_up=w_up, conv_w=conv_w, conv_b=conv_b, w_down=w_down, g_norm1=g_norm1,
        g_norm2=g_norm2))
    ropd, ropm = _rope_tables()
    consts = dict(_matrix_tables(), ropd=ropd, ropm=ropm, icnt=_pool_inv_count())

    cond = jnp.concatenate([c_ctx[None, :], c, jnp.zeros((16 - 1 - LAT_B, D), F32)], axis=0)
    mods = _mod_call(cond, w_ada, b_ada).reshape(DEPTH, 16, 1, 6 * D)

    kc, vc = _ctxkv_call(cache_mla_ckv, _pad_last(cache_mla_krope, MLA_NOPE, LANES - MLA_QK), w)
    cdk = cache_diff_k.reshape(LAT_B, DEPTH, PAST, BW).astype(BF16)
    cdv = cache_diff_v.reshape(LAT_B, DEPTH, PAST, BW).astype(BF16)

    xs = [x_prompt.reshape(CTX_ROWS, D), x_sample.reshape(LAT_ROWS, D)]
    new = [[], [], [], []]
    for l in range(DEPTH):
        pa, qd, kd, vd, oc, qm, km, vm, *ctx_outs = _inproj_call(l, xs, mods, w, consts)
        ob, od = _attn_call(l, qd, kd, vd, cdk, cdv, qm, km, vm, kc, vc, w)
        x = _merge_call(l, xs, mods, w, pa, ob, oc, od)
        xs = _ffn_call(l, x, mods, w, split_out=(l == DEPTH - 1))
        for acc, a in zip(new, ctx_outs):
            acc.append(a)

    stack = lambda arrs, shape: jnp.stack(arrs, axis=1).reshape((CTX_B, DEPTH, CTX_T) + shape)
    return (xs[0].reshape(CTX_B, CTX_T, D), xs[1].reshape(LAT_B, LAT_T, D),
            stack([a.reshape(CTX_B, CTX_T, BW) for a in new[0]], (HEADS, DA_HD)),
            stack([a.reshape(CTX_B, CTX_T, BW) for a in new[1]], (HEADS, DA_HD)),
            stack([a.reshape(CTX_B, CTX_T, KV_LORA) for a in new[2]], (KV_LORA,)),
            stack([a.reshape(CTX_B, CTX_T, MLA_ROPE) for a in new[3]], (MLA_ROPE,)))
```

```python
import functools
import math

import jax
import jax.numpy as jnp
import numpy as np
from jax import lax
from jax.experimental import pallas as pl
from jax.experimental.pallas import tpu as pltpu

F32 = jnp.float32
BF16 = jnp.bfloat16

D = 1024
DEPTH = 4
CTX_B, CTX_T = 16, 256
LAT_B, LAT_T = 8, 1024
PAST = 512
TILE = 1024
CTX_TILES = CTX_B * CTX_T // TILE
LAT_TILES = LAT_B * LAT_T // TILE
TILES = CTX_TILES + LAT_TILES
ROWS = TILES * TILE
CTX_ROWS = CTX_TILES * TILE
LAT_ROWS = LAT_TILES * TILE
GRID_W = 64
BW = 256
DA_D = 32
DA_HD = 64
HEADS = 4
MLA_NOPE, MLA_ROPE, MLA_V, MLA_QK = 64, 32, 64, 96
Q_LORA, KV_LORA = 192, 128
FFN = 2816
FFN_CHUNK = 1024
SUBLANES = 8
POOL_HALF = (1, 2, 4, 8)
POOL_PAD = 16
ROPE_BASE = 10000.0
EPS = 1e-6
LOG2E = 1.4426950408889634
TQ = 512
LANES = 128
VMEM_LIMIT = 60 * 1024 * 1024

W_BLOCKS = 4 * D // BW
W_MLA = 4 * D + 5 * BW
C_CQ, C_CKV, C_END = 0, 256, 512


def _lambda_init(l):
    return 0.8 - 0.6 * math.exp(-0.3 * l)


def _sigmoid(x):
    return 0.5 * jnp.tanh(0.5 * x) + 0.5


def _rms(x, n):
    ms = jnp.sum(x * x, axis=-1, keepdims=True) * (1.0 / n)
    return x * lax.rsqrt(ms + EPS)


def _dot(a, b):
    return jnp.dot(a, b, preferred_element_type=F32)


def _dot_nt(a, b):
    return lax.dot_general(a, b, (((1,), (1,)), ((), ())), preferred_element_type=F32)


def _const(shape):
    n = len(shape)
    return pl.BlockSpec(shape, lambda *_: (0,) * n, pipeline_mode=pl.Buffered(1))


def _layer(shape, l):
    n = len(shape)
    return pl.BlockSpec((None,) + shape, lambda *_: (l,) + (0,) * n, pipeline_mode=pl.Buffered(1))


def _params(n_grid):
    return pltpu.CompilerParams(dimension_semantics=("arbitrary",) * n_grid, vmem_limit_bytes=VMEM_LIMIT)


def _row_spec(width):
    return pl.BlockSpec((TILE, width), lambda i: (i, 0))


def _ctx_row_spec(width):
    return pl.BlockSpec((TILE, width), lambda i: (jnp.minimum(i, CTX_TILES - 1), 0))


def _lat_row_spec(width):
    return pl.BlockSpec((TILE, width), lambda i: (jnp.maximum(i - CTX_TILES, 0), 0))


def _x_specs(split):
    return [_ctx_row_spec(D), _lat_row_spec(D)] if split else [_row_spec(D)]


def _read_x(x_refs, is_lat):
    if len(x_refs) == 2:
        return jnp.where(is_lat, x_refs[1][...], x_refs[0][...])
    return x_refs[0][...]


def _in_cols(l, n):
    return pl.BlockSpec((None, BW, D), lambda *_: (l, n, 0), pipeline_mode=pl.Buffered(1))


def _mod_spec(l):
    return pl.BlockSpec((None, 1, 1, 6 * D), lambda i: (l, jnp.maximum(i - (CTX_TILES - 1), 0), 0, 0))


def _group_spec(shape):
    n = len(shape)
    return pl.BlockSpec((None,) + shape, lambda i: (jnp.minimum(i // CTX_TILES, 1),) + (0,) * n,
                        pipeline_mode=pl.Buffered(1))


def _mod_kernel(c_ref, w_ref, b_ref, o_ref):
    c = c_ref[...]
    s = (c * _sigmoid(c)).astype(BF16)
    o_ref[0] = _dot(s, w_ref[0].astype(BF16)) + b_ref[0]


def _mod_call(cond, w_ada, b_ada):
    nc = 4
    cw = 6 * D // nc
    return pl.pallas_call(
        _mod_kernel,
        grid=(DEPTH, nc),
        in_specs=[
            pl.BlockSpec((16, D), lambda l, j: (0, 0)),
            pl.BlockSpec((1, D, cw), lambda l, j: (l, 0, j)),
            pl.BlockSpec((1, 1, cw), lambda l, j: (l, 0, j)),
        ],
        out_specs=pl.BlockSpec((1, 16, cw), lambda l, j: (l, 0, j)),
        out_shape=jax.ShapeDtypeStruct((DEPTH, 16, 6 * D), F32),
        compiler_params=_params(2),
        name="adaln_mod",
    )(cond, w_ada, b_ada.reshape(DEPTH, 1, 6 * D))


def _mla_key_heads(knope, kr_pad, gkn):
    out = []
    for h in range(HEADS):
        kh = knope[:, h * LANES:(h + 1) * LANES] + kr_pad
        out.append(_rms(kh, MLA_QK) * gkn)
    return out


def _ctxkv_kernel(ckv_ref, kr_ref, wuk_ref, wuv_ref, gkn_ref, kc_ref, vc_ref):
    rows = LAT_B * PAST
    cb = ckv_ref[...].reshape(rows, KV_LORA).astype(BF16)
    knope = _dot(cb, wuk_ref[...])
    heads = _mla_key_heads(knope, kr_ref[...].reshape(rows, LANES), gkn_ref[...])
    for h in range(HEADS):
        kc_ref[:, :, h * LANES:(h + 1) * LANES] = heads[h].astype(BF16).reshape(LAT_B, PAST, LANES)
    vc_ref[...] = _dot(cb, wuv_ref[...]).astype(BF16).reshape(LAT_B, PAST, BW)


def _ctxkv_call(ckv, kr_pad, w):
    cache = lambda width: pl.BlockSpec((LAT_B, None, PAST, width), lambda l: (0, l, 0, 0))
    wspec = lambda r, c: pl.BlockSpec((None, r, c), lambda l: (l, 0, 0))
    return pl.pallas_call(
        _ctxkv_kernel,
        grid=(DEPTH,),
        in_specs=[cache(KV_LORA), cache(LANES), wspec(KV_LORA, 512), wspec(KV_LORA, BW), wspec(1, LANES)],
        out_specs=[cache(512), cache(BW)],
        out_shape=[
            jax.ShapeDtypeStruct((LAT_B, DEPTH, PAST, 512), BF16),
            jax.ShapeDtypeStruct((LAT_B, DEPTH, PAST, BW), BF16),
        ],
        compiler_params=_params(1),
        name="mla_ctx_keys",
    )(ckv, kr_pad, w["wuk"], w["wuv"], w["gkn"])


def _norm_mod(x, g, sc, sh):
    return _rms(x, D) * (g * (1.0 + sc)) + sh


def _rope(x, tab_ref, rot):
    hi = x.astype(BF16)
    pieces = jnp.concatenate([hi, (x - hi.astype(F32)).astype(BF16)], axis=1)
    return x * tab_ref[0] + _dot(pieces, rot) * tab_ref[1]


def _group_inv_rms(x, ones, n):
    sq = (x * x).astype(BF16)
    ms = jnp.concatenate([_dot(sq[:, c:c + BW], ones) for c in range(0, x.shape[1], BW)], axis=1)
    return lax.rsqrt(ms * (1.0 / n) + EPS)


def _inproj_kernel(n_x, *refs):
    x_refs = refs[:n_x]
    (mod_ref, g1_ref, wxa_ref, wq_ref, wk_ref, wv_ref, wxc_ref, wmla_ref,
     icnt_ref, poolw_ref, pools_ref, seg_ref, ones_ref, rotd_ref, rotm_ref, dgq_ref, dgk_ref,
     ropd_ref, ropm_ref, cc_ref, ss_ref, ft_ref, gqa_ref, wuq_ref, gqn_ref, gkv_ref,
     wuk_ref, wuv_ref, gkn_ref,
     pa_ref, qd_ref, kd_ref, vd_ref, oc_ref, qm_ref, km_ref, vm_ref,
     kcf_ref, vf_ref, ckv_ref, kr_ref, pad_ref) = refs[n_x:]
    is_lat = pl.program_id(0) >= CTX_TILES
    mod = mod_ref[0]
    hb = _norm_mod(_read_x(x_refs, is_lat), g1_ref[...], mod[:, D:2 * D], mod[:, 0:D]).astype(BF16)

    xa = _dot_nt(hb, wxa_ref[...])
    q = _dot_nt(hb, wq_ref[...])
    k = _dot_nt(hb, wk_ref[...])
    v = _dot_nt(hb, wv_ref[...])
    xcb = _dot_nt(hb, wxc_ref[...]).astype(BF16)
    cq = _dot_nt(hb, wmla_ref[C_CQ:C_CQ + BW, :])
    ckv_kr = _dot_nt(hb, wmla_ref[C_CKV:C_END, :])

    pad_ref[0:POOL_PAD, :] = jnp.zeros((POOL_PAD, BW), F32)
    pad_ref[POOL_PAD + TILE:, :] = jnp.zeros((POOL_PAD, BW), F32)
    pad_ref[POOL_PAD:POOL_PAD + TILE, :] = xa
    lane = lax.broadcasted_iota(jnp.int32, (TILE, BW), 1)
    row8 = lax.broadcasted_iota(jnp.int32, (SUBLANES, BW), 0)

    def term(s):
        v = pad_ref[POOL_PAD + s:POOL_PAD + s + TILE, :]
        kill = row8 < jnp.where(is_lat, 0, -s) if s < 0 else row8 >= jnp.where(is_lat, SUBLANES, SUBLANES - s)
        parts = []
        for r0 in range(0, TILE, CTX_T):
            r1 = r0 + CTX_T
            if s < 0:
                head = v[r0:r0 + SUBLANES]
                parts += [head if r0 == 0 else jnp.where(kill, 0.0, head), v[r0 + SUBLANES:r1]]
            else:
                tail = v[r1 - SUBLANES:r1]
                parts += [v[r0:r1 - SUBLANES], tail if r1 == TILE else jnp.where(kill, 0.0, tail)]
        return jnp.concatenate(parts, axis=0)

    acc = xa
    wins = []
    lo_s, hi_s = 0, 0
    for half in POOL_HALF:
        for s in list(range(-half, lo_s)) + list(range(hi_s + 1, half)):
            acc = acc + term(s)
        lo_s, hi_s = -half, half - 1
        wins.append(acc)
    group = BW // len(POOL_HALF)
    win = jnp.where(lane < group, wins[0],
                    jnp.where(lane < 2 * group, wins[1], jnp.where(lane < 3 * group, wins[2], wins[3])))
    pooled = (win * icnt_ref[...] - xa).astype(BF16)
    pa_ref[...] = (_dot(pooled, poolw_ref[...]) * pools_ref[...]).astype(BF16)

    qscale = DA_D ** -0.5 * LOG2E
    rq = _group_inv_rms(q, seg_ref[...], DA_D) * qscale
    rk = _group_inv_rms(k, seg_ref[...], DA_D)
    kns = []
    for s in range(BW // LANES):
        sl = slice(s * LANES, (s + 1) * LANES)
        kg = k[:, sl] * dgk_ref[...]
        kns.append(kg * rk[:, sl])
        qd_ref[:, sl] = (_rope(q[:, sl] * dgq_ref[...], ropd_ref, rotd_ref[...]) * rq[:, sl]).astype(BF16)
        kd_ref[:, sl] = (_rope(kg, ropd_ref, rotd_ref[...]) * rk[:, sl]).astype(BF16)
    vd_ref[...] = v.astype(BF16)

    xcc = _dot(xcb, cc_ref[...]).astype(BF16)
    xcs = _dot(xcb, ss_ref[...]).astype(BF16)
    y = _dot(ft_ref[0], xcc) + _dot(ft_ref[1], xcs)
    fnorm = jnp.where(is_lat, (LAT_T * 64.0) ** -0.5, (CTX_T * 64.0) ** -0.5)
    oc_ref[...] = (y * fnorm).astype(BF16)

    qa = (_rms(cq, Q_LORA) * gqa_ref[...]).astype(BF16)
    qf = _dot(qa, wuq_ref[...])
    mscale = MLA_QK ** -0.5 * LOG2E
    rqm = _group_inv_rms(qf, ones_ref[...], MLA_QK) * mscale
    for h in range(HEADS):
        sl = slice(h * LANES, (h + 1) * LANES)
        qm_ref[:, sl] = (_rope(qf[:, sl] * gqn_ref[...], ropm_ref, rotm_ref[...]) * rqm[:, sl]).astype(BF16)
    ckv, kr_pad = ckv_kr[:, :KV_LORA], ckv_kr[:, KV_LORA:]
    ckv_n = _rms(ckv, KV_LORA) * gkv_ref[...]
    cb = ckv_n.astype(BF16)
    knope = _dot(cb, wuk_ref[...])
    kr_rot = _rope(kr_pad * gkn_ref[...], ropm_ref, rotm_ref[...])
    kr_ssq = _dot((kr_pad * kr_pad).astype(BF16), ones_ref[0:LANES, 0:LANES])
    kn_ssq = jnp.concatenate([_dot((knope[:, c:c + BW] * knope[:, c:c + BW]).astype(BF16), ones_ref[...])
                              for c in range(0, HEADS * LANES, BW)], axis=1)
    for h in range(HEADS):
        sl = slice(h * LANES, (h + 1) * LANES)
        rkm = lax.rsqrt((kn_ssq[:, sl] + kr_ssq) * (1.0 / MLA_QK) + EPS)
        km_ref[:, sl] = ((knope[:, sl] * gkn_ref[...] + kr_rot) * rkm).astype(BF16)
    vm_ref[...] = _dot(cb, wuv_ref[...]).astype(BF16)

    @pl.when(jnp.logical_not(is_lat))
    def _():
        for s, kn in enumerate(kns):
            kcf_ref[:, s * LANES:(s + 1) * LANES] = kn
        vf_ref[...] = v
        ckv_ref[...] = ckv_n
        kr_ref[...] = kr_pad[:, MLA_NOPE:MLA_NOPE + MLA_ROPE]


def _inproj_call(l, xs, mods, w, consts):
    row_outs = [(BW, BF16), (BW, BF16), (BW, BF16), (BW, BF16), (BW, BF16), (512, BF16), (512, BF16), (BW, BF16)]
    ctx_outs = [(BW, F32), (BW, F32), (KV_LORA, F32), (MLA_ROPE, F32)]
    return pl.pallas_call(
        functools.partial(_inproj_kernel, len(xs)),
        grid=(TILES,),
        in_specs=_x_specs(len(xs) == 2) + [
            _mod_spec(l), _layer((1, D), l), *[_in_cols(l, W_BLOCKS + n) for n in range(5)],
            _layer((C_END, D), l), _group_spec((TILE, BW)), _layer((BW, BW), l), _layer((1, BW), l),
            _const((BW, BW)), _const((BW, BW)), _const((BW, LANES)), _const((BW, LANES)),
            _layer((1, LANES), l), _layer((1, LANES), l),
            _group_spec((2, TILE, LANES)), _group_spec((2, TILE, LANES)),
            _const((BW, BW)), _const((BW, BW)), _group_spec((2, TILE, TILE)),
            _layer((1, BW), l), _layer((BW, 512), l), _layer((1, LANES), l), _layer((1, KV_LORA), l),
            _layer((KV_LORA, 512), l), _layer((KV_LORA, BW), l), _layer((1, LANES), l),
        ],
        out_specs=[_row_spec(wd) for wd, _ in row_outs] + [_ctx_row_spec(wd) for wd, _ in ctx_outs],
        out_shape=([jax.ShapeDtypeStruct((ROWS, wd), dt) for wd, dt in row_outs]
                   + [jax.ShapeDtypeStruct((CTX_ROWS, wd), dt) for wd, dt in ctx_outs]),
        scratch_shapes=[pltpu.VMEM((TILE + 2 * POOL_PAD, BW), F32)],
        compiler_params=_params(1),
        name="inproj",
    )(*xs, mods, w["g1"], *[w["win"]] * 5, w["wmla"], consts["icnt"], w["poolw"], w["pools"], consts["seg"],
      consts["ones"], consts["rotd"], consts["rotm"], w["dgq"], w["dgk"],
      consts["ropd"], consts["ropm"], consts["cc"], consts["ss"], consts["ft"],
      w["gqa"], w["wuq"], w["gqn"], w["gkv"], w["wuk"], w["wuv"], w["gkn"])


def _softmax_pv(ss, vs):
    m = ss[0].max(axis=-1, keepdims=True)
    for s in ss[1:]:
        m = jnp.maximum(m, s.max(axis=-1, keepdims=True))
    ps = [jnp.exp2(s - m) for s in ss]
    l = ps[0].sum(axis=-1, keepdims=True)
    for p in ps[1:]:
        l = l + p.sum(axis=-1, keepdims=True)
    return sum(_dot(p.astype(BF16), v) for p, v in zip(ps, vs)) * (1.0 / l)


def _attend(qd, kds, vds, qm, kms, vms, lam, gsub, lam_init, ob_ref, od_ref, out_rows):
    n = qd.shape[0]
    lane = lax.broadcasted_iota(jnp.int32, (n, BW), 1)
    zero = jnp.zeros((), BF16)

    jobs = []
    for h in range(HEADS):
        for half in range(2):
            lo = h * DA_HD + half * DA_D
            jobs.append((jnp.where((lane >= lo) & (lane < lo + DA_D), qd, zero), kds, vds))
        sl = slice(h * LANES, (h + 1) * LANES)
        jobs.append((qm[:, sl], [k[:, sl] for k in kms], vms))

    score = lambda job: [_dot_nt(job[0], k) for k in job[1]]
    out_b = jnp.zeros((n, BW), F32)
    out_d = jnp.zeros((n, BW), F32)
    pvs, nxt = [], score(jobs[0])
    for c, job in enumerate(jobs):
        ss, nxt = nxt, score(jobs[c + 1]) if c + 1 < len(jobs) else None
        pvs.append(_softmax_pv(ss, job[2]))
        if len(pvs) == 3:
            h = c // 3
            head = (lane >= h * DA_HD) & (lane < (h + 1) * DA_HD)
            o = jnp.where(head, pvs[0] - lam * pvs[1], 0.0)
            ms = jnp.sum(o * o, axis=-1, keepdims=True) * (1.0 / DA_HD)
            out_b = out_b + o * lax.rsqrt(ms + EPS)
            out_d = out_d + jnp.where(head, pvs[2], 0.0)
            pvs = []
    ob_ref[out_rows, :] = (out_b * gsub * (1.0 - lam_init)).astype(BF16)
    od_ref[out_rows, :] = out_d.astype(BF16)


def _attn_kernel(lam_init, qd_ref, kd_ref, vd_ref, cdk_ref, cdv_ref, qm_ref, km_ref, vm_ref, kc_ref, vc_ref,
                 dl_ref, gsub_ref, ob_ref, od_ref):
    i = pl.program_id(0)
    j = pl.program_id(1)
    dl = dl_ref[...]
    lam = (jnp.exp(jnp.sum(dl[0:1] * dl[1:2], axis=-1, keepdims=True))
           - jnp.exp(jnp.sum(dl[2:3] * dl[3:4], axis=-1, keepdims=True)) + lam_init)
    @pl.when(i < CTX_TILES)
    def _():
        for sub in range(TQ // CTX_T):
            rows = pl.ds(pl.multiple_of(j * TQ + sub * CTX_T, CTX_T), CTX_T)
            _attend(qd_ref[rows, :], [kd_ref[rows, :]], [vd_ref[rows, :]],
                    qm_ref[rows, :], [km_ref[rows, :]], [vm_ref[rows, :]],
                    lam, gsub_ref[...], lam_init, ob_ref, od_ref, slice(sub * CTX_T, (sub + 1) * CTX_T))

    @pl.when(i >= CTX_TILES)
    def _():
        rows = pl.ds(pl.multiple_of(j * TQ, TQ), TQ)
        _attend(qd_ref[rows, :], [cdk_ref[...], kd_ref[...]], [cdv_ref[...], vd_ref[...]],
                qm_ref[rows, :], [kc_ref[...], km_ref[...]], [vc_ref[...], vm_ref[...]],
                lam, gsub_ref[...], lam_init, ob_ref, od_ref, slice(None))


def _attn_call(l, qd, kd, vd, cdk, cdv, qm, km, vm, kc, vc, w):
    tile = lambda width: pl.BlockSpec((TILE, width), lambda i, j: (i, 0))
    cache = lambda width: pl.BlockSpec((None, None, PAST, width),
                                       lambda i, j: (jnp.maximum(i - CTX_TILES, 0), l, 0, 0))
    blk = pl.BlockSpec((TQ, BW), lambda i, j: (i * (TILE // TQ) + j, 0))
    return pl.pallas_call(
        functools.partial(_attn_kernel, _lambda_init(l)),
        grid=(TILES, TILE // TQ),
        in_specs=[tile(BW), tile(BW), tile(BW), cache(BW), cache(BW), tile(512), tile(512), tile(BW),
                  cache(512), cache(BW), _layer((4, DA_D), l), _layer((1, BW), l)],
        out_specs=[blk, blk],
        out_shape=[jax.ShapeDtypeStruct((ROWS, BW), BF16)] * 2,
        compiler_params=_params(2),
        name="attention",
    )(qd, kd, vd, cdk, cdv, qm, km, vm, kc, vc, w["dl"], w["gsub"])


def _merge_kernel(n_x, *refs):
    x_refs = refs[:n_x]
    mod_ref, g1_ref, wg_ref, wb_ref, wo_ref, pa_ref, pb_ref, pc_ref, pd_ref, o_ref = refs[n_x:]
    mod = mod_ref[0]
    x = _read_x(x_refs, pl.program_id(0) >= CTX_TILES)
    hb = _norm_mod(x, g1_ref[...], mod[:, D:2 * D], mod[:, 0:D]).astype(BF16)
    merged = None
    for n, br_ref in enumerate((pa_ref, pb_ref, pc_ref, pd_ref)):
        gate = _sigmoid(_dot_nt(hb, wg_ref[n * D:(n + 1) * D, :]))
        t = gate * _dot(br_ref[...], wb_ref[n])
        merged = t if merged is None else merged + t
    o_ref[...] = x + mod[:, 2 * D:3 * D] * _dot(merged.astype(BF16), wo_ref[...])


def _merge_call(l, xs, mods, w, pa, pb, pc, pd):
    return pl.pallas_call(
        functools.partial(_merge_kernel, len(xs)),
        grid=(TILES,),
        in_specs=_x_specs(len(xs) == 2) + [
            _mod_spec(l), _layer((1, D), l), _layer((4 * D, D), l), _layer((4, BW, D), l), _layer((D, D), l),
            _row_spec(BW), _row_spec(BW), _row_spec(BW), _row_spec(BW)],
        out_specs=_row_spec(D),
        out_shape=jax.ShapeDtypeStruct((ROWS, D), F32),
        compiler_params=_params(1),
        name="merge",
    )(*xs, mods, w["g1"], w["win"], w["wbr"], w["wout"], pa, pb, pc, pd)


def _ffn_kernel(n_o, x_ref, mod_ref, g2_ref, wu_ref, cv_ref, wd_ref, *refs):
    o_refs = refs[:n_o]
    hb_ref, acc_ref = refs[n_o:]
    is_lat = pl.program_id(0) >= CTX_TILES
    is_ctx = jnp.logical_not(is_lat)
    mod = mod_ref[0]
    hb_ref[...] = _norm_mod(x_ref[...], g2_ref[...], mod[:, 4 * D:5 * D], mod[:, 3 * D:4 * D]).astype(BF16)
    acc_ref[...] = jnp.zeros((TILE, D), F32)

    def edge_masks(width):
        row8 = lax.broadcasted_iota(jnp.int32, (SUBLANES, width), 0)
        return (row8 == 0, row8 == SUBLANES - 1,
                row8 == jnp.where(is_ctx, 0, -1), row8 == jnp.where(is_ctx, SUBLANES - 1, -1))

    def up(cols):
        return _dot(hb_ref[...], wu_ref[:, cols])

    def taps(u, cols, masks):
        tile_start, tile_end, seq_start, seq_end = masks
        cv = cv_ref[:, cols]
        prev, nxt = pltpu.roll(u, 1, 0), pltpu.roll(u, TILE - 1, 0)
        prev_parts, nxt_parts = [], []
        for r0 in range(0, TILE, CTX_T):
            r1 = r0 + CTX_T
            start = tile_start if r0 == 0 else seq_start
            end = tile_end if r1 == TILE else seq_end
            prev_parts += [jnp.where(start, 0.0, prev[r0:r0 + SUBLANES]), prev[r0 + SUBLANES:r1]]
            nxt_parts += [nxt[r0:r1 - SUBLANES], jnp.where(end, 0.0, nxt[r1 - SUBLANES:r1])]
        prev, nxt = jnp.concatenate(prev_parts, axis=0), jnp.concatenate(nxt_parts, axis=0)
        return prev * cv[0:1] + u * cv[1:2] + nxt * cv[2:3] + cv[3:4]

    widths = [FFN_CHUNK] * (FFN // FFN_CHUNK) + ([FFN % FFN_CHUNK] if FFN % FFN_CHUNK else [])
    starts = [sum(widths[:i]) for i in range(len(widths))]
    masks = {wd: edge_masks(wd) for wd in set(widths)}
    ug = up(pl.ds(starts[0], widths[0]))
    for i, (c0, wd) in enumerate(zip(starts, widths)):
        uv = up(pl.ds(FFN + c0, wd))
        ug_next = up(pl.ds(starts[i + 1], widths[i + 1])) if i + 1 < len(widths) else None
        h = taps(ug, pl.ds(c0, wd), masks[wd])
        v = taps(uv, pl.ds(FFN + c0, wd), masks[wd])
        act = (h * (1.0 + jnp.tanh(h)) * v).astype(BF16)
        acc_ref[...] += _dot(act, wd_ref[pl.ds(c0, wd), :])
        ug = ug_next
    y = x_ref[...] + mod[:, 5 * D:6 * D] * acc_ref[...]
    if n_o == 1:
        o_refs[0][...] = y
    else:
        @pl.when(is_ctx)
        def _():
            o_refs[0][...] = y

        @pl.when(is_lat)
        def _():
            o_refs[1][...] = y


def _ffn_call(l, x, mods, w, split_out):
    if split_out:
        out_specs = _x_specs(True)
        out_shape = [jax.ShapeDtypeStruct((CTX_ROWS, D), F32), jax.ShapeDtypeStruct((LAT_ROWS, D), F32)]
    else:
        out_specs = _x_specs(False)
        out_shape = [jax.ShapeDtypeStruct((ROWS, D), F32)]
    return pl.pallas_call(
        functools.partial(_ffn_kernel, len(out_shape)),
        grid=(TILES,),
        in_specs=[_row_spec(D), _mod_spec(l), _layer((1, D), l),
                  _layer((D, 2 * FFN), l), _layer((8, 2 * FFN), l), _layer((FFN, D), l)],
        out_specs=out_specs,
        out_shape=out_shape,
        scratch_shapes=[pltpu.VMEM((TILE, D), BF16), pltpu.VMEM((TILE, D), F32)],
        compiler_params=_params(1),
        name="ffn",
    )(x, mods, w["g2"], w["wup"], w["cv"], w["wdown"])


def _rope_tables():
    half = 8
    inv = ROPE_BASE ** (-np.arange(half, dtype=np.float64) / half)
    t = np.arange(LAT_T)
    rows, cols = (t // GRID_W).astype(np.float64), (t % GRID_W).astype(np.float64)
    ang = np.concatenate([rows[:, None] * inv[None, :]] * 2 + [cols[:, None] * inv[None, :]] * 2, axis=1)
    cos, sin = np.cos(ang), np.sin(ang)

    def diff_layout(a):
        return np.tile(a, (1, LANES // 32))

    def mla_layout(a, fill):
        out = np.full((LAT_T, LANES), fill)
        out[:, MLA_NOPE:MLA_QK] = a
        return out

    ident = lambda layout: np.stack([layout(np.ones_like(cos)), layout(np.zeros_like(cos))])
    ropd = np.stack([ident(diff_layout), np.stack([diff_layout(cos), diff_layout(sin)])])
    ropm = np.stack([np.stack([mla_layout(np.ones_like(cos), 1.0), mla_layout(np.zeros_like(cos), 0.0)]),
                     np.stack([mla_layout(cos, 1.0), mla_layout(sin, 0.0)])])
    return jnp.asarray(ropd, F32), jnp.asarray(ropm, F32)


def _pool_inv_count():
    out = []
    for seq in (CTX_T, LAT_T):
        t = np.arange(TILE) % seq
        cols = [np.repeat((1.0 / (np.minimum(t + half, seq) - np.maximum(t - half, 0)))[:, None],
                          BW // len(POOL_HALF), axis=1)
                for half in POOL_HALF]
        out.append(np.concatenate(cols, axis=1))
    return jnp.asarray(np.stack(out), F32)


def _rotate_half_matrix(lanes):
    r = np.zeros((LANES, LANES))
    for j in lanes:
        if j % 16 < 8:
            r[j + 8, j] = -1.0
        else:
            r[j - 8, j] = 1.0
    return np.concatenate([r, r], axis=0)


def _matrix_tables():
    def cs(n):
        k = np.arange(n)
        ang = 2.0 * np.pi * ((k[:, None] * k[None, :]) % n) / n
        return np.cos(ang), np.sin(ang)

    c64, s64 = cs(64)
    eye4 = np.eye(4)
    cl, sl = cs(LAT_T)
    cc_, sc_ = cs(CTX_T)
    ft = np.stack([np.stack([np.kron(eye4, cc_), -np.kron(eye4, sc_)]), np.stack([cl, -sl])])
    as_bf16 = lambda a: jnp.asarray(a, F32).astype(BF16)
    return dict(
        cc=as_bf16(np.kron(eye4, c64)), ss=as_bf16(np.kron(eye4, s64)), ft=as_bf16(ft),
        seg=as_bf16(np.kron(np.eye(BW // DA_D), np.ones((DA_D, DA_D)))),
        ones=as_bf16(np.kron(np.eye(BW // LANES), np.ones((LANES, LANES)))),
        rotd=as_bf16(_rotate_half_matrix(range(LANES))),
        rotm=as_bf16(_rotate_half_matrix(range(MLA_NOPE, MLA_QK))))


def _pad_last(a, lo, hi):
    return jnp.pad(a, ((0, 0),) * (a.ndim - 1) + ((lo, hi),))


def _prep_weights(p):
    win = jnp.swapaxes(p["w_in"], 1, 2).astype(BF16)
    mla = win[:, W_MLA:, :]
    cq, ckv, kr = mla[:, :Q_LORA], mla[:, Q_LORA:Q_LORA + KV_LORA], mla[:, Q_LORA + KV_LORA:]
    pad_rows = lambda a, lo, hi: jnp.pad(a, ((0, 0), (lo, hi), (0, 0)))
    wmla = jnp.concatenate([pad_rows(cq, 0, BW - Q_LORA), ckv, pad_rows(kr, MLA_NOPE, LANES - MLA_QK)], axis=1)
    poolw = jnp.einsum("lgce,gh->lgche", p["pool_w"], jnp.eye(4, dtype=F32)).reshape(DEPTH, BW, BW).astype(BF16)
    wuq = p["w_uq"].reshape(DEPTH, Q_LORA, HEADS, MLA_QK)
    wuq = jnp.pad(wuq, ((0, 0), (0, BW - Q_LORA), (0, 0), (0, LANES - MLA_QK))).reshape(DEPTH, BW, HEADS * LANES)
    wukv = p["w_ukv"].reshape(DEPTH, KV_LORA, HEADS, MLA_NOPE + MLA_V)
    wuk = _pad_last(wukv[..., :MLA_NOPE], 0, LANES - MLA_NOPE).reshape(DEPTH, KV_LORA, HEADS * LANES)
    wuv = wukv[..., MLA_NOPE:].reshape(DEPTH, KV_LORA, HEADS * MLA_V)
    cv = jnp.concatenate([p["conv_w"], p["conv_b"][:, None, :], jnp.zeros((DEPTH, 4, 2 * FFN), F32)], axis=1)
    cv = cv * jnp.where(jnp.arange(2 * FFN) < FFN, 0.5, 1.0)

    row = lambda a, width: _pad_last(a, 0, width - a.shape[-1])[:, None, :]
    tiled = lambda a, reps: jnp.tile(a, (1, reps))[:, None, :]
    return {
        "g1": row(p["g_norm1"], D), "g2": row(p["g_norm2"], D),
        "win": win, "wmla": wmla,
        "poolw": poolw, "pools": row(p["pool_scale"], BW),
        "dgq": tiled(p["diff_qnorm"], LANES // DA_D), "dgk": tiled(p["diff_knorm"], LANES // DA_D),
        "dl": p["diff_lambda"], "gsub": tiled(p["diff_subln"], HEADS),
        "gqa": row(p["mla_qa_norm"], BW), "wuq": wuq.astype(BF16), "gqn": row(p["mla_qnorm"], LANES),
        "gkv": row(p["mla_kv_norm"], KV_LORA), "wuk": wuk.astype(BF16), "wuv": wuv.astype(BF16),
        "gkn": row(p["mla_knorm"], LANES),
        "wbr": p["w_branch"].astype(BF16), "wout": p["w_out"].astype(BF16),
        "wup": p["w_up"].astype(BF16), "cv": cv, "wdown": p["w_down"].astype(BF16),
    }


def kernel(x_prompt, x_sample, c, c_ctx, cache_diff_k, cache_diff_v, cache_mla_ckv, cache_mla_krope, w_ada, b_ada, g_norm1, g_norm2, w_in, pool_w, pool_scale, diff_qnorm, diff_knorm, diff_lambda, diff_subln, mla_qa_norm, w_uq, mla_kv_norm, w_ukv, mla_qnorm, mla_knorm, w_branch, w_out, w_up, conv_w, conv_b, w_down):
    w = _prep_weights(dict(
        w_in=w_in, pool_w=pool_w, pool_scale=pool_scale, diff_qnorm=diff_qnorm, diff_knorm=diff_knorm,
        diff_lambda=diff_lambda, diff_subln=diff_subln, mla_qa_norm=mla_qa_norm, w_uq=w_uq,
        mla_kv_norm=mla_kv_norm, w_ukv=w_ukv, mla_qnorm=mla_qnorm, mla_knorm=mla_knorm, w_branch=w_branch,
        w_out=w_out, w_up=w_up, conv_w=conv_w, conv_b=conv_b, w_down=w_down, g_norm1=g_norm1,
        g_norm2=g_norm2))
    ropd, ropm = _rope_tables()
    consts = dict(_matrix_tables(), ropd=ropd, ropm=ropm, icnt=_pool_inv_count())

    cond = jnp.concatenate([c_ctx[None, :], c, jnp.zeros((16 - 1 - LAT_B, D), F32)], axis=0)
    mods = _mod_call(cond, w_ada, b_ada).reshape(DEPTH, 16, 1, 6 * D)

    kc, vc = _ctxkv_call(cache_mla_ckv, _pad_last(cache_mla_krope, MLA_NOPE, LANES - MLA_QK), w)
    cdk = cache_diff_k.reshape(LAT_B, DEPTH, PAST, BW).astype(BF16)
    cdv = cache_diff_v.reshape(LAT_B, DEPTH, PAST, BW).astype(BF16)

    xs = [x_prompt.reshape(CTX_ROWS, D), x_sample.reshape(LAT_ROWS, D)]
    new = [[], [], [], []]
    for l in range(DEPTH):
        pa, qd, kd, vd, oc, qm, km, vm, *ctx_outs = _inproj_call(l, xs, mods, w, consts)
        ob, od = _attn_call(l, qd, kd, vd, cdk, cdv, qm, km, vm, kc, vc, w)
        x = _merge_call(l, xs, mods, w, pa, ob, oc, od)
        xs = _ffn_call(l, x, mods, w, split_out=(l == DEPTH - 1))
        for acc, a in zip(new, ctx_outs):
            acc.append(a)

    stack = lambda arrs, shape: jnp.stack(arrs, axis=1).reshape((CTX_B, DEPTH, CTX_T) + shape)
    return (xs[0].reshape(CTX_B, CTX_T, D), xs[1].reshape(LAT_B, LAT_T, D),
            stack([a.reshape(CTX_B, CTX_T, BW) for a in new[0]], (HEADS, DA_HD)),
            stack([a.reshape(CTX_B, CTX_T, BW) for a in new[1]], (HEADS, DA_HD)),
            stack([a.reshape(CTX_B, CTX_T, KV_LORA) for a in new[2]], (KV_LORA,)),
            stack([a.reshape(CTX_B, CTX_T, MLA_ROPE) for a in new[3]], (MLA_ROPE,)))
```

```python
import functools
import math

import jax
import jax.numpy as jnp
import numpy as np
from jax import lax
from jax.experimental import pallas as pl
from jax.experimental.pallas import tpu as pltpu

F32 = jnp.float32
BF16 = jnp.bfloat16

D = 1024
DEPTH = 4
CTX_B, CTX_T = 16, 256
LAT_B, LAT_T = 8, 1024
PAST = 512
TILE = 1024
CTX_TILES = CTX_B * CTX_T // TILE
LAT_TILES = LAT_B * LAT_T // TILE
TILES = CTX_TILES + LAT_TILES
ROWS = TILES * TILE
CTX_ROWS = CTX_TILES * TILE
LAT_ROWS = LAT_TILES * TILE
GRID_W = 64
BW = 256
DA_D = 32
DA_HD = 64
HEADS = 4
MLA_NOPE, MLA_ROPE, MLA_V, MLA_QK = 64, 32, 64, 96
Q_LORA, KV_LORA = 192, 128
FFN = 2816
FFN_CHUNK = 1024
SUBLANES = 8
POOL_HALF = (1, 2, 4, 8)
POOL_PAD = 16
ROPE_BASE = 10000.0
EPS = 1e-6
LOG2E = 1.4426950408889634
TQ = 512
LANES = 128
VMEM_LIMIT = 60 * 1024 * 1024

W_BLOCKS = 4 * D // BW
W_MLA = 4 * D + 5 * BW
C_CQ, C_CKV, C_END = 0, 256, 512


def _lambda_init(l):
    return 0.8 - 0.6 * math.exp(-0.3 * l)


def _sigmoid(x):
    return 0.5 * jnp.tanh(0.5 * x) + 0.5


def _rms(x, n):
    ms = jnp.sum(x * x, axis=-1, keepdims=True) * (1.0 / n)
    return x * lax.rsqrt(ms + EPS)


def _dot(a, b):
    return jnp.dot(a, b, preferred_element_type=F32)


def _dot_nt(a, b):
    return lax.dot_general(a, b, (((1,), (1,)), ((), ())), preferred_element_type=F32)


def _const(shape):
    n = len(shape)
    return pl.BlockSpec(shape, lambda *_: (0,) * n, pipeline_mode=pl.Buffered(1))


def _layer(shape, l):
    n = len(shape)
    return pl.BlockSpec((None,) + shape, lambda *_: (l,) + (0,) * n, pipeline_mode=pl.Buffered(1))


def _params(n_grid):
    return pltpu.CompilerParams(dimension_semantics=("arbitrary",) * n_grid, vmem_limit_bytes=VMEM_LIMIT)


def _row_spec(width):
    return pl.BlockSpec((TILE, width), lambda i: (i, 0))


def _ctx_row_spec(width):
    return pl.BlockSpec((TILE, width), lambda i: (jnp.minimum(i, CTX_TILES - 1), 0))


def _lat_row_spec(width):
    return pl.BlockSpec((TILE, width), lambda i: (jnp.maximum(i - CTX_TILES, 0), 0))


def _x_specs(split):
    return [_ctx_row_spec(D), _lat_row_spec(D)] if split else [_row_spec(D)]


def _read_x(x_refs, is_lat):
    if len(x_refs) == 2:
        return jnp.where(is_lat, x_refs[1][...], x_refs[0][...])
    return x_refs[0][...]


def _in_cols(l, n):
    return pl.BlockSpec((None, BW, D), lambda *_: (l, n, 0), pipeline_mode=pl.Buffered(1))


def _mod_spec(l):
    return pl.BlockSpec((None, 1, 1, 6 * D), lambda i: (l, jnp.maximum(i - (CTX_TILES - 1), 0), 0, 0))


def _group_spec(shape):
    n = len(shape)
    return pl.BlockSpec((None,) + shape, lambda i: (jnp.minimum(i // CTX_TILES, 1),) + (0,) * n,
                        pipeline_mode=pl.Buffered(1))


def _mod_kernel(c_ref, w_ref, b_ref, o_ref):
    c = c_ref[...]
    s = (c * _sigmoid(c)).astype(BF16)
    o_ref[0] = _dot(s, w_ref[0].astype(BF16)) + b_ref[0]


def _mod_call(cond, w_ada, b_ada):
    nc = 4
    cw = 6 * D // nc
    return pl.pallas_call(
        _mod_kernel,
        grid=(DEPTH, nc),
        in_specs=[
            pl.BlockSpec((16, D), lambda l, j: (0, 0)),
            pl.BlockSpec((1, D, cw), lambda l, j: (l, 0, j)),
            pl.BlockSpec((1, 1, cw), lambda l, j: (l, 0, j)),
        ],
        out_specs=pl.BlockSpec((1, 16, cw), lambda l, j: (l, 0, j)),
        out_shape=jax.ShapeDtypeStruct((DEPTH, 16, 6 * D), F32),
        compiler_params=_params(2),
        name="adaln_mod",
    )(cond, w_ada, b_ada.reshape(DEPTH, 1, 6 * D))


def _mla_key_heads(knope, kr_pad, gkn):
    out = []
    for h in range(HEADS):
        kh = knope[:, h * LANES:(h + 1) * LANES] + kr_pad
        out.append(_rms(kh, MLA_QK) * gkn)
    return out


def _ctxkv_kernel(ckv_ref, kr_ref, wuk_ref, wuv_ref, gkn_ref, kc_ref, vc_ref):
    rows = LAT_B * PAST
    cb = ckv_ref[...].reshape(rows, KV_LORA).astype(BF16)
    knope = _dot(cb, wuk_ref[...])
    heads = _mla_key_heads(knope, kr_ref[...].reshape(rows, LANES), gkn_ref[...])
    for h in range(HEADS):
        kc_ref[:, :, h * LANES:(h + 1) * LANES] = heads[h].astype(BF16).reshape(LAT_B, PAST, LANES)
    vc_ref[...] = _dot(cb, wuv_ref[...]).astype(BF16).reshape(LAT_B, PAST, BW)


def _ctxkv_call(ckv, kr_pad, w):
    cache = lambda width: pl.BlockSpec((LAT_B, None, PAST, width), lambda l: (0, l, 0, 0))
    wspec = lambda r, c: pl.BlockSpec((None, r, c), lambda l: (l, 0, 0))
    return pl.pallas_call(
        _ctxkv_kernel,
        grid=(DEPTH,),
        in_specs=[cache(KV_LORA), cache(LANES), wspec(KV_LORA, 512), wspec(KV_LORA, BW), wspec(1, LANES)],
        out_specs=[cache(512), cache(BW)],
        out_shape=[
            jax.ShapeDtypeStruct((LAT_B, DEPTH, PAST, 512), BF16),
            jax.ShapeDtypeStruct((LAT_B, DEPTH, PAST, BW), BF16),
        ],
        compiler_params=_params(1),
        name="mla_ctx_keys",
    )(ckv, kr_pad, w["wuk"], w["wuv"], w["gkn"])


def _norm_mod(x, g, sc, sh):
    return _rms(x, D) * (g * (1.0 + sc)) + sh


def _rope(x, tab_ref, rot):
    hi = x.astype(BF16)
    pieces = jnp.concatenate([hi, (x - hi.astype(F32)).astype(BF16)], axis=1)
    return x * tab_ref[0] + _dot(pieces, rot) * tab_ref[1]


def _group_inv_rms(x, ones, n):
    sq = (x * x).astype(BF16)
    ms = jnp.concatenate([_dot(sq[:, c:c + BW], ones) for c in range(0, x.shape[1], BW)], axis=1)
    return lax.rsqrt(ms * (1.0 / n) + EPS)


def _inproj_kernel(n_x, *refs):
    x_refs = refs[:n_x]
    (mod_ref, g1_ref, wxa_ref, wq_ref, wk_ref, wv_ref, wxc_ref, wmla_ref,
     icnt_ref, poolw_ref, pools_ref, seg_ref, ones_ref, rotd_ref, rotm_ref, dgq_ref, dgk_ref,
     ropd_ref, ropm_ref, cc_ref, ss_ref, ft_ref, gqa_ref, wuq_ref, gqn_ref, gkv_ref,
     wuk_ref, wuv_ref, gkn_ref,
     pa_ref, qd_ref, kd_ref, vd_ref, oc_ref, qm_ref, km_ref, vm_ref,
     kcf_ref, vf_ref, ckv_ref, kr_ref, pad_ref) = refs[n_x:]
    is_lat = pl.program_id(0) >= CTX_TILES
    mod = mod_ref[0]
    hb = _norm_mod(_read_x(x_refs, is_lat), g1_ref[...], mod[:, D:2 * D], mod[:, 0:D]).astype(BF16)

    xa = _dot_nt(hb, wxa_ref[...])
    q = _dot_nt(hb, wq_ref[...])
    k = _dot_nt(hb, wk_ref[...])
    v = _dot_nt(hb, wv_ref[...])
    xcb = _dot_nt(hb, wxc_ref[...]).astype(BF16)
    cq = _dot_nt(hb, wmla_ref[C_CQ:C_CQ + BW, :])
    ckv_kr = _dot_nt(hb, wmla_ref[C_CKV:C_END, :])

    pad_ref[0:POOL_PAD, :] = jnp.zeros((POOL_PAD, BW), F32)
    pad_ref[POOL_PAD + TILE:, :] = jnp.zeros((POOL_PAD, BW), F32)
    pad_ref[POOL_PAD:POOL_PAD + TILE, :] = xa
    lane = lax.broadcasted_iota(jnp.int32, (TILE, BW), 1)
    row8 = lax.broadcasted_iota(jnp.int32, (SUBLANES, BW), 0)

    def term(s):
        v = pad_ref[POOL_PAD + s:POOL_PAD + s + TILE, :]
        kill = row8 < jnp.where(is_lat, 0, -s) if s < 0 else row8 >= jnp.where(is_lat, SUBLANES, SUBLANES - s)
        parts = []
        for r0 in range(0, TILE, CTX_T):
            r1 = r0 + CTX_T
            if s < 0:
                head = v[r0:r0 + SUBLANES]
                parts += [head if r0 == 0 else jnp.where(kill, 0.0, head), v[r0 + SUBLANES:r1]]
            else:
                tail = v[r1 - SUBLANES:r1]
                parts += [v[r0:r1 - SUBLANES], tail if r1 == TILE else jnp.where(kill, 0.0, tail)]
        return jnp.concatenate(parts, axis=0)

    acc = xa
    wins = []
    lo_s, hi_s = 0, 0
    for half in POOL_HALF:
        for s in list(range(-half, lo_s)) + list(range(hi_s + 1, half)):
            acc = acc + term(s)
        lo_s, hi_s = -half, half - 1
        wins.append(acc)
    group = BW // len(POOL_HALF)
    win = jnp.where(lane < group, wins[0],
                    jnp.where(lane < 2 * group, wins[1], jnp.where(lane < 3 * group, wins[2], wins[3])))
    pooled = (win * icnt_ref[...] - xa).astype(BF16)
    pa_ref[...] = (_dot(pooled, poolw_ref[...]) * pools_ref[...]).astype(BF16)

    qscale = DA_D ** -0.5 * LOG2E
    rq = _group_inv_rms(q, seg_ref[...], DA_D) * qscale
    rk = _group_inv_rms(k, seg_ref[...], DA_D)
    kns = []
    for s in range(BW // LANES):
        sl = slice(s * LANES, (s + 1) * LANES)
        kg = k[:, sl] * dgk_ref[...]
        kns.append(kg * rk[:, sl])
        qd_ref[:, sl] = (_rope(q[:, sl] * dgq_ref[...], ropd_ref, rotd_ref[...]) * rq[:, sl]).astype(BF16)
        kd_ref[:, sl] = (_rope(kg, ropd_ref, rotd_ref[...]) * rk[:, sl]).astype(BF16)
    vd_ref[...] = v.astype(BF16)

    xcc = _dot(xcb, cc_ref[...]).astype(BF16)
    xcs = _dot(xcb, ss_ref[...]).astype(BF16)
    y = _dot(ft_ref[0], xcc) + _dot(ft_ref[1], xcs)
    fnorm = jnp.where(is_lat, (LAT_T * 64.0) ** -0.5, (CTX_T * 64.0) ** -0.5)
    oc_ref[...] = (y * fnorm).astype(BF16)

    qa = (_rms(cq, Q_LORA) * gqa_ref[...]).astype(BF16)
    qf = _dot(qa, wuq_ref[...])
    mscale = MLA_QK ** -0.5 * LOG2E
    rqm = _group_inv_rms(qf, ones_ref[...], MLA_QK) * mscale
    for h in range(HEADS):
        sl = slice(h * LANES, (h + 1) * LANES)
        qm_ref[:, sl] = (_rope(qf[:, sl] * gqn_ref[...], ropm_ref, rotm_ref[...]) * rqm[:, sl]).astype(BF16)
    ckv, kr_pad = ckv_kr[:, :KV_LORA], ckv_kr[:, KV_LORA:]
    ckv_n = _rms(ckv, KV_LORA) * gkv_ref[...]
    cb = ckv_n.astype(BF16)
    knope = _dot(cb, wuk_ref[...])
    kr_rot = _rope(kr_pad * gkn_ref[...], ropm_ref, rotm_ref[...])
    kr_ssq = _dot((kr_pad * kr_pad).astype(BF16), ones_ref[0:LANES, 0:LANES])
    kn_ssq = jnp.concatenate([_dot((knope[:, c:c + BW] * knope[:, c:c + BW]).astype(BF16), ones_ref[...])
                              for c in range(0, HEADS * LANES, BW)], axis=1)
    for h in range(HEADS):
        sl = slice(h * LANES, (h + 1) * LANES)
        rkm = lax.rsqrt((kn_ssq[:, sl] + kr_ssq) * (1.0 / MLA_QK) + EPS)
        km_ref[:, sl] = ((knope[:, sl] * gkn_ref[...] + kr_rot) * rkm).astype(BF16)
    vm_ref[...] = _dot(cb, wuv_ref[...]).astype(BF16)

    @pl.when(jnp.logical_not(is_lat))
    def _():
        for s, kn in enumerate(kns):
            kcf_ref[:, s * LANES:(s + 1) * LANES] = kn
        vf_ref[...] = v
        ckv_ref[...] = ckv_n
        kr_ref[...] = kr_pad[:, MLA_NOPE:MLA_NOPE + MLA_ROPE]


def _inproj_call(l, xs, mods, w, consts):
    row_outs = [(BW, BF16), (BW, BF16), (BW, BF16), (BW, BF16), (BW, BF16), (512, BF16), (512, BF16), (BW, BF16)]
    ctx_outs = [(BW, F32), (BW, F32), (KV_LORA, F32), (MLA_ROPE, F32)]
    return pl.pallas_call(
        functools.partial(_inproj_kernel, len(xs)),
        grid=(TILES,),
        in_specs=_x_specs(len(xs) == 2) + [
            _mod_spec(l), _layer((1, D), l), *[_in_cols(l, W_BLOCKS + n) for n in range(5)],
            _layer((C_END, D), l), _group_spec((TILE, BW)), _layer((BW, BW), l), _layer((1, BW), l),
            _const((BW, BW)), _const((BW, BW)), _const((BW, LANES)), _const((BW, LANES)),
            _layer((1, LANES), l), _layer((1, LANES), l),
            _group_spec((2, TILE, LANES)), _group_spec((2, TILE, LANES)),
            _const((BW, BW)), _const((BW, BW)), _group_spec((2, TILE, TILE)),
            _layer((1, BW), l), _layer((BW, 512), l), _layer((1, LANES), l), _layer((1, KV_LORA), l),
            _layer((KV_LORA, 512), l), _layer((KV_LORA, BW), l), _layer((1, LANES), l),
        ],
        out_specs=[_row_spec(wd) for wd, _ in row_outs] + [_ctx_row_spec(wd) for wd, _ in ctx_outs],
        out_shape=([jax.ShapeDtypeStruct((ROWS, wd), dt) for wd, dt in row_outs]
                   + [jax.ShapeDtypeStruct((CTX_ROWS, wd), dt) for wd, dt in ctx_outs]),
        scratch_shapes=[pltpu.VMEM((TILE + 2 * POOL_PAD, BW), F32)],
        compiler_params=_params(1),
        name="inproj",
    )(*xs, mods, w["g1"], *[w["win"]] * 5, w["wmla"], consts["icnt"], w["poolw"], w["pools"], consts["seg"],
      consts["ones"], consts["rotd"], consts["rotm"], w["dgq"], w["dgk"],
      consts["ropd"], consts["ropm"], consts["cc"], consts["ss"], consts["ft"],
      w["gqa"], w["wuq"], w["gqn"], w["gkv"], w["wuk"], w["wuv"], w["gkn"])


def _softmax_pv(ss, vs):
    m = ss[0].max(axis=-1, keepdims=True)
    for s in ss[1:]:
        m = jnp.maximum(m, s.max(axis=-1, keepdims=True))
    ps = [jnp.exp2(s - m) for s in ss]
    l = ps[0].sum(axis=-1, keepdims=True)
    for p in ps[1:]:
        l = l + p.sum(axis=-1, keepdims=True)
    return sum(_dot(p.astype(BF16), v) for p, v in zip(ps, vs)) * (1.0 / l)


def _attend(qd, kds, vds, qm, kms, vms, lam, gsub, lam_init, ob_ref, od_ref, out_rows):
    n = qd.shape[0]
    lane = lax.broadcasted_iota(jnp.int32, (n, BW), 1)
    zero = jnp.zeros((), BF16)

    jobs = []
    for h in range(HEADS):
        for half in range(2):
            lo = h * DA_HD + half * DA_D
            jobs.append((jnp.where((lane >= lo) & (lane < lo + DA_D), qd, zero), kds, vds))
        sl = slice(h * LANES, (h + 1) * LANES)
        jobs.append((qm[:, sl], [k[:, sl] for k in kms], vms))

    score = lambda job: [_dot_nt(job[0], k) for k in job[1]]
    out_b = jnp.zeros((n, BW), F32)
    out_d = jnp.zeros((n, BW), F32)
    pvs, nxt = [], score(jobs[0])
    for c, job in enumerate(jobs):
        ss, nxt = nxt, score(jobs[c + 1]) if c + 1 < len(jobs) else None
        pvs.append(_softmax_pv(ss, job[2]))
        if len(pvs) == 3:
            h = c // 3
            head = (lane >= h * DA_HD) & (lane < (h + 1) * DA_HD)
            o = jnp.where(head, pvs[0] - lam * pvs[1], 0.0)
            ms = jnp.sum(o * o, axis=-1, keepdims=True) * (1.0 / DA_HD)
            out_b = out_b + o * lax.rsqrt(ms + EPS)
            out_d = out_d + jnp.where(head, pvs[2], 0.0)
            pvs = []
    ob_ref[out_rows, :] = (out_b * gsub * (1.0 - lam_init)).astype(BF16)
    od_ref[out_rows, :] = out_d.astype(BF16)


def _attn_kernel(lam_init, qd_ref, kd_ref, vd_ref, cdk_ref, cdv_ref, qm_ref, km_ref, vm_ref, kc_ref, vc_ref,
                 dl_ref, gsub_ref, ob_ref, od_ref):
    i = pl.program_id(0)
    j = pl.program_id(1)
    dl = dl_ref[...]
    lam = (jnp.exp(jnp.sum(dl[0:1] * dl[1:2], axis=-1, keepdims=True))
           - jnp.exp(jnp.sum(dl[2:3] * dl[3:4], axis=-1, keepdims=True)) + lam_init)
    @pl.when(i < CTX_TILES)
    def _():
        for sub in range(TQ // CTX_T):
            rows = pl.ds(pl.multiple_of(j * TQ + sub * CTX_T, CTX_T), CTX_T)
            _attend(qd_ref[rows, :], [kd_ref[rows, :]], [vd_ref[rows, :]],
                    qm_ref[rows, :], [km_ref[rows, :]], [vm_ref[rows, :]],
                    lam, gsub_ref[...], lam_init, ob_ref, od_ref, slice(sub * CTX_T, (sub + 1) * CTX_T))

    @pl.when(i >= CTX_TILES)
    def _():
        rows = pl.ds(pl.multiple_of(j * TQ, TQ), TQ)
        _attend(qd_ref[rows, :], [cdk_ref[...], kd_ref[...]], [cdv_ref[...], vd_ref[...]],
                qm_ref[rows, :], [kc_ref[...], km_ref[...]], [vc_ref[...], vm_ref[...]],
                lam, gsub_ref[...], lam_init, ob_ref, od_ref, slice(None))


def _attn_call(l, qd, kd, vd, cdk, cdv, qm, km, vm, kc, vc, w):
    tile = lambda width: pl.BlockSpec((TILE, width), lambda i, j: (i, 0))
    cache = lambda width: pl.BlockSpec((None, None, PAST, width),
                                       lambda i, j: (jnp.maximum(i - CTX_TILES, 0), l, 0, 0))
    blk = pl.BlockSpec((TQ, BW), lambda i, j: (i * (TILE // TQ) + j, 0))
    return pl.pallas_call(
        functools.partial(_attn_kernel, _lambda_init(l)),
        grid=(TILES, TILE // TQ),
        in_specs=[tile(BW), tile(BW), tile(BW), cache(BW), cache(BW), tile(512), tile(512), tile(BW),
                  cache(512), cache(BW), _layer((4, DA_D), l), _layer((1, BW), l)],
        out_specs=[blk, blk],
        out_shape=[jax.ShapeDtypeStruct((ROWS, BW), BF16)] * 2,
        compiler_params=_params(2),
        name="attention",
    )(qd, kd, vd, cdk, cdv, qm, km, vm, kc, vc, w["dl"], w["gsub"])


def _merge_kernel(n_x, *refs):
    x_refs = refs[:n_x]
    mod_ref, g1_ref, wg_ref, wb_ref, wo_ref, pa_ref, pb_ref, pc_ref, pd_ref, o_ref = refs[n_x:]
    mod = mod_ref[0]
    x = _read_x(x_refs, pl.program_id(0) >= CTX_TILES)
    hb = _norm_mod(x, g1_ref[...], mod[:, D:2 * D], mod[:, 0:D]).astype(BF16)
    merged = None
    for n, br_ref in enumerate((pa_ref, pb_ref, pc_ref, pd_ref)):
        gate = _sigmoid(_dot_nt(hb, wg_ref[n * D:(n + 1) * D, :]))
        t = gate * _dot(br_ref[...], wb_ref[n])
        merged = t if merged is None else merged + t
    o_ref[...] = x + mod[:, 2 * D:3 * D] * _dot(merged.astype(BF16), wo_ref[...])


def _merge_call(l, xs, mods, w, pa, pb, pc, pd):
    return pl.pallas_call(
        functools.partial(_merge_kernel, len(xs)),
        grid=(TILES,),
        in_specs=_x_specs(len(xs) == 2) + [
            _mod_spec(l), _layer((1, D), l), _layer((4 * D, D), l), _layer((4, BW, D), l), _layer((D, D), l),
            _row_spec(BW), _row_spec(BW), _row_spec(BW), _row_spec(BW)],
        out_specs=_row_spec(D),
        out_shape=jax.ShapeDtypeStruct((ROWS, D), F32),
        compiler_params=_params(1),
        name="merge",
    )(*xs, mods, w["g1"], w["win"], w["wbr"], w["wout"], pa, pb, pc, pd)


def _ffn_kernel(n_o, x_ref, mod_ref, g2_ref, wu_ref, cv_ref, wd_ref, *refs):
    o_refs = refs[:n_o]
    hb_ref, acc_ref = refs[n_o:]
    is_lat = pl.program_id(0) >= CTX_TILES
    is_ctx = jnp.logical_not(is_lat)
    mod = mod_ref[0]
    hb_ref[...] = _norm_mod(x_ref[...], g2_ref[...], mod[:, 4 * D:5 * D], mod[:, 3 * D:4 * D]).astype(BF16)
    acc_ref[...] = jnp.zeros((TILE, D), F32)

    def edge_masks(width):
        row8 = lax.broadcasted_iota(jnp.int32, (SUBLANES, width), 0)
        return (row8 == 0, row8 == SUBLANES - 1,
                row8 == jnp.where(is_ctx, 0, -1), row8 == jnp.where(is_ctx, SUBLANES - 1, -1))

    def up(cols):
        return _dot(hb_ref[...], wu_ref[:, cols])

    def taps(u, cols, masks):
        tile_start, tile_end, seq_start, seq_end = masks
        cv = cv_ref[:, cols]
        prev, nxt = pltpu.roll(u, 1, 0), pltpu.roll(u, TILE - 1, 0)
        prev_parts, nxt_parts = [], []
        for r0 in range(0, TILE, CTX_T):
            r1 = r0 + CTX_T
            start = tile_start if r0 == 0 else seq_start
            end = tile_end if r1 == TILE else seq_end
            prev_parts += [jnp.where(start, 0.0, prev[r0:r0 + SUBLANES]), prev[r0 + SUBLANES:r1]]
            nxt_parts += [nxt[r0:r1 - SUBLANES], jnp.where(end, 0.0, nxt[r1 - SUBLANES:r1])]
        prev, nxt = jnp.concatenate(prev_parts, axis=0), jnp.concatenate(nxt_parts, axis=0)
        return prev * cv[0:1] + u * cv[1:2] + nxt * cv[2:3] + cv[3:4]

    widths = [FFN_CHUNK] * (FFN // FFN_CHUNK) + ([FFN % FFN_CHUNK] if FFN % FFN_CHUNK else [])
    starts = [sum(widths[:i]) for i in range(len(widths))]
    masks = {wd: edge_masks(wd) for wd in set(widths)}
    gate = lambda i: pl.ds(starts[i], widths[i])
    val = lambda i: pl.ds(FFN + starts[i], widths[i])

    def down(act, i):
        acc_ref[...] += _dot(act, wd_ref[gate(i), :])

    ups, act = (up(gate(0)), up(val(0))), None
    for i, wd in enumerate(widths):
        if act is not None:
            down(act, i - 1)
        nxt = (up(gate(i + 1)), up(val(i + 1))) if i + 1 < len(widths) else None
        h = taps(ups[0], gate(i), masks[wd])
        v = taps(ups[1], val(i), masks[wd])
        act = (h * (1.0 + jnp.tanh(h)) * v).astype(BF16)
        ups = nxt
    down(act, len(widths) - 1)
    y = x_ref[...] + mod[:, 5 * D:6 * D] * acc_ref[...]
    if n_o == 1:
        o_refs[0][...] = y
    else:
        @pl.when(is_ctx)
        def _():
            o_refs[0][...] = y

        @pl.when(is_lat)
        def _():
            o_refs[1][...] = y


def _ffn_call(l, x, mods, w, split_out):
    if split_out:
        out_specs = _x_specs(True)
        out_shape = [jax.ShapeDtypeStruct((CTX_ROWS, D), F32), jax.ShapeDtypeStruct((LAT_ROWS, D), F32)]
    else:
        out_specs = _x_specs(False)
        out_shape = [jax.ShapeDtypeStruct((ROWS, D), F32)]
    return pl.pallas_call(
        functools.partial(_ffn_kernel, len(out_shape)),
        grid=(TILES,),
        in_specs=[_row_spec(D), _mod_spec(l), _layer((1, D), l),
                  _layer((D, 2 * FFN), l), _layer((8, 2 * FFN), l), _layer((FFN, D), l)],
        out_specs=out_specs,
        out_shape=out_shape,
        scratch_shapes=[pltpu.VMEM((TILE, D), BF16), pltpu.VMEM((TILE, D), F32)],
        compiler_params=_params(1),
        name="ffn",
    )(x, mods, w["g2"], w["wup"], w["cv"], w["wdown"])


def _rope_tables():
    half = 8
    inv = ROPE_BASE ** (-np.arange(half, dtype=np.float64) / half)
    t = np.arange(LAT_T)
    rows, cols = (t // GRID_W).astype(np.float64), (t % GRID_W).astype(np.float64)
    ang = np.concatenate([rows[:, None] * inv[None, :]] * 2 + [cols[:, None] * inv[None, :]] * 2, axis=1)
    cos, sin = np.cos(ang), np.sin(ang)

    def diff_layout(a):
        return np.tile(a, (1, LANES // 32))

    def mla_layout(a, fill):
        out = np.full((LAT_T, LANES), fill)
        out[:, MLA_NOPE:MLA_QK] = a
        return out

    ident = lambda layout: np.stack([layout(np.ones_like(cos)), layout(np.zeros_like(cos))])
    ropd = np.stack([ident(diff_layout), np.stack([diff_layout(cos), diff_layout(sin)])])
    ropm = np.stack([np.stack([mla_layout(np.ones_like(cos), 1.0), mla_layout(np.zeros_like(cos), 0.0)]),
                     np.stack([mla_layout(cos, 1.0), mla_layout(sin, 0.0)])])
    return jnp.asarray(ropd, F32), jnp.asarray(ropm, F32)


def _pool_inv_count():
    out = []
    for seq in (CTX_T, LAT_T):
        t = np.arange(TILE) % seq
        cols = [np.repeat((1.0 / (np.minimum(t + half, seq) - np.maximum(t - half, 0)))[:, None],
                          BW // len(POOL_HALF), axis=1)
                for half in POOL_HALF]
        out.append(np.concatenate(cols, axis=1))
    return jnp.asarray(np.stack(out), F32)


def _rotate_half_matrix(lanes):
    r = np.zeros((LANES, LANES))
    for j in lanes:
        if j % 16 < 8:
            r[j + 8, j] = -1.0
        else:
            r[j - 8, j] = 1.0
    return np.concatenate([r, r], axis=0)


def _matrix_tables():
    def cs(n):
        k = np.arange(n)
        ang = 2.0 * np.pi * ((k[:, None] * k[None, :]) % n) / n
        return np.cos(ang), np.sin(ang)

    c64, s64 = cs(64)
    eye4 = np.eye(4)
    cl, sl = cs(LAT_T)
    cc_, sc_ = cs(CTX_T)
    ft = np.stack([np.stack([np.kron(eye4, cc_), -np.kron(eye4, sc_)]), np.stack([cl, -sl])])
    as_bf16 = lambda a: jnp.asarray(a, F32).astype(BF16)
    return dict(
        cc=as_bf16(np.kron(eye4, c64)), ss=as_bf16(np.kron(eye4, s64)), ft=as_bf16(ft),
        seg=as_bf16(np.kron(np.eye(BW // DA_D), np.ones((DA_D, DA_D)))),
        ones=as_bf16(np.kron(np.eye(BW // LANES), np.ones((LANES, LANES)))),
        rotd=as_bf16(_rotate_half_matrix(range(LANES))),
        rotm=as_bf16(_rotate_half_matrix(range(MLA_NOPE, MLA_QK))))


def _pad_last(a, lo, hi):
    return jnp.pad(a, ((0, 0),) * (a.ndim - 1) + ((lo, hi),))


def _prep_weights(p):
    win = jnp.swapaxes(p["w_in"], 1, 2).astype(BF16)
    mla = win[:, W_MLA:, :]
    cq, ckv, kr = mla[:, :Q_LORA], mla[:, Q_LORA:Q_LORA + KV_LORA], mla[:, Q_LORA + KV_LORA:]
    pad_rows = lambda a, lo, hi: jnp.pad(a, ((0, 0), (lo, hi), (0, 0)))
    wmla = jnp.concatenate([pad_rows(cq, 0, BW - Q_LORA), ckv, pad_rows(kr, MLA_NOPE, LANES - MLA_QK)], axis=1)
    poolw = jnp.einsum("lgce,gh->lgche", p["pool_w"], jnp.eye(4, dtype=F32)).reshape(DEPTH, BW, BW).astype(BF16)
    wuq = p["w_uq"].reshape(DEPTH, Q_LORA, HEADS, MLA_QK)
    wuq = jnp.pad(wuq, ((0, 0), (0, BW - Q_LORA), (0, 0), (0, LANES - MLA_QK))).reshape(DEPTH, BW, HEADS * LANES)
    wukv = p["w_ukv"].reshape(DEPTH, KV_LORA, HEADS, MLA_NOPE + MLA_V)
    wuk = _pad_last(wukv[..., :MLA_NOPE], 0, LANES - MLA_NOPE).reshape(DEPTH, KV_LORA, HEADS * LANES)
    wuv = wukv[..., MLA_NOPE:].reshape(DEPTH, KV_LORA, HEADS * MLA_V)
    cv = jnp.concatenate([p["conv_w"], p["conv_b"][:, None, :], jnp.zeros((DEPTH, 4, 2 * FFN), F32)], axis=1)
    cv = cv * jnp.where(jnp.arange(2 * FFN) < FFN, 0.5, 1.0)

    row = lambda a, width: _pad_last(a, 0, width - a.shape[-1])[:, None, :]
    tiled = lambda a, reps: jnp.tile(a, (1, reps))[:, None, :]
    return {
        "g1": row(p["g_norm1"], D), "g2": row(p["g_norm2"], D),
        "win": win, "wmla": wmla,
        "poolw": poolw, "pools": row(p["pool_scale"], BW),
        "dgq": tiled(p["diff_qnorm"], LANES // DA_D), "dgk": tiled(p["diff_knorm"], LANES // DA_D),
        "dl": p["diff_lambda"], "gsub": tiled(p["diff_subln"], HEADS),
        "gqa": row(p["mla_qa_norm"], BW), "wuq": wuq.astype(BF16), "gqn": row(p["mla_qnorm"], LANES),
        "gkv": row(p["mla_kv_norm"], KV_LORA), "wuk": wuk.astype(BF16), "wuv": wuv.astype(BF16),
        "gkn": row(p["mla_knorm"], LANES),
        "wbr": p["w_branch"].astype(BF16), "wout": p["w_out"].astype(BF16),
        "wup": p["w_up"].astype(BF16), "cv": cv, "wdown": p["w_down"].astype(BF16),
    }


def kernel(x_prompt, x_sample, c, c_ctx, cache_diff_k, cache_diff_v, cache_mla_ckv, cache_mla_krope, w_ada, b_ada, g_norm1, g_norm2, w_in, pool_w, pool_scale, diff_qnorm, diff_knorm, diff_lambda, diff_subln, mla_qa_norm, w_uq, mla_kv_norm, w_ukv, mla_qnorm, mla_knorm, w_branch, w_out, w_up, conv_w, conv_b, w_down):
    w = _prep_weights(dict(
        w_in=w_in, pool_w=pool_w, pool_scale=pool_scale, diff_qnorm=diff_qnorm, diff_knorm=diff_knorm,
        diff_lambda=diff_lambda, diff_subln=diff_subln, mla_qa_norm=mla_qa_norm, w_uq=w_uq,
        mla_kv_norm=mla_kv_norm, w_ukv=w_ukv, mla_qnorm=mla_qnorm, mla_knorm=mla_knorm, w_branch=w_branch,
        w_out=w_out, w_up=w_up, conv_w=conv_w, conv_b=conv_b, w_down=w_down, g_norm1=g_norm1,
        g_norm2=g_norm2))
    ropd, ropm = _rope_tables()
    consts = dict(_matrix_tables(), ropd=ropd, ropm=ropm, icnt=_pool_inv_count())

    cond = jnp.concatenate([c_ctx[None, :], c, jnp.zeros((16 - 1 - LAT_B, D), F32)], axis=0)
    mods = _mod_call(cond, w_ada, b_ada).reshape(DEPTH, 16, 1, 6 * D)

    kc, vc = _ctxkv_call(cache_mla_ckv, _pad_last(cache_mla_krope, MLA_NOPE, LANES - MLA_QK), w)
    cdk = cache_diff_k.reshape(LAT_B, DEPTH, PAST, BW).astype(BF16)
    cdv = cache_diff_v.reshape(LAT_B, DEPTH, PAST, BW).astype(BF16)

    xs = [x_prompt.reshape(CTX_ROWS, D), x_sample.reshape(LAT_ROWS, D)]
    new = [[], [], [], []]
    for l in range(DEPTH):
        pa, qd, kd, vd, oc, qm, km, vm, *ctx_outs = _inproj_call(l, xs, mods, w, consts)
        ob, od = _attn_call(l, qd, kd, vd, cdk, cdv, qm, km, vm, kc, vc, w)
        x = _merge_call(l, xs, mods, w, pa, ob, oc, od)
        xs = _ffn_call(l, x, mods, w, split_out=(l == DEPTH - 1))
        for acc, a in zip(new, ctx_outs):
            acc.append(a)

    stack = lambda arrs, shape: jnp.stack(arrs, axis=1).reshape((CTX_B, DEPTH, CTX_T) + shape)
    return (xs[0].reshape(CTX_B, CTX_T, D), xs[1].reshape(LAT_B, LAT_T, D),
            stack([a.reshape(CTX_B, CTX_T, BW) for a in new[0]], (HEADS, DA_HD)),
            stack([a.reshape(CTX_B, CTX_T, BW) for a in new[1]], (HEADS, DA_HD)),
            stack([a.reshape(CTX_B, CTX_T, KV_LORA) for a in new[2]], (KV_LORA,)),
            stack([a.reshape(CTX_B, CTX_T, MLA_ROPE) for a in new[3]], (MLA_ROPE,)))
```
